```python
import math
import jax, jax.numpy as jnp
from jax import lax
import numpy as np

D_MODEL = 1024
BATCH = 32
SEQ = 2048
DEPTH = 1

SSD_EXPAND = 2
SSD_INNER = SSD_EXPAND * D_MODEL
SSD_HEAD_DIM = 64
SSD_HEADS = SSD_INNER // SSD_HEAD_DIM
SSD_GROUPS = 4
SSD_STATE = 128
SSD_CONV = 4
SSD_CHUNK = 128
XBC_WIDTH = SSD_INNER + 2 * SSD_GROUPS * SSD_STATE
DIFF_HEADS = 8
DIFF_HEAD_DIM = D_MODEL // (2 * DIFF_HEADS)
DIFF_V_DIM = 2 * DIFF_HEAD_DIM
DIFF_WIDTH = DIFF_HEADS * DIFF_V_DIM
Q_BLOCK = 128
REL_BUCKETS = 32
REL_MAX_DIST = 128
MEM_LEN = 256
MEM_HEADS = 4
MEM_HEAD_DIM = D_MODEL // MEM_HEADS
N_EXPERTS = 64
TOP_K = 8
N_EXPERT_GROUPS = 8
TOP_GROUPS = 4
EXPERT_FF = 256
SHARED_FF = 256
ROUTED_SCALE = 2.5
DN_ALPHA = (2.0 * DEPTH) ** 0.25
DN_BETA = (8.0 * DEPTH) ** -0.25
NORM_EPS = 1e-5
IN_WIDTH = SSD_INNER + XBC_WIDTH + SSD_HEADS + 3 * DIFF_WIDTH + 2 * D_MODEL

kernel_name = "hybrid_ssd_diffattn_moe_deepnorm"


def layer_norm(x, g, b):
    xf = x.astype(jnp.float32)
    mu = jnp.mean(xf, axis=-1, keepdims=True)
    var = jnp.mean(jnp.square(xf - mu), axis=-1, keepdims=True)
    return ((xf - mu) * lax.rsqrt(var + NORM_EPS) * g.astype(jnp.float32) + b.astype(jnp.float32)).astype(x.dtype)


def rms_norm(x, g):
    xf = x.astype(jnp.float32)
    return (xf * lax.rsqrt(jnp.mean(jnp.square(xf), axis=-1, keepdims=True) + NORM_EPS) * g.astype(jnp.float32)).astype(x.dtype)


def causal_dwconv(u, w, b):
    c = u.shape[-1]
    out = lax.conv_general_dilated(u, w[:, None, :].astype(u.dtype), window_strides=(1,),
                                   padding=[(SSD_CONV - 1, 0)],
                                   dimension_numbers=('NWC', 'WIO', 'NWC'),
                                   feature_group_count=c)
    return out + b


def ssd_chunked(xs, a, bm, cm):
    b, s, h, p = xs.shape
    c, l, g, j = s // SSD_CHUNK, SSD_CHUNK, SSD_GROUPS, h // SSD_GROUPS
    xs = xs.reshape(b, c, l, g, j, p)
    bm = bm.reshape(b, c, l, g, SSD_STATE)
    cm = cm.reshape(b, c, l, g, SSD_STATE)
    acs = jnp.cumsum(a.reshape(b, c, l, g, j).transpose(0, 3, 4, 1, 2), axis=-1)
    causal = jnp.tril(jnp.ones((l, l), dtype=bool))
    decay_in = jnp.exp(jnp.where(causal, acs[..., :, None] - acs[..., None, :], -jnp.inf))
    cb = jnp.einsum('bclgn,bcsgn->bgcls', cm, bm)
    y_diag = jnp.einsum('bgcls,bgjcls,bcsgjp->bclgjp', cb, decay_in, xs)
    decay_out = jnp.exp(acs[..., -1:] - acs)
    states = jnp.einsum('bclgn,bgjcl,bclgjp->cbgjpn', bm, decay_out, xs)
    chunk_decay = jnp.exp(acs[..., -1]).transpose(3, 0, 1, 2)

    def step(hstate, inp):
        st, dec = inp
        return hstate * dec[..., None, None] + st, hstate

    h0 = jnp.zeros(states.shape[1:], states.dtype)
    _, prev = lax.scan(step, h0, (states, chunk_decay.astype(states.dtype)))
    y_off = jnp.einsum('bclgn,cbgjpn,bgjcl->bclgjp', cm, prev, jnp.exp(acs))
    return (y_diag + y_off).reshape(b, s, h, p)


def ssd_branch(z, xbc, dt_raw, conv_w, conv_b, dt_bias, a_log, d_skip, norm_g, w_br):
    b, s, _ = z.shape
    nbc = SSD_GROUPS * SSD_STATE
    xbc = jax.nn.silu(causal_dwconv(xbc, conv_w, conv_b))
    xs = xbc[..., :SSD_INNER].reshape(b, s, SSD_HEADS, SSD_HEAD_DIM)
    bm = xbc[..., SSD_INNER:SSD_INNER + nbc].reshape(b, s, SSD_GROUPS, SSD_STATE)
    cm = xbc[..., SSD_INNER + nbc:].reshape(b, s, SSD_GROUPS, SSD_STATE)
    dt = jax.nn.softplus(dt_raw.astype(jnp.float32) + dt_bias.astype(jnp.float32))
    a = -jnp.exp(a_log.astype(jnp.float32)) * dt
    y = ssd_chunked(xs * dt[..., None], a, bm, cm)
    y = (y + d_skip[:, None] * xs).reshape(b, s, SSD_INNER) * jax.nn.silu(z)
    y = rms_norm(y.reshape(b, s, SSD_GROUPS, -1), norm_g.reshape(SSD_GROUPS, -1)).reshape(b, s, SSD_INNER)
    return y @ w_br


def t5_bucket(dist):
    max_exact = REL_BUCKETS // 2
    d = jnp.maximum(dist, 1).astype(jnp.float32)
    large = max_exact + (jnp.log(d / max_exact) / math.log(REL_MAX_DIST / max_exact)
                         * (REL_BUCKETS - max_exact)).astype(jnp.int32)
    large = jnp.minimum(large, REL_BUCKETS - 1)
    return jnp.where(dist < max_exact, dist, large)


def diff_attention(q, k, v, lam, rel_bias):
    s = q.shape[1]
    q = q * DIFF_HEAD_DIM ** -0.5
    outs = []
    for i in range(s // Q_BLOCK):
        q0, kend = i * Q_BLOCK, (i + 1) * Q_BLOCK
        logits = jnp.einsum('bqhmd,bkhmd->bhmqk', q[:, q0:kend], k[:, :kend]).astype(jnp.float32)
        dist = (q0 + jnp.arange(Q_BLOCK))[:, None] - jnp.arange(kend)[None, :]
        bias = rel_bias.astype(jnp.float32)[t5_bucket(jnp.maximum(dist, 0))]
        logits = jnp.where(dist >= 0, logits + bias.transpose(2, 0, 1)[:, None], -jnp.inf)
        p = jax.nn.softmax(logits, axis=-1)
        attn = p[:, :, 0] - lam * p[:, :, 1]
        outs.append(jnp.einsum('bhqk,bkhe->bqhe', attn.astype(v.dtype), v[:, :kend]))
    return jnp.concatenate(outs, axis=1)


def diff_branch(q, k, v, lq1, lk1, lq2, lk2, subln_g, rel_bias, w_br, layer_idx):
    b, s, _ = q.shape
    lam_init = 0.8 - 0.6 * math.exp(-0.3 * layer_idx)
    f32 = jnp.float32
    lam = (jnp.exp(jnp.sum(lq1.astype(f32) * lk1.astype(f32)))
           - jnp.exp(jnp.sum(lq2.astype(f32) * lk2.astype(f32))) + lam_init)
    o = diff_attention(q.reshape(b, s, DIFF_HEADS, 2, DIFF_HEAD_DIM),
                       k.reshape(b, s, DIFF_HEADS, 2, DIFF_HEAD_DIM),
                       v.reshape(b, s, DIFF_HEADS, DIFF_V_DIM), lam, rel_bias)
    o = rms_norm(o, subln_g) * (1.0 - lam_init)
    return o.reshape(b, s, DIFF_WIDTH) @ w_br


def memory_cross_attention(x, mem, w_cq, w_ckv, w_co):
    b, s, d = x.shape
    q = (x @ w_cq).reshape(b, s, MEM_HEADS, MEM_HEAD_DIM)
    kv = (mem @ w_ckv).reshape(b, mem.shape[1], 2, MEM_HEADS, MEM_HEAD_DIM)
    logits = jnp.einsum('bqhd,bkhd->bhqk', q, kv[:, :, 0]).astype(jnp.float32) * MEM_HEAD_DIM ** -0.5
    p = jax.nn.softmax(logits, axis=-1).astype(x.dtype)
    o = jnp.einsum('bhqk,bkhd->bqhd', p, kv[:, :, 1]).reshape(b, s, d)
    return o @ w_co


def moe_ffn(x, w_router, router_bias, w_exp_gu, w_exp_down, w_sh_gu, w_sh_down):
    b, s, d = x.shape
    t = x.reshape(-1, d)
    n_tok = t.shape[0]
    scores = jax.nn.sigmoid(jnp.dot(t, w_router).astype(jnp.float32))
    choice = scores + router_bias.astype(jnp.float32)
    grp = choice.reshape(n_tok, N_EXPERT_GROUPS, N_EXPERTS // N_EXPERT_GROUPS)
    grp_score = jnp.sum(lax.top_k(grp, 2)[0], axis=-1)
    _, top_g = lax.top_k(grp_score, TOP_GROUPS)
    gmask = jnp.any(top_g[..., None] == jnp.arange(N_EXPERT_GROUPS), axis=-2)
    emask = jnp.repeat(gmask, N_EXPERTS // N_EXPERT_GROUPS, axis=-1)
    _, idx = lax.top_k(jnp.where(emask, choice, -jnp.inf), TOP_K)
    w = jnp.take_along_axis(scores, idx, axis=-1)
    w = w / jnp.sum(w, axis=-1, keepdims=True) * ROUTED_SCALE
    gates = jnp.zeros((n_tok, N_EXPERTS), jnp.float32).at[jnp.arange(n_tok)[:, None], idx].set(w).astype(t.dtype)
    gu = t @ w_sh_gu
    y = (jax.nn.silu(gu[:, :SHARED_FF]) * gu[:, SHARED_FF:]) @ w_sh_down
    for e in range(N_EXPERTS):
        gu = t @ w_exp_gu[e]
        h = jax.nn.silu(gu[:, :EXPERT_FF]) * gu[:, EXPERT_FF:]
        y = y + gates[:, e:e + 1] * (h @ w_exp_down[e])
    return y.reshape(b, s, d)


def setup_inputs(seed: int = 0) -> dict:
    key = jax.random.key(seed)
    ks = iter(jax.random.split(key, 48))
    L = DEPTH

    def nrm(shape, scale):
        return jax.random.normal(next(ks), shape, jnp.float32) * scale

    dt0 = jnp.exp(jax.random.uniform(next(ks), (L, SSD_HEADS), jnp.float32,
                                     minval=math.log(1e-3), maxval=math.log(1e-1)))
    return {
        "x": nrm((BATCH, SEQ, D_MODEL), 1.0),
        "mem": nrm((BATCH, MEM_LEN, D_MODEL), 1.0),
        "w_in": nrm((L, D_MODEL, IN_WIDTH), D_MODEL ** -0.5),
        "conv_w": nrm((L, SSD_CONV, XBC_WIDTH), SSD_CONV ** -0.5),
        "conv_b": nrm((L, XBC_WIDTH), 0.02),
        "dt_bias": dt0 + jnp.log(-jnp.expm1(-dt0)),
        "a_log": jnp.log(jax.random.uniform(next(ks), (L, SSD_HEADS), jnp.float32, minval=1.0, maxval=16.0)),
        "d_skip": 1.0 + nrm((L, SSD_HEADS), 0.1),
        "ssd_norm_g": 1.0 + nrm((L, SSD_INNER), 0.02),
        "lambda_q1": nrm((L, DIFF_HEAD_DIM), 0.1),
        "lambda_k1": nrm((L, DIFF_HEAD_DIM), 0.1),
        "lambda_q2": nrm((L, DIFF_HEAD_DIM), 0.1),
        "lambda_k2": nrm((L, DIFF_HEAD_DIM), 0.1),
        "subln_g": 1.0 + nrm((L, DIFF_V_DIM), 0.02),
        "rel_bias": nrm((REL_BUCKETS, DIFF_HEADS), 0.5),
        "w_ssd_br": nrm((L, SSD_INNER, D_MODEL), SSD_INNER ** -0.5),
        "w_diff_br": nrm((L, DIFF_WIDTH, D_MODEL), DIFF_WIDTH ** -0.5),
        "w_mix_out": nrm((L, D_MODEL, D_MODEL), D_MODEL ** -0.5 * DN_BETA),
        "ln1_g": 1.0 + nrm((L, D_MODEL), 0.02),
        "ln1_b": nrm((L, D_MODEL), 0.02),
        "w_cq": nrm((L, D_MODEL, D_MODEL), D_MODEL ** -0.5),
        "w_ckv": nrm((L, D_MODEL, 2 * D_MODEL), D_MODEL ** -0.5),
        "w_co": nrm((L, D_MODEL, D_MODEL), D_MODEL ** -0.5 * DN_BETA),
        "ln2_g": 1.0 + nrm((L, D_MODEL), 0.02),
        "ln2_b": nrm((L, D_MODEL), 0.02),
        "w_router": nrm((L, D_MODEL, N_EXPERTS), D_MODEL ** -0.5),
        "router_bias": nrm((L, N_EXPERTS), 0.01),
        "w_exp_gu": nrm((L, N_EXPERTS, D_MODEL, 2 * EXPERT_FF), D_MODEL ** -0.5),
        "w_exp_down": nrm((L, N_EXPERTS, EXPERT_FF, D_MODEL), EXPERT_FF ** -0.5 * DN_BETA),
        "w_sh_gu": nrm((L, D_MODEL, 2 * SHARED_FF), D_MODEL ** -0.5),
        "w_sh_down": nrm((L, SHARED_FF, D_MODEL), SHARED_FF ** -0.5 * DN_BETA),
        "ln3_g": 1.0 + nrm((L, D_MODEL), 0.02),
        "ln3_b": nrm((L, D_MODEL), 0.02),
    }


def reference(x, mem, w_in, conv_w, conv_b, dt_bias, a_log, d_skip, ssd_norm_g,
              lambda_q1, lambda_k1, lambda_q2, lambda_k2, subln_g, rel_bias,
              w_ssd_br, w_diff_br, w_mix_out, ln1_g, ln1_b,
              w_cq, w_ckv, w_co, ln2_g, ln2_b,
              w_router, router_bias, w_exp_gu, w_exp_down, w_sh_gu, w_sh_down,
              ln3_g, ln3_b):
    o_z = SSD_INNER
    o_xbc = o_z + XBC_WIDTH
    o_dt = o_xbc + SSD_HEADS
    o_q = o_dt + DIFF_WIDTH
    o_k = o_q + DIFF_WIDTH
    o_v = o_k + DIFF_WIDTH
    for l in range(DEPTH):
        proj = jnp.einsum('bsd,de->bse', x, w_in[l])
        z, xbc, dt_raw = proj[..., :o_z], proj[..., o_z:o_xbc], proj[..., o_xbc:o_dt]
        q, k, v = proj[..., o_dt:o_q], proj[..., o_q:o_k], proj[..., o_k:o_v]
        gates = jax.nn.sigmoid(proj[..., o_v:])
        ssd_out = ssd_branch(z, xbc, dt_raw, conv_w[l], conv_b[l], dt_bias[l], a_log[l],
                             d_skip[l], ssd_norm_g[l], w_ssd_br[l])
        diff_out = diff_branch(q, k, v, lambda_q1[l], lambda_k1[l], lambda_q2[l], lambda_k2[l],
                               subln_g[l], rel_bias, w_diff_br[l], l)
        mixed = (gates[..., :D_MODEL] * ssd_out + gates[..., D_MODEL:] * diff_out) @ w_mix_out[l]
        x = layer_norm(DN_ALPHA * x + mixed, ln1_g[l], ln1_b[l])
        x = layer_norm(DN_ALPHA * x + memory_cross_attention(x, mem, w_cq[l], w_ckv[l], w_co[l]),
                       ln2_g[l], ln2_b[l])
        x = layer_norm(DN_ALPHA * x + moe_ffn(x, w_router[l], router_bias[l], w_exp_gu[l],
                                              w_exp_down[l], w_sh_gu[l], w_sh_down[l]),
                       ln3_g[l], ln3_b[l])
    return x
```

```python
import functools
import math

import numpy as np
import jax
import jax.numpy as jnp
from jax import lax
from jax.experimental import pallas as pl
from jax.experimental.pallas import tpu as pltpu

F32 = jnp.float32
BF16 = jnp.bfloat16

SSD_HEAD_DIM = 64
SSD_GROUPS = 4
SSD_STATE = 128
SSD_CONV = 4
SSD_CHUNK = 128
DIFF_HEADS = 8
REL_BUCKETS = 32
REL_MAX_DIST = 128
MEM_HEADS = 4
TOP_K = 8
N_EXPERT_GROUPS = 8
TOP_GROUPS = 4
ROUTED_SCALE = 2.5
NORM_EPS = 1e-5
DEPTH = 1
DN_ALPHA = (2.0 * DEPTH) ** 0.25

LANES = 128
SUBLANES = 8
VMEM_CAP_BYTES = 64 * 1024 * 1024
MASK_VALUE = -1e30


def _vmem_limit(estimate_bytes):
    return int(min(estimate_bytes * 5 // 4 + (4 << 20), VMEM_CAP_BYTES - (6 << 20)))


def _params(semantics, vmem_estimate):
    return pltpu.CompilerParams(dimension_semantics=semantics,
                                vmem_limit_bytes=_vmem_limit(vmem_estimate))


def _resident(shape):
    nd = len(shape)
    return pl.BlockSpec(shape, lambda *_: (0,) * nd, pipeline_mode=pl.Buffered(1))


def _layer_norm(v, g, b):
    mu = jnp.mean(v, axis=-1, keepdims=True)
    d = v - mu
    var = jnp.mean(d * d, axis=-1, keepdims=True)
    return d * lax.rsqrt(var + NORM_EPS) * g + b


def _sigmoid(v):
    return 1.0 / (1.0 + jnp.exp(-v))


def _silu(v):
    return v * _sigmoid(v)


def _matmul_kernel(x_ref, w_ref, o_ref, xb_ref, *, act):
    @pl.when(pl.program_id(1) == 0)
    def _():
        xb_ref[...] = x_ref[...].astype(BF16)

    acc = jnp.dot(xb_ref[...], w_ref[...], preferred_element_type=F32)
    if act == "sigmoid":
        acc = _sigmoid(acc)
    o_ref[...] = acc.astype(o_ref.dtype)


def _matmul(x, w, out_dtype, *, tm, tn, act=None, name):
    m, k = x.shape
    n = w.shape[1]
    tm, tn = min(tm, m), min(tn, n)
    est = (2 * tm * k * x.dtype.itemsize + tm * k * 2 + 2 * k * tn * 2
           + 2 * tm * tn * jnp.dtype(out_dtype).itemsize + tm * tn * 4)
    return pl.pallas_call(
        functools.partial(_matmul_kernel, act=act),
        out_shape=jax.ShapeDtypeStruct((m, n), out_dtype),
        grid=(m // tm, n // tn),
        in_specs=[pl.BlockSpec((tm, k), lambda i, j: (i, 0)),
                  pl.BlockSpec((k, tn), lambda i, j: (0, j))],
        out_specs=pl.BlockSpec((tm, tn), lambda i, j: (i, j)),
        scratch_shapes=[pltpu.VMEM((tm, k), BF16)],
        compiler_params=_params(("parallel", "arbitrary"), est),
        name=name,
    )(x, w)


def _split3(v):
    hi = v.astype(BF16)
    r1 = v - hi.astype(F32)
    mid = r1.astype(BF16)
    lo = (r1 - mid.astype(F32)).astype(BF16)
    return hi, mid, lo


def _ssd_kernel(z_ref, xbc_ref, dt_ref, convw_ref, convb_ref, dtb_ref, alog_ref,
                dskip_ref, g_ref, y_ref, tail_ref, state_ref, xa_ref, yacc_ref,
                *, inner, n_groups, d_state):
    L = SSD_CHUNK
    width = xbc_ref.shape[1]
    n_pairs = inner // LANES
    pairs_per_group = n_pairs // n_groups
    group_width = inner // n_groups

    @pl.when(pl.program_id(1) == 0)
    def _():
        tail_ref[...] = jnp.zeros_like(tail_ref)
        state_ref[...] = jnp.zeros_like(state_ref)

    slab = 512
    for c0 in range(0, width, slab):
        cs = slice(c0, c0 + slab)
        u = xbc_ref[:, cs].astype(F32)
        cat = jnp.concatenate([tail_ref[:, cs], u], axis=0)
        acc = convb_ref[:, cs] + cat[SUBLANES:SUBLANES + L] * convw_ref[SSD_CONV - 1:SSD_CONV, cs]
        for kk in range(SSD_CONV - 1):
            off = SUBLANES - (SSD_CONV - 1) + kk
            acc = acc + cat[off:off + L] * convw_ref[kk:kk + 1, cs]
        tail_ref[:, cs] = u[L - SUBLANES:L]
        xa_ref[:, cs] = _silu(acc)

    dtr = dt_ref[...] + dtb_ref[...]
    dt = jnp.maximum(dtr, 0.0) + jnp.log1p(jnp.exp(-jnp.abs(dtr)))
    a = -jnp.exp(alog_ref[...]) * dt
    row_i = lax.broadcasted_iota(jnp.int32, (L, L), 0)
    col_i = lax.broadcasted_iota(jnp.int32, (L, L), 1)
    causal = row_i >= col_i
    tril = jnp.where(causal, 1.0, 0.0).astype(BF16)
    acs = sum(jnp.dot(tril, part, preferred_element_type=F32) for part in _split3(a))
    acs_t = acs.T
    dt_t = dt.T
    lane_lo = lax.broadcasted_iota(jnp.int32, (L, LANES), 1) < SSD_HEAD_DIM

    b0 = inner
    c0 = inner + n_groups * d_state
    for g in range(n_groups):
        bg = xa_ref[:, b0 + g * d_state:b0 + (g + 1) * d_state]
        cg = xa_ref[:, c0 + g * d_state:c0 + (g + 1) * d_state]
        cb = lax.dot_general(cg.astype(BF16), bg.astype(BF16), (((1,), (1,)), ((), ())),
                             preferred_element_type=F32)
        bg_t = bg.T
        for pp in range(g * pairs_per_group, (g + 1) * pairs_per_group):
            xs_pair = xa_ref[:, pp * LANES:(pp + 1) * LANES].astype(BF16)
            st_old = state_ref[pp]
            rhs = jnp.concatenate([xs_pair, st_old.astype(BF16)], axis=0)
            ys, sts = [], []
            for side in range(2):
                h = 2 * pp + side
                col = jnp.broadcast_to(acs[:, h:h + 1], (L, L))
                row = acs_t[h:h + 1, :]
                dt_row = dt_t[h:h + 1, :]
                last = acs[L - 1:L, h:h + 1]
                dec = jnp.exp(jnp.where(causal, col - row, MASK_VALUE))
                m_in = (cb * dec * dt_row).astype(BF16)
                c_w = (cg * jnp.exp(col[:, :d_state])).astype(BF16)
                lhs = jnp.concatenate([m_in, c_w], axis=1)
                ys.append(jnp.dot(lhs, rhs, preferred_element_type=F32))
                b_w = (bg_t * (jnp.exp(last - row) * dt_row)).astype(BF16)
                st_new = jnp.dot(b_w, xs_pair, preferred_element_type=F32)
                sts.append(st_old * jnp.exp(last) + st_new)
            yacc_ref[:, pp * LANES:(pp + 1) * LANES] = jnp.where(lane_lo, ys[0], ys[1])
            state_ref[pp] = jnp.where(lane_lo, sts[0], sts[1])

    for g in range(n_groups):
        cs = slice(g * group_width, (g + 1) * group_width)
        zz = z_ref[:, cs].astype(F32)
        yv = (yacc_ref[:, cs] + dskip_ref[:, cs] * xa_ref[:, cs]) * _silu(zz)
        ms = jnp.mean(yv * yv, axis=-1, keepdims=True)
        y_ref[:, cs] = (yv * lax.rsqrt(ms + NORM_EPS) * g_ref[:, cs]).astype(y_ref.dtype)


def _ssd_branch(z, xbc, dt_raw, conv_w, conv_b, dt_bias, a_log, d_skip, norm_g, *, batch, seq):
    t, inner = z.shape
    width = xbc.shape[1]
    heads = inner // SSD_HEAD_DIM
    n_chunks = seq // SSD_CHUNK
    L = SSD_CHUNK

    def pad_heads(v):
        return jnp.pad(v.astype(F32), (0, LANES - heads)).reshape(1, LANES)

    convw = jnp.pad(conv_w.astype(F32), ((0, SUBLANES - SSD_CONV), (0, 0)))
    dskip = jnp.repeat(d_skip.astype(F32), SSD_HEAD_DIM).reshape(1, inner)
    row = lambda b, c: (b * n_chunks + c, 0)
    const = lambda b, c: (0, 0)
    est = (2 * L * (inner + width) * 2 + 2 * L * LANES * 4 + 2 * L * inner * 2
           + (inner // LANES) * SSD_STATE * LANES * 4 + L * (width + inner) * 4 + (8 << 20))
    kern = functools.partial(_ssd_kernel, inner=inner, n_groups=SSD_GROUPS, d_state=SSD_STATE)
    return pl.pallas_call(
        kern,
        out_shape=jax.ShapeDtypeStruct((t, inner), BF16),
        grid=(batch, n_chunks),
        in_specs=[pl.BlockSpec((L, inner), row),
                  pl.BlockSpec((L, width), row),
                  pl.BlockSpec((L, LANES), row),
                  pl.BlockSpec((SUBLANES, width), const),
                  pl.BlockSpec((1, width), const),
                  pl.BlockSpec((1, LANES), const),
                  pl.BlockSpec((1, LANES), const),
                  pl.BlockSpec((1, inner), const),
                  pl.BlockSpec((1, inner), const)],
        out_specs=pl.BlockSpec((L, inner), row),
        scratch_shapes=[pltpu.VMEM((SUBLANES, width), F32),
                        pltpu.VMEM((inner // LANES, SSD_STATE, LANES), F32),
                        pltpu.VMEM((L, width), F32),
                        pltpu.VMEM((L, inner), F32)],
        compiler_params=_params(("parallel", "arbitrary"), est),
        name="ssd_scan",
    )(z, xbc, dt_raw, convw, conv_b.astype(F32).reshape(1, width), pad_heads(dt_bias),
      pad_heads(a_log), dskip, norm_g.astype(F32).reshape(1, inner))


def _bucket_tiles(tq):
    max_exact = REL_BUCKETS // 2
    qi = np.arange(tq)[:, None]
    ki = np.arange(tq)[None, :]

    def bucket(dist):
        d = np.maximum(dist, 1).astype(np.float32)
        large = max_exact + (np.log(d / np.float32(max_exact)) / np.float32(math.log(REL_MAX_DIST / max_exact))
                             * np.float32(REL_BUCKETS - max_exact)).astype(np.int32)
        large = np.minimum(large, REL_BUCKETS - 1)
        return np.where(dist < max_exact, dist, large).astype(np.int32)

    diag = np.where(qi >= ki, bucket(np.maximum(qi - ki, 0)), -1)
    prev = bucket(tq + qi - ki)
    far = bucket(np.arange(tq + 1, 1 << 16))
    assert (far == REL_BUCKETS - 1).all()
    return np.stack([diag, prev]).astype(np.int32)


def _bias_kernel(rb_ref, bucket_ref, o_ref):
    h = pl.program_id(0)
    bk = bucket_ref[...]
    acc = jnp.zeros(bk.shape, F32)
    for b in range(REL_BUCKETS):
        acc = jnp.where(bk == b, rb_ref[b, h], acc)
    o_ref[0] = jnp.where(bk < 0, MASK_VALUE, acc - rb_ref[REL_BUCKETS - 1, h])


def _bias_tiles(rel_bias, tq):
    buckets = jnp.asarray(_bucket_tiles(tq))
    return pl.pallas_call(
        _bias_kernel,
        out_shape=jax.ShapeDtypeStruct((DIFF_HEADS, 2, tq, tq), F32),
        grid=(DIFF_HEADS,),
        in_specs=[pl.BlockSpec(memory_space=pltpu.SMEM),
                  pl.BlockSpec((2, tq, tq), lambda h: (0, 0, 0))],
        out_specs=pl.BlockSpec((1, 2, tq, tq), lambda h: (h, 0, 0, 0)),
        compiler_params=_params(("arbitrary",), 8 * tq * tq * 4),
        name="t5_bias_tiles",
    )(rel_bias.astype(F32), buckets)


def _attn_kernel(lam_ref, q_ref, k_ref, v_ref, bias_ref, g_ref, o_ref,
                 q2_ref, m_ref, l_ref, acc_ref, *, tq, lam_init):
    i = pl.program_id(2)
    dh = LANES // 2
    lane = lax.broadcasted_iota(jnp.int32, (tq, LANES), 1)
    qs = q_ref[...] * (dh ** -0.5)
    zero = jnp.zeros_like(qs)
    q2_ref[0:tq] = jnp.where(lane < dh, qs, zero)
    q2_ref[tq:2 * tq] = jnp.where(lane >= dh, qs, zero)
    m_ref[...] = jnp.full(m_ref.shape, MASK_VALUE, F32)
    l_ref[...] = jnp.zeros_like(l_ref)
    acc_ref[...] = jnp.zeros_like(acc_ref)

    def step(j, bias):
        start = pl.multiple_of(j * tq, tq)
        kb = k_ref[pl.ds(start, tq), :]
        vb = v_ref[pl.ds(start, tq), :]
        s = lax.dot_general(q2_ref[...], kb, (((1,), (1,)), ((), ())),
                            preferred_element_type=F32)
        if bias is not None:
            s = s + jnp.concatenate([bias, bias], axis=0)
        m_old = m_ref[...]
        m_new = jnp.maximum(m_old, jnp.max(s, axis=1, keepdims=True))
        alpha = jnp.exp(m_old - m_new)
        p = jnp.exp(s - m_new)
        l_ref[...] = alpha * l_ref[...] + jnp.sum(p, axis=1, keepdims=True)
        acc_ref[...] = alpha * acc_ref[...] + jnp.dot(p.astype(BF16), vb, preferred_element_type=F32)
        m_ref[...] = m_new

    def far_step(j, carry):
        step(j, None)
        return carry

    lax.fori_loop(0, jnp.maximum(i - 1, 0), far_step, 0)

    @pl.when(i >= 1)
    def _():
        step(i - 1, bias_ref[0, 1])

    step(i, bias_ref[0, 0])

    lv = lam_ref[...]
    s1 = jnp.sum(lv[0:1] * lv[1:2], axis=1, keepdims=True)
    s2 = jnp.sum(lv[2:3] * lv[3:4], axis=1, keepdims=True)
    lam = jnp.exp(s1) - jnp.exp(s2) + lam_init
    acc = acc_ref[...]
    lsum = l_ref[...]
    o = acc[0:tq] / lsum[0:tq] - lam * (acc[tq:2 * tq] / lsum[tq:2 * tq])
    ms = jnp.mean(o * o, axis=-1, keepdims=True)
    o_ref[...] = (o * lax.rsqrt(ms + NORM_EPS) * g_ref[...] * (1.0 - lam_init)).astype(o_ref.dtype)


def _diff_attention(qkv, lam_vecs, bias_tiles, subln_g, *, batch, seq, tq, layer_idx):
    t = qkv.shape[0]
    nq = seq // tq
    lam_init = 0.8 - 0.6 * math.exp(-0.3 * layer_idx)
    kern = functools.partial(_attn_kernel, tq=tq, lam_init=lam_init)
    est = (2 * tq * LANES * 2 + 4 * seq * LANES * 2 + 4 * tq * tq * 4 + 2 * tq * LANES * 2
           + 2 * tq * LANES * 2 + 2 * 2 * tq * LANES * 4 + 2 * tq * LANES * 4 + 6 * 2 * tq * tq * 4)
    return pl.pallas_call(
        kern,
        out_shape=jax.ShapeDtypeStruct((t, DIFF_HEADS * LANES), BF16),
        grid=(batch, DIFF_HEADS, nq),
        in_specs=[pl.BlockSpec((SUBLANES, LANES), lambda b, h, i: (0, 0)),
                  pl.BlockSpec((tq, LANES), lambda b, h, i: (b * nq + i, h)),
                  pl.BlockSpec((seq, LANES), lambda b, h, i: (b, DIFF_HEADS + h)),
                  pl.BlockSpec((seq, LANES), lambda b, h, i: (b, 2 * DIFF_HEADS + h)),
                  pl.BlockSpec((1, 2, tq, tq), lambda b, h, i: (h, 0, 0, 0)),
                  pl.BlockSpec((1, LANES), lambda b, h, i: (0, 0))],
        out_specs=pl.BlockSpec((tq, LANES), lambda b, h, i: (b * nq + i, h)),
        scratch_shapes=[pltpu.VMEM((2 * tq, LANES), BF16),
                        pltpu.VMEM((2 * tq, 1), F32),
                        pltpu.VMEM((2 * tq, 1), F32),
                        pltpu.VMEM((2 * tq, LANES), F32)],
        compiler_params=_params(("parallel", "parallel", "arbitrary"), est),
        name="diff_attention",
    )(lam_vecs, qkv, qkv, qkv, bias_tiles, subln_g)


def _mix_kernel(x_ref, y_ref, o_ref, gate_ref, wssd_ref, wdiff_ref, wmix_ref, g_ref, b_ref, out_ref):
    d = x_ref.shape[1]
    ssd = jnp.dot(y_ref[...], wssd_ref[...], preferred_element_type=F32)
    dif = jnp.dot(o_ref[...], wdiff_ref[...], preferred_element_type=F32)
    merged = gate_ref[:, 0:d].astype(F32) * ssd + gate_ref[:, d:2 * d].astype(F32) * dif
    mixed = jnp.dot(merged.astype(BF16), wmix_ref[...], preferred_element_type=F32)
    out_ref[...] = _layer_norm(DN_ALPHA * x_ref[...] + mixed, g_ref[...], b_ref[...])


def _mix(x, y_ssd, o_diff, gates, w_ssd, w_diff, w_mix, ln_g, ln_b, *, tm):
    t, d = x.shape
    tm = min(tm, t)
    inner = y_ssd.shape[1]
    dw = o_diff.shape[1]
    row = lambda i: (i, 0)
    est = (2 * tm * (d * 4 + inner * 2 + dw * 2 + 2 * d * 2 + d * 4)
           + (inner * d + dw * d + d * d) * 2 + 6 * tm * d * 4)
    return pl.pallas_call(
        _mix_kernel,
        out_shape=jax.ShapeDtypeStruct((t, d), F32),
        grid=(t // tm,),
        in_specs=[pl.BlockSpec((tm, d), row),
                  pl.BlockSpec((tm, inner), row),
                  pl.BlockSpec((tm, dw), row),
                  pl.BlockSpec((tm, 2 * d), row),
                  _resident((inner, d)), _resident((dw, d)), _resident((d, d)),
                  _resident((1, d)), _resident((1, d))],
        out_specs=pl.BlockSpec((tm, d), row),
        compiler_params=_params(("parallel",), est),
        name="mix_ln1",
    )(x, y_ssd, o_diff, gates, w_ssd, w_diff, w_mix, ln_g, ln_b)


def _xattn_kernel(x_ref, kv_ref, wq_ref, wo_ref, g_ref, b_ref, out_ref, o_scr, *, heads):
    d = x_ref.shape[1]
    dh = d // heads
    xv = x_ref[...]
    q = jnp.dot(xv.astype(BF16), wq_ref[...], preferred_element_type=F32)
    q = (q * (dh ** -0.5)).astype(BF16)
    for h in range(heads):
        kh = kv_ref[:, h * dh:(h + 1) * dh]
        vh = kv_ref[:, d + h * dh:d + (h + 1) * dh]
        s = lax.dot_general(q[:, h * dh:(h + 1) * dh], kh, (((1,), (1,)), ((), ())),
                            preferred_element_type=F32)
        p = jnp.exp(s - jnp.max(s, axis=1, keepdims=True))
        p = p / jnp.sum(p, axis=1, keepdims=True)
        o_scr[:, h * dh:(h + 1) * dh] = jnp.dot(p.astype(BF16), vh, preferred_element_type=F32).astype(BF16)
    att = jnp.dot(o_scr[...], wo_ref[...], preferred_element_type=F32)
    out_ref[...] = _layer_norm(DN_ALPHA * xv + att, g_ref[...], b_ref[...])


def _cross_attention(x, kv, w_cq, w_co, ln_g, ln_b, *, batch, seq, tq):
    t, d = x.shape
    mem_len = kv.shape[0] // batch
    tq = min(tq, seq)
    nq = seq // tq
    est = (2 * tq * d * 4 * 2 + 2 * mem_len * 2 * d * 2 + 2 * d * d * 2 + tq * d * 2 + 8 * tq * d * 4)
    return pl.pallas_call(
        functools.partial(_xattn_kernel, heads=MEM_HEADS),
        out_shape=jax.ShapeDtypeStruct((t, d), F32),
        grid=(batch, nq),
        in_specs=[pl.BlockSpec((tq, d), lambda b, i: (b * nq + i, 0)),
                  pl.BlockSpec((mem_len, 2 * d), lambda b, i: (b, 0)),
                  _resident((d, d)), _resident((d, d)), _resident((1, d)), _resident((1, d))],
        out_specs=pl.BlockSpec((tq, d), lambda b, i: (b * nq + i, 0)),
        scratch_shapes=[pltpu.VMEM((tq, d), BF16)],
        compiler_params=_params(("parallel", "parallel"), est),
        name="cross_attention_ln2",
    )(x, kv, w_cq, w_co, ln_g, ln_b)


def _router_kernel(x_ref, wr_ref, rb_ref, gates_ref, *, n_experts):
    tm = x_ref.shape[0]
    n_groups = N_EXPERT_GROUPS
    per = n_experts // n_groups
    logits = lax.dot_general(wr_ref[...], x_ref[...], (((1,), (1,)), ((), ())),
                             precision=lax.Precision.HIGHEST,
                             preferred_element_type=F32)
    sc = _sigmoid(logits[0:n_experts]).reshape(n_groups, per, tm)
    choice = sc + rb_ref[0:n_experts].reshape(n_groups, per, 1)
    neg = -jnp.inf
    member = lax.broadcasted_iota(jnp.int32, (n_groups, per, tm), 1)
    group3 = lax.broadcasted_iota(jnp.int32, (n_groups, per, tm), 0)
    m1 = jnp.max(choice, axis=1, keepdims=True)
    i1 = jnp.min(jnp.where(choice == m1, member, per), axis=1, keepdims=True)
    m2 = jnp.max(jnp.where(member == i1, neg, choice), axis=1, keepdims=True)
    gscore = (m1 + m2)
    gsel = jnp.zeros((n_groups, 1, tm), F32)
    gidx = lax.broadcasted_iota(jnp.int32, (n_groups, 1, tm), 0)
    cur = gscore
    for _ in range(TOP_GROUPS):
        mx = jnp.max(cur, axis=0, keepdims=True)
        ix = jnp.min(jnp.where(cur == mx, gidx, n_groups), axis=0, keepdims=True)
        hit = gidx == ix
        gsel = jnp.where(hit, 1.0, gsel)
        cur = jnp.where(hit, neg, cur)
    cur = jnp.where(gsel > 0.0, choice, neg)
    eidx = group3 * per + member
    esel = jnp.zeros((n_groups, per, tm), F32)
    for _ in range(TOP_K):
        mx = jnp.max(jnp.max(cur, axis=1, keepdims=True), axis=0, keepdims=True)
        ix = jnp.min(jnp.min(jnp.where(cur == mx, eidx, n_experts), axis=1, keepdims=True),
                     axis=0, keepdims=True)
        hit = eidx == ix
        esel = jnp.where(hit, 1.0, esel)
        cur = jnp.where(hit, neg, cur)
    w = esel * sc
    tot = jnp.sum(jnp.sum(w, axis=1, keepdims=True), axis=0, keepdims=True)
    gates_t = (w / tot * ROUTED_SCALE).reshape(n_experts, tm)
    gates_t = jnp.concatenate([gates_t, jnp.zeros((LANES - n_experts, tm), F32)], axis=0)
    gates_ref[...] = gates_t.T


def _router(x, w_router, router_bias, *, tm):
    t, d = x.shape
    n_experts = w_router.shape[1]
    tm = min(tm, t)
    wr = jnp.pad(w_router.astype(F32).T, ((0, LANES - n_experts), (0, 0)))
    rb = jnp.pad(router_bias.astype(F32), (0, LANES - n_experts)).reshape(LANES, 1)
    est = 2 * tm * d * 4 + LANES * d * 4 + 2 * tm * LANES * 4 + 40 * n_experts * tm * 4
    return pl.pallas_call(
        functools.partial(_router_kernel, n_experts=n_experts),
        out_shape=jax.ShapeDtypeStruct((t, LANES), F32),
        grid=(t // tm,),
        in_specs=[pl.BlockSpec((tm, d), lambda i: (i, 0)),
                  _resident((LANES, d)), _resident((LANES, 1))],
        out_specs=pl.BlockSpec((tm, LANES), lambda i: (i, 0)),
        compiler_params=_params(("parallel",), est),
        name="router",
    )(x, wr, rb)


def _moe_kernel(x_ref, gates_ref, wgu_ref, wdn_ref, wsgu_ref, wsdn_ref, g_ref, b_ref,
                out_ref, xb_ref, acc_ref):
    e = pl.program_id(1)
    ff = wdn_ref.shape[1]

    def ffn(wgu, wdn, scale):
        gu = jnp.dot(xb_ref[...], wgu, preferred_element_type=F32)
        hid = _silu(gu[:, 0:ff]) * gu[:, ff:2 * ff]
        if scale is not None:
            hid = hid * scale
        return jnp.dot(hid.astype(BF16), wdn, preferred_element_type=F32)

    @pl.when(e == 0)
    def _():
        xb_ref[...] = x_ref[...].astype(BF16)
        acc_ref[...] = ffn(wsgu_ref[...], wsdn_ref[...], None)

    lane = lax.broadcasted_iota(jnp.int32, gates_ref.shape, 1)
    gcol = jnp.sum(jnp.where(lane == e, gates_ref[...], 0.0), axis=1, keepdims=True)
    acc_ref[...] += ffn(wgu_ref[0], wdn_ref[0], gcol)

    @pl.when(e == pl.num_programs(1) - 1)
    def _():
        out_ref[...] = _layer_norm(DN_ALPHA * x_ref[...] + acc_ref[...], g_ref[...], b_ref[...])


def _moe(x, gates, w_gu, w_dn, w_sgu, w_sdn, ln_g, ln_b, *, tm):
    t, d = x.shape
    n_experts, _, ff2 = w_gu.shape
    ff = ff2 // 2
    sff2 = w_sgu.shape[1]
    tm = min(tm, t)
    est = (2 * tm * d * 4 * 2 + 2 * tm * LANES * 4 + 2 * (d * ff2 + ff * d) * 2
           + (d * sff2 + (sff2 // 2) * d) * 2 + tm * d * 2 + tm * d * 4 + 4 * tm * ff2 * 4 + 2 * tm * d * 4)
    return pl.pallas_call(
        _moe_kernel,
        out_shape=jax.ShapeDtypeStruct((t, d), F32),
        grid=(t // tm, n_experts),
        in_specs=[pl.BlockSpec((tm, d), lambda i, e: (i, 0)),
                  pl.BlockSpec((tm, LANES), lambda i, e: (i, 0)),
                  pl.BlockSpec((1, d, ff2), lambda i, e: (e, 0, 0)),
                  pl.BlockSpec((1, ff, d), lambda i, e: (e, 0, 0)),
                  _resident((d, sff2)), _resident((sff2 // 2, d)),
                  _resident((1, d)), _resident((1, d))],
        out_specs=pl.BlockSpec((tm, d), lambda i, e: (i, 0)),
        scratch_shapes=[pltpu.VMEM((tm, d), BF16), pltpu.VMEM((tm, d), F32)],
        compiler_params=_params(("parallel", "arbitrary"), est),
        name="moe_ln3",
    )(x, gates, w_gu, w_dn, w_sgu, w_sdn, ln_g, ln_b)


def kernel(x, mem, w_in, conv_w, conv_b, dt_bias, a_log, d_skip, ssd_norm_g, lambda_q1, lambda_k1, lambda_q2, lambda_k2, subln_g, rel_bias, w_ssd_br, w_diff_br, w_mix_out, ln1_g, ln1_b, w_cq, w_ckv, w_co, ln2_g, ln2_b, w_router, router_bias, w_exp_gu, w_exp_down, w_sh_gu, w_sh_down, ln3_g, ln3_b):
    batch, seq, d = x.shape
    depth = w_in.shape[0]
    inner = w_ssd_br.shape[1]
    xbc_width = conv_w.shape[2]
    heads = dt_bias.shape[1]
    diff_width = w_diff_br.shape[1]
    o_z, o_xbc = inner, inner + xbc_width
    o_dt = o_xbc + heads
    o_v = o_dt + 3 * diff_width
    t = batch * seq
    tq_attn = min(256, seq)

    def vec(v):
        return v.astype(F32).reshape(1, -1)

    xt = x.reshape(t, d)
    memt = mem.reshape(-1, d)
    bias_tiles = _bias_tiles(rel_bias, tq_attn)

    for l in range(depth):
        w = w_in[l].astype(BF16)
        w_dt = jnp.pad(w_in[l][:, o_xbc:o_dt], ((0, 0), (0, LANES - heads))).astype(BF16)
        z = _matmul(xt, w[:, :o_z], BF16, tm=1024, tn=1024, name="in_proj_z")
        xbc = _matmul(xt, w[:, o_z:o_xbc], BF16, tm=1024, tn=1024, name="in_proj_xbc")
        dt_raw = _matmul(xt, w_dt, F32, tm=1024, tn=LANES, name="in_proj_dt")
        qkv = _matmul(xt, w[:, o_dt:o_v], BF16, tm=1024, tn=1024, name="in_proj_qkv")
        gates = _matmul(xt, w[:, o_v:], BF16, tm=1024, tn=1024, act="sigmoid", name="in_proj_gates")

        y_ssd = _ssd_branch(z, xbc, dt_raw, conv_w[l], conv_b[l], dt_bias[l], a_log[l], d_skip[l],
                            ssd_norm_g[l], batch=batch, seq=seq)

        lam_vecs = jnp.pad(jnp.stack([lambda_q1[l], lambda_k1[l], lambda_q2[l], lambda_k2[l]]).astype(F32),
                           ((0, SUBLANES - 4), (0, LANES - lambda_q1.shape[1])))
        o_diff = _diff_attention(qkv, lam_vecs, bias_tiles, vec(subln_g[l]), batch=batch, seq=seq,
                                 tq=tq_attn, layer_idx=l)

        xt = _mix(xt, y_ssd, o_diff, gates, w_ssd_br[l].astype(BF16), w_diff_br[l].astype(BF16),
                  w_mix_out[l].astype(BF16), vec(ln1_g[l]), vec(ln1_b[l]), tm=512)

        kv = _matmul(memt, w_ckv[l].astype(BF16), BF16, tm=1024, tn=1024, name="mem_kv_proj")
        xt = _cross_attention(xt, kv, w_cq[l].astype(BF16), w_co[l].astype(BF16), vec(ln2_g[l]),
                              vec(ln2_b[l]), batch=batch, seq=seq, tq=512)

        route = _router(xt, w_router[l], router_bias[l], tm=1024)
        xt = _moe(xt, route, w_exp_gu[l].astype(BF16), w_exp_down[l].astype(BF16),
                  w_sh_gu[l].astype(BF16), w_sh_down[l].astype(BF16), vec(ln3_g[l]), vec(ln3_b[l]), tm=1024)
    return xt.reshape(batch, seq, d)
```

```python
import functools
import math

import numpy as np
import jax
import jax.numpy as jnp
from jax import lax
from jax.experimental import pallas as pl
from jax.experimental.pallas import tpu as pltpu

F32 = jnp.float32
BF16 = jnp.bfloat16

SSD_HEAD_DIM = 64
SSD_GROUPS = 4
SSD_STATE = 128
SSD_CONV = 4
SSD_CHUNK = 128
DIFF_HEADS = 8
REL_BUCKETS = 32
REL_MAX_DIST = 128
MEM_HEADS = 4
TOP_K = 8
N_EXPERT_GROUPS = 8
TOP_GROUPS = 4
ROUTED_SCALE = 2.5
NORM_EPS = 1e-5
DEPTH = 1
DN_ALPHA = (2.0 * DEPTH) ** 0.25

LANES = 128
SUBLANES = 8
VMEM_CAP_BYTES = 64 * 1024 * 1024
MASK_VALUE = -1e30


def _vmem_limit(estimate_bytes):
    return int(min(estimate_bytes * 5 // 4 + (4 << 20), VMEM_CAP_BYTES - (6 << 20)))


def _params(semantics, vmem_estimate):
    return pltpu.CompilerParams(dimension_semantics=semantics,
                                vmem_limit_bytes=_vmem_limit(vmem_estimate))


def _resident(shape):
    nd = len(shape)
    return pl.BlockSpec(shape, lambda *_: (0,) * nd, pipeline_mode=pl.Buffered(1))


def _layer_norm(v, g, b):
    mu = jnp.mean(v, axis=-1, keepdims=True)
    d = v - mu
    var = jnp.mean(d * d, axis=-1, keepdims=True)
    return d * lax.rsqrt(var + NORM_EPS) * g + b


def _sigmoid(v):
    return 1.0 / (1.0 + jnp.exp(-v))


def _silu(v):
    return v * _sigmoid(v)


def _matmul_kernel(x_ref, w_ref, o_ref, xb_ref, *, act):
    @pl.when(pl.program_id(1) == 0)
    def _():
        xb_ref[...] = x_ref[...].astype(BF16)

    acc = jnp.dot(xb_ref[...], w_ref[...], preferred_element_type=F32)
    if act == "sigmoid":
        acc = _sigmoid(acc)
    o_ref[...] = acc.astype(o_ref.dtype)


def _matmul(x, w, out_dtype, *, tm, tn, act=None, name):
    m, k = x.shape
    n = w.shape[1]
    tm, tn = min(tm, m), min(tn, n)
    est = (2 * tm * k * x.dtype.itemsize + tm * k * 2 + 2 * k * tn * 2
           + 2 * tm * tn * jnp.dtype(out_dtype).itemsize + tm * tn * 4)
    return pl.pallas_call(
        functools.partial(_matmul_kernel, act=act),
        out_shape=jax.ShapeDtypeStruct((m, n), out_dtype),
        grid=(m // tm, n // tn),
        in_specs=[pl.BlockSpec((tm, k), lambda i, j: (i, 0)),
                  pl.BlockSpec((k, tn), lambda i, j: (0, j))],
        out_specs=pl.BlockSpec((tm, tn), lambda i, j: (i, j)),
        scratch_shapes=[pltpu.VMEM((tm, k), BF16)],
        compiler_params=_params(("parallel", "arbitrary"), est),
        name=name,
    )(x, w)


def _split3(v):
    hi = v.astype(BF16)
    r1 = v - hi.astype(F32)
    mid = r1.astype(BF16)
    lo = (r1 - mid.astype(F32)).astype(BF16)
    return hi, mid, lo


def _ssd_kernel(z_ref, xbc_ref, dt_ref, convw_ref, convb_ref, dtb_ref, alog_ref,
                dskip_ref, g_ref, y_ref, tail_ref, state_ref, xa_ref, yacc_ref,
                *, inner, n_groups, d_state):
    L = SSD_CHUNK
    width = xbc_ref.shape[1]
    n_pairs = inner // LANES
    pairs_per_group = n_pairs // n_groups
    group_width = inner // n_groups

    @pl.when(pl.program_id(1) == 0)
    def _():
        tail_ref[...] = jnp.zeros_like(tail_ref)
        state_ref[...] = jnp.zeros_like(state_ref)

    slab = 512
    for c0 in range(0, width, slab):
        cs = slice(c0, c0 + slab)
        u = xbc_ref[:, cs].astype(F32)
        cat = jnp.concatenate([tail_ref[:, cs], u], axis=0)
        acc = convb_ref[:, cs] + cat[SUBLANES:SUBLANES + L] * convw_ref[SSD_CONV - 1:SSD_CONV, cs]
        for kk in range(SSD_CONV - 1):
            off = SUBLANES - (SSD_CONV - 1) + kk
            acc = acc + cat[off:off + L] * convw_ref[kk:kk + 1, cs]
        tail_ref[:, cs] = u[L - SUBLANES:L]
        xa_ref[:, cs] = _silu(acc)

    dtr = dt_ref[...] + dtb_ref[...]
    dt = jnp.maximum(dtr, 0.0) + jnp.log1p(jnp.exp(-jnp.abs(dtr)))
    a = -jnp.exp(alog_ref[...]) * dt
    row_i = lax.broadcasted_iota(jnp.int32, (L, L), 0)
    col_i = lax.broadcasted_iota(jnp.int32, (L, L), 1)
    causal = row_i >= col_i
    tril = jnp.where(causal, 1.0, 0.0).astype(BF16)
    acs = sum(jnp.dot(tril, part, preferred_element_type=F32) for part in _split3(a))
    acs_t = acs.T
    dt_t = dt.T
    lane_lo = lax.broadcasted_iota(jnp.int32, (L, LANES), 1) < SSD_HEAD_DIM

    b0 = inner
    c0 = inner + n_groups * d_state
    for g in range(n_groups):
        bg = xa_ref[:, b0 + g * d_state:b0 + (g + 1) * d_state]
        cg = xa_ref[:, c0 + g * d_state:c0 + (g + 1) * d_state]
        cb = lax.dot_general(cg.astype(BF16), bg.astype(BF16), (((1,), (1,)), ((), ())),
                             preferred_element_type=F32)
        bg_t = bg.T
        for pp in range(g * pairs_per_group, (g + 1) * pairs_per_group):
            xs_pair = xa_ref[:, pp * LANES:(pp + 1) * LANES].astype(BF16)
            st_old = state_ref[pp]
            rhs = jnp.concatenate([xs_pair, st_old.astype(BF16)], axis=0)
            ys, sts = [], []
            for side in range(2):
                h = 2 * pp + side
                col = jnp.broadcast_to(acs[:, h:h + 1], (L, L))
                row = acs_t[h:h + 1, :]
                dt_row = dt_t[h:h + 1, :]
                last = acs[L - 1:L, h:h + 1]
                dec = jnp.exp(jnp.where(causal, col - row, MASK_VALUE))
                m_in = (cb * dec * dt_row).astype(BF16)
                c_w = (cg * jnp.exp(col[:, :d_state])).astype(BF16)
                lhs = jnp.concatenate([m_in, c_w], axis=1)
                ys.append(jnp.dot(lhs, rhs, preferred_element_type=F32))
                b_w = (bg_t * (jnp.exp(last - row) * dt_row)).astype(BF16)
                st_new = jnp.dot(b_w, xs_pair, preferred_element_type=F32)
                sts.append(st_old * jnp.exp(last) + st_new)
            yacc_ref[:, pp * LANES:(pp + 1) * LANES] = jnp.where(lane_lo, ys[0], ys[1])
            state_ref[pp] = jnp.where(lane_lo, sts[0], sts[1])

    for g in range(n_groups):
        cs = slice(g * group_width, (g + 1) * group_width)
        zz = z_ref[:, cs].astype(F32)
        yv = (yacc_ref[:, cs] + dskip_ref[:, cs] * xa_ref[:, cs]) * _silu(zz)
        ms = jnp.mean(yv * yv, axis=-1, keepdims=True)
        y_ref[:, cs] = (yv * lax.rsqrt(ms + NORM_EPS) * g_ref[:, cs]).astype(y_ref.dtype)


def _ssd_branch(z, xbc, dt_raw, conv_w, conv_b, dt_bias, a_log, d_skip, norm_g, *, batch, seq):
    t, inner = z.shape
    width = xbc.shape[1]
    heads = inner // SSD_HEAD_DIM
    n_chunks = seq // SSD_CHUNK
    L = SSD_CHUNK

    def pad_heads(v):
        return jnp.pad(v.astype(F32), (0, LANES - heads)).reshape(1, LANES)

    convw = jnp.pad(conv_w.astype(F32), ((0, SUBLANES - SSD_CONV), (0, 0)))
    dskip = jnp.repeat(d_skip.astype(F32), SSD_HEAD_DIM).reshape(1, inner)
    row = lambda b, c: (b * n_chunks + c, 0)
    const = lambda b, c: (0, 0)
    est = (2 * L * (inner + width) * 2 + 2 * L * LANES * 4 + 2 * L * inner * 2
           + (inner // LANES) * SSD_STATE * LANES * 4 + L * (width + inner) * 4 + (8 << 20))
    kern = functools.partial(_ssd_kernel, inner=inner, n_groups=SSD_GROUPS, d_state=SSD_STATE)
    return pl.pallas_call(
        kern,
        out_shape=jax.ShapeDtypeStruct((t, inner), BF16),
        grid=(batch, n_chunks),
        in_specs=[pl.BlockSpec((L, inner), row),
                  pl.BlockSpec((L, width), row),
                  pl.BlockSpec((L, LANES), row),
                  pl.BlockSpec((SUBLANES, width), const),
                  pl.BlockSpec((1, width), const),
                  pl.BlockSpec((1, LANES), const),
                  pl.BlockSpec((1, LANES), const),
                  pl.BlockSpec((1, inner), const),
                  pl.BlockSpec((1, inner), const)],
        out_specs=pl.BlockSpec((L, inner), row),
        scratch_shapes=[pltpu.VMEM((SUBLANES, width), F32),
                        pltpu.VMEM((inner // LANES, SSD_STATE, LANES), F32),
                        pltpu.VMEM((L, width), F32),
                        pltpu.VMEM((L, inner), F32)],
        compiler_params=_params(("parallel", "arbitrary"), est),
        name="ssd_scan",
    )(z, xbc, dt_raw, convw, conv_b.astype(F32).reshape(1, width), pad_heads(dt_bias),
      pad_heads(a_log), dskip, norm_g.astype(F32).reshape(1, inner))


def _bucket_tiles(tq):
    max_exact = REL_BUCKETS // 2
    qi = np.arange(tq)[:, None]
    ki = np.arange(tq)[None, :]

    def bucket(dist):
        d = np.maximum(dist, 1).astype(np.float32)
        large = max_exact + (np.log(d / np.float32(max_exact)) / np.float32(math.log(REL_MAX_DIST / max_exact))
                             * np.float32(REL_BUCKETS - max_exact)).astype(np.int32)
        large = np.minimum(large, REL_BUCKETS - 1)
        return np.where(dist < max_exact, dist, large).astype(np.int32)

    diag = np.where(qi >= ki, bucket(np.maximum(qi - ki, 0)), -1)
    prev = bucket(tq + qi - ki)
    far = bucket(np.arange(tq + 1, 1 << 16))
    assert (far == REL_BUCKETS - 1).all()
    return np.stack([diag, prev]).astype(np.int32)


def _bias_kernel(rb_ref, bucket_ref, o_ref):
    h = pl.program_id(0)
    bk = bucket_ref[...]
    acc = jnp.zeros(bk.shape, F32)
    for b in range(REL_BUCKETS):
        acc = jnp.where(bk == b, rb_ref[b, h], acc)
    o_ref[0] = jnp.where(bk < 0, MASK_VALUE, acc - rb_ref[REL_BUCKETS - 1, h])


def _bias_tiles(rel_bias, tq):
    buckets = jnp.asarray(_bucket_tiles(tq))
    return pl.pallas_call(
        _bias_kernel,
        out_shape=jax.ShapeDtypeStruct((DIFF_HEADS, 2, tq, tq), F32),
        grid=(DIFF_HEADS,),
        in_specs=[pl.BlockSpec(memory_space=pltpu.SMEM),
                  pl.BlockSpec((2, tq, tq), lambda h: (0, 0, 0))],
        out_specs=pl.BlockSpec((1, 2, tq, tq), lambda h: (h, 0, 0, 0)),
        compiler_params=_params(("arbitrary",), 8 * tq * tq * 4),
        name="t5_bias_tiles",
    )(rel_bias.astype(F32), buckets)


def _attn_kernel(lam_ref, q_ref, k_ref, v_ref, bias_ref, g_ref, o_ref,
                 q2_ref, vaug_ref, m_ref, acc_ref, *, tq, rows, lam_init):
    i = pl.program_id(2)
    dh = LANES // 2
    reps = tq // LANES

    @pl.when(i == 0)
    def _():
        vaug_ref[:, 0:LANES] = v_ref[...]
        vaug_ref[:, LANES:2 * LANES] = jnp.ones((v_ref.shape[0], LANES), BF16)

    lane = lax.broadcasted_iota(jnp.int32, (tq, LANES), 1)
    qs = q_ref[...] * (dh ** -0.5)
    zero = jnp.zeros_like(qs)
    q2_ref[0:tq] = jnp.where(lane < dh, qs, zero)
    q2_ref[tq:2 * tq] = jnp.where(lane >= dh, qs, zero)
    m_ref[...] = jnp.full(m_ref.shape, MASK_VALUE, F32)
    acc_ref[...] = jnp.zeros_like(acc_ref)

    def step(j, bias_idx):
        start = pl.multiple_of(j * tq, tq)
        kb = k_ref[pl.ds(start, tq), :]
        vb = vaug_ref[pl.ds(start, tq), :]
        for r0 in range(0, 2 * tq, rows):
            rs = slice(r0, r0 + rows)
            s = lax.dot_general(q2_ref[rs], kb, (((1,), (1,)), ((), ())),
                                preferred_element_type=F32)
            if bias_idx is not None:
                s = s + bias_ref[0, bias_idx, r0 % tq:r0 % tq + rows, :]
            m_old = m_ref[rs]
            m_new = jnp.maximum(m_old, jnp.max(s, axis=1, keepdims=True))
            alpha = jnp.exp(m_old - m_new)
            p = jnp.exp(s - jnp.concatenate([m_new] * reps, axis=1))
            pv = jnp.dot(p.astype(BF16), vb, preferred_element_type=F32)
            acc_ref[rs] = jnp.concatenate([alpha, alpha], axis=1) * acc_ref[rs] + pv
            m_ref[rs] = m_new

    def far_step(j, carry):
        step(j, None)
        return carry

    lax.fori_loop(0, jnp.maximum(i - 1, 0), far_step, 0)

    @pl.when(i >= 1)
    def _():
        step(i - 1, 1)

    step(i, 0)

    lv = lam_ref[...]
    s1 = jnp.sum(lv[0:1] * lv[1:2], axis=1, keepdims=True)
    s2 = jnp.sum(lv[2:3] * lv[3:4], axis=1, keepdims=True)
    lam = jnp.exp(s1) - jnp.exp(s2) + lam_init
    o1 = acc_ref[0:tq, 0:LANES] / acc_ref[0:tq, LANES:2 * LANES]
    o2 = acc_ref[tq:2 * tq, 0:LANES] / acc_ref[tq:2 * tq, LANES:2 * LANES]
    o = o1 - lam * o2
    ms = jnp.mean(o * o, axis=-1, keepdims=True)
    o_ref[...] = (o * lax.rsqrt(ms + NORM_EPS) * g_ref[...] * (1.0 - lam_init)).astype(o_ref.dtype)


def _diff_attention(qkv, lam_vecs, bias_tiles, subln_g, *, batch, seq, tq, layer_idx):
    t = qkv.shape[0]
    nq = seq // tq
    rows = min(256, tq)
    lam_init = 0.8 - 0.6 * math.exp(-0.3 * layer_idx)
    kern = functools.partial(_attn_kernel, tq=tq, rows=rows, lam_init=lam_init)
    est = (2 * tq * LANES * 2 + 4 * seq * LANES * 2 + 4 * tq * tq * 4 + 2 * tq * LANES * 2
           + 2 * tq * LANES * 2 + seq * 2 * LANES * 2 + 2 * tq * LANES * 4 + 2 * tq * 2 * LANES * 4
           + 8 * rows * tq * 4)
    return pl.pallas_call(
        kern,
        out_shape=jax.ShapeDtypeStruct((t, DIFF_HEADS * LANES), BF16),
        grid=(batch, DIFF_HEADS, nq),
        in_specs=[pl.BlockSpec((SUBLANES, LANES), lambda b, h, i: (0, 0)),
                  pl.BlockSpec((tq, LANES), lambda b, h, i: (b * nq + i, h)),
                  pl.BlockSpec((seq, LANES), lambda b, h, i: (b, DIFF_HEADS + h)),
                  pl.BlockSpec((seq, LANES), lambda b, h, i: (b, 2 * DIFF_HEADS + h)),
                  pl.BlockSpec((1, 2, tq, tq), lambda b, h, i: (h, 0, 0, 0)),
                  pl.BlockSpec((1, LANES), lambda b, h, i: (0, 0))],
        out_specs=pl.BlockSpec((tq, LANES), lambda b, h, i: (b * nq + i, h)),
        scratch_shapes=[pltpu.VMEM((2 * tq, LANES), BF16),
                        pltpu.VMEM((seq, 2 * LANES), BF16),
                        pltpu.VMEM((2 * tq, LANES), F32),
                        pltpu.VMEM((2 * tq, 2 * LANES), F32)],
        compiler_params=_params(("parallel", "parallel", "arbitrary"), est),
        name="diff_attention",
    )(lam_vecs, qkv, qkv, qkv, bias_tiles, subln_g)


def _mix_kernel(x_ref, y_ref, o_ref, gate_ref, wssd_ref, wdiff_ref, wmix_ref, g_ref, b_ref, out_ref):
    d = x_ref.shape[1]
    ssd = jnp.dot(y_ref[...], wssd_ref[...], preferred_element_type=F32)
    dif = jnp.dot(o_ref[...], wdiff_ref[...], preferred_element_type=F32)
    merged = gate_ref[:, 0:d].astype(F32) * ssd + gate_ref[:, d:2 * d].astype(F32) * dif
    mixed = jnp.dot(merged.astype(BF16), wmix_ref[...], preferred_element_type=F32)
    out_ref[...] = _layer_norm(DN_ALPHA * x_ref[...] + mixed, g_ref[...], b_ref[...])


def _mix(x, y_ssd, o_diff, gates, w_ssd, w_diff, w_mix, ln_g, ln_b, *, tm):
    t, d = x.shape
    tm = min(tm, t)
    inner = y_ssd.shape[1]
    dw = o_diff.shape[1]
    row = lambda i: (i, 0)
    est = (2 * tm * (d * 4 + inner * 2 + dw * 2 + 2 * d * 2 + d * 4)
           + (inner * d + dw * d + d * d) * 2 + 6 * tm * d * 4)
    return pl.pallas_call(
        _mix_kernel,
        out_shape=jax.ShapeDtypeStruct((t, d), F32),
        grid=(t // tm,),
        in_specs=[pl.BlockSpec((tm, d), row),
                  pl.BlockSpec((tm, inner), row),
                  pl.BlockSpec((tm, dw), row),
                  pl.BlockSpec((tm, 2 * d), row),
                  _resident((inner, d)), _resident((dw, d)), _resident((d, d)),
                  _resident((1, d)), _resident((1, d))],
        out_specs=pl.BlockSpec((tm, d), row),
        compiler_params=_params(("parallel",), est),
        name="mix_ln1",
    )(x, y_ssd, o_diff, gates, w_ssd, w_diff, w_mix, ln_g, ln_b)


def _xattn_kernel(x_ref, kv_ref, wq_ref, wo_ref, g_ref, b_ref, out_ref, o_scr, *, heads):
    d = x_ref.shape[1]
    dh = d // heads
    xv = x_ref[...]
    q = jnp.dot(xv.astype(BF16), wq_ref[...], preferred_element_type=F32)
    q = (q * (dh ** -0.5)).astype(BF16)
    for h in range(heads):
        kh = kv_ref[:, h * dh:(h + 1) * dh]
        vh = kv_ref[:, d + h * dh:d + (h + 1) * dh]
        s = lax.dot_general(q[:, h * dh:(h + 1) * dh], kh, (((1,), (1,)), ((), ())),
                            preferred_element_type=F32)
        p = jnp.exp(s - jnp.max(s, axis=1, keepdims=True))
        p = p / jnp.sum(p, axis=1, keepdims=True)
        o_scr[:, h * dh:(h + 1) * dh] = jnp.dot(p.astype(BF16), vh, preferred_element_type=F32).astype(BF16)
    att = jnp.dot(o_scr[...], wo_ref[...], preferred_element_type=F32)
    out_ref[...] = _layer_norm(DN_ALPHA * xv + att, g_ref[...], b_ref[...])


def _cross_attention(x, kv, w_cq, w_co, ln_g, ln_b, *, batch, seq, tq):
    t, d = x.shape
    mem_len = kv.shape[0] // batch
    tq = min(tq, seq)
    nq = seq // tq
    est = (2 * tq * d * 4 * 2 + 2 * mem_len * 2 * d * 2 + 2 * d * d * 2 + tq * d * 2 + 8 * tq * d * 4)
    return pl.pallas_call(
        functools.partial(_xattn_kernel, heads=MEM_HEADS),
        out_shape=jax.ShapeDtypeStruct((t, d), F32),
        grid=(batch, nq),
        in_specs=[pl.BlockSpec((tq, d), lambda b, i: (b * nq + i, 0)),
                  pl.BlockSpec((mem_len, 2 * d), lambda b, i: (b, 0)),
                  _resident((d, d)), _resident((d, d)), _resident((1, d)), _resident((1, d))],
        out_specs=pl.BlockSpec((tq, d), lambda b, i: (b * nq + i, 0)),
        scratch_shapes=[pltpu.VMEM((tq, d), BF16)],
        compiler_params=_params(("parallel", "parallel"), est),
        name="cross_attention_ln2",
    )(x, kv, w_cq, w_co, ln_g, ln_b)


def _router_kernel(x_ref, wr_ref, rb_ref, gates_ref, *, n_experts):
    tm = x_ref.shape[0]
    n_groups = N_EXPERT_GROUPS
    per = n_experts // n_groups
    logits = lax.dot_general(wr_ref[...], x_ref[...], (((1,), (1,)), ((), ())),
                             precision=lax.Precision.HIGHEST,
                             preferred_element_type=F32)
    sc = _sigmoid(logits[0:n_experts]).reshape(n_groups, per, tm)
    choice = sc + rb_ref[0:n_experts].reshape(n_groups, per, 1)
    neg = -jnp.inf
    member = lax.broadcasted_iota(jnp.int32, (n_groups, per, tm), 1)
    group3 = lax.broadcasted_iota(jnp.int32, (n_groups, per, tm), 0)
    m1 = jnp.max(choice, axis=1, keepdims=True)
    i1 = jnp.min(jnp.where(choice == m1, member, per), axis=1, keepdims=True)
    m2 = jnp.max(jnp.where(member == i1, neg, choice), axis=1, keepdims=True)
    gscore = (m1 + m2)
    gsel = jnp.zeros((n_groups, 1, tm), F32)
    gidx = lax.broadcasted_iota(jnp.int32, (n_groups, 1, tm), 0)
    cur = gscore
    for _ in range(TOP_GROUPS):
        mx = jnp.max(cur, axis=0, keepdims=True)
        ix = jnp.min(jnp.where(cur == mx, gidx, n_groups), axis=0, keepdims=True)
        hit = gidx == ix
        gsel = jnp.where(hit, 1.0, gsel)
        cur = jnp.where(hit, neg, cur)
    cur = jnp.where(gsel > 0.0, choice, neg)
    eidx = group3 * per + member
    esel = jnp.zeros((n_groups, per, tm), F32)
    for _ in range(TOP_K):
        mx = jnp.max(jnp.max(cur, axis=1, keepdims=True), axis=0, keepdims=True)
        ix = jnp.min(jnp.min(jnp.where(cur == mx, eidx, n_experts), axis=1, keepdims=True),
                     axis=0, keepdims=True)
        hit = eidx == ix
        esel = jnp.where(hit, 1.0, esel)
        cur = jnp.where(hit, neg, cur)
    w = esel * sc
    tot = jnp.sum(jnp.sum(w, axis=1, keepdims=True), axis=0, keepdims=True)
    gates_t = (w / tot * ROUTED_SCALE).reshape(n_experts, tm)
    gates_t = jnp.concatenate([gates_t, jnp.zeros((LANES - n_experts, tm), F32)], axis=0)
    gates_ref[...] = gates_t.T


def _router(x, w_router, router_bias, *, tm):
    t, d = x.shape
    n_experts = w_router.shape[1]
    tm = min(tm, t)
    wr = jnp.pad(w_router.astype(F32).T, ((0, LANES - n_experts), (0, 0)))
    rb = jnp.pad(router_bias.astype(F32), (0, LANES - n_experts)).reshape(LANES, 1)
    est = 2 * tm * d * 4 + LANES * d * 4 + 2 * tm * LANES * 4 + 40 * n_experts * tm * 4
    return pl.pallas_call(
        functools.partial(_router_kernel, n_experts=n_experts),
        out_shape=jax.ShapeDtypeStruct((t, LANES), F32),
        grid=(t // tm,),
        in_specs=[pl.BlockSpec((tm, d), lambda i: (i, 0)),
                  _resident((LANES, d)), _resident((LANES, 1))],
        out_specs=pl.BlockSpec((tm, LANES), lambda i: (i, 0)),
        compiler_params=_params(("parallel",), est),
        name="router",
    )(x, wr, rb)


def _moe_kernel(x_ref, gates_ref, wgu_ref, wdn_ref, wsgu_ref, wsdn_ref, g_ref, b_ref,
                out_ref, xb_ref, acc_ref):
    e = pl.program_id(1)
    ff = wdn_ref.shape[1]

    def ffn(wgu, wdn, scale):
        gu = jnp.dot(xb_ref[...], wgu, preferred_element_type=F32)
        hid = _silu(gu[:, 0:ff]) * gu[:, ff:2 * ff]
        if scale is not None:
            hid = hid * scale
        return jnp.dot(hid.astype(BF16), wdn, preferred_element_type=F32)

    @pl.when(e == 0)
    def _():
        xb_ref[...] = x_ref[...].astype(BF16)
        acc_ref[...] = ffn(wsgu_ref[...], wsdn_ref[...], None)

    lane = lax.broadcasted_iota(jnp.int32, gates_ref.shape, 1)
    gcol = jnp.sum(jnp.where(lane == e, gates_ref[...], 0.0), axis=1, keepdims=True)
    acc_ref[...] += ffn(wgu_ref[0], wdn_ref[0], gcol)

    @pl.when(e == pl.num_programs(1) - 1)
    def _():
        out_ref[...] = _layer_norm(DN_ALPHA * x_ref[...] + acc_ref[...], g_ref[...], b_ref[...])


def _moe(x, gates, w_gu, w_dn, w_sgu, w_sdn, ln_g, ln_b, *, tm):
    t, d = x.shape
    n_experts, _, ff2 = w_gu.shape
    ff = ff2 // 2
    sff2 = w_sgu.shape[1]
    tm = min(tm, t)
    est = (2 * tm * d * 4 * 2 + 2 * tm * LANES * 4 + 2 * (d * ff2 + ff * d) * 2
           + (d * sff2 + (sff2 // 2) * d) * 2 + tm * d * 2 + tm * d * 4 + 4 * tm * ff2 * 4 + 2 * tm * d * 4)
    return pl.pallas_call(
        _moe_kernel,
        out_shape=jax.ShapeDtypeStruct((t, d), F32),
        grid=(t // tm, n_experts),
        in_specs=[pl.BlockSpec((tm, d), lambda i, e: (i, 0)),
                  pl.BlockSpec((tm, LANES), lambda i, e: (i, 0)),
                  pl.BlockSpec((1, d, ff2), lambda i, e: (e, 0, 0)),
                  pl.BlockSpec((1, ff, d), lambda i, e: (e, 0, 0)),
                  _resident((d, sff2)), _resident((sff2 // 2, d)),
                  _resident((1, d)), _resident((1, d))],
        out_specs=pl.BlockSpec((tm, d), lambda i, e: (i, 0)),
        scratch_shapes=[pltpu.VMEM((tm, d), BF16), pltpu.VMEM((tm, d), F32)],
        compiler_params=_params(("parallel", "arbitrary"), est),
        name="moe_ln3",
    )(x, gates, w_gu, w_dn, w_sgu, w_sdn, ln_g, ln_b)


def kernel(x, mem, w_in, conv_w, conv_b, dt_bias, a_log, d_skip, ssd_norm_g, lambda_q1, lambda_k1, lambda_q2, lambda_k2, subln_g, rel_bias, w_ssd_br, w_diff_br, w_mix_out, ln1_g, ln1_b, w_cq, w_ckv, w_co, ln2_g, ln2_b, w_router, router_bias, w_exp_gu, w_exp_down, w_sh_gu, w_sh_down, ln3_g, ln3_b):
    batch, seq, d = x.shape
    depth = w_in.shape[0]
    inner = w_ssd_br.shape[1]
    xbc_width = conv_w.shape[2]
    heads = dt_bias.shape[1]
    diff_width = w_diff_br.shape[1]
    o_z, o_xbc = inner, inner + xbc_width
    o_dt = o_xbc + heads
    o_v = o_dt + 3 * diff_width
    t = batch * seq
    tq_attn = min(512, seq)

    def vec(v):
        return v.astype(F32).reshape(1, -1)

    xt = x.reshape(t, d)
    memt = mem.reshape(-1, d)
    bias_tiles = _bias_tiles(rel_bias, tq_attn)

    for l in range(depth):
        w = w_in[l].astype(BF16)
        w_dt = jnp.pad(w_in[l][:, o_xbc:o_dt], ((0, 0), (0, LANES - heads))).astype(BF16)
        z = _matmul(xt, w[:, :o_z], BF16, tm=1024, tn=1024, name="in_proj_z")
        xbc = _matmul(xt, w[:, o_z:o_xbc], BF16, tm=1024, tn=1024, name="in_proj_xbc")
        dt_raw = _matmul(xt, w_dt, F32, tm=1024, tn=LANES, name="in_proj_dt")
        qkv = _matmul(xt, w[:, o_dt:o_v], BF16, tm=1024, tn=1024, name="in_proj_qkv")
        gates = _matmul(xt, w[:, o_v:], BF16, tm=1024, tn=1024, act="sigmoid", name="in_proj_gates")

        y_ssd = _ssd_branch(z, xbc, dt_raw, conv_w[l], conv_b[l], dt_bias[l], a_log[l], d_skip[l],
                            ssd_norm_g[l], batch=batch, seq=seq)

        lam_vecs = jnp.pad(jnp.stack([lambda_q1[l], lambda_k1[l], lambda_q2[l], lambda_k2[l]]).astype(F32),
                           ((0, SUBLANES - 4), (0, LANES - lambda_q1.shape[1])))
        o_diff = _diff_attention(qkv, lam_vecs, bias_tiles, vec(subln_g[l]), batch=batch, seq=seq,
                                 tq=tq_attn, layer_idx=l)

        xt = _mix(xt, y_ssd, o_diff, gates, w_ssd_br[l].astype(BF16), w_diff_br[l].astype(BF16),
                  w_mix_out[l].astype(BF16), vec(ln1_g[l]), vec(ln1_b[l]), tm=512)

        kv = _matmul(memt, w_ckv[l].astype(BF16), BF16, tm=1024, tn=1024, name="mem_kv_proj")
        xt = _cross_attention(xt, kv, w_cq[l].astype(BF16), w_co[l].astype(BF16), vec(ln2_g[l]),
                              vec(ln2_b[l]), batch=batch, seq=seq, tq=512)

        route = _router(xt, w_router[l], router_bias[l], tm=1024)
        xt = _moe(xt, route, w_exp_gu[l].astype(BF16), w_exp_down[l].astype(BF16),
                  w_sh_gu[l].astype(BF16), w_sh_down[l].astype(BF16), vec(ln3_g[l]), vec(ln3_b[l]), tm=1024)
    return xt.reshape(batch, seq, d)
```

```python
import functools
import math

import numpy as np
import jax
import jax.numpy as jnp
from jax import lax
from jax.experimental import pallas as pl
from jax.experimental.pallas import tpu as pltpu
from jax.experimental.pallas import tpu_sc as plsc

F32 = jnp.float32
BF16 = jnp.bfloat16

SSD_HEAD_DIM = 64
SSD_GROUPS = 4
SSD_STATE = 128
SSD_CONV = 4
SSD_CHUNK = 128
DIFF_HEADS = 8
REL_BUCKETS = 32
REL_MAX_DIST = 128
MEM_HEADS = 4
TOP_K = 8
N_EXPERT_GROUPS = 8
TOP_GROUPS = 4
ROUTED_SCALE = 2.5
NORM_EPS = 1e-5
DEPTH = 1
DN_ALPHA = (2.0 * DEPTH) ** 0.25

LANES = 128
SUBLANES = 8
VMEM_CAP_BYTES = 64 * 1024 * 1024
MASK_VALUE = -1e30
SC_CORES = 2
SC_SUBCORES = 16
SC_CHUNK_TOKENS = 128
MOE_ROWS_PER_TILE = 512


def _vmem_limit(estimate_bytes):
    return int(min(estimate_bytes * 5 // 4 + (4 << 20), VMEM_CAP_BYTES - (6 << 20)))


def _params(semantics, vmem_estimate):
    return pltpu.CompilerParams(dimension_semantics=semantics,
                                vmem_limit_bytes=_vmem_limit(vmem_estimate))


def _resident(shape):
    nd = len(shape)
    return pl.BlockSpec(shape, lambda *_: (0,) * nd, pipeline_mode=pl.Buffered(1))


def _layer_norm(v, g, b):
    mu = jnp.mean(v, axis=-1, keepdims=True)
    d = v - mu
    var = jnp.mean(d * d, axis=-1, keepdims=True)
    return d * lax.rsqrt(var + NORM_EPS) * g + b


def _sigmoid(v):
    return 1.0 / (1.0 + jnp.exp(-v))


def _silu(v):
    return v * _sigmoid(v)


def _matmul_kernel(x_ref, w_ref, o_ref, xb_ref, *, act):
    @pl.when(pl.program_id(1) == 0)
    def _():
        xb_ref[...] = x_ref[...].astype(BF16)

    acc = jnp.dot(xb_ref[...], w_ref[...], preferred_element_type=F32)
    if act == "sigmoid":
        acc = _sigmoid(acc)
    o_ref[...] = acc.astype(o_ref.dtype)


def _matmul(x, w, out_dtype, *, tm, tn, act=None, name):
    m, k = x.shape
    n = w.shape[1]
    tm, tn = min(tm, m), min(tn, n)
    est = (2 * tm * k * x.dtype.itemsize + tm * k * 2 + 2 * k * tn * 2
           + 2 * tm * tn * jnp.dtype(out_dtype).itemsize + tm * tn * 4)
    return pl.pallas_call(
        functools.partial(_matmul_kernel, act=act),
        out_shape=jax.ShapeDtypeStruct((m, n), out_dtype),
        grid=(m // tm, n // tn),
        in_specs=[pl.BlockSpec((tm, k), lambda i, j: (i, 0)),
                  pl.BlockSpec((k, tn), lambda i, j: (0, j))],
        out_specs=pl.BlockSpec((tm, tn), lambda i, j: (i, j)),
        scratch_shapes=[pltpu.VMEM((tm, k), BF16)],
        compiler_params=_params(("parallel", "arbitrary"), est),
        name=name,
    )(x, w)


def _split3(v):
    hi = v.astype(BF16)
    r1 = v - hi.astype(F32)
    mid = r1.astype(BF16)
    lo = (r1 - mid.astype(F32)).astype(BF16)
    return hi, mid, lo


def _ssd_kernel(z_ref, xbc_ref, dt_ref, convw_ref, convb_ref, dtb_ref, alog_ref,
                dskip_ref, g_ref, y_ref, tail_ref, state_ref, xa_ref, yacc_ref,
                *, inner, n_groups, d_state):
    L = SSD_CHUNK
    width = xbc_ref.shape[1]
    n_pairs = inner // LANES
    pairs_per_group = n_pairs // n_groups
    group_width = inner // n_groups

    @pl.when(pl.program_id(1) == 0)
    def _():
        tail_ref[...] = jnp.zeros_like(tail_ref)
        state_ref[...] = jnp.zeros_like(state_ref)

    slab = 512
    for c0 in range(0, width, slab):
        cs = slice(c0, c0 + slab)
        u = xbc_ref[:, cs].astype(F32)
        cat = jnp.concatenate([tail_ref[:, cs], u], axis=0)
        acc = convb_ref[:, cs] + cat[SUBLANES:SUBLANES + L] * convw_ref[SSD_CONV - 1:SSD_CONV, cs]
        for kk in range(SSD_CONV - 1):
            off = SUBLANES - (SSD_CONV - 1) + kk
            acc = acc + cat[off:off + L] * convw_ref[kk:kk + 1, cs]
        tail_ref[:, cs] = u[L - SUBLANES:L]
        xa_ref[:, cs] = _silu(acc)

    dtr = dt_ref[...] + dtb_ref[...]
    dt = jnp.maximum(dtr, 0.0) + jnp.log1p(jnp.exp(-jnp.abs(dtr)))
    a = -jnp.exp(alog_ref[...]) * dt
    row_i = lax.broadcasted_iota(jnp.int32, (L, L), 0)
    col_i = lax.broadcasted_iota(jnp.int32, (L, L), 1)
    causal = row_i >= col_i
    tril = jnp.where(causal, 1.0, 0.0).astype(BF16)
    acs = sum(jnp.dot(tril, part, preferred_element_type=F32) for part in _split3(a))
    acs_t = acs.T
    dt_t = dt.T
    lane_lo = lax.broadcasted_iota(jnp.int32, (L, LANES), 1) < SSD_HEAD_DIM

    b0 = inner
    c0 = inner + n_groups * d_state
    for g in range(n_groups):
        bg = xa_ref[:, b0 + g * d_state:b0 + (g + 1) * d_state]
        cg = xa_ref[:, c0 + g * d_state:c0 + (g + 1) * d_state]
        cb = lax.dot_general(cg.astype(BF16), bg.astype(BF16), (((1,), (1,)), ((), ())),
                             preferred_element_type=F32)
        bg_t = bg.T
        for pp in range(g * pairs_per_group, (g + 1) * pairs_per_group):
            xs_pair = xa_ref[:, pp * LANES:(pp + 1) * LANES].astype(BF16)
            st_old = state_ref[pp]
            rhs = jnp.concatenate([xs_pair, st_old.astype(BF16)], axis=0)
            ys, sts = [], []
            for side in range(2):
                h = 2 * pp + side
                col = jnp.broadcast_to(acs[:, h:h + 1], (L, L))
                row = acs_t[h:h + 1, :]
                dt_row = dt_t[h:h + 1, :]
                last = acs[L - 1:L, h:h + 1]
                dec = jnp.exp(jnp.where(causal, col - row, MASK_VALUE))
                m_in = (cb * dec * dt_row).astype(BF16)
                c_w = (cg * jnp.exp(col[:, :d_state])).astype(BF16)
                lhs = jnp.concatenate([m_in, c_w], axis=1)
                ys.append(jnp.dot(lhs, rhs, preferred_element_type=F32))
                b_w = (bg_t * (jnp.exp(last - row) * dt_row)).astype(BF16)
                st_new = jnp.dot(b_w, xs_pair, preferred_element_type=F32)
                sts.append(st_old * jnp.exp(last) + st_new)
            yacc_ref[:, pp * LANES:(pp + 1) * LANES] = jnp.where(lane_lo, ys[0], ys[1])
            state_ref[pp] = jnp.where(lane_lo, sts[0], sts[1])

    for g in range(n_groups):
        cs = slice(g * group_width, (g + 1) * group_width)
        zz = z_ref[:, cs].astype(F32)
        yv = (yacc_ref[:, cs] + dskip_ref[:, cs] * xa_ref[:, cs]) * _silu(zz)
        ms = jnp.mean(yv * yv, axis=-1, keepdims=True)
        y_ref[:, cs] = (yv * lax.rsqrt(ms + NORM_EPS) * g_ref[:, cs]).astype(y_ref.dtype)


def _ssd_branch(z, xbc, dt_raw, conv_w, conv_b, dt_bias, a_log, d_skip, norm_g, *, batch, seq):
    t, inner = z.shape
    width = xbc.shape[1]
    heads = inner // SSD_HEAD_DIM
    n_chunks = seq // SSD_CHUNK
    L = SSD_CHUNK

    def pad_heads(v):
        return jnp.pad(v.astype(F32), (0, LANES - heads)).reshape(1, LANES)

    convw = jnp.pad(conv_w.astype(F32), ((0, SUBLANES - SSD_CONV), (0, 0)))
    dskip = jnp.repeat(d_skip.astype(F32), SSD_HEAD_DIM).reshape(1, inner)
    row = lambda b, c: (b * n_chunks + c, 0)
    const = lambda b, c: (0, 0)
    est = (2 * L * (inner + width) * 2 + 2 * L * LANES * 4 + 2 * L * inner * 2
           + (inner // LANES) * SSD_STATE * LANES * 4 + L * (width + inner) * 4 + (8 << 20))
    kern = functools.partial(_ssd_kernel, inner=inner, n_groups=SSD_GROUPS, d_state=SSD_STATE)
    return pl.pallas_call(
        kern,
        out_shape=jax.ShapeDtypeStruct((t, inner), BF16),
        grid=(batch, n_chunks),
        in_specs=[pl.BlockSpec((L, inner), row),
                  pl.BlockSpec((L, width), row),
                  pl.BlockSpec((L, LANES), row),
                  pl.BlockSpec((SUBLANES, width), const),
                  pl.BlockSpec((1, width), const),
                  pl.BlockSpec((1, LANES), const),
                  pl.BlockSpec((1, LANES), const),
                  pl.BlockSpec((1, inner), const),
                  pl.BlockSpec((1, inner), const)],
        out_specs=pl.BlockSpec((L, inner), row),
        scratch_shapes=[pltpu.VMEM((SUBLANES, width), F32),
                        pltpu.VMEM((inner // LANES, SSD_STATE, LANES), F32),
                        pltpu.VMEM((L, width), F32),
                        pltpu.VMEM((L, inner), F32)],
        compiler_params=_params(("parallel", "arbitrary"), est),
        name="ssd_scan",
    )(z, xbc, dt_raw, convw, conv_b.astype(F32).reshape(1, width), pad_heads(dt_bias),
      pad_heads(a_log), dskip, norm_g.astype(F32).reshape(1, inner))


def _bucket_tiles(tq):
    max_exact = REL_BUCKETS // 2
    qi = np.arange(tq)[:, None]
    ki = np.arange(tq)[None, :]

    def bucket(dist):
        d = np.maximum(dist, 1).astype(np.float32)
        large = max_exact + (np.log(d / np.float32(max_exact)) / np.float32(math.log(REL_MAX_DIST / max_exact))
                             * np.float32(REL_BUCKETS - max_exact)).astype(np.int32)
        large = np.minimum(large, REL_BUCKETS - 1)
        return np.where(dist < max_exact, dist, large).astype(np.int32)

    diag = np.where(qi >= ki, bucket(np.maximum(qi - ki, 0)), -1)
    prev = bucket(tq + qi - ki)
    far = bucket(np.arange(tq + 1, 1 << 16))
    assert (far == REL_BUCKETS - 1).all()
    return np.stack([diag, prev]).astype(np.int32)


def _bias_kernel(rb_ref, bucket_ref, o_ref):
    h = pl.program_id(0)
    bk = bucket_ref[...]
    acc = jnp.zeros(bk.shape, F32)
    for b in range(REL_BUCKETS):
        acc = jnp.where(bk == b, rb_ref[b, h], acc)
    o_ref[0] = jnp.where(bk < 0, MASK_VALUE, acc - rb_ref[REL_BUCKETS - 1, h])


def _bias_tiles(rel_bias, tq):
    buckets = jnp.asarray(_bucket_tiles(tq))
    return pl.pallas_call(
        _bias_kernel,
        out_shape=jax.ShapeDtypeStruct((DIFF_HEADS, 2, tq, tq), F32),
        grid=(DIFF_HEADS,),
        in_specs=[pl.BlockSpec(memory_space=pltpu.SMEM),
                  pl.BlockSpec((2, tq, tq), lambda h: (0, 0, 0))],
        out_specs=pl.BlockSpec((1, 2, tq, tq), lambda h: (h, 0, 0, 0)),
        compiler_params=_params(("arbitrary",), 8 * tq * tq * 4),
        name="t5_bias_tiles",
    )(rel_bias.astype(F32), buckets)


def _attn_kernel(lam_ref, q_ref, k_ref, v_ref, bias_ref, g_ref, o_ref,
                 q2_ref, vaug_ref, m_ref, acc_ref, *, tq, rows, lam_init):
    i = pl.program_id(2)
    dh = LANES // 2
    reps = tq // LANES

    @pl.when(i == 0)
    def _():
        vaug_ref[:, 0:LANES] = v_ref[...]
        vaug_ref[:, LANES:2 * LANES] = jnp.ones((v_ref.shape[0], LANES), BF16)

    lane = lax.broadcasted_iota(jnp.int32, (tq, LANES), 1)
    qs = q_ref[...] * (dh ** -0.5)
    zero = jnp.zeros_like(qs)
    q2_ref[0:tq] = jnp.where(lane < dh, qs, zero)
    q2_ref[tq:2 * tq] = jnp.where(lane >= dh, qs, zero)
    m_ref[...] = jnp.full(m_ref.shape, MASK_VALUE, F32)
    acc_ref[...] = jnp.zeros_like(acc_ref)

    def step(j, bias_idx):
        start = pl.multiple_of(j * tq, tq)
        kb = k_ref[pl.ds(start, tq), :]
        vb = vaug_ref[pl.ds(start, tq), :]
        for r0 in range(0, 2 * tq, rows):
            rs = slice(r0, r0 + rows)
            s = lax.dot_general(q2_ref[rs], kb, (((1,), (1,)), ((), ())),
                                preferred_element_type=F32)
            if bias_idx is not None:
                s = s + bias_ref[0, bias_idx, r0 % tq:r0 % tq + rows, :]
            m_old = m_ref[rs]
            m_new = jnp.maximum(m_old, jnp.max(s, axis=1, keepdims=True))
            alpha = jnp.exp(m_old - m_new)
            p = jnp.exp(s - jnp.concatenate([m_new] * reps, axis=1))
            pv = jnp.dot(p.astype(BF16), vb, preferred_element_type=F32)
            acc_ref[rs] = jnp.concatenate([alpha, alpha], axis=1) * acc_ref[rs] + pv
            m_ref[rs] = m_new

    def far_step(j, carry):
        step(j, None)
        return carry

    lax.fori_loop(0, jnp.maximum(i - 1, 0), far_step, 0)

    @pl.when(i >= 1)
    def _():
        step(i - 1, 1)

    step(i, 0)

    lv = lam_ref[...]
    s1 = jnp.sum(lv[0:1] * lv[1:2], axis=1, keepdims=True)
    s2 = jnp.sum(lv[2:3] * lv[3:4], axis=1, keepdims=True)
    lam = jnp.exp(s1) - jnp.exp(s2) + lam_init
    o1 = acc_ref[0:tq, 0:LANES] / acc_ref[0:tq, LANES:2 * LANES]
    o2 = acc_ref[tq:2 * tq, 0:LANES] / acc_ref[tq:2 * tq, LANES:2 * LANES]
    o = o1 - lam * o2
    ms = jnp.mean(o * o, axis=-1, keepdims=True)
    o_ref[...] = (o * lax.rsqrt(ms + NORM_EPS) * g_ref[...] * (1.0 - lam_init)).astype(o_ref.dtype)


def _diff_attention(qkv, lam_vecs, bias_tiles, subln_g, *, batch, seq, tq, layer_idx):
    t = qkv.shape[0]
    nq = seq // tq
    rows = min(256, tq)
    lam_init = 0.8 - 0.6 * math.exp(-0.3 * layer_idx)
    kern = functools.partial(_attn_kernel, tq=tq, rows=rows, lam_init=lam_init)
    est = (2 * tq * LANES * 2 + 4 * seq * LANES * 2 + 4 * tq * tq * 4 + 2 * tq * LANES * 2
           + 2 * tq * LANES * 2 + seq * 2 * LANES * 2 + 2 * tq * LANES * 4 + 2 * tq * 2 * LANES * 4
           + 8 * rows * tq * 4)
    return pl.pallas_call(
        kern,
        out_shape=jax.ShapeDtypeStruct((t, DIFF_HEADS * LANES), BF16),
        grid=(batch, DIFF_HEADS, nq),
        in_specs=[pl.BlockSpec((SUBLANES, LANES), lambda b, h, i: (0, 0)),
                  pl.BlockSpec((tq, LANES), lambda b, h, i: (b * nq + i, h)),
                  pl.BlockSpec((seq, LANES), lambda b, h, i: (b, DIFF_HEADS + h)),
                  pl.BlockSpec((seq, LANES), lambda b, h, i: (b, 2 * DIFF_HEADS + h)),
                  pl.BlockSpec((1, 2, tq, tq), lambda b, h, i: (h, 0, 0, 0)),
                  pl.BlockSpec((1, LANES), lambda b, h, i: (0, 0))],
        out_specs=pl.BlockSpec((tq, LANES), lambda b, h, i: (b * nq + i, h)),
        scratch_shapes=[pltpu.VMEM((2 * tq, LANES), BF16),
                        pltpu.VMEM((seq, 2 * LANES), BF16),
                        pltpu.VMEM((2 * tq, LANES), F32),
                        pltpu.VMEM((2 * tq, 2 * LANES), F32)],
        compiler_params=_params(("parallel", "parallel", "arbitrary"), est),
        name="diff_attention",
    )(lam_vecs, qkv, qkv, qkv, bias_tiles, subln_g)


def _mix_kernel(x_ref, y_ref, o_ref, gate_ref, wssd_ref, wdiff_ref, wmix_ref, g_ref, b_ref, out_ref):
    d = x_ref.shape[1]
    ssd = jnp.dot(y_ref[...], wssd_ref[...], preferred_element_type=F32)
    dif = jnp.dot(o_ref[...], wdiff_ref[...], preferred_element_type=F32)
    merged = gate_ref[:, 0:d].astype(F32) * ssd + gate_ref[:, d:2 * d].astype(F32) * dif
    mixed = jnp.dot(merged.astype(BF16), wmix_ref[...], preferred_element_type=F32)
    out_ref[...] = _layer_norm(DN_ALPHA * x_ref[...] + mixed, g_ref[...], b_ref[...])


def _mix(x, y_ssd, o_diff, gates, w_ssd, w_diff, w_mix, ln_g, ln_b, *, tm):
    t, d = x.shape
    tm = min(tm, t)
    inner = y_ssd.shape[1]
    dw = o_diff.shape[1]
    row = lambda i: (i, 0)
    est = (2 * tm * (d * 4 + inner * 2 + dw * 2 + 2 * d * 2 + d * 4)
           + (inner * d + dw * d + d * d) * 2 + 6 * tm * d * 4)
    return pl.pallas_call(
        _mix_kernel,
        out_shape=jax.ShapeDtypeStruct((t, d), F32),
        grid=(t // tm,),
        in_specs=[pl.BlockSpec((tm, d), row),
                  pl.BlockSpec((tm, inner), row),
                  pl.BlockSpec((tm, dw), row),
                  pl.BlockSpec((tm, 2 * d), row),
                  _resident((inner, d)), _resident((dw, d)), _resident((d, d)),
                  _resident((1, d)), _resident((1, d))],
        out_specs=pl.BlockSpec((tm, d), row),
        compiler_params=_params(("parallel",), est),
        name="mix_ln1",
    )(x, y_ssd, o_diff, gates, w_ssd, w_diff, w_mix, ln_g, ln_b)


def _pack_bf16_pairs(v):
    w = v.shape[1] // 2
    lo = lax.bitcast_convert_type(v[:, :w].astype(BF16).astype(F32), jnp.int32)
    hi = lax.bitcast_convert_type(v[:, w:].astype(BF16).astype(F32), jnp.int32)
    return jnp.bitwise_or(hi, lax.shift_right_logical(lo, 16))


def _unpack_bf16_pairs(words):
    lo = lax.bitcast_convert_type(lax.shift_left(words, 16), F32)
    hi = lax.bitcast_convert_type(jnp.bitwise_and(words, -65536), F32)
    return jnp.concatenate([lo, hi], axis=1)


def _xattn_kernel(x_ref, kv_ref, wq_ref, wo_ref, g_ref, b_ref, out_ref, packed_ref, o_scr, *, heads):
    d = x_ref.shape[1]
    dh = d // heads
    xv = x_ref[...]
    q = jnp.dot(xv.astype(BF16), wq_ref[...], preferred_element_type=F32)
    q = (q * (dh ** -0.5)).astype(BF16)
    for h in range(heads):
        kh = kv_ref[:, h * dh:(h + 1) * dh]
        vh = kv_ref[:, d + h * dh:d + (h + 1) * dh]
        s = lax.dot_general(q[:, h * dh:(h + 1) * dh], kh, (((1,), (1,)), ((), ())),
                            preferred_element_type=F32)
        p = jnp.exp(s - jnp.max(s, axis=1, keepdims=True))
        p = p / jnp.sum(p, axis=1, keepdims=True)
        o_scr[:, h * dh:(h + 1) * dh] = jnp.dot(p.astype(BF16), vh, preferred_element_type=F32).astype(BF16)
    att = jnp.dot(o_scr[...], wo_ref[...], preferred_element_type=F32)
    y = _layer_norm(DN_ALPHA * xv + att, g_ref[...], b_ref[...])
    out_ref[...] = y
    packed_ref[...] = _pack_bf16_pairs(y)


def _cross_attention(x, kv, w_cq, w_co, ln_g, ln_b, *, batch, seq, tq):
    t, d = x.shape
    mem_len = kv.shape[0] // batch
    tq = min(tq, seq)
    nq = seq // tq
    est = (2 * tq * d * 4 * 2 + 2 * tq * d * 2 + 2 * mem_len * 2 * d * 2 + 2 * d * d * 2 + tq * d * 2
           + 8 * tq * d * 4)
    row = lambda b, i: (b * nq + i, 0)
    return pl.pallas_call(
        functools.partial(_xattn_kernel, heads=MEM_HEADS),
        out_shape=(jax.ShapeDtypeStruct((t, d), F32), jax.ShapeDtypeStruct((t, d // 2), jnp.int32)),
        grid=(batch, nq),
        in_specs=[pl.BlockSpec((tq, d), row),
                  pl.BlockSpec((mem_len, 2 * d), lambda b, i: (b, 0)),
                  _resident((d, d)), _resident((d, d)), _resident((1, d)), _resident((1, d))],
        out_specs=(pl.BlockSpec((tq, d), row), pl.BlockSpec((tq, d // 2), row)),
        scratch_shapes=[pltpu.VMEM((tq, d), BF16)],
        compiler_params=_params(("parallel", "parallel"), est),
        name="cross_attention_ln2",
    )(x, kv, w_cq, w_co, ln_g, ln_b)


def _router_kernel(x_ref, wr_ref, rb_ref, idx_ref, rank_ref, gates_ref, cnt_ref, run_ref, *, n_experts):
    tm = x_ref.shape[0]
    n_groups = N_EXPERT_GROUPS
    per = n_experts // n_groups

    @pl.when(pl.program_id(0) == 0)
    def _():
        run_ref[...] = jnp.zeros_like(run_ref)

    logits = lax.dot_general(wr_ref[...], x_ref[...], (((1,), (1,)), ((), ())),
                             precision=lax.Precision.HIGHEST,
                             preferred_element_type=F32)
    sc = _sigmoid(logits[0:n_experts]).reshape(n_groups, per, tm)
    choice = sc + rb_ref[0:n_experts].reshape(n_groups, per, 1)
    neg = -jnp.inf
    member = lax.broadcasted_iota(jnp.int32, (n_groups, per, tm), 1)
    group3 = lax.broadcasted_iota(jnp.int32, (n_groups, per, tm), 0)
    m1 = jnp.max(choice, axis=1, keepdims=True)
    i1 = jnp.min(jnp.where(choice == m1, member, per), axis=1, keepdims=True)
    m2 = jnp.max(jnp.where(member == i1, neg, choice), axis=1, keepdims=True)
    gscore = (m1 + m2)
    gsel = jnp.zeros((n_groups, 1, tm), F32)
    gidx = lax.broadcasted_iota(jnp.int32, (n_groups, 1, tm), 0)
    cur = gscore
    for _ in range(TOP_GROUPS):
        mx = jnp.max(cur, axis=0, keepdims=True)
        ix = jnp.min(jnp.where(cur == mx, gidx, n_groups), axis=0, keepdims=True)
        hit = gidx == ix
        gsel = jnp.where(hit, 1.0, gsel)
        cur = jnp.where(hit, neg, cur)
    cur = jnp.where(gsel > 0.0, choice, neg)
    eidx = group3 * per + member
    esel = jnp.zeros((n_groups, per, tm), F32)
    hits, idx_rows = [], []
    for _ in range(TOP_K):
        mx = jnp.max(jnp.max(cur, axis=1, keepdims=True), axis=0, keepdims=True)
        ix = jnp.min(jnp.min(jnp.where(cur == mx, eidx, n_experts), axis=1, keepdims=True),
                     axis=0, keepdims=True)
        hit = eidx == ix
        esel = jnp.where(hit, 1.0, esel)
        cur = jnp.where(hit, neg, cur)
        hits.append(hit)
        idx_rows.append(ix.reshape(1, tm))
    w = esel * sc
    tot = jnp.sum(jnp.sum(w, axis=1, keepdims=True), axis=0, keepdims=True)
    gw = w / tot * ROUTED_SCALE

    r_i = lax.broadcasted_iota(jnp.int32, (tm, tm), 0)
    c_i = lax.broadcasted_iota(jnp.int32, (tm, tm), 1)
    upper = jnp.where(r_i <= c_i, 1.0, 0.0).astype(BF16)
    pref = jnp.dot(esel.reshape(n_experts, tm).astype(BF16), upper, preferred_element_type=F32)
    run = run_ref[...]
    rank3 = (jnp.concatenate([run] * (tm // LANES), axis=1) + pref - 1.0).reshape(n_groups, per, tm)

    def pick(hit, vals):
        return jnp.sum(jnp.sum(jnp.where(hit, vals, 0.0), axis=1, keepdims=True), axis=0).reshape(1, tm)

    idx_ref[...] = jnp.concatenate(idx_rows, axis=0)
    rank_ref[...] = jnp.concatenate([pick(h, rank3) for h in hits], axis=0).astype(jnp.int32)
    wk = jnp.concatenate([pick(h, gw) for h in hits] + [jnp.zeros((LANES - TOP_K, tm), F32)], axis=0)
    gates_ref[...] = wk.T
    run = run + jnp.broadcast_to(pref[:, tm - 1:tm], run.shape)
    run_ref[...] = run
    cnt_ref[...] = run


def _router(x, w_router, router_bias, *, tm):
    t, d = x.shape
    n_experts = w_router.shape[1]
    tm = min(tm, t)
    wr = jnp.pad(w_router.astype(F32).T, ((0, LANES - n_experts), (0, 0)))
    rb = jnp.pad(router_bias.astype(F32), (0, LANES - n_experts)).reshape(LANES, 1)
    est = (2 * tm * d * 4 + LANES * d * 4 + 2 * tm * LANES * 4 + 60 * n_experts * tm * 4 + 3 * tm * tm * 4)
    pick_spec = pl.BlockSpec((TOP_K, tm), lambda i: (0, i))
    return pl.pallas_call(
        functools.partial(_router_kernel, n_experts=n_experts),
        out_shape=(jax.ShapeDtypeStruct((TOP_K, t), jnp.int32),
                   jax.ShapeDtypeStruct((TOP_K, t), jnp.int32),
                   jax.ShapeDtypeStruct((t, LANES), F32),
                   jax.ShapeDtypeStruct((n_experts, LANES), F32)),
        grid=(t // tm,),
        in_specs=[pl.BlockSpec((tm, d), lambda i: (i, 0)),
                  _resident((LANES, d)), _resident((LANES, 1))],
        out_specs=(pick_spec, pick_spec,
                   pl.BlockSpec((tm, LANES), lambda i: (i, 0)),
                   pl.BlockSpec((n_experts, LANES), lambda i: (0, 0))),
        scratch_shapes=[pltpu.VMEM((n_experts, LANES), F32)],
        compiler_params=_params(("arbitrary",), est),
        name="router",
    )(x, wr, rb)


def _positions_kernel(cnt_ref, idx_ref, rank_ref, pos_ref, texp_ref, off_ref, *, n_experts, rows, n_tiles):
    @pl.when(pl.program_id(0) == 0)
    def _():
        def per_expert(e, toff):
            off_ref[e] = toff * rows
            nt = lax.shift_right_logical(cnt_ref[e] + (rows - 1), int(math.log2(rows)))

            def fill(j, c):
                texp_ref[toff + j] = e
                return c

            lax.fori_loop(0, nt, fill, 0)
            return toff + nt

        n_used = lax.fori_loop(0, n_experts, per_expert, 0)

        def tail(j, c):
            texp_ref[j] = n_experts - 1
            return c

        lax.fori_loop(n_used, n_tiles, tail, 0)
        texp_ref[n_tiles] = n_used

    idx = idx_ref[...]
    pos = rank_ref[...]
    for e in range(n_experts):
        pos = pos + jnp.where(idx == e, off_ref[e], 0)
    pos_ref[...] = pos


def _positions(counts, idx_t, rank_t, *, rows, n_tiles, tm):
    n_experts = counts.shape[0]
    k, t = idx_t.shape
    tm = min(tm, t)
    spec = pl.BlockSpec((k, tm), lambda i: (0, i))
    kern = functools.partial(_positions_kernel, n_experts=n_experts, rows=rows, n_tiles=n_tiles)
    return pl.pallas_call(
        kern,
        out_shape=(jax.ShapeDtypeStruct((k, t), jnp.int32),
                   jax.ShapeDtypeStruct((n_tiles + 1,), jnp.int32)),
        grid=(t // tm,),
        in_specs=[pl.BlockSpec(memory_space=pltpu.SMEM), spec, spec],
        out_specs=(spec, pl.BlockSpec(memory_space=pltpu.SMEM)),
        scratch_shapes=[pltpu.SMEM((n_experts,), jnp.int32)],
        compiler_params=_params(("arbitrary",), 16 * k * tm * 4),
        name="moe_positions",
    )(counts, idx_t, rank_t)


def _sc_mesh():
    return plsc.VectorSubcoreMesh(core_axis_name="c", subcore_axis_name="s",
                                  num_cores=SC_CORES, num_subcores=SC_SUBCORES)


def _sc_dispatch(packed, pos_chunks, n_rows):
    t, w = packed.shape
    n_chunks, k, n = pos_chunks.shape
    per_worker = n_chunks // (SC_CORES * SC_SUBCORES)

    @functools.partial(
        pl.kernel, mesh=_sc_mesh(),
        out_type=jax.ShapeDtypeStruct((n_rows, w), packed.dtype),
        scratch_types=[pltpu.VMEM((k, n), jnp.int32), pltpu.VMEM((n, w), packed.dtype),
                       pltpu.SemaphoreType.DMA],
        name="moe_dispatch_sc",
    )
    def scatter_rows(x_hbm, pos_hbm, out_hbm, idx_v, rows_v, sem):
        wid = lax.axis_index("s") * SC_CORES + lax.axis_index("c")

        @pl.loop(0, per_worker)
        def _(step):
            c = wid * per_worker + step
            pltpu.sync_copy(pos_hbm.at[c], idx_v)
            pltpu.sync_copy(x_hbm.at[pl.ds(c * n, n)], rows_v)
            copies = [pltpu.async_copy(rows_v, out_hbm.at[idx_v.at[kk]], sem) for kk in range(k)]
            for cp in copies:
                cp.wait()

    return scatter_rows(packed, pos_chunks)


def _sc_combine(sorted_rows, pos_chunks):
    _, w = sorted_rows.shape
    n_chunks, k, n = pos_chunks.shape
    per_worker = n_chunks // (SC_CORES * SC_SUBCORES)

    @functools.partial(
        pl.kernel, mesh=_sc_mesh(),
        out_type=jax.ShapeDtypeStruct((k, n_chunks * n, w), sorted_rows.dtype),
        scratch_types=[pltpu.VMEM((k, n), jnp.int32), pltpu.VMEM((n, w), sorted_rows.dtype),
                       pltpu.SemaphoreType.DMA],
        name="moe_combine_sc",
    )
    def gather_rows(y_hbm, pos_hbm, out_hbm, idx_v, rows_v, sem):
        wid = lax.axis_index("s") * SC_CORES + lax.axis_index("c")

        @pl.loop(0, per_worker)
        def _(step):
            c = wid * per_worker + step
            pltpu.sync_copy(pos_hbm.at[c], idx_v)
            for kk in range(k):
                pltpu.async_copy(y_hbm.at[idx_v.at[kk]], rows_v, sem).wait()
                pltpu.sync_copy(rows_v, out_hbm.at[kk, pl.ds(c * n, n)])

    return gather_rows(sorted_rows, pos_chunks)


def _expert_kernel(texp_ref, xs_ref, wgu_ref, wdn_ref, ys_ref, *, n_tiles):
    ff = wdn_ref.shape[1]

    @pl.when(pl.program_id(0) < texp_ref[n_tiles])
    def _():
        xv = _unpack_bf16_pairs(xs_ref[...]).astype(BF16)
        gu = jnp.dot(xv, wgu_ref[0], preferred_element_type=F32)
        hid = _silu(gu[:, 0:ff]) * gu[:, ff:2 * ff]
        ys_ref[...] = _pack_bf16_pairs(jnp.dot(hid.astype(BF16), wdn_ref[0], preferred_element_type=F32))


def _experts(tile_expert, sorted_rows, w_gu, w_dn, *, rows):
    n_rows, w = sorted_rows.shape
    n_tiles = n_rows // rows
    _, d, ff2 = w_gu.shape
    ff = ff2 // 2
    tile = lambda j, te: (jnp.minimum(j, te[n_tiles] - 1), 0)
    est = 4 * rows * w * 4 + 2 * (d * ff2 + ff * d) * 2 + 4 * rows * d * 4 + 4 * rows * ff2 * 4
    return pl.pallas_call(
        functools.partial(_expert_kernel, n_tiles=n_tiles),
        out_shape=jax.ShapeDtypeStruct((n_rows, w), sorted_rows.dtype),
        grid_spec=pltpu.PrefetchScalarGridSpec(
            num_scalar_prefetch=1,
            grid=(n_tiles,),
            in_specs=[pl.BlockSpec((rows, w), tile),
                      pl.BlockSpec((1, d, ff2), lambda j, te: (te[j], 0, 0)),
                      pl.BlockSpec((1, ff, d), lambda j, te: (te[j], 0, 0))],
            out_specs=pl.BlockSpec((rows, w), tile)),
        compiler_params=_params(("arbitrary",), est),
        name="moe_experts",
    )(tile_expert, sorted_rows, w_gu, w_dn)


def _moe_out_kernel(x_ref, yk_ref, gates_ref, wsgu_ref, wsdn_ref, g_ref, b_ref, out_ref):
    ff = wsdn_ref.shape[0]
    xv = x_ref[...]
    gu = jnp.dot(xv.astype(BF16), wsgu_ref[...], preferred_element_type=F32)
    hid = _silu(gu[:, 0:ff]) * gu[:, ff:2 * ff]
    acc = jnp.dot(hid.astype(BF16), wsdn_ref[...], preferred_element_type=F32)
    for k in range(yk_ref.shape[0]):
        acc = acc + gates_ref[:, k:k + 1] * _unpack_bf16_pairs(yk_ref[k])
    out_ref[...] = _layer_norm(DN_ALPHA * xv + acc, g_ref[...], b_ref[...])


def _moe_out(x, yk, gates, w_sgu, w_sdn, ln_g, ln_b, *, tm):
    t, d = x.shape
    k, _, w = yk.shape
    sff2 = w_sgu.shape[1]
    tm = min(tm, t)
    est = (2 * tm * d * 4 * 2 + 2 * k * tm * w * 4 + 2 * tm * LANES * 4
           + (d * sff2 + (sff2 // 2) * d) * 2 + 6 * tm * d * 4)
    return pl.pallas_call(
        _moe_out_kernel,
        out_shape=jax.ShapeDtypeStruct((t, d), F32),
        grid=(t // tm,),
        in_specs=[pl.BlockSpec((tm, d), lambda i: (i, 0)),
                  pl.BlockSpec((k, tm, w), lambda i: (0, i, 0)),
                  pl.BlockSpec((tm, LANES), lambda i: (i, 0)),
                  _resident((d, sff2)), _resident((sff2 // 2, d)),
                  _resident((1, d)), _resident((1, d))],
        out_specs=pl.BlockSpec((tm, d), lambda i: (i, 0)),
        compiler_params=_params(("parallel",), est),
        name="moe_out_ln3",
    )(x, yk, gates, w_sgu, w_sdn, ln_g, ln_b)


def _moe(x, packed, w_router, router_bias, w_gu, w_dn, w_sgu, w_sdn, ln_g, ln_b):
    t, d = x.shape
    n_experts = w_router.shape[1]
    rows = min(MOE_ROWS_PER_TILE, t)
    n_tiles = (t * TOP_K) // rows + n_experts
    idx_t, rank_t, gates, counts = _router(x, w_router, router_bias, tm=512)
    counts = counts[:, 0].astype(jnp.int32)
    pos_t, tile_expert = _positions(counts, idx_t, rank_t, rows=rows, n_tiles=n_tiles, tm=2048)
    n = SC_CHUNK_TOKENS
    pos_chunks = pos_t.reshape(TOP_K, t // n, n).transpose(1, 0, 2)
    sorted_x = _sc_dispatch(packed, pos_chunks, n_tiles * rows)
    sorted_y = _experts(tile_expert, sorted_x, w_gu, w_dn, rows=rows)
    yk = _sc_combine(sorted_y, pos_chunks)
    return _moe_out(x, yk, gates, w_sgu, w_sdn, ln_g, ln_b, tm=256)


def kernel(x, mem, w_in, conv_w, conv_b, dt_bias, a_log, d_skip, ssd_norm_g, lambda_q1, lambda_k1, lambda_q2, lambda_k2, subln_g, rel_bias, w_ssd_br, w_diff_br, w_mix_out, ln1_g, ln1_b, w_cq, w_ckv, w_co, ln2_g, ln2_b, w_router, router_bias, w_exp_gu, w_exp_down, w_sh_gu, w_sh_down, ln3_g, ln3_b):
    batch, seq, d = x.shape
    depth = w_in.shape[0]
    inner = w_ssd_br.shape[1]
    xbc_width = conv_w.shape[2]
    heads = dt_bias.shape[1]
    diff_width = w_diff_br.shape[1]
    o_z, o_xbc = inner, inner + xbc_width
    o_dt = o_xbc + heads
    o_v = o_dt + 3 * diff_width
    t = batch * seq
    tq_attn = min(512, seq)

    def vec(v):
        return v.astype(F32).reshape(1, -1)

    xt = x.reshape(t, d)
    memt = mem.reshape(-1, d)
    bias_tiles = _bias_tiles(rel_bias, tq_attn)

    for l in range(depth):
        w = w_in[l].astype(BF16)
        w_dt = jnp.pad(w_in[l][:, o_xbc:o_dt], ((0, 0), (0, LANES - heads))).astype(BF16)
        z = _matmul(xt, w[:, :o_z], BF16, tm=1024, tn=1024, name="in_proj_z")
        xbc = _matmul(xt, w[:, o_z:o_xbc], BF16, tm=1024, tn=1024, name="in_proj_xbc")
        dt_raw = _matmul(xt, w_dt, F32, tm=1024, tn=LANES, name="in_proj_dt")
        qkv = _matmul(xt, w[:, o_dt:o_v], BF16, tm=1024, tn=1024, name="in_proj_qkv")
        gates = _matmul(xt, w[:, o_v:], BF16, tm=1024, tn=1024, act="sigmoid", name="in_proj_gates")

        y_ssd = _ssd_branch(z, xbc, dt_raw, conv_w[l], conv_b[l], dt_bias[l], a_log[l], d_skip[l],
                            ssd_norm_g[l], batch=batch, seq=seq)

        lam_vecs = jnp.pad(jnp.stack([lambda_q1[l], lambda_k1[l], lambda_q2[l], lambda_k2[l]]).astype(F32),
                           ((0, SUBLANES - 4), (0, LANES - lambda_q1.shape[1])))
        o_diff = _diff_attention(qkv, lam_vecs, bias_tiles, vec(subln_g[l]), batch=batch, seq=seq,
                                 tq=tq_attn, layer_idx=l)

        xt = _mix(xt, y_ssd, o_diff, gates, w_ssd_br[l].astype(BF16), w_diff_br[l].astype(BF16),
                  w_mix_out[l].astype(BF16), vec(ln1_g[l]), vec(ln1_b[l]), tm=512)

        kv = _matmul(memt, w_ckv[l].astype(BF16), BF16, tm=1024, tn=1024, name="mem_kv_proj")
        xt, packed = _cross_attention(xt, kv, w_cq[l].astype(BF16), w_co[l].astype(BF16), vec(ln2_g[l]),
                                      vec(ln2_b[l]), batch=batch, seq=seq, tq=512)

        xt = _moe(xt, packed, w_router[l], router_bias[l], w_exp_gu[l].astype(BF16),
                  w_exp_down[l].astype(BF16), w_sh_gu[l].astype(BF16), w_sh_down[l].astype(BF16),
                  vec(ln3_g[l]), vec(ln3_b[l]))
    return xt.reshape(batch, seq, d)
```

```python
import functools
import math

import numpy as np
import jax
import jax.numpy as jnp
from jax import lax
from jax.experimental import pallas as pl
from jax.experimental.pallas import tpu as pltpu
from jax.experimental.pallas import tpu_sc as plsc

F32 = jnp.float32
BF16 = jnp.bfloat16

SSD_HEAD_DIM = 64
SSD_GROUPS = 4
SSD_STATE = 128
SSD_CONV = 4
SSD_CHUNK = 128
DIFF_HEADS = 8
REL_BUCKETS = 32
REL_MAX_DIST = 128
MEM_HEADS = 4
TOP_K = 8
N_EXPERT_GROUPS = 8
TOP_GROUPS = 4
ROUTED_SCALE = 2.5
NORM_EPS = 1e-5
DEPTH = 1
DN_ALPHA = (2.0 * DEPTH) ** 0.25

LANES = 128
SUBLANES = 8
VMEM_CAP_BYTES = 64 * 1024 * 1024
MASK_VALUE = -1e30
CONV_TAIL = 16
LOG2E = math.log2(math.e)
SC_CORES = 2
SC_SUBCORES = 16
SC_CHUNK_TOKENS = 128
MOE_ROWS_PER_TILE = 1024


def _vmem_limit(estimate_bytes):
    return int(min(estimate_bytes * 5 // 4 + (4 << 20), VMEM_CAP_BYTES - (6 << 20)))


def _params(semantics, vmem_estimate):
    return pltpu.CompilerParams(dimension_semantics=semantics,
                                vmem_limit_bytes=_vmem_limit(vmem_estimate))


def _resident(shape):
    nd = len(shape)
    return pl.BlockSpec(shape, lambda *_: (0,) * nd, pipeline_mode=pl.Buffered(1))


def _layer_norm(v, g, b):
    mu = jnp.mean(v, axis=-1, keepdims=True)
    d = v - mu
    var = jnp.mean(d * d, axis=-1, keepdims=True)
    return d * lax.rsqrt(var + NORM_EPS) * g + b


def _sigmoid(v):
    return 0.5 * jnp.tanh(0.5 * v) + 0.5


def _silu(v):
    return v * _sigmoid(v)


def _matmul_kernel(x_ref, w_ref, o_ref, xb_ref, *, sigmoid_from):
    j = pl.program_id(1)

    @pl.when(j == 0)
    def _():
        xb_ref[...] = x_ref[...].astype(BF16)

    acc = jnp.dot(xb_ref[...], w_ref[...], preferred_element_type=F32)
    if sigmoid_from is None:
        o_ref[...] = acc.astype(o_ref.dtype)
    else:
        @pl.when(j < sigmoid_from)
        def _():
            o_ref[...] = acc.astype(o_ref.dtype)

        @pl.when(j >= sigmoid_from)
        def _():
            o_ref[...] = _sigmoid(acc).astype(o_ref.dtype)


def _matmul(x, w, out_dtype, *, tm, tn, sigmoid_from=None, name):
    m, k = x.shape
    n = w.shape[1]
    tm, tn = min(tm, m), min(tn, n)
    est = (2 * tm * k * x.dtype.itemsize + tm * k * 2 + 2 * k * tn * 2
           + 2 * tm * tn * jnp.dtype(out_dtype).itemsize + tm * tn * 4)
    return pl.pallas_call(
        functools.partial(_matmul_kernel, sigmoid_from=sigmoid_from),
        out_shape=jax.ShapeDtypeStruct((m, n), out_dtype),
        grid=(m // tm, n // tn),
        in_specs=[pl.BlockSpec((tm, k), lambda i, j: (i, 0)),
                  pl.BlockSpec((k, tn), lambda i, j: (0, j))],
        out_specs=pl.BlockSpec((tm, tn), lambda i, j: (i, j)),
        scratch_shapes=[pltpu.VMEM((tm, k), BF16)],
        compiler_params=_params(("parallel", "arbitrary"), est),
        name=name,
    )(x, w)


def _split3(v):
    hi = v.astype(BF16)
    r1 = v - hi.astype(F32)
    mid = r1.astype(BF16)
    lo = (r1 - mid.astype(F32)).astype(BF16)
    return hi, mid, lo


def _shift_matrix():
    L = SSD_CHUNK
    s = np.zeros((SSD_CONV - 1, L, 2 * L), np.float32)
    for j in range(1, SSD_CONV):
        for t in range(L):
            s[j - 1, t, t - j if t >= j else L + CONV_TAIL - j + t] = 1.0
    return s.reshape((SSD_CONV - 1) * L, 2 * L)


def _ssd_kernel(z_ref, xbc_ref, dt_ref, shift_ref, convw_ref, convb_ref, dtb_ref, alog_ref,
                dskip_ref, g_ref, y_ref, cat_ref, state_ref, xa_ref, yacc_ref,
                *, inner, n_groups, d_state):
    L = SSD_CHUNK
    width = xbc_ref.shape[1]
    n_pairs = inner // LANES
    pairs_per_group = n_pairs // n_groups
    group_width = inner // n_groups

    @pl.when(pl.program_id(1) == 0)
    def _():
        cat_ref[...] = jnp.zeros_like(cat_ref)
        state_ref[...] = jnp.zeros_like(state_ref)

    slab = 512
    for c0 in range(0, width, slab):
        cs = slice(c0, c0 + slab)
        ub = xbc_ref[:, cs]
        cat_ref[0:L, cs] = ub
        sh = jnp.dot(shift_ref[...], cat_ref[:, cs], preferred_element_type=F32)
        acc = convb_ref[:, cs] + ub.astype(F32) * convw_ref[SSD_CONV - 1:SSD_CONV, cs]
        for j in range(1, SSD_CONV):
            acc = acc + sh[(j - 1) * L:j * L] * convw_ref[SSD_CONV - 1 - j:SSD_CONV - j, cs]
        cat_ref[L:L + CONV_TAIL, cs] = ub[L - CONV_TAIL:L]
        xa_ref[:, cs] = _silu(acc)

    dtr = dt_ref[...] + dtb_ref[...]
    dt = jnp.maximum(dtr, 0.0) + jnp.log(1.0 + jnp.exp(-jnp.abs(dtr)))
    a = -jnp.exp(alog_ref[...]) * dt
    row_i = lax.broadcasted_iota(jnp.int32, (L, L), 0)
    col_i = lax.broadcasted_iota(jnp.int32, (L, L), 1)
    causal = row_i >= col_i
    tril = jnp.where(causal, 1.0, 0.0).astype(BF16)
    acs = sum(jnp.dot(tril, part, preferred_element_type=F32) for part in _split3(a))
    acs_t = acs.T
    dt_t = dt.T
    lane_lo = lax.broadcasted_iota(jnp.int32, (L, LANES), 1) < SSD_HEAD_DIM

    b0 = inner
    c0 = inner + n_groups * d_state
    for g in range(n_groups):
        bg = xa_ref[:, b0 + g * d_state:b0 + (g + 1) * d_state]
        cg = xa_ref[:, c0 + g * d_state:c0 + (g + 1) * d_state]
        cb = lax.dot_general(cg.astype(BF16), bg.astype(BF16), (((1,), (1,)), ((), ())),
                             preferred_element_type=F32)
        bg_t = bg.T
        for pp in range(g * pairs_per_group, (g + 1) * pairs_per_group):
            xs_pair = xa_ref[:, pp * LANES:(pp + 1) * LANES].astype(BF16)
            st_old = state_ref[pp]
            rhs = jnp.concatenate([xs_pair, st_old.astype(BF16)], axis=0)
            ys, sts = [], []
            for side in range(2):
                h = 2 * pp + side
                col = jnp.broadcast_to(acs[:, h:h + 1], (L, L))
                row = acs_t[h:h + 1, :]
                dt_row = dt_t[h:h + 1, :]
                last = acs[L - 1:L, h:h + 1]
                dec = jnp.exp(jnp.where(causal, col - row, MASK_VALUE))
                m_in = (cb * dec * dt_row).astype(BF16)
                c_w = (cg * jnp.exp(col[:, :d_state])).astype(BF16)
                lhs = jnp.concatenate([m_in, c_w], axis=1)
                ys.append(jnp.dot(lhs, rhs, preferred_element_type=F32))
                b_w = (bg_t * (jnp.exp(last - row) * dt_row)).astype(BF16)
                st_new = jnp.dot(b_w, xs_pair, preferred_element_type=F32)
                sts.append(st_old * jnp.exp(last) + st_new)
            yacc_ref[:, pp * LANES:(pp + 1) * LANES] = jnp.where(lane_lo, ys[0], ys[1])
            state_ref[pp] = jnp.where(lane_lo, sts[0], sts[1])

    for g in range(n_groups):
        cs = slice(g * group_width, (g + 1) * group_width)
        zz = z_ref[:, cs].astype(F32)
        yv = (yacc_ref[:, cs] + dskip_ref[:, cs] * xa_ref[:, cs]) * _silu(zz)
        ms = jnp.mean(yv * yv, axis=-1, keepdims=True)
        y_ref[:, cs] = (yv * lax.rsqrt(ms + NORM_EPS) * g_ref[:, cs]).astype(y_ref.dtype)


def _ssd_branch(proj, dt_raw, conv_w, conv_b, dt_bias, a_log, d_skip, norm_g, *, batch, seq,
                z_block, xbc_block):
    t = proj.shape[0]
    inner = norm_g.shape[0]
    width = conv_w.shape[1]
    heads = inner // SSD_HEAD_DIM
    n_chunks = seq // SSD_CHUNK
    L = SSD_CHUNK

    def pad_heads(v):
        return jnp.pad(v.astype(F32), (0, LANES - heads)).reshape(1, LANES)

    convw = jnp.pad(conv_w.astype(F32), ((0, SUBLANES - SSD_CONV), (0, 0)))
    dskip = jnp.repeat(d_skip.astype(F32), SSD_HEAD_DIM).reshape(1, inner)
    row = lambda b, c: (b * n_chunks + c, 0)
    const = lambda b, c: (0, 0)
    shift = jnp.asarray(_shift_matrix(), BF16)
    est = (2 * L * (inner + width) * 2 + 2 * L * LANES * 4 + 2 * L * inner * 2 + 2 * shift.size * 2
           + (inner // LANES) * SSD_STATE * LANES * 4 + L * (2 * width + inner) * 4 + (8 << 20))
    kern = functools.partial(_ssd_kernel, inner=inner, n_groups=SSD_GROUPS, d_state=SSD_STATE)
    return pl.pallas_call(
        kern,
        out_shape=jax.ShapeDtypeStruct((t, inner), BF16),
        grid=(batch, n_chunks),
        in_specs=[pl.BlockSpec((L, inner), lambda b, c: (b * n_chunks + c, z_block)),
                  pl.BlockSpec((L, width), lambda b, c: (b * n_chunks + c, xbc_block)),
                  pl.BlockSpec((L, LANES), row),
                  pl.BlockSpec(shift.shape, const),
                  pl.BlockSpec((SUBLANES, width), const),
                  pl.BlockSpec((1, width), const),
                  pl.BlockSpec((1, LANES), const),
                  pl.BlockSpec((1, LANES), const),
                  pl.BlockSpec((1, inner), const),
                  pl.BlockSpec((1, inner), const)],
        out_specs=pl.BlockSpec((L, inner), row),
        scratch_shapes=[pltpu.VMEM((2 * L, width), BF16),
                        pltpu.VMEM((inner // LANES, SSD_STATE, LANES), F32),
                        pltpu.VMEM((L, width), F32),
                        pltpu.VMEM((L, inner), F32)],
        compiler_params=_params(("parallel", "arbitrary"), est),
        name="ssd_scan",
    )(proj, proj, dt_raw, shift, convw, conv_b.astype(F32).reshape(1, width), pad_heads(dt_bias),
      pad_heads(a_log), dskip, norm_g.astype(F32).reshape(1, inner))


def _bucket_tiles(tq):
    max_exact = REL_BUCKETS // 2
    qi = np.arange(tq)[:, None]
    ki = np.arange(tq)[None, :]

    def bucket(dist):
        d = np.maximum(dist, 1).astype(np.float32)
        large = max_exact + (np.log(d / np.float32(max_exact)) / np.float32(math.log(REL_MAX_DIST / max_exact))
                             * np.float32(REL_BUCKETS - max_exact)).astype(np.int32)
        large = np.minimum(large, REL_BUCKETS - 1)
        return np.where(dist < max_exact, dist, large).astype(np.int32)

    diag = np.where(qi >= ki, bucket(np.maximum(qi - ki, 0)), -1)
    prev = bucket(tq + qi - ki)
    far = bucket(np.arange(tq + 1, 1 << 16))
    assert (far == REL_BUCKETS - 1).all()
    return np.stack([diag, prev]).astype(np.int32)


def _bias_kernel(rb_ref, bucket_ref, o_ref):
    h = pl.program_id(0)
    bk = bucket_ref[...]
    acc = jnp.zeros(bk.shape, F32)
    for b in range(REL_BUCKETS):
        acc = jnp.where(bk == b, rb_ref[b, h], acc)
    o_ref[0] = jnp.where(bk < 0, MASK_VALUE, (acc - rb_ref[REL_BUCKETS - 1, h]) * LOG2E)


def _bias_tiles(rel_bias, tq):
    buckets = jnp.asarray(_bucket_tiles(tq))
    return pl.pallas_call(
        _bias_kernel,
        out_shape=jax.ShapeDtypeStruct((DIFF_HEADS, 2, tq, tq), F32),
        grid=(DIFF_HEADS,),
        in_specs=[pl.BlockSpec(memory_space=pltpu.SMEM),
                  pl.BlockSpec((2, tq, tq), lambda h: (0, 0, 0))],
        out_specs=pl.BlockSpec((1, 2, tq, tq), lambda h: (h, 0, 0, 0)),
        compiler_params=_params(("arbitrary",), 8 * tq * tq * 4),
        name="t5_bias_tiles",
    )(rel_bias.astype(F32), buckets)


def _attn_kernel(lam_ref, q_ref, k_ref, v_ref, bias_ref, g_ref, o_ref,
                 q2_ref, vaug_ref, m_ref, acc_ref, *, tq, rows, lam_init):
    i = pl.program_id(2)
    dh = LANES // 2

    @pl.when(i == 0)
    def _():
        vaug_ref[:, 0:LANES] = v_ref[...]
        vaug_ref[:, LANES:2 * LANES] = jnp.ones((v_ref.shape[0], LANES), BF16)

    lane = lax.broadcasted_iota(jnp.int32, (tq, LANES), 1)
    qs = (q_ref[...].astype(F32) * (dh ** -0.5 * LOG2E)).astype(BF16)
    zero = jnp.zeros_like(qs)
    q2_ref[0:tq] = jnp.where(lane < dh, qs, zero)
    q2_ref[tq:2 * tq] = jnp.where(lane >= dh, qs, zero)
    m_ref[...] = jnp.full(m_ref.shape, MASK_VALUE, F32)
    acc_ref[...] = jnp.zeros_like(acc_ref)

    def step(j, bias_idx):
        start = pl.multiple_of(j * tq, tq)
        for r0 in range(0, 2 * tq, rows):
            rs = slice(r0, r0 + rows)
            q0 = r0 % tq
            klen = q0 + rows if bias_idx == 0 else tq
            kb = k_ref[pl.ds(start, klen), :]
            vb = vaug_ref[pl.ds(start, klen), :]
            s = lax.dot_general(q2_ref[rs], kb, (((1,), (1,)), ((), ())),
                                preferred_element_type=F32)
            if bias_idx is not None:
                s = s + bias_ref[0, bias_idx, q0:q0 + rows, 0:klen]
            m_old = m_ref[rs]
            m_new = jnp.maximum(m_old, jnp.max(s, axis=1, keepdims=True))
            alpha = jnp.exp2(m_old - m_new)
            p = jnp.exp2(s - jnp.concatenate([m_new] * (klen // LANES), axis=1))
            pv = jnp.dot(p.astype(BF16), vb, preferred_element_type=F32)
            acc_ref[rs] = jnp.concatenate([alpha, alpha], axis=1) * acc_ref[rs] + pv
            m_ref[rs] = m_new

    def far_step(j, carry):
        step(j, None)
        return carry

    lax.fori_loop(0, jnp.maximum(i - 1, 0), far_step, 0)

    @pl.when(i >= 1)
    def _():
        step(i - 1, 1)

    step(i, 0)

    lv = lam_ref[...]
    s1 = jnp.sum(lv[0:1] * lv[1:2], axis=1, keepdims=True)
    s2 = jnp.sum(lv[2:3] * lv[3:4], axis=1, keepdims=True)
    lam = jnp.exp(s1) - jnp.exp(s2) + lam_init
    o1 = acc_ref[0:tq, 0:LANES] / acc_ref[0:tq, LANES:2 * LANES]
    o2 = acc_ref[tq:2 * tq, 0:LANES] / acc_ref[tq:2 * tq, LANES:2 * LANES]
    o = o1 - lam * o2
    ms = jnp.mean(o * o, axis=-1, keepdims=True)
    o_ref[...] = (o * lax.rsqrt(ms + NORM_EPS) * g_ref[...] * (1.0 - lam_init)).astype(o_ref.dtype)


def _diff_attention(proj, lam_vecs, bias_tiles, subln_g, *, batch, seq, tq, layer_idx, q_block):
    t = proj.shape[0]
    nq = seq // tq
    rows = min(128, tq)
    lam_init = 0.8 - 0.6 * math.exp(-0.3 * layer_idx)
    kern = functools.partial(_attn_kernel, tq=tq, rows=rows, lam_init=lam_init)
    est = (2 * tq * LANES * 2 + 4 * seq * LANES * 2 + 4 * tq * tq * 4 + 2 * tq * LANES * 2
           + 2 * tq * LANES * 2 + seq * 2 * LANES * 2 + 2 * tq * LANES * 4 + 2 * tq * 2 * LANES * 4
           + 8 * rows * tq * 4)
    return pl.pallas_call(
        kern,
        out_shape=jax.ShapeDtypeStruct((t, DIFF_HEADS * LANES), BF16),
        grid=(batch, DIFF_HEADS, nq),
        in_specs=[pl.BlockSpec((SUBLANES, LANES), lambda b, h, i: (0, 0)),
                  pl.BlockSpec((tq, LANES), lambda b, h, i: (b * nq + i, q_block + h)),
                  pl.BlockSpec((seq, LANES), lambda b, h, i: (b, q_block + DIFF_HEADS + h)),
                  pl.BlockSpec((seq, LANES), lambda b, h, i: (b, q_block + 2 * DIFF_HEADS + h)),
                  pl.BlockSpec((1, 2, tq, tq), lambda b, h, i: (h, 0, 0, 0)),
                  pl.BlockSpec((1, LANES), lambda b, h, i: (0, 0))],
        out_specs=pl.BlockSpec((tq, LANES), lambda b, h, i: (b * nq + i, h)),
        scratch_shapes=[pltpu.VMEM((2 * tq, LANES), BF16),
                        pltpu.VMEM((seq, 2 * LANES), BF16),
                        pltpu.VMEM((2 * tq, LANES), F32),
                        pltpu.VMEM((2 * tq, 2 * LANES), F32)],
        compiler_params=_params(("parallel", "parallel", "arbitrary"), est),
        name="diff_attention",
    )(lam_vecs, proj, proj, proj, bias_tiles, subln_g)


def _mix_kernel(x_ref, y_ref, o_ref, gate_ref, wssd_ref, wdiff_ref, wmix_ref, g_ref, b_ref, out_ref):
    d = x_ref.shape[1]
    ssd = jnp.dot(y_ref[...], wssd_ref[...], preferred_element_type=F32)
    dif = jnp.dot(o_ref[...], wdiff_ref[...], preferred_element_type=F32)
    merged = gate_ref[:, 0:d].astype(F32) * ssd + gate_ref[:, d:2 * d].astype(F32) * dif
    mixed = jnp.dot(merged.astype(BF16), wmix_ref[...], preferred_element_type=F32)
    out_ref[...] = _layer_norm(DN_ALPHA * x_ref[...] + mixed, g_ref[...], b_ref[...])


def _mix(x, y_ssd, o_diff, proj, w_ssd, w_diff, w_mix, ln_g, ln_b, *, tm, gate_block):
    t, d = x.shape
    tm = min(tm, t)
    inner = y_ssd.shape[1]
    dw = o_diff.shape[1]
    row = lambda i: (i, 0)
    est = (2 * tm * (d * 4 + inner * 2 + dw * 2 + 2 * d * 2 + d * 4)
           + (inner * d + dw * d + d * d) * 2 + 6 * tm * d * 4)
    return pl.pallas_call(
        _mix_kernel,
        out_shape=jax.ShapeDtypeStruct((t, d), F32),
        grid=(t // tm,),
        in_specs=[pl.BlockSpec((tm, d), row),
                  pl.BlockSpec((tm, inner), row),
                  pl.BlockSpec((tm, dw), row),
                  pl.BlockSpec((tm, 2 * d), lambda i: (i, gate_block)),
                  _resident((inner, d)), _resident((dw, d)), _resident((d, d)),
                  _resident((1, d)), _resident((1, d))],
        out_specs=pl.BlockSpec((tm, d), row),
        compiler_params=_params(("parallel",), est),
        name="mix_ln1",
    )(x, y_ssd, o_diff, proj, w_ssd, w_diff, w_mix, ln_g, ln_b)


def _pack_bf16_pairs(v):
    w = v.shape[1] // 2
    lo = lax.bitcast_convert_type(v[:, :w].astype(BF16).astype(F32), jnp.int32)
    hi = lax.bitcast_convert_type(v[:, w:].astype(BF16).astype(F32), jnp.int32)
    return jnp.bitwise_or(hi, lax.shift_right_logical(lo, 16))


def _unpack_bf16_pairs(words):
    lo = lax.bitcast_convert_type(lax.shift_left(words, 16), F32)
    hi = lax.bitcast_convert_type(jnp.bitwise_and(words, -65536), F32)
    return jnp.concatenate([lo, hi], axis=1)


def _xattn_kernel(x_ref, kv_ref, wq_ref, wo_ref, g_ref, b_ref, out_ref, packed_ref, o_scr, *, heads):
    d = x_ref.shape[1]
    dh = d // heads
    xv = x_ref[...]
    q = jnp.dot(xv.astype(BF16), wq_ref[...], preferred_element_type=F32)
    q = (q * (dh ** -0.5)).astype(BF16)
    for h in range(heads):
        kh = kv_ref[:, h * dh:(h + 1) * dh]
        vh = kv_ref[:, d + h * dh:d + (h + 1) * dh]
        s = lax.dot_general(q[:, h * dh:(h + 1) * dh], kh, (((1,), (1,)), ((), ())),
                            preferred_element_type=F32)
        p = jnp.exp(s - jnp.max(s, axis=1, keepdims=True))
        p = p / jnp.sum(p, axis=1, keepdims=True)
        o_scr[:, h * dh:(h + 1) * dh] = jnp.dot(p.astype(BF16), vh, preferred_element_type=F32).astype(BF16)
    att = jnp.dot(o_scr[...], wo_ref[...], preferred_element_type=F32)
    y = _layer_norm(DN_ALPHA * xv + att, g_ref[...], b_ref[...])
    out_ref[...] = y
    packed_ref[...] = _pack_bf16_pairs(y)


def _cross_attention(x, kv, w_cq, w_co, ln_g, ln_b, *, batch, seq, tq):
    t, d = x.shape
    mem_len = kv.shape[0] // batch
    tq = min(tq, seq)
    nq = seq // tq
    est = (2 * tq * d * 4 * 2 + 2 * tq * d * 2 + 2 * mem_len * 2 * d * 2 + 2 * d * d * 2 + tq * d * 2
           + 8 * tq * d * 4)
    row = lambda b, i: (b * nq + i, 0)
    return pl.pallas_call(
        functools.partial(_xattn_kernel, heads=MEM_HEADS),
        out_shape=(jax.ShapeDtypeStruct((t, d), F32), jax.ShapeDtypeStruct((t, d // 2), jnp.int32)),
        grid=(batch, nq),
        in_specs=[pl.BlockSpec((tq, d), row),
                  pl.BlockSpec((mem_len, 2 * d), lambda b, i: (b, 0)),
                  _resident((d, d)), _resident((d, d)), _resident((1, d)), _resident((1, d))],
        out_specs=(pl.BlockSpec((tq, d), row), pl.BlockSpec((tq, d // 2), row)),
        scratch_shapes=[pltpu.VMEM((tq, d), BF16)],
        compiler_params=_params(("parallel", "parallel"), est),
        name="cross_attention_ln2",
    )(x, kv, w_cq, w_co, ln_g, ln_b)


def _router_kernel(x_ref, wr_ref, rb_ref, idx_ref, rank_ref, gates_ref, cnt_ref, run_ref, *, n_experts):
    tm = x_ref.shape[0]
    n_groups = N_EXPERT_GROUPS
    per = n_experts // n_groups

    @pl.when(pl.program_id(0) == 0)
    def _():
        run_ref[...] = jnp.zeros_like(run_ref)

    xv = x_ref[...]
    x_hi = xv.astype(BF16)
    x_lo = (xv - x_hi.astype(F32)).astype(BF16)
    nt = (((1,), (1,)), ((), ()))
    logits = (lax.dot_general(wr_ref[0], x_hi, nt, preferred_element_type=F32)
              + lax.dot_general(wr_ref[0], x_lo, nt, preferred_element_type=F32)
              + lax.dot_general(wr_ref[1], x_hi, nt, preferred_element_type=F32))
    sc = _sigmoid(logits[0:n_experts]).reshape(per, n_groups, tm)
    choice = sc + rb_ref[0:n_experts].reshape(per, n_groups, 1)
    neg = -jnp.inf
    member = lax.broadcasted_iota(jnp.int32, (per, n_groups, tm), 0)
    group3 = lax.broadcasted_iota(jnp.int32, (per, n_groups, tm), 1)
    m1 = jnp.max(choice, axis=0, keepdims=True)
    i1 = jnp.min(jnp.where(choice == m1, member, per), axis=0, keepdims=True)
    m2 = jnp.max(jnp.where(member == i1, neg, choice), axis=0, keepdims=True)
    gscore = (m1 + m2)
    gsel = jnp.zeros((1, n_groups, tm), F32)
    gidx = lax.broadcasted_iota(jnp.int32, (1, n_groups, tm), 1)
    cur = gscore
    for _ in range(TOP_GROUPS):
        mx = jnp.max(cur, axis=1, keepdims=True)
        ix = jnp.min(jnp.where(cur == mx, gidx, n_groups), axis=1, keepdims=True)
        hit = gidx == ix
        gsel = jnp.where(hit, 1.0, gsel)
        cur = jnp.where(hit, neg, cur)
    cur = jnp.where(gsel > 0.0, choice, neg)
    eidx = group3 * per + member
    esel = jnp.zeros((per, n_groups, tm), F32)
    hits, idx_rows = [], []
    for _ in range(TOP_K):
        mx = jnp.max(jnp.max(cur, axis=0, keepdims=True), axis=1, keepdims=True)
        ix = jnp.min(jnp.min(jnp.where(cur == mx, eidx, n_experts), axis=0, keepdims=True),
                     axis=1, keepdims=True)
        hit = eidx == ix
        esel = jnp.where(hit, 1.0, esel)
        cur = jnp.where(hit, neg, cur)
        hits.append(hit)
        idx_rows.append(ix.reshape(1, tm))
    w = esel * sc
    tot = jnp.sum(jnp.sum(w, axis=0, keepdims=True), axis=1, keepdims=True)
    gw = w / tot * ROUTED_SCALE

    r_i = lax.broadcasted_iota(jnp.int32, (tm, tm), 0)
    c_i = lax.broadcasted_iota(jnp.int32, (tm, tm), 1)
    upper = jnp.where(r_i <= c_i, 1.0, 0.0).astype(BF16)
    pref = jnp.dot(esel.reshape(n_experts, tm).astype(BF16), upper, preferred_element_type=F32)
    run = run_ref[...]
    rank3 = (jnp.concatenate([run] * (tm // LANES), axis=1) + pref - 1.0).reshape(per, n_groups, tm)

    def pick(hit, vals):
        return jnp.sum(jnp.sum(jnp.where(hit, vals, 0.0), axis=0, keepdims=True), axis=1).reshape(1, tm)

    idx_ref[...] = jnp.concatenate(idx_rows, axis=0)
    rank_ref[...] = jnp.concatenate([pick(h, rank3) for h in hits], axis=0).astype(jnp.int32)
    wk = jnp.concatenate([pick(h, gw) for h in hits] + [jnp.zeros((LANES - TOP_K, tm), F32)], axis=0)
    gates_ref[...] = wk.T
    run = run + jnp.broadcast_to(pref[:, tm - 1:tm], run.shape)
    run_ref[...] = run
    cnt_ref[...] = run


def _router(x, w_router, router_bias, *, tm):
    t, d = x.shape
    n_experts = w_router.shape[1]
    per = n_experts // N_EXPERT_GROUPS
    tm = min(tm, t)

    def member_major(v):
        return v.reshape(N_EXPERT_GROUPS, per, *v.shape[1:]).swapaxes(0, 1).reshape(v.shape)

    wt = jnp.pad(member_major(w_router.astype(F32).T), ((0, LANES - n_experts), (0, 0)))
    w_hi = wt.astype(BF16)
    wr = jnp.stack([w_hi, (wt - w_hi.astype(F32)).astype(BF16)])
    rb = jnp.pad(member_major(router_bias.astype(F32)), (0, LANES - n_experts)).reshape(LANES, 1)
    est = (2 * tm * d * 4 + 2 * LANES * d * 2 + 2 * tm * LANES * 4 + 60 * n_experts * tm * 4 + 3 * tm * tm * 4)
    pick_spec = pl.BlockSpec((TOP_K, tm), lambda i: (0, i))
    idx_t, rank_t, gates, counts = pl.pallas_call(
        functools.partial(_router_kernel, n_experts=n_experts),
        out_shape=(jax.ShapeDtypeStruct((TOP_K, t), jnp.int32),
                   jax.ShapeDtypeStruct((TOP_K, t), jnp.int32),
                   jax.ShapeDtypeStruct((t, LANES), F32),
                   jax.ShapeDtypeStruct((n_experts, LANES), F32)),
        grid=(t // tm,),
        in_specs=[pl.BlockSpec((tm, d), lambda i: (i, 0)),
                  _resident((2, LANES, d)), _resident((LANES, 1))],
        out_specs=(pick_spec, pick_spec,
                   pl.BlockSpec((tm, LANES), lambda i: (i, 0)),
                   pl.BlockSpec((n_experts, LANES), lambda i: (0, 0))),
        scratch_shapes=[pltpu.VMEM((n_experts, LANES), F32)],
        compiler_params=_params(("arbitrary",), est),
        name="router",
    )(x, wr, rb)
    counts = counts[:, 0].astype(jnp.int32).reshape(per, N_EXPERT_GROUPS).T.reshape(n_experts)
    return idx_t, rank_t, gates, counts


def _positions_kernel(cnt_ref, idx_ref, rank_ref, pos_ref, texp_ref, off_ref, *, n_experts, rows, n_tiles):
    @pl.when(pl.program_id(0) == 0)
    def _():
        def per_expert(e, toff):
            off_ref[e] = toff * rows
            nt = lax.shift_right_logical(cnt_ref[e] + (rows - 1), int(math.log2(rows)))

            def fill(j, c):
                texp_ref[toff + j] = e
                return c

            lax.fori_loop(0, nt, fill, 0)
            return toff + nt

        n_used = lax.fori_loop(0, n_experts, per_expert, 0)

        def tail(j, c):
            texp_ref[j] = n_experts - 1
            return c

        lax.fori_loop(n_used, n_tiles, tail, 0)
        texp_ref[n_tiles] = n_used

    idx = idx_ref[...]
    pos = rank_ref[...]
    for e in range(n_experts):
        pos = pos + jnp.where(idx == e, off_ref[e], 0)
    pos_ref[...] = pos


def _positions(counts, idx_t, rank_t, *, rows, n_tiles, tm):
    n_experts = counts.shape[0]
    k, t = idx_t.shape
    tm = min(tm, t)
    spec = pl.BlockSpec((k, tm), lambda i: (0, i))
    kern = functools.partial(_positions_kernel, n_experts=n_experts, rows=rows, n_tiles=n_tiles)
    return pl.pallas_call(
        kern,
        out_shape=(jax.ShapeDtypeStruct((k, t), jnp.int32),
                   jax.ShapeDtypeStruct((n_tiles + 1,), jnp.int32)),
        grid=(t // tm,),
        in_specs=[pl.BlockSpec(memory_space=pltpu.SMEM), spec, spec],
        out_specs=(spec, pl.BlockSpec(memory_space=pltpu.SMEM)),
        scratch_shapes=[pltpu.SMEM((n_experts,), jnp.int32)],
        compiler_params=_params(("arbitrary",), 16 * k * tm * 4),
        name="moe_positions",
    )(counts, idx_t, rank_t)


def _sc_mesh():
    return plsc.VectorSubcoreMesh(core_axis_name="c", subcore_axis_name="s",
                                  num_cores=SC_CORES, num_subcores=SC_SUBCORES)


def _sc_dispatch(packed, pos_chunks, n_rows):
    t, w = packed.shape
    n_chunks, k, n = pos_chunks.shape
    per_worker = n_chunks // (SC_CORES * SC_SUBCORES)

    @functools.partial(
        pl.kernel, mesh=_sc_mesh(),
        out_type=jax.ShapeDtypeStruct((n_rows, w), packed.dtype),
        scratch_types=[pltpu.VMEM((k, n), jnp.int32), pltpu.VMEM((n, w), packed.dtype),
                       pltpu.SemaphoreType.DMA],
        name="moe_dispatch_sc",
    )
    def scatter_rows(x_hbm, pos_hbm, out_hbm, idx_v, rows_v, sem):
        wid = lax.axis_index("s") * SC_CORES + lax.axis_index("c")

        @pl.loop(0, per_worker)
        def _(step):
            c = wid * per_worker + step
            pltpu.sync_copy(pos_hbm.at[c], idx_v)
            pltpu.sync_copy(x_hbm.at[pl.ds(c * n, n)], rows_v)
            copies = [pltpu.async_copy(rows_v, out_hbm.at[idx_v.at[kk]], sem) for kk in range(k)]
            for cp in copies:
                cp.wait()

    return scatter_rows(packed, pos_chunks)


def _sc_combine(sorted_rows, pos_chunks):
    _, w = sorted_rows.shape
    n_chunks, k, n = pos_chunks.shape
    per_worker = n_chunks // (SC_CORES * SC_SUBCORES)

    @functools.partial(
        pl.kernel, mesh=_sc_mesh(),
        out_type=jax.ShapeDtypeStruct((k, n_chunks * n, w), sorted_rows.dtype),
        scratch_types=[pltpu.VMEM((k, n), jnp.int32), pltpu.VMEM((n, w), sorted_rows.dtype),
                       pltpu.SemaphoreType.DMA],
        name="moe_combine_sc",
    )
    def gather_rows(y_hbm, pos_hbm, out_hbm, idx_v, rows_v, sem):
        wid = lax.axis_index("s") * SC_CORES + lax.axis_index("c")

        @pl.loop(0, per_worker)
        def _(step):
            c = wid * per_worker + step
            pltpu.sync_copy(pos_hbm.at[c], idx_v)
            for kk in range(k):
                pltpu.async_copy(y_hbm.at[idx_v.at[kk]], rows_v, sem).wait()
                pltpu.sync_copy(rows_v, out_hbm.at[kk, pl.ds(c * n, n)])

    return gather_rows(sorted_rows, pos_chunks)


def _expert_kernel(texp_ref, xs_ref, wgu_ref, wdn_ref, ys_ref, *, n_tiles):
    ff = wdn_ref.shape[1]

    @pl.when(pl.program_id(0) < texp_ref[n_tiles])
    def _():
        half = xs_ref.shape[0] // 2
        for r0 in (0, half):
            xv = _unpack_bf16_pairs(xs_ref[r0:r0 + half]).astype(BF16)
            gu = jnp.dot(xv, wgu_ref[0], preferred_element_type=F32)
            hid = _silu(gu[:, 0:ff]) * gu[:, ff:2 * ff]
            ys_ref[r0:r0 + half] = _pack_bf16_pairs(
                jnp.dot(hid.astype(BF16), wdn_ref[0], preferred_element_type=F32))


def _experts(tile_expert, sorted_rows, w_gu, w_dn, *, rows):
    n_rows, w = sorted_rows.shape
    n_tiles = n_rows // rows
    _, d, ff2 = w_gu.shape
    ff = ff2 // 2
    tile = lambda j, te: (jnp.minimum(j, te[n_tiles] - 1), 0)
    est = 4 * rows * w * 4 + 2 * (d * ff2 + ff * d) * 2 + 4 * rows * d * 4 + 4 * rows * ff2 * 4
    return pl.pallas_call(
        functools.partial(_expert_kernel, n_tiles=n_tiles),
        out_shape=jax.ShapeDtypeStruct((n_rows, w), sorted_rows.dtype),
        grid_spec=pltpu.PrefetchScalarGridSpec(
            num_scalar_prefetch=1,
            grid=(n_tiles,),
            in_specs=[pl.BlockSpec((rows, w), tile),
                      pl.BlockSpec((1, d, ff2), lambda j, te: (te[j], 0, 0)),
                      pl.BlockSpec((1, ff, d), lambda j, te: (te[j], 0, 0))],
            out_specs=pl.BlockSpec((rows, w), tile)),
        compiler_params=_params(("arbitrary",), est),
        name="moe_experts",
    )(tile_expert, sorted_rows, w_gu, w_dn)


def _moe_out_kernel(x_ref, yk_ref, gates_ref, wsgu_ref, wsdn_ref, g_ref, b_ref, out_ref):
    ff = wsdn_ref.shape[0]
    xv = x_ref[...]
    gu = jnp.dot(xv.astype(BF16), wsgu_ref[...], preferred_element_type=F32)
    hid = _silu(gu[:, 0:ff]) * gu[:, ff:2 * ff]
    acc = jnp.dot(hid.astype(BF16), wsdn_ref[...], preferred_element_type=F32)
    for k in range(yk_ref.shape[0]):
        acc = acc + gates_ref[:, k:k + 1] * _unpack_bf16_pairs(yk_ref[k])
    out_ref[...] = _layer_norm(DN_ALPHA * xv + acc, g_ref[...], b_ref[...])


def _moe_out(x, yk, gates, w_sgu, w_sdn, ln_g, ln_b, *, tm):
    t, d = x.shape
    k, _, w = yk.shape
    sff2 = w_sgu.shape[1]
    tm = min(tm, t)
    est = (2 * tm * d * 4 * 2 + 2 * k * tm * w * 4 + 2 * tm * LANES * 4
           + (d * sff2 + (sff2 // 2) * d) * 2 + 6 * tm * d * 4)
    return pl.pallas_call(
        _moe_out_kernel,
        out_shape=jax.ShapeDtypeStruct((t, d), F32),
        grid=(t // tm,),
        in_specs=[pl.BlockSpec((tm, d), lambda i: (i, 0)),
                  pl.BlockSpec((k, tm, w), lambda i: (0, i, 0)),
                  pl.BlockSpec((tm, LANES), lambda i: (i, 0)),
                  _resident((d, sff2)), _resident((sff2 // 2, d)),
                  _resident((1, d)), _resident((1, d))],
        out_specs=pl.BlockSpec((tm, d), lambda i: (i, 0)),
        compiler_params=_params(("parallel",), est),
        name="moe_out_ln3",
    )(x, yk, gates, w_sgu, w_sdn, ln_g, ln_b)


def _moe(x, packed, w_router, router_bias, w_gu, w_dn, w_sgu, w_sdn, ln_g, ln_b):
    t, d = x.shape
    n_experts = w_router.shape[1]
    rows = min(MOE_ROWS_PER_TILE, t)
    n_tiles = (t * TOP_K) // rows + n_experts
    idx_t, rank_t, gates, counts = _router(x, w_router, router_bias, tm=512)
    pos_t, tile_expert = _positions(counts, idx_t, rank_t, rows=rows, n_tiles=n_tiles, tm=2048)
    n = SC_CHUNK_TOKENS
    pos_chunks = pos_t.reshape(TOP_K, t // n, n).transpose(1, 0, 2)
    sorted_x = _sc_dispatch(packed, pos_chunks, n_tiles * rows)
    sorted_y = _experts(tile_expert, sorted_x, w_gu, w_dn, rows=rows)
    yk = _sc_combine(sorted_y, pos_chunks)
    return _moe_out(x, yk, gates, w_sgu, w_sdn, ln_g, ln_b, tm=256)


def kernel(x, mem, w_in, conv_w, conv_b, dt_bias, a_log, d_skip, ssd_norm_g, lambda_q1, lambda_k1, lambda_q2, lambda_k2, subln_g, rel_bias, w_ssd_br, w_diff_br, w_mix_out, ln1_g, ln1_b, w_cq, w_ckv, w_co, ln2_g, ln2_b, w_router, router_bias, w_exp_gu, w_exp_down, w_sh_gu, w_sh_down, ln3_g, ln3_b):
    batch, seq, d = x.shape
    depth = w_in.shape[0]
    inner = w_ssd_br.shape[1]
    xbc_width = conv_w.shape[2]
    heads = dt_bias.shape[1]
    diff_width = w_diff_br.shape[1]
    o_z, o_xbc = inner, inner + xbc_width
    o_dt = o_xbc + heads
    o_v = o_dt + 3 * diff_width
    t = batch * seq
    tq_attn = min(512, seq)

    def vec(v):
        return v.astype(F32).reshape(1, -1)

    xt = x.reshape(t, d)
    memt = mem.reshape(-1, d)
    bias_tiles = _bias_tiles(rel_bias, tq_attn)

    for l in range(depth):
        wl = w_in[l]
        w_big = jnp.concatenate([wl[:, o_z:o_xbc], wl[:, o_dt:o_v], wl[:, :o_z], wl[:, o_v:]],
                                axis=1).astype(BF16)
        c_qkv, c_z, c_gate = xbc_width, xbc_width + 3 * diff_width, xbc_width + 3 * diff_width + inner
        assert c_z % inner == 0 and c_gate % (2 * d) == 0 and c_qkv % LANES == 0
        tn = 1024
        proj = _matmul(xt, w_big, BF16, tm=1024, tn=tn, sigmoid_from=c_gate // tn, name="in_proj")
        w_dt = jnp.pad(wl[:, o_xbc:o_dt], ((0, 0), (0, LANES - heads))).astype(BF16)
        dt_raw = _matmul(xt, w_dt, F32, tm=1024, tn=LANES, name="in_proj_dt")

        y_ssd = _ssd_branch(proj, dt_raw, conv_w[l], conv_b[l], dt_bias[l], a_log[l], d_skip[l],
                            ssd_norm_g[l], batch=batch, seq=seq, z_block=c_z // inner, xbc_block=0)

        lam_vecs = jnp.pad(jnp.stack([lambda_q1[l], lambda_k1[l], lambda_q2[l], lambda_k2[l]]).astype(F32),
                           ((0, SUBLANES - 4), (0, LANES - lambda_q1.shape[1])))
        o_diff = _diff_attention(proj, lam_vecs, bias_tiles, vec(subln_g[l]), batch=batch, seq=seq,
                                 tq=tq_attn, layer_idx=l, q_block=c_qkv // LANES)

        xt = _mix(xt, y_ssd, o_diff, proj, w_ssd_br[l].astype(BF16), w_diff_br[l].astype(BF16),
                  w_mix_out[l].astype(BF16), vec(ln1_g[l]), vec(ln1_b[l]), tm=512,
                  gate_block=c_gate // (2 * d))

        kv = _matmul(memt, w_ckv[l].astype(BF16), BF16, tm=1024, tn=1024, name="mem_kv_proj")
        xt, packed = _cross_attention(xt, kv, w_cq[l].astype(BF16), w_co[l].astype(BF16), vec(ln2_g[l]),
                                      vec(ln2_b[l]), batch=batch, seq=seq, tq=512)

        xt = _moe(xt, packed, w_router[l], router_bias[l], w_exp_gu[l].astype(BF16),
                  w_exp_down[l].astype(BF16), w_sh_gu[l].astype(BF16), w_sh_down[l].astype(BF16),
                  vec(ln3_g[l]), vec(ln3_b[l]))
    return xt.reshape(batch, seq, d)
```

```python
import functools
import math

import numpy as np
import jax
import jax.numpy as jnp
from jax import lax
from jax.experimental import pallas as pl
from jax.experimental.pallas import tpu as pltpu
from jax.experimental.pallas import tpu_sc as plsc

F32 = jnp.float32
BF16 = jnp.bfloat16

SSD_HEAD_DIM = 64
SSD_GROUPS = 4
SSD_STATE = 128
SSD_CONV = 4
SSD_CHUNK = 128
DIFF_HEADS = 8
REL_BUCKETS = 32
REL_MAX_DIST = 128
MEM_HEADS = 4
TOP_K = 8
N_EXPERT_GROUPS = 8
TOP_GROUPS = 4
ROUTED_SCALE = 2.5
NORM_EPS = 1e-5
DEPTH = 1
DN_ALPHA = (2.0 * DEPTH) ** 0.25

LANES = 128
SUBLANES = 8
VMEM_CAP_BYTES = 64 * 1024 * 1024
MASK_VALUE = -1e30
CONV_TAIL = 16
LOG2E = math.log2(math.e)
SC_CORES = 2
SC_SUBCORES = 16
SC_CHUNK_TOKENS = 128
MOE_SPLITS = 2
MOE_ROWS_PER_TILE = 1024


def _vmem_limit(estimate_bytes):
    return int(min(estimate_bytes * 5 // 4 + (4 << 20), VMEM_CAP_BYTES - (6 << 20)))


def _params(semantics, vmem_estimate):
    return pltpu.CompilerParams(dimension_semantics=semantics,
                                vmem_limit_bytes=_vmem_limit(vmem_estimate))


def _resident(shape):
    nd = len(shape)
    return pl.BlockSpec(shape, lambda *_: (0,) * nd, pipeline_mode=pl.Buffered(1))


def _layer_norm(v, g, b):
    mu = jnp.mean(v, axis=-1, keepdims=True)
    d = v - mu
    var = jnp.mean(d * d, axis=-1, keepdims=True)
    return d * lax.rsqrt(var + NORM_EPS) * g + b


def _sigmoid(v):
    return 0.5 * jnp.tanh(0.5 * v) + 0.5


def _silu(v):
    return v * _sigmoid(v)


def _matmul_kernel(x_ref, w_ref, o_ref, xb_ref):
    @pl.when(pl.program_id(1) == 0)
    def _():
        xb_ref[...] = x_ref[...].astype(BF16)

    o_ref[...] = jnp.dot(xb_ref[...], w_ref[...], preferred_element_type=F32).astype(o_ref.dtype)


def _matmul(x, w, out_dtype, *, tm, tn, name):
    m, k = x.shape
    n = w.shape[1]
    tm, tn = min(tm, m), min(tn, n)
    est = (2 * tm * k * x.dtype.itemsize + tm * k * 2 + 2 * k * tn * 2
           + 2 * tm * tn * jnp.dtype(out_dtype).itemsize + tm * tn * 4)
    return pl.pallas_call(
        _matmul_kernel,
        out_shape=jax.ShapeDtypeStruct((m, n), out_dtype),
        grid=(m // tm, n // tn),
        in_specs=[pl.BlockSpec((tm, k), lambda i, j: (i, 0)),
                  pl.BlockSpec((k, tn), lambda i, j: (0, j))],
        out_specs=pl.BlockSpec((tm, tn), lambda i, j: (i, j)),
        scratch_shapes=[pltpu.VMEM((tm, k), BF16)],
        compiler_params=_params(("parallel", "arbitrary"), est),
        name=name,
    )(x, w)


def _in_proj_kernel(x_ref, w_ref, xbc_ref, qkv_ref, zg_ref, xb_ref, *, n_xbc, n_qkv, n_z):
    j = pl.program_id(1)
    j_z = n_xbc + n_qkv

    @pl.when(j == 0)
    def _():
        xb_ref[...] = x_ref[...].astype(BF16)

    acc = jnp.dot(xb_ref[...], w_ref[...], preferred_element_type=F32)

    @pl.when(j < n_xbc)
    def _():
        xbc_ref[...] = acc.astype(BF16)

    @pl.when(jnp.logical_and(j >= n_xbc, j < j_z))
    def _():
        for hh in range(qkv_ref.shape[0]):
            qkv_ref[hh] = acc[:, hh * LANES:(hh + 1) * LANES].astype(BF16)

    @pl.when(jnp.logical_and(j >= j_z, j < j_z + n_z))
    def _():
        zg_ref[...] = acc.astype(BF16)

    @pl.when(j >= j_z + n_z)
    def _():
        zg_ref[...] = _sigmoid(acc).astype(BF16)


def _in_proj(x, w, *, n_xbc_cols, n_qkv_cols, n_z_cols, tm, tn):
    m, k = x.shape
    n = w.shape[1]
    tm = min(tm, m)
    n_xbc, n_qkv, n_z = n_xbc_cols // tn, n_qkv_cols // tn, n_z_cols // tn
    n_zg = n // tn - n_xbc - n_qkv
    per = tn // LANES
    est = 2 * tm * k * 4 + tm * k * 2 + 2 * k * tn * 2 + 3 * 2 * tm * tn * 2 + 2 * tm * tn * 4
    kern = functools.partial(_in_proj_kernel, n_xbc=n_xbc, n_qkv=n_qkv, n_z=n_z)
    return pl.pallas_call(
        kern,
        out_shape=(jax.ShapeDtypeStruct((m, n_xbc_cols), BF16),
                   jax.ShapeDtypeStruct((n_qkv_cols // LANES, m, LANES), BF16),
                   jax.ShapeDtypeStruct((m, n_zg * tn), BF16)),
        grid=(m // tm, n // tn),
        in_specs=[pl.BlockSpec((tm, k), lambda i, j: (i, 0)),
                  pl.BlockSpec((k, tn), lambda i, j: (0, j))],
        out_specs=(pl.BlockSpec((tm, tn), lambda i, j: (i, jnp.minimum(j, n_xbc - 1))),
                   pl.BlockSpec((per, tm, LANES), lambda i, j: (jnp.clip(j - n_xbc, 0, n_qkv - 1), i, 0)),
                   pl.BlockSpec((tm, tn), lambda i, j: (i, jnp.clip(j - n_xbc - n_qkv, 0, n_zg - 1)))),
        scratch_shapes=[pltpu.VMEM((tm, k), BF16)],
        compiler_params=_params(("parallel", "arbitrary"), est),
        name="in_proj",
    )(x, w)


def _split3(v):
    hi = v.astype(BF16)
    r1 = v - hi.astype(F32)
    mid = r1.astype(BF16)
    lo = (r1 - mid.astype(F32)).astype(BF16)
    return hi, mid, lo


def _shift_matrix():
    L = SSD_CHUNK
    s = np.zeros((SSD_CONV - 1, L, 2 * L), np.float32)
    for j in range(1, SSD_CONV):
        for t in range(L):
            s[j - 1, t, t - j if t >= j else L + CONV_TAIL - j + t] = 1.0
    return s.reshape((SSD_CONV - 1) * L, 2 * L)


def _ssd_kernel(z_ref, xbc_ref, dt_ref, shift_ref, convw_ref, convb_ref, dtb_ref, alog_ref,
                dskip_ref, g_ref, y_ref, cat_ref, state_ref, xa_ref, yacc_ref,
                *, inner, n_groups, d_state):
    L = SSD_CHUNK
    width = xbc_ref.shape[1]
    n_pairs = inner // LANES
    pairs_per_group = n_pairs // n_groups
    group_width = inner // n_groups

    @pl.when(pl.program_id(1) == 0)
    def _():
        cat_ref[...] = jnp.zeros_like(cat_ref)
        state_ref[...] = jnp.zeros_like(state_ref)

    slab = 512
    for c0 in range(0, width, slab):
        cs = slice(c0, c0 + slab)
        ub = xbc_ref[:, cs]
        cat_ref[0:L, cs] = ub
        sh = jnp.dot(shift_ref[...], cat_ref[:, cs], preferred_element_type=F32)
        acc = convb_ref[:, cs] + ub.astype(F32) * convw_ref[SSD_CONV - 1:SSD_CONV, cs]
        for j in range(1, SSD_CONV):
            acc = acc + sh[(j - 1) * L:j * L] * convw_ref[SSD_CONV - 1 - j:SSD_CONV - j, cs]
        cat_ref[L:L + CONV_TAIL, cs] = ub[L - CONV_TAIL:L]
        xa_ref[:, cs] = _silu(acc)

    dtr = dt_ref[...] + dtb_ref[...]
    dt = jnp.maximum(dtr, 0.0) + jnp.log(1.0 + jnp.exp(-jnp.abs(dtr)))
    a = -jnp.exp(alog_ref[...]) * dt
    row_i = lax.broadcasted_iota(jnp.int32, (L, L), 0)
    col_i = lax.broadcasted_iota(jnp.int32, (L, L), 1)
    causal = row_i >= col_i
    tril = jnp.where(causal, 1.0, 0.0).astype(BF16)
    acs = sum(jnp.dot(tril, part, preferred_element_type=F32) for part in _split3(a))
    acs_t = acs.T
    dt_t = dt.T
    lane_lo = lax.broadcasted_iota(jnp.int32, (L, LANES), 1) < SSD_HEAD_DIM

    b0 = inner
    c0 = inner + n_groups * d_state
    for g in range(n_groups):
        bg = xa_ref[:, b0 + g * d_state:b0 + (g + 1) * d_state]
        cg = xa_ref[:, c0 + g * d_state:c0 + (g + 1) * d_state]
        cb = lax.dot_general(cg.astype(BF16), bg.astype(BF16), (((1,), (1,)), ((), ())),
                             preferred_element_type=F32)
        bg_t = bg.T
        for pp in range(g * pairs_per_group, (g + 1) * pairs_per_group):
            xs_pair = xa_ref[:, pp * LANES:(pp + 1) * LANES].astype(BF16)
            st_old = state_ref[pp]
            rhs = jnp.concatenate([xs_pair, st_old.astype(BF16)], axis=0)
            ys, sts = [], []
            for side in range(2):
                h = 2 * pp + side
                col = jnp.broadcast_to(acs[:, h:h + 1], (L, L))
                row = acs_t[h:h + 1, :]
                dt_row = dt_t[h:h + 1, :]
                last = acs[L - 1:L, h:h + 1]
                dec = jnp.exp(jnp.where(causal, col - row, MASK_VALUE))
                m_in = (cb * dec * dt_row).astype(BF16)
                c_w = (cg * jnp.exp(col[:, :d_state])).astype(BF16)
                lhs = jnp.concatenate([m_in, c_w], axis=1)
                ys.append(jnp.dot(lhs, rhs, preferred_element_type=F32))
                b_w = (bg_t * (jnp.exp(last - row) * dt_row)).astype(BF16)
                st_new = jnp.dot(b_w, xs_pair, preferred_element_type=F32)
                sts.append(st_old * jnp.exp(last) + st_new)
            yacc_ref[:, pp * LANES:(pp + 1) * LANES] = jnp.where(lane_lo, ys[0], ys[1])
            state_ref[pp] = jnp.where(lane_lo, sts[0], sts[1])

    for g in range(n_groups):
        cs = slice(g * group_width, (g + 1) * group_width)
        zz = z_ref[:, cs].astype(F32)
        yv = (yacc_ref[:, cs] + dskip_ref[:, cs] * xa_ref[:, cs]) * _silu(zz)
        ms = jnp.mean(yv * yv, axis=-1, keepdims=True)
        y_ref[:, cs] = (yv * lax.rsqrt(ms + NORM_EPS) * g_ref[:, cs]).astype(y_ref.dtype)


def _ssd_branch(zg, xbc, dt_raw, conv_w, conv_b, dt_bias, a_log, d_skip, norm_g, *, batch, seq):
    t = zg.shape[0]
    inner = norm_g.shape[0]
    width = conv_w.shape[1]
    heads = inner // SSD_HEAD_DIM
    n_chunks = seq // SSD_CHUNK
    L = SSD_CHUNK

    def pad_heads(v):
        return jnp.pad(v.astype(F32), (0, LANES - heads)).reshape(1, LANES)

    convw = jnp.pad(conv_w.astype(F32), ((0, SUBLANES - SSD_CONV), (0, 0)))
    dskip = jnp.repeat(d_skip.astype(F32), SSD_HEAD_DIM).reshape(1, inner)
    row = lambda b, c: (b * n_chunks + c, 0)
    const = lambda b, c: (0, 0)
    shift = jnp.asarray(_shift_matrix(), BF16)
    est = (2 * L * (inner + width) * 2 + 2 * L * LANES * 4 + 2 * L * inner * 2 + 2 * shift.size * 2
           + (inner // LANES) * SSD_STATE * LANES * 4 + L * (2 * width + inner) * 4 + (8 << 20))
    kern = functools.partial(_ssd_kernel, inner=inner, n_groups=SSD_GROUPS, d_state=SSD_STATE)
    return pl.pallas_call(
        kern,
        out_shape=jax.ShapeDtypeStruct((t, inner), BF16),
        grid=(batch, n_chunks),
        in_specs=[pl.BlockSpec((L, inner), row),
                  pl.BlockSpec((L, width), row),
                  pl.BlockSpec((L, LANES), row),
                  pl.BlockSpec(shift.shape, const),
                  pl.BlockSpec((SUBLANES, width), const),
                  pl.BlockSpec((1, width), const),
                  pl.BlockSpec((1, LANES), const),
                  pl.BlockSpec((1, LANES), const),
                  pl.BlockSpec((1, inner), const),
                  pl.BlockSpec((1, inner), const)],
        out_specs=pl.BlockSpec((L, inner), row),
        scratch_shapes=[pltpu.VMEM((2 * L, width), BF16),
                        pltpu.VMEM((inner // LANES, SSD_STATE, LANES), F32),
                        pltpu.VMEM((L, width), F32),
                        pltpu.VMEM((L, inner), F32)],
        compiler_params=_params(("parallel", "arbitrary"), est),
        name="ssd_scan",
    )(zg, xbc, dt_raw, shift, convw, conv_b.astype(F32).reshape(1, width), pad_heads(dt_bias),
      pad_heads(a_log), dskip, norm_g.astype(F32).reshape(1, inner))


def _bucket_tiles(tq):
    max_exact = REL_BUCKETS // 2
    qi = np.arange(tq)[:, None]
    ki = np.arange(tq)[None, :]

    def bucket(dist):
        d = np.maximum(dist, 1).astype(np.float32)
        large = max_exact + (np.log(d / np.float32(max_exact)) / np.float32(math.log(REL_MAX_DIST / max_exact))
                             * np.float32(REL_BUCKETS - max_exact)).astype(np.int32)
        large = np.minimum(large, REL_BUCKETS - 1)
        return np.where(dist < max_exact, dist, large).astype(np.int32)

    diag = np.where(qi >= ki, bucket(np.maximum(qi - ki, 0)), -1)
    prev = bucket(tq + qi - ki)
    far = bucket(np.arange(tq + 1, 1 << 16))
    assert (far == REL_BUCKETS - 1).all()
    return np.stack([diag, prev]).astype(np.int32)


def _bias_kernel(rb_ref, bucket_ref, o_ref):
    h = pl.program_id(0)
    bk = bucket_ref[...]
    acc = jnp.zeros(bk.shape, F32)
    for b in range(REL_BUCKETS):
        acc = jnp.where(bk == b, rb_ref[b, h], acc)
    o_ref[0] = jnp.where(bk < 0, MASK_VALUE, (acc - rb_ref[REL_BUCKETS - 1, h]) * LOG2E)


def _bias_tiles(rel_bias, tq):
    buckets = jnp.asarray(_bucket_tiles(tq))
    return pl.pallas_call(
        _bias_kernel,
        out_shape=jax.ShapeDtypeStruct((DIFF_HEADS, 2, tq, tq), F32),
        grid=(DIFF_HEADS,),
        in_specs=[pl.BlockSpec(memory_space=pltpu.SMEM),
                  pl.BlockSpec((2, tq, tq), lambda h: (0, 0, 0))],
        out_specs=pl.BlockSpec((1, 2, tq, tq), lambda h: (h, 0, 0, 0)),
        compiler_params=_params(("arbitrary",), 8 * tq * tq * 4),
        name="t5_bias_tiles",
    )(rel_bias.astype(F32), buckets)


def _attn_kernel(lam_ref, q_ref, k_ref, v_ref, bias_ref, g_ref, o_ref,
                 q2_ref, vaug_ref, m_ref, acc_ref, *, tq, rows, lam_init):
    i = pl.program_id(2)
    dh = LANES // 2

    @pl.when(i == 0)
    def _():
        vaug_ref[:, 0:LANES] = v_ref[0]
        vaug_ref[:, LANES:2 * LANES] = jnp.ones((v_ref.shape[1], LANES), BF16)

    lane = lax.broadcasted_iota(jnp.int32, (tq, LANES), 1)
    qs = (q_ref[0].astype(F32) * (dh ** -0.5 * LOG2E)).astype(BF16)
    zero = jnp.zeros_like(qs)
    q2_ref[0:tq] = jnp.where(lane < dh, qs, zero)
    q2_ref[tq:2 * tq] = jnp.where(lane >= dh, qs, zero)
    m_ref[...] = jnp.full(m_ref.shape, MASK_VALUE, F32)
    acc_ref[...] = jnp.zeros_like(acc_ref)

    def step(j, bias_idx):
        start = pl.multiple_of(j * tq, tq)
        s_all = lax.dot_general(q2_ref[...], k_ref[0, pl.ds(start, tq), :], (((1,), (1,)), ((), ())),
                                preferred_element_type=F32)
        for r0 in range(0, 2 * tq, rows):
            rs = slice(r0, r0 + rows)
            q0 = r0 % tq
            klen = q0 + rows if bias_idx == 0 else tq
            vb = vaug_ref[pl.ds(start, klen), :]
            s = s_all[rs, 0:klen]
            if bias_idx is not None:
                s = s + bias_ref[0, bias_idx, q0:q0 + rows, 0:klen]
            m_old = m_ref[rs]
            m_new = jnp.maximum(m_old, jnp.max(s, axis=1, keepdims=True))
            alpha = jnp.exp2(m_old - m_new)
            p = jnp.exp2(s - jnp.concatenate([m_new] * (klen // LANES), axis=1))
            pv = jnp.dot(p.astype(BF16), vb, preferred_element_type=F32)
            acc_ref[rs] = jnp.concatenate([alpha, alpha], axis=1) * acc_ref[rs] + pv
            m_ref[rs] = m_new

    def far_step(j, carry):
        step(j, None)
        return carry

    lax.fori_loop(0, jnp.maximum(i - 1, 0), far_step, 0)

    @pl.when(i >= 1)
    def _():
        step(i - 1, 1)

    step(i, 0)

    lv = lam_ref[...]
    s1 = jnp.sum(lv[0:1] * lv[1:2], axis=1, keepdims=True)
    s2 = jnp.sum(lv[2:3] * lv[3:4], axis=1, keepdims=True)
    lam = jnp.exp(s1) - jnp.exp(s2) + lam_init
    o1 = acc_ref[0:tq, 0:LANES] / acc_ref[0:tq, LANES:2 * LANES]
    o2 = acc_ref[tq:2 * tq, 0:LANES] / acc_ref[tq:2 * tq, LANES:2 * LANES]
    o = o1 - lam * o2
    ms = jnp.mean(o * o, axis=-1, keepdims=True)
    o_ref[...] = (o * lax.rsqrt(ms + NORM_EPS) * g_ref[...] * (1.0 - lam_init)).astype(o_ref.dtype)


def _diff_attention(qkv, lam_vecs, bias_tiles, subln_g, *, batch, seq, tq, layer_idx):
    t = qkv.shape[1]
    nq = seq // tq
    rows = min(256, tq)
    lam_init = 0.8 - 0.6 * math.exp(-0.3 * layer_idx)
    kern = functools.partial(_attn_kernel, tq=tq, rows=rows, lam_init=lam_init)
    est = (2 * tq * LANES * 2 + 4 * seq * LANES * 2 + 4 * tq * tq * 4 + 2 * tq * LANES * 2
           + 2 * tq * LANES * 2 + seq * 2 * LANES * 2 + 2 * tq * LANES * 4 + 2 * tq * 2 * LANES * 4
           + 8 * rows * tq * 4)
    return pl.pallas_call(
        kern,
        out_shape=jax.ShapeDtypeStruct((t, DIFF_HEADS * LANES), BF16),
        grid=(batch, DIFF_HEADS, nq),
        in_specs=[pl.BlockSpec((SUBLANES, LANES), lambda b, h, i: (0, 0)),
                  pl.BlockSpec((1, tq, LANES), lambda b, h, i: (h, b * nq + i, 0)),
                  pl.BlockSpec((1, seq, LANES), lambda b, h, i: (DIFF_HEADS + h, b, 0)),
                  pl.BlockSpec((1, seq, LANES), lambda b, h, i: (2 * DIFF_HEADS + h, b, 0)),
                  pl.BlockSpec((1, 2, tq, tq), lambda b, h, i: (h, 0, 0, 0)),
                  pl.BlockSpec((1, LANES), lambda b, h, i: (0, 0))],
        out_specs=pl.BlockSpec((tq, LANES), lambda b, h, i: (b * nq + i, h)),
        scratch_shapes=[pltpu.VMEM((2 * tq, LANES), BF16),
                        pltpu.VMEM((seq, 2 * LANES), BF16),
                        pltpu.VMEM((2 * tq, LANES), F32),
                        pltpu.VMEM((2 * tq, 2 * LANES), F32)],
        compiler_params=_params(("parallel", "parallel", "arbitrary"), est),
        name="diff_attention",
    )(lam_vecs, qkv, qkv, qkv, bias_tiles, subln_g)


def _mix_kernel(x_ref, y_ref, o_ref, gate_ref, wssd_ref, wdiff_ref, wmix_ref, g_ref, b_ref, out_ref):
    d = x_ref.shape[1]
    ssd = jnp.dot(y_ref[...], wssd_ref[...], preferred_element_type=F32)
    dif = jnp.dot(o_ref[...], wdiff_ref[...], preferred_element_type=F32)
    merged = gate_ref[:, 0:d].astype(F32) * ssd + gate_ref[:, d:2 * d].astype(F32) * dif
    mixed = jnp.dot(merged.astype(BF16), wmix_ref[...], preferred_element_type=F32)
    out_ref[...] = _layer_norm(DN_ALPHA * x_ref[...] + mixed, g_ref[...], b_ref[...])


def _mix(x, y_ssd, o_diff, proj, w_ssd, w_diff, w_mix, ln_g, ln_b, *, tm, gate_block):
    t, d = x.shape
    tm = min(tm, t)
    inner = y_ssd.shape[1]
    dw = o_diff.shape[1]
    row = lambda i: (i, 0)
    est = (2 * tm * (d * 4 + inner * 2 + dw * 2 + 2 * d * 2 + d * 4)
           + (inner * d + dw * d + d * d) * 2 + 6 * tm * d * 4)
    return pl.pallas_call(
        _mix_kernel,
        out_shape=jax.ShapeDtypeStruct((t, d), F32),
        grid=(t // tm,),
        in_specs=[pl.BlockSpec((tm, d), row),
                  pl.BlockSpec((tm, inner), row),
                  pl.BlockSpec((tm, dw), row),
                  pl.BlockSpec((tm, 2 * d), lambda i: (i, gate_block)),
                  _resident((inner, d)), _resident((dw, d)), _resident((d, d)),
                  _resident((1, d)), _resident((1, d))],
        out_specs=pl.BlockSpec((tm, d), row),
        compiler_params=_params(("parallel",), est),
        name="mix_ln1",
    )(x, y_ssd, o_diff, proj, w_ssd, w_diff, w_mix, ln_g, ln_b)


def _pack_bf16_pairs(v):
    w = v.shape[1] // 2
    lo = lax.bitcast_convert_type(v[:, :w].astype(BF16).astype(F32), jnp.int32)
    hi = lax.bitcast_convert_type(v[:, w:].astype(BF16).astype(F32), jnp.int32)
    return jnp.bitwise_or(hi, lax.shift_right_logical(lo, 16))


def _unpack_bf16_pairs(words):
    lo = lax.bitcast_convert_type(lax.shift_left(words, 16), F32)
    hi = lax.bitcast_convert_type(jnp.bitwise_and(words, -65536), F32)
    return jnp.concatenate([lo, hi], axis=1)


def _xattn_kernel(x_ref, kv_ref, wq_ref, wo_ref, g_ref, b_ref, out_ref, packed_ref, o_scr, *, heads):
    d = x_ref.shape[1]
    dh = d // heads
    xv = x_ref[...]
    q = jnp.dot(xv.astype(BF16), wq_ref[...], preferred_element_type=F32)
    q = (q * (dh ** -0.5)).astype(BF16)
    for h in range(heads):
        kh = kv_ref[:, h * dh:(h + 1) * dh]
        vh = kv_ref[:, d + h * dh:d + (h + 1) * dh]
        s = lax.dot_general(q[:, h * dh:(h + 1) * dh], kh, (((1,), (1,)), ((), ())),
                            preferred_element_type=F32)
        p = jnp.exp(s - jnp.max(s, axis=1, keepdims=True))
        p = p / jnp.sum(p, axis=1, keepdims=True)
        o_scr[:, h * dh:(h + 1) * dh] = jnp.dot(p.astype(BF16), vh, preferred_element_type=F32).astype(BF16)
    att = jnp.dot(o_scr[...], wo_ref[...], preferred_element_type=F32)
    y = _layer_norm(DN_ALPHA * xv + att, g_ref[...], b_ref[...])
    out_ref[...] = y
    packed_ref[...] = _pack_bf16_pairs(y)


def _cross_attention(x, kv, w_cq, w_co, ln_g, ln_b, *, batch, seq, tq):
    t, d = x.shape
    mem_len = kv.shape[0] // batch
    tq = min(tq, seq)
    nq = seq // tq
    est = (2 * tq * d * 4 * 2 + 2 * tq * d * 2 + 2 * mem_len * 2 * d * 2 + 2 * d * d * 2 + tq * d * 2
           + 8 * tq * d * 4)
    row = lambda b, i: (b * nq + i, 0)
    return pl.pallas_call(
        functools.partial(_xattn_kernel, heads=MEM_HEADS),
        out_shape=(jax.ShapeDtypeStruct((t, d), F32), jax.ShapeDtypeStruct((t, d // 2), jnp.int32)),
        grid=(batch, nq),
        in_specs=[pl.BlockSpec((tq, d), row),
                  pl.BlockSpec((mem_len, 2 * d), lambda b, i: (b, 0)),
                  _resident((d, d)), _resident((d, d)), _resident((1, d)), _resident((1, d))],
        out_specs=(pl.BlockSpec((tq, d), row), pl.BlockSpec((tq, d // 2), row)),
        scratch_shapes=[pltpu.VMEM((tq, d), BF16)],
        compiler_params=_params(("parallel", "parallel"), est),
        name="cross_attention_ln2",
    )(x, kv, w_cq, w_co, ln_g, ln_b)


def _router_kernel(x_ref, wr_ref, rb_ref, idx_ref, rank_ref, gates_ref, cnt_ref, run_ref, *, n_experts):
    tm = x_ref.shape[0]
    n_groups = N_EXPERT_GROUPS
    per = n_experts // n_groups

    @pl.when(pl.program_id(0) == 0)
    def _():
        run_ref[...] = jnp.zeros_like(run_ref)

    xv = x_ref[...]
    x_hi = xv.astype(BF16)
    x_lo = (xv - x_hi.astype(F32)).astype(BF16)
    nt = (((1,), (1,)), ((), ()))
    logits = (lax.dot_general(wr_ref[0], x_hi, nt, preferred_element_type=F32)
              + lax.dot_general(wr_ref[0], x_lo, nt, preferred_element_type=F32)
              + lax.dot_general(wr_ref[1], x_hi, nt, preferred_element_type=F32))
    sc = _sigmoid(logits[0:n_experts]).reshape(per, n_groups, tm)
    choice = sc + rb_ref[0:n_experts].reshape(per, n_groups, 1)
    neg = -jnp.inf
    member = lax.broadcasted_iota(jnp.int32, (per, n_groups, tm), 0)
    group3 = lax.broadcasted_iota(jnp.int32, (per, n_groups, tm), 1)
    m1 = jnp.max(choice, axis=0, keepdims=True)
    i1 = jnp.min(jnp.where(choice == m1, member, per), axis=0, keepdims=True)
    m2 = jnp.max(jnp.where(member == i1, neg, choice), axis=0, keepdims=True)
    gscore = (m1 + m2)
    gsel = jnp.zeros((1, n_groups, tm), F32)
    gidx = lax.broadcasted_iota(jnp.int32, (1, n_groups, tm), 1)
    cur = gscore
    for _ in range(TOP_GROUPS):
        mx = jnp.max(cur, axis=1, keepdims=True)
        ix = jnp.min(jnp.where(cur == mx, gidx, n_groups), axis=1, keepdims=True)
        hit = gidx == ix
        gsel = jnp.where(hit, 1.0, gsel)
        cur = jnp.where(hit, neg, cur)
    cur = jnp.where(gsel > 0.0, choice, neg)
    eidx = group3 * per + member
    esel = jnp.zeros((per, n_groups, tm), F32)
    hits, idx_rows = [], []
    for _ in range(TOP_K):
        mx = jnp.max(jnp.max(cur, axis=0, keepdims=True), axis=1, keepdims=True)
        ix = jnp.min(jnp.min(jnp.where(cur == mx, eidx, n_experts), axis=0, keepdims=True),
                     axis=1, keepdims=True)
        hit = eidx == ix
        esel = jnp.where(hit, 1.0, esel)
        cur = jnp.where(hit, neg, cur)
        hits.append(hit)
        idx_rows.append(ix.reshape(1, tm))
    w = esel * sc
    tot = jnp.sum(jnp.sum(w, axis=0, keepdims=True), axis=1, keepdims=True)
    gw = w / tot * ROUTED_SCALE

    r_i = lax.broadcasted_iota(jnp.int32, (tm, tm), 0)
    c_i = lax.broadcasted_iota(jnp.int32, (tm, tm), 1)
    upper = jnp.where(r_i <= c_i, 1.0, 0.0).astype(BF16)
    pref = jnp.dot(esel.reshape(n_experts, tm).astype(BF16), upper, preferred_element_type=F32)
    run = run_ref[...]
    rank3 = (jnp.concatenate([run] * (tm // LANES), axis=1) + pref - 1.0).reshape(per, n_groups, tm)

    def pick(hit, vals):
        return jnp.sum(jnp.sum(jnp.where(hit, vals, 0.0), axis=0, keepdims=True), axis=1).reshape(1, tm)

    idx_ref[...] = jnp.concatenate(idx_rows, axis=0)
    rank_ref[...] = jnp.concatenate([pick(h, rank3) for h in hits], axis=0).astype(jnp.int32)
    wk = jnp.concatenate([pick(h, gw) for h in hits] + [jnp.zeros((LANES - TOP_K, tm), F32)], axis=0)
    gates_ref[...] = wk.T
    run = run + jnp.broadcast_to(pref[:, tm - 1:tm], run.shape)
    run_ref[...] = run
    cnt_ref[...] = run


def _router(x, w_router, router_bias, *, tm, row0, t):
    d = x.shape[1]
    n_experts = w_router.shape[1]
    per = n_experts // N_EXPERT_GROUPS
    tm = min(tm, t)
    blk0 = row0 // tm

    def member_major(v):
        return v.reshape(N_EXPERT_GROUPS, per, *v.shape[1:]).swapaxes(0, 1).reshape(v.shape)

    wt = jnp.pad(member_major(w_router.astype(F32).T), ((0, LANES - n_experts), (0, 0)))
    w_hi = wt.astype(BF16)
    wr = jnp.stack([w_hi, (wt - w_hi.astype(F32)).astype(BF16)])
    rb = jnp.pad(member_major(router_bias.astype(F32)), (0, LANES - n_experts)).reshape(LANES, 1)
    est = (2 * tm * d * 4 + 2 * LANES * d * 2 + 2 * tm * LANES * 4 + 60 * n_experts * tm * 4 + 3 * tm * tm * 4)
    pick_spec = pl.BlockSpec((TOP_K, tm), lambda i: (0, i))
    idx_t, rank_t, gates, counts = pl.pallas_call(
        functools.partial(_router_kernel, n_experts=n_experts),
        out_shape=(jax.ShapeDtypeStruct((TOP_K, t), jnp.int32),
                   jax.ShapeDtypeStruct((TOP_K, t), jnp.int32),
                   jax.ShapeDtypeStruct((t, LANES), F32),
                   jax.ShapeDtypeStruct((n_experts, LANES), F32)),
        grid=(t // tm,),
        in_specs=[pl.BlockSpec((tm, d), lambda i: (blk0 + i, 0)),
                  _resident((2, LANES, d)), _resident((LANES, 1))],
        out_specs=(pick_spec, pick_spec,
                   pl.BlockSpec((tm, LANES), lambda i: (i, 0)),
                   pl.BlockSpec((n_experts, LANES), lambda i: (0, 0))),
        scratch_shapes=[pltpu.VMEM((n_experts, LANES), F32)],
        compiler_params=_params(("arbitrary",), est),
        name="router",
    )(x, wr, rb)
    counts = counts[:, 0].astype(jnp.int32).reshape(per, N_EXPERT_GROUPS).T.reshape(n_experts)
    return idx_t, rank_t, gates, counts


def _positions_kernel(cnt_ref, idx_ref, rank_ref, pos_ref, texp_ref, off_ref, *, n_experts, rows, n_tiles):
    @pl.when(pl.program_id(0) == 0)
    def _():
        def per_expert(e, toff):
            off_ref[e] = toff * rows
            nt = lax.shift_right_logical(cnt_ref[e] + (rows - 1), int(math.log2(rows)))

            def fill(j, c):
                texp_ref[toff + j] = e
                return c

            lax.fori_loop(0, nt, fill, 0)
            return toff + nt

        n_used = lax.fori_loop(0, n_experts, per_expert, 0)

        def tail(j, c):
            texp_ref[j] = n_experts - 1
            return c

        lax.fori_loop(n_used, n_tiles, tail, 0)
        texp_ref[n_tiles] = n_used

    idx = idx_ref[...]
    pos = rank_ref[...]
    for e in range(n_experts):
        pos = pos + jnp.where(idx == e, off_ref[e], 0)
    pos_ref[...] = pos


def _positions(counts, idx_t, rank_t, *, rows, n_tiles, tm):
    n_experts = counts.shape[0]
    k, t = idx_t.shape
    tm = min(tm, t)
    spec = pl.BlockSpec((k, tm), lambda i: (0, i))
    kern = functools.partial(_positions_kernel, n_experts=n_experts, rows=rows, n_tiles=n_tiles)
    return pl.pallas_call(
        kern,
        out_shape=(jax.ShapeDtypeStruct((k, t), jnp.int32),
                   jax.ShapeDtypeStruct((n_tiles + 1,), jnp.int32)),
        grid=(t // tm,),
        in_specs=[pl.BlockSpec(memory_space=pltpu.SMEM), spec, spec],
        out_specs=(spec, pl.BlockSpec(memory_space=pltpu.SMEM)),
        scratch_shapes=[pltpu.SMEM((n_experts,), jnp.int32)],
        compiler_params=_params(("arbitrary",), 16 * k * tm * 4),
        name="moe_positions",
    )(counts, idx_t, rank_t)


def _sc_mesh():
    return plsc.VectorSubcoreMesh(core_axis_name="c", subcore_axis_name="s",
                                  num_cores=SC_CORES, num_subcores=SC_SUBCORES)


def _sc_dispatch(packed, pos_chunks, n_rows, row0):
    w = packed.shape[1]
    n_chunks, k, n = pos_chunks.shape
    per_worker = n_chunks // (SC_CORES * SC_SUBCORES)

    @functools.partial(
        pl.kernel, mesh=_sc_mesh(),
        out_type=jax.ShapeDtypeStruct((n_rows, w), packed.dtype),
        scratch_types=[pltpu.VMEM((k, n), jnp.int32), pltpu.VMEM((n, w), packed.dtype),
                       pltpu.SemaphoreType.DMA],
        name="moe_dispatch_sc",
    )
    def scatter_rows(x_hbm, pos_hbm, out_hbm, idx_v, rows_v, sem):
        wid = lax.axis_index("s") * SC_CORES + lax.axis_index("c")

        @pl.loop(0, per_worker)
        def _(step):
            c = wid * per_worker + step
            pltpu.sync_copy(pos_hbm.at[c], idx_v)
            pltpu.sync_copy(x_hbm.at[pl.ds(row0 + c * n, n)], rows_v)
            copies = [pltpu.async_copy(rows_v, out_hbm.at[idx_v.at[kk]], sem) for kk in range(k)]
            for cp in copies:
                cp.wait()

    return scatter_rows(packed, pos_chunks)


def _sc_combine(sorted_rows, pos_chunks):
    _, w = sorted_rows.shape
    n_chunks, k, n = pos_chunks.shape
    per_worker = n_chunks // (SC_CORES * SC_SUBCORES)

    @functools.partial(
        pl.kernel, mesh=_sc_mesh(),
        out_type=jax.ShapeDtypeStruct((k, n_chunks * n, w), sorted_rows.dtype),
        scratch_types=[pltpu.VMEM((k, n), jnp.int32), pltpu.VMEM((n, w), sorted_rows.dtype),
                       pltpu.SemaphoreType.DMA],
        name="moe_combine_sc",
    )
    def gather_rows(y_hbm, pos_hbm, out_hbm, idx_v, rows_v, sem):
        wid = lax.axis_index("s") * SC_CORES + lax.axis_index("c")

        @pl.loop(0, per_worker)
        def _(step):
            c = wid * per_worker + step
            pltpu.sync_copy(pos_hbm.at[c], idx_v)
            for kk in range(k):
                pltpu.async_copy(y_hbm.at[idx_v.at[kk]], rows_v, sem).wait()
                pltpu.sync_copy(rows_v, out_hbm.at[kk, pl.ds(c * n, n)])

    return gather_rows(sorted_rows, pos_chunks)


def _expert_kernel(texp_ref, xs_ref, wgu_ref, wdn_ref, ys_ref, *, n_tiles):
    ff = wdn_ref.shape[1]

    @pl.when(pl.program_id(0) < texp_ref[n_tiles])
    def _():
        half = xs_ref.shape[0] // 2
        for r0 in (0, half):
            xv = _unpack_bf16_pairs(xs_ref[r0:r0 + half]).astype(BF16)
            gu = jnp.dot(xv, wgu_ref[0], preferred_element_type=F32)
            hid = _silu(gu[:, 0:ff]) * gu[:, ff:2 * ff]
            ys_ref[r0:r0 + half] = _pack_bf16_pairs(
                jnp.dot(hid.astype(BF16), wdn_ref[0], preferred_element_type=F32))


def _experts(tile_expert, sorted_rows, w_gu, w_dn, *, rows):
    n_rows, w = sorted_rows.shape
    n_tiles = n_rows // rows
    _, d, ff2 = w_gu.shape
    ff = ff2 // 2
    tile = lambda j, te: (jnp.minimum(j, te[n_tiles] - 1), 0)
    est = 4 * rows * w * 4 + 2 * (d * ff2 + ff * d) * 2 + 4 * rows * d * 4 + 4 * rows * ff2 * 4
    return pl.pallas_call(
        functools.partial(_expert_kernel, n_tiles=n_tiles),
        out_shape=jax.ShapeDtypeStruct((n_rows, w), sorted_rows.dtype),
        grid_spec=pltpu.PrefetchScalarGridSpec(
            num_scalar_prefetch=1,
            grid=(n_tiles,),
            in_specs=[pl.BlockSpec((rows, w), tile),
                      pl.BlockSpec((1, d, ff2), lambda j, te: (te[j], 0, 0)),
                      pl.BlockSpec((1, ff, d), lambda j, te: (te[j], 0, 0))],
            out_specs=pl.BlockSpec((rows, w), tile)),
        compiler_params=_params(("arbitrary",), est),
        name="moe_experts",
    )(tile_expert, sorted_rows, w_gu, w_dn)


def _moe_out_kernel(x_ref, yk_ref, gates_ref, wsgu_ref, wsdn_ref, g_ref, b_ref, *rest):
    out_ref = rest[-1]
    ff = wsdn_ref.shape[0]
    xv = x_ref[...]
    gu = jnp.dot(xv.astype(BF16), wsgu_ref[...], preferred_element_type=F32)
    hid = _silu(gu[:, 0:ff]) * gu[:, ff:2 * ff]
    acc = jnp.dot(hid.astype(BF16), wsdn_ref[...], preferred_element_type=F32)
    for k in range(yk_ref.shape[0]):
        acc = acc + gates_ref[:, k:k + 1] * _unpack_bf16_pairs(yk_ref[k])
    out_ref[...] = _layer_norm(DN_ALPHA * xv + acc, g_ref[...], b_ref[...])


def _moe_out(x, yk, gates, w_sgu, w_sdn, ln_g, ln_b, *, tm, row0, prev):
    t_all, d = x.shape
    k, t, w = yk.shape
    sff2 = w_sgu.shape[1]
    tm = min(tm, t)
    blk0 = row0 // tm
    est = (2 * tm * d * 4 * 2 + 2 * k * tm * w * 4 + 2 * tm * LANES * 4
           + (d * sff2 + (sff2 // 2) * d) * 2 + 6 * tm * d * 4)
    in_specs = [pl.BlockSpec((tm, d), lambda i: (blk0 + i, 0)),
                pl.BlockSpec((k, tm, w), lambda i: (0, i, 0)),
                pl.BlockSpec((tm, LANES), lambda i: (i, 0)),
                _resident((d, sff2)), _resident((sff2 // 2, d)),
                _resident((1, d)), _resident((1, d))]
    args = [x, yk, gates, w_sgu, w_sdn, ln_g, ln_b]
    aliases = {}
    if prev is not None:
        in_specs.append(pl.BlockSpec(memory_space=pl.ANY))
        args.append(prev)
        aliases = {len(args) - 1: 0}
    return pl.pallas_call(
        _moe_out_kernel,
        out_shape=jax.ShapeDtypeStruct((t_all, d), F32),
        grid=(t // tm,),
        in_specs=in_specs,
        out_specs=pl.BlockSpec((tm, d), lambda i: (blk0 + i, 0)),
        input_output_aliases=aliases,
        compiler_params=_params(("parallel",), est),
        name="moe_out_ln3",
    )(*args)


def _moe(x, packed, w_router, router_bias, w_gu, w_dn, w_sgu, w_sdn, ln_g, ln_b):
    t_all, d = x.shape
    n_experts = w_router.shape[1]
    n = SC_CHUNK_TOKENS
    splits = MOE_SPLITS if t_all % (MOE_SPLITS * n * SC_CORES * SC_SUBCORES) == 0 else 1
    t = t_all // splits
    rows = min(MOE_ROWS_PER_TILE, t)
    n_tiles = (t * TOP_K) // rows + n_experts
    parts = []
    for s in range(splits):
        row0 = s * t
        idx_t, rank_t, gates, counts = _router(x, w_router, router_bias, tm=512, row0=row0, t=t)
        pos_t, tile_expert = _positions(counts, idx_t, rank_t, rows=rows, n_tiles=n_tiles, tm=2048)
        pos_chunks = pos_t.reshape(TOP_K, t // n, n).transpose(1, 0, 2)
        sorted_x = _sc_dispatch(packed, pos_chunks, n_tiles * rows, row0)
        parts.append((row0, gates, tile_expert, pos_chunks, sorted_x))
    out = None
    for row0, gates, tile_expert, pos_chunks, sorted_x in parts:
        sorted_y = _experts(tile_expert, sorted_x, w_gu, w_dn, rows=rows)
        yk = _sc_combine(sorted_y, pos_chunks)
        out = _moe_out(x, yk, gates, w_sgu, w_sdn, ln_g, ln_b, tm=256, row0=row0, prev=out)
    return out


def kernel(x, mem, w_in, conv_w, conv_b, dt_bias, a_log, d_skip, ssd_norm_g, lambda_q1, lambda_k1, lambda_q2, lambda_k2, subln_g, rel_bias, w_ssd_br, w_diff_br, w_mix_out, ln1_g, ln1_b, w_cq, w_ckv, w_co, ln2_g, ln2_b, w_router, router_bias, w_exp_gu, w_exp_down, w_sh_gu, w_sh_down, ln3_g, ln3_b):
    batch, seq, d = x.shape
    depth = w_in.shape[0]
    inner = w_ssd_br.shape[1]
    xbc_width = conv_w.shape[2]
    heads = dt_bias.shape[1]
    diff_width = w_diff_br.shape[1]
    o_z, o_xbc = inner, inner + xbc_width
    o_dt = o_xbc + heads
    o_v = o_dt + 3 * diff_width
    t = batch * seq
    tq_attn = min(512, seq)

    def vec(v):
        return v.astype(F32).reshape(1, -1)

    xt = x.reshape(t, d)
    memt = mem.reshape(-1, d)
    bias_tiles = _bias_tiles(rel_bias, tq_attn)

    for l in range(depth):
        wl = w_in[l]
        w_big = jnp.concatenate([wl[:, o_z:o_xbc], wl[:, o_dt:o_v], wl[:, :o_z], wl[:, o_v:]],
                                axis=1).astype(BF16)
        assert 2 * d == inner
        xbc, qkv, zg = _in_proj(xt, w_big, n_xbc_cols=xbc_width, n_qkv_cols=3 * diff_width,
                                n_z_cols=inner, tm=1024, tn=1024)
        w_dt = jnp.pad(wl[:, o_xbc:o_dt], ((0, 0), (0, LANES - heads))).astype(BF16)
        dt_raw = _matmul(xt, w_dt, F32, tm=1024, tn=LANES, name="in_proj_dt")

        y_ssd = _ssd_branch(zg, xbc, dt_raw, conv_w[l], conv_b[l], dt_bias[l], a_log[l], d_skip[l],
                            ssd_norm_g[l], batch=batch, seq=seq)

        lam_vecs = jnp.pad(jnp.stack([lambda_q1[l], lambda_k1[l], lambda_q2[l], lambda_k2[l]]).astype(F32),
                           ((0, SUBLANES - 4), (0, LANES - lambda_q1.shape[1])))
        o_diff = _diff_attention(qkv, lam_vecs, bias_tiles, vec(subln_g[l]), batch=batch, seq=seq,
                                 tq=tq_attn, layer_idx=l)

        xt = _mix(xt, y_ssd, o_diff, zg, w_ssd_br[l].astype(BF16), w_diff_br[l].astype(BF16),
                  w_mix_out[l].astype(BF16), vec(ln1_g[l]), vec(ln1_b[l]), tm=512, gate_block=1)

        kv = _matmul(memt, w_ckv[l].astype(BF16), BF16, tm=1024, tn=1024, name="mem_kv_proj")
        xt, packed = _cross_attention(xt, kv, w_cq[l].astype(BF16), w_co[l].astype(BF16), vec(ln2_g[l]),
                                      vec(ln2_b[l]), batch=batch, seq=seq, tq=512)

        xt = _moe(xt, packed, w_router[l], router_bias[l], w_exp_gu[l].astype(BF16),
                  w_exp_down[l].astype(BF16), w_sh_gu[l].astype(BF16), w_sh_down[l].astype(BF16),
                  vec(ln3_g[l]), vec(ln3_b[l]))
    return xt.reshape(batch, seq, d)
```

```python
import functools
import math

import numpy as np
import jax
import jax.numpy as jnp
from jax import lax
from jax.experimental import pallas as pl
from jax.experimental.pallas import tpu as pltpu
from jax.experimental.pallas import tpu_sc as plsc

F32 = jnp.float32
BF16 = jnp.bfloat16

SSD_HEAD_DIM = 64
SSD_GROUPS = 4
SSD_STATE = 128
SSD_CONV = 4
SSD_CHUNK = 128
DIFF_HEADS = 8
REL_BUCKETS = 32
REL_MAX_DIST = 128
MEM_HEADS = 4
TOP_K = 8
N_EXPERT_GROUPS = 8
TOP_GROUPS = 4
ROUTED_SCALE = 2.5
NORM_EPS = 1e-5
DEPTH = 1
DN_ALPHA = (2.0 * DEPTH) ** 0.25

LANES = 128
SUBLANES = 8
VMEM_CAP_BYTES = 64 * 1024 * 1024
MASK_VALUE = -1e30
CONV_TAIL = 16
LOG2E = math.log2(math.e)
SC_CORES = 2
SC_SUBCORES = 16
SC_CHUNK_TOKENS = 128
MOE_SPLITS = 2
MOE_ROWS_PER_TILE = 1024


def _vmem_limit(estimate_bytes):
    return int(min(estimate_bytes * 5 // 4 + (4 << 20), VMEM_CAP_BYTES - (6 << 20)))


def _params(semantics, vmem_estimate):
    return pltpu.CompilerParams(dimension_semantics=semantics,
                                vmem_limit_bytes=_vmem_limit(vmem_estimate))


def _resident(shape):
    nd = len(shape)
    return pl.BlockSpec(shape, lambda *_: (0,) * nd, pipeline_mode=pl.Buffered(1))


def _layer_norm(v, g, b):
    mu = jnp.mean(v, axis=-1, keepdims=True)
    d = v - mu
    var = jnp.mean(d * d, axis=-1, keepdims=True)
    return d * lax.rsqrt(var + NORM_EPS) * g + b


def _sigmoid(v):
    return 0.5 * jnp.tanh(0.5 * v) + 0.5


def _silu(v):
    h = 0.5 * v
    return h + h * jnp.tanh(h)


def _matmul_kernel(x_ref, w_ref, o_ref, xb_ref):
    @pl.when(pl.program_id(1) == 0)
    def _():
        xb_ref[...] = x_ref[...].astype(BF16)

    o_ref[...] = jnp.dot(xb_ref[...], w_ref[...], preferred_element_type=F32).astype(o_ref.dtype)


def _matmul(x, w, out_dtype, *, tm, tn, name):
    m, k = x.shape
    n = w.shape[1]
    tm, tn = min(tm, m), min(tn, n)
    est = (2 * tm * k * x.dtype.itemsize + tm * k * 2 + 2 * k * tn * 2
           + 2 * tm * tn * jnp.dtype(out_dtype).itemsize + tm * tn * 4)
    return pl.pallas_call(
        _matmul_kernel,
        out_shape=jax.ShapeDtypeStruct((m, n), out_dtype),
        grid=(m // tm, n // tn),
        in_specs=[pl.BlockSpec((tm, k), lambda i, j: (i, 0)),
                  pl.BlockSpec((k, tn), lambda i, j: (0, j))],
        out_specs=pl.BlockSpec((tm, tn), lambda i, j: (i, j)),
        scratch_shapes=[pltpu.VMEM((tm, k), BF16)],
        compiler_params=_params(("parallel", "arbitrary"), est),
        name=name,
    )(x, w)


def _in_proj_kernel(x_ref, w_ref, wdt_ref, xbc_ref, qkv_ref, zg_ref, dt_ref, xb_ref, *, n_xbc, n_qkv, n_z):
    j = pl.program_id(1)
    j_z = n_xbc + n_qkv

    @pl.when(j == 0)
    def _():
        xb = x_ref[...].astype(BF16)
        xb_ref[...] = xb
        dt_ref[...] = jnp.dot(xb, wdt_ref[...], preferred_element_type=F32)

    acc = jnp.dot(xb_ref[...], w_ref[...], preferred_element_type=F32)

    @pl.when(j < n_xbc)
    def _():
        xbc_ref[...] = acc.astype(BF16)

    @pl.when(jnp.logical_and(j >= n_xbc, j < j_z))
    def _():
        for hh in range(qkv_ref.shape[0]):
            qkv_ref[hh] = acc[:, hh * LANES:(hh + 1) * LANES].astype(BF16)

    @pl.when(jnp.logical_and(j >= j_z, j < j_z + n_z))
    def _():
        zg_ref[...] = acc.astype(BF16)

    @pl.when(j >= j_z + n_z)
    def _():
        zg_ref[...] = _sigmoid(acc).astype(BF16)


def _in_proj(x, w, w_dt, *, n_xbc_cols, n_qkv_cols, n_z_cols, tm, tn):
    m, k = x.shape
    n = w.shape[1]
    tm = min(tm, m)
    n_xbc, n_qkv, n_z = n_xbc_cols // tn, n_qkv_cols // tn, n_z_cols // tn
    n_zg = n // tn - n_xbc - n_qkv
    per = tn // LANES
    est = (2 * tm * k * 4 + tm * k * 2 + 2 * k * tn * 2 + 3 * 2 * tm * tn * 2 + 2 * tm * tn * 4
           + k * LANES * 2 + 2 * tm * LANES * 4)
    kern = functools.partial(_in_proj_kernel, n_xbc=n_xbc, n_qkv=n_qkv, n_z=n_z)
    return pl.pallas_call(
        kern,
        out_shape=(jax.ShapeDtypeStruct((m, n_xbc_cols), BF16),
                   jax.ShapeDtypeStruct((n_qkv_cols // LANES, m, LANES), BF16),
                   jax.ShapeDtypeStruct((m, n_zg * tn), BF16),
                   jax.ShapeDtypeStruct((m, LANES), F32)),
        grid=(m // tm, n // tn),
        in_specs=[pl.BlockSpec((tm, k), lambda i, j: (i, 0)),
                  pl.BlockSpec((k, tn), lambda i, j: (0, j)),
                  _resident((k, LANES))],
        out_specs=(pl.BlockSpec((tm, tn), lambda i, j: (i, jnp.minimum(j, n_xbc - 1))),
                   pl.BlockSpec((per, tm, LANES), lambda i, j: (jnp.clip(j - n_xbc, 0, n_qkv - 1), i, 0)),
                   pl.BlockSpec((tm, tn), lambda i, j: (i, jnp.clip(j - n_xbc - n_qkv, 0, n_zg - 1))),
                   pl.BlockSpec((tm, LANES), lambda i, j: (i, 0))),
        scratch_shapes=[pltpu.VMEM((tm, k), BF16)],
        compiler_params=_params(("parallel", "arbitrary"), est),
        name="in_proj",
    )(x, w, w_dt)


def _split3(v):
    hi = v.astype(BF16)
    r1 = v - hi.astype(F32)
    mid = r1.astype(BF16)
    lo = (r1 - mid.astype(F32)).astype(BF16)
    return hi, mid, lo


def _shift_matrix():
    L = SSD_CHUNK
    s = np.zeros((SSD_CONV - 1, L, 2 * L), np.float32)
    for j in range(1, SSD_CONV):
        for t in range(L):
            s[j - 1, t, t - j if t >= j else L + CONV_TAIL - j + t] = 1.0
    return s.reshape((SSD_CONV - 1) * L, 2 * L)


def _ssd_kernel(z_ref, xbc_ref, dt_ref, shift_ref, convw_ref, convb_ref, dtb_ref, alog_ref,
                dskip_ref, g_ref, y_ref, cat_ref, state_ref, xa_ref, yacc_ref,
                *, inner, n_groups, d_state):
    L = SSD_CHUNK
    width = xbc_ref.shape[1]
    n_pairs = inner // LANES
    pairs_per_group = n_pairs // n_groups
    group_width = inner // n_groups

    @pl.when(pl.program_id(1) == 0)
    def _():
        cat_ref[...] = jnp.zeros_like(cat_ref)
        state_ref[...] = jnp.zeros_like(state_ref)

    slab = 512
    for c0 in range(0, width, slab):
        cs = slice(c0, c0 + slab)
        ub = xbc_ref[:, cs]
        cat_ref[0:L, cs] = ub
        sh = jnp.dot(shift_ref[...], cat_ref[:, cs], preferred_element_type=F32)
        acc = convb_ref[:, cs] + ub.astype(F32) * convw_ref[SSD_CONV - 1:SSD_CONV, cs]
        for j in range(1, SSD_CONV):
            acc = acc + sh[(j - 1) * L:j * L] * convw_ref[SSD_CONV - 1 - j:SSD_CONV - j, cs]
        cat_ref[L:L + CONV_TAIL, cs] = ub[L - CONV_TAIL:L]
        xa_ref[:, cs] = _silu(acc)

    dtr = dt_ref[...] + dtb_ref[...]
    dt = jnp.maximum(dtr, 0.0) + jnp.log(1.0 + jnp.exp(-jnp.abs(dtr)))
    a = -jnp.exp(alog_ref[...]) * dt
    row_i = lax.broadcasted_iota(jnp.int32, (L, L), 0)
    col_i = lax.broadcasted_iota(jnp.int32, (L, L), 1)
    causal = row_i >= col_i
    tril = jnp.where(causal, 1.0, 0.0).astype(BF16)
    acs = sum(jnp.dot(tril, part, preferred_element_type=F32) for part in _split3(a)) * LOG2E
    acs_t = acs.T
    dt_t = dt.T
    lane_lo = lax.broadcasted_iota(jnp.int32, (L, LANES), 1) < SSD_HEAD_DIM

    b0 = inner
    c0 = inner + n_groups * d_state
    for g in range(n_groups):
        bg = xa_ref[:, b0 + g * d_state:b0 + (g + 1) * d_state]
        cg = xa_ref[:, c0 + g * d_state:c0 + (g + 1) * d_state]
        cb = lax.dot_general(cg.astype(BF16), bg.astype(BF16), (((1,), (1,)), ((), ())),
                             preferred_element_type=F32)
        bg_t = bg.T
        for pp in range(g * pairs_per_group, (g + 1) * pairs_per_group):
            xs_pair = xa_ref[:, pp * LANES:(pp + 1) * LANES].astype(BF16)
            st_old = state_ref[pp]
            rhs = jnp.concatenate([xs_pair, st_old.astype(BF16)], axis=0)
            ys, sts = [], []
            for side in range(2):
                h = 2 * pp + side
                col = jnp.broadcast_to(acs[:, h:h + 1], (L, L))
                row = acs_t[h:h + 1, :]
                dt_row = dt_t[h:h + 1, :]
                last = acs[L - 1:L, h:h + 1]
                dec = jnp.exp2(jnp.where(causal, col - row, MASK_VALUE))
                m_in = (cb * dec * dt_row).astype(BF16)
                c_w = (cg * jnp.exp2(col[:, :d_state])).astype(BF16)
                lhs = jnp.concatenate([m_in, c_w], axis=1)
                ys.append(jnp.dot(lhs, rhs, preferred_element_type=F32))
                b_w = (bg_t * (jnp.exp2(last - row) * dt_row)).astype(BF16)
                st_new = jnp.dot(b_w, xs_pair, preferred_element_type=F32)
                sts.append(st_old * jnp.exp2(last) + st_new)
            yacc_ref[:, pp * LANES:(pp + 1) * LANES] = jnp.where(lane_lo, ys[0], ys[1])
            state_ref[pp] = jnp.where(lane_lo, sts[0], sts[1])

    for g in range(n_groups):
        cs = slice(g * group_width, (g + 1) * group_width)
        zz = z_ref[:, cs].astype(F32)
        yv = (yacc_ref[:, cs] + dskip_ref[:, cs] * xa_ref[:, cs]) * _silu(zz)
        ms = jnp.mean(yv * yv, axis=-1, keepdims=True)
        y_ref[:, cs] = (yv * lax.rsqrt(ms + NORM_EPS) * g_ref[:, cs]).astype(y_ref.dtype)


def _ssd_branch(zg, xbc, dt_raw, conv_w, conv_b, dt_bias, a_log, d_skip, norm_g, *, batch, seq):
    t = zg.shape[0]
    inner = norm_g.shape[0]
    width = conv_w.shape[1]
    heads = inner // SSD_HEAD_DIM
    n_chunks = seq // SSD_CHUNK
    L = SSD_CHUNK

    def pad_heads(v):
        return jnp.pad(v.astype(F32), (0, LANES - heads)).reshape(1, LANES)

    convw = jnp.pad(conv_w.astype(F32), ((0, SUBLANES - SSD_CONV), (0, 0)))
    dskip = jnp.repeat(d_skip.astype(F32), SSD_HEAD_DIM).reshape(1, inner)
    row = lambda b, c: (b * n_chunks + c, 0)
    const = lambda b, c: (0, 0)
    shift = jnp.asarray(_shift_matrix(), BF16)
    est = (2 * L * (inner + width) * 2 + 2 * L * LANES * 4 + 2 * L * inner * 2 + 2 * shift.size * 2
           + (inner // LANES) * SSD_STATE * LANES * 4 + L * (2 * width + inner) * 4 + (8 << 20))
    kern = functools.partial(_ssd_kernel, inner=inner, n_groups=SSD_GROUPS, d_state=SSD_STATE)
    return pl.pallas_call(
        kern,
        out_shape=jax.ShapeDtypeStruct((t, inner), BF16),
        grid=(batch, n_chunks),
        in_specs=[pl.BlockSpec((L, inner), row),
                  pl.BlockSpec((L, width), row),
                  pl.BlockSpec((L, LANES), row),
                  pl.BlockSpec(shift.shape, const),
                  pl.BlockSpec((SUBLANES, width), const),
                  pl.BlockSpec((1, width), const),
                  pl.BlockSpec((1, LANES), const),
                  pl.BlockSpec((1, LANES), const),
                  pl.BlockSpec((1, inner), const),
                  pl.BlockSpec((1, inner), const)],
        out_specs=pl.BlockSpec((L, inner), row),
        scratch_shapes=[pltpu.VMEM((2 * L, width), BF16),
                        pltpu.VMEM((inner // LANES, SSD_STATE, LANES), F32),
                        pltpu.VMEM((L, width), F32),
                        pltpu.VMEM((L, inner), F32)],
        compiler_params=_params(("parallel", "arbitrary"), est),
        name="ssd_scan",
    )(zg, xbc, dt_raw, shift, convw, conv_b.astype(F32).reshape(1, width), pad_heads(dt_bias),
      pad_heads(a_log), dskip, norm_g.astype(F32).reshape(1, inner))


def _bucket_tiles(tq):
    max_exact = REL_BUCKETS // 2
    qi = np.arange(tq)[:, None]
    ki = np.arange(tq)[None, :]

    def bucket(dist):
        d = np.maximum(dist, 1).astype(np.float32)
        large = max_exact + (np.log(d / np.float32(max_exact)) / np.float32(math.log(REL_MAX_DIST / max_exact))
                             * np.float32(REL_BUCKETS - max_exact)).astype(np.int32)
        large = np.minimum(large, REL_BUCKETS - 1)
        return np.where(dist < max_exact, dist, large).astype(np.int32)

    diag = np.where(qi >= ki, bucket(np.maximum(qi - ki, 0)), -1)
    prev = bucket(tq + qi - ki)
    far = bucket(np.arange(tq + 1, 1 << 16))
    assert (far == REL_BUCKETS - 1).all()
    return np.stack([diag, prev]).astype(np.int32)


def _bias_kernel(rb_ref, bucket_ref, o_ref):
    h = pl.program_id(0)
    bk = bucket_ref[...]
    acc = jnp.zeros(bk.shape, F32)
    for b in range(REL_BUCKETS):
        acc = jnp.where(bk == b, rb_ref[b, h], acc)
    o_ref[0] = jnp.where(bk < 0, MASK_VALUE, (acc - rb_ref[REL_BUCKETS - 1, h]) * LOG2E)


def _bias_tiles(rel_bias, tq):
    buckets = jnp.asarray(_bucket_tiles(tq))
    return pl.pallas_call(
        _bias_kernel,
        out_shape=jax.ShapeDtypeStruct((DIFF_HEADS, 2, tq, tq), F32),
        grid=(DIFF_HEADS,),
        in_specs=[pl.BlockSpec(memory_space=pltpu.SMEM),
                  pl.BlockSpec((2, tq, tq), lambda h: (0, 0, 0))],
        out_specs=pl.BlockSpec((1, 2, tq, tq), lambda h: (h, 0, 0, 0)),
        compiler_params=_params(("arbitrary",), 8 * tq * tq * 4),
        name="t5_bias_tiles",
    )(rel_bias.astype(F32), buckets)


def _attn_kernel(lam_ref, q_ref, k_ref, v_ref, bias_ref, g_ref, o_ref,
                 q2_ref, vaug_ref, m_ref, acc_ref, *, tq, rows, lam_init):
    i = pl.program_id(2)
    dh = LANES // 2

    @pl.when(i == 0)
    def _():
        vaug_ref[:, 0:LANES] = v_ref[0]
        vaug_ref[:, LANES:2 * LANES] = jnp.ones((v_ref.shape[1], LANES), BF16)

    lane = lax.broadcasted_iota(jnp.int32, (tq, LANES), 1)
    qs = (q_ref[0].astype(F32) * (dh ** -0.5 * LOG2E)).astype(BF16)
    zero = jnp.zeros_like(qs)
    q2_ref[0:tq] = jnp.where(lane < dh, qs, zero)
    q2_ref[tq:2 * tq] = jnp.where(lane >= dh, qs, zero)
    m_ref[...] = jnp.full(m_ref.shape, MASK_VALUE, F32)
    acc_ref[...] = jnp.zeros_like(acc_ref)

    def step(j, bias_idx):
        start = pl.multiple_of(j * tq, tq)
        s_all = lax.dot_general(q2_ref[...], k_ref[0, pl.ds(start, tq), :], (((1,), (1,)), ((), ())),
                                preferred_element_type=F32)
        for r0 in range(0, 2 * tq, rows):
            rs = slice(r0, r0 + rows)
            q0 = r0 % tq
            klen = q0 + rows if bias_idx == 0 else tq
            vb = vaug_ref[pl.ds(start, klen), :]
            s = s_all[rs, 0:klen]
            if bias_idx is not None:
                s = s + bias_ref[0, bias_idx, q0:q0 + rows, 0:klen]
            m_old = m_ref[rs]
            m_new = jnp.maximum(m_old, jnp.max(s, axis=1, keepdims=True))
            alpha = jnp.exp2(m_old - m_new)
            p = jnp.exp2(s - jnp.concatenate([m_new] * (klen // LANES), axis=1))
            pv = jnp.dot(p.astype(BF16), vb, preferred_element_type=F32)
            acc_ref[rs] = jnp.concatenate([alpha, alpha], axis=1) * acc_ref[rs] + pv
            m_ref[rs] = m_new

    def far_step(j, carry):
        step(j, None)
        return carry

    lax.fori_loop(0, jnp.maximum(i - 1, 0), far_step, 0)

    @pl.when(i >= 1)
    def _():
        step(i - 1, 1)

    step(i, 0)

    lv = lam_ref[...]
    s1 = jnp.sum(lv[0:1] * lv[1:2], axis=1, keepdims=True)
    s2 = jnp.sum(lv[2:3] * lv[3:4], axis=1, keepdims=True)
    lam = jnp.exp(s1) - jnp.exp(s2) + lam_init
    o1 = acc_ref[0:tq, 0:LANES] / acc_ref[0:tq, LANES:2 * LANES]
    o2 = acc_ref[tq:2 * tq, 0:LANES] / acc_ref[tq:2 * tq, LANES:2 * LANES]
    o = o1 - lam * o2
    ms = jnp.mean(o * o, axis=-1, keepdims=True)
    o_ref[...] = (o * lax.rsqrt(ms + NORM_EPS) * g_ref[...] * (1.0 - lam_init)).astype(o_ref.dtype)


def _diff_attention(qkv, lam_vecs, bias_tiles, subln_g, *, batch, seq, tq, layer_idx):
    t = qkv.shape[1]
    nq = seq // tq
    rows = min(256, tq)
    lam_init = 0.8 - 0.6 * math.exp(-0.3 * layer_idx)
    kern = functools.partial(_attn_kernel, tq=tq, rows=rows, lam_init=lam_init)
    est = (2 * tq * LANES * 2 + 4 * seq * LANES * 2 + 4 * tq * tq * 4 + 2 * tq * LANES * 2
           + 2 * tq * LANES * 2 + seq * 2 * LANES * 2 + 2 * tq * LANES * 4 + 2 * tq * 2 * LANES * 4
           + 8 * rows * tq * 4)
    return pl.pallas_call(
        kern,
        out_shape=jax.ShapeDtypeStruct((t, DIFF_HEADS * LANES), BF16),
        grid=(batch, DIFF_HEADS, nq),
        in_specs=[pl.BlockSpec((SUBLANES, LANES), lambda b, h, i: (0, 0)),
                  pl.BlockSpec((1, tq, LANES), lambda b, h, i: (h, b * nq + i, 0)),
                  pl.BlockSpec((1, seq, LANES), lambda b, h, i: (DIFF_HEADS + h, b, 0)),
                  pl.BlockSpec((1, seq, LANES), lambda b, h, i: (2 * DIFF_HEADS + h, b, 0)),
                  pl.BlockSpec((1, 2, tq, tq), lambda b, h, i: (h, 0, 0, 0)),
                  pl.BlockSpec((1, LANES), lambda b, h, i: (0, 0))],
        out_specs=pl.BlockSpec((tq, LANES), lambda b, h, i: (b * nq + i, h)),
        scratch_shapes=[pltpu.VMEM((2 * tq, LANES), BF16),
                        pltpu.VMEM((seq, 2 * LANES), BF16),
                        pltpu.VMEM((2 * tq, LANES), F32),
                        pltpu.VMEM((2 * tq, 2 * LANES), F32)],
        compiler_params=_params(("parallel", "parallel", "arbitrary"), est),
        name="diff_attention",
    )(lam_vecs, qkv, qkv, qkv, bias_tiles, subln_g)


def _mix_kernel(x_ref, y_ref, o_ref, gate_ref, wssd_ref, wdiff_ref, wmix_ref, g_ref, b_ref, out_ref):
    d = x_ref.shape[1]
    ssd = jnp.dot(y_ref[...], wssd_ref[...], preferred_element_type=F32)
    dif = jnp.dot(o_ref[...], wdiff_ref[...], preferred_element_type=F32)
    merged = gate_ref[:, 0:d].astype(F32) * ssd + gate_ref[:, d:2 * d].astype(F32) * dif
    mixed = jnp.dot(merged.astype(BF16), wmix_ref[...], preferred_element_type=F32)
    out_ref[...] = _layer_norm(DN_ALPHA * x_ref[...] + mixed, g_ref[...], b_ref[...])


def _mix(x, y_ssd, o_diff, proj, w_ssd, w_diff, w_mix, ln_g, ln_b, *, tm, gate_block):
    t, d = x.shape
    tm = min(tm, t)
    inner = y_ssd.shape[1]
    dw = o_diff.shape[1]
    row = lambda i: (i, 0)
    est = (2 * tm * (d * 4 + inner * 2 + dw * 2 + 2 * d * 2 + d * 4)
           + (inner * d + dw * d + d * d) * 2 + 6 * tm * d * 4)
    return pl.pallas_call(
        _mix_kernel,
        out_shape=jax.ShapeDtypeStruct((t, d), F32),
        grid=(t // tm,),
        in_specs=[pl.BlockSpec((tm, d), row),
                  pl.BlockSpec((tm, inner), row),
                  pl.BlockSpec((tm, dw), row),
                  pl.BlockSpec((tm, 2 * d), lambda i: (i, gate_block)),
                  _resident((inner, d)), _resident((dw, d)), _resident((d, d)),
                  _resident((1, d)), _resident((1, d))],
        out_specs=pl.BlockSpec((tm, d), row),
        compiler_params=_params(("parallel",), est),
        name="mix_ln1",
    )(x, y_ssd, o_diff, proj, w_ssd, w_diff, w_mix, ln_g, ln_b)


def _pack_bf16_pairs(v):
    w = v.shape[1] // 2
    lo = lax.bitcast_convert_type(v[:, :w].astype(BF16).astype(F32), jnp.int32)
    hi = lax.bitcast_convert_type(v[:, w:].astype(BF16).astype(F32), jnp.int32)
    return jnp.bitwise_or(hi, lax.shift_right_logical(lo, 16))


def _unpack_bf16_pairs(words):
    lo = lax.bitcast_convert_type(lax.shift_left(words, 16), F32)
    hi = lax.bitcast_convert_type(jnp.bitwise_and(words, -65536), F32)
    return jnp.concatenate([lo, hi], axis=1)


def _xattn_kernel(x_ref, kv_ref, wq_ref, wo_ref, g_ref, b_ref, out_ref, packed_ref, o_scr, *, heads):
    d = x_ref.shape[1]
    dh = d // heads
    xv = x_ref[...]
    q = jnp.dot(xv.astype(BF16), wq_ref[...], preferred_element_type=F32)
    q = (q * (dh ** -0.5)).astype(BF16)
    for h in range(heads):
        kh = kv_ref[:, h * dh:(h + 1) * dh]
        vh = kv_ref[:, d + h * dh:d + (h + 1) * dh]
        s = lax.dot_general(q[:, h * dh:(h + 1) * dh], kh, (((1,), (1,)), ((), ())),
                            preferred_element_type=F32)
        p = jnp.exp(s - jnp.max(s, axis=1, keepdims=True))
        p = p / jnp.sum(p, axis=1, keepdims=True)
        o_scr[:, h * dh:(h + 1) * dh] = jnp.dot(p.astype(BF16), vh, preferred_element_type=F32).astype(BF16)
    att = jnp.dot(o_scr[...], wo_ref[...], preferred_element_type=F32)
    y = _layer_norm(DN_ALPHA * xv + att, g_ref[...], b_ref[...])
    out_ref[...] = y
    packed_ref[...] = _pack_bf16_pairs(y)


def _cross_attention(x, kv, w_cq, w_co, ln_g, ln_b, *, batch, seq, tq):
    t, d = x.shape
    mem_len = kv.shape[0] // batch
    tq = min(tq, seq)
    nq = seq // tq
    est = (2 * tq * d * 4 * 2 + 2 * tq * d * 2 + 2 * mem_len * 2 * d * 2 + 2 * d * d * 2 + tq * d * 2
           + 8 * tq * d * 4)
    row = lambda b, i: (b * nq + i, 0)
    return pl.pallas_call(
        functools.partial(_xattn_kernel, heads=MEM_HEADS),
        out_shape=(jax.ShapeDtypeStruct((t, d), F32), jax.ShapeDtypeStruct((t, d // 2), jnp.int32)),
        grid=(batch, nq),
        in_specs=[pl.BlockSpec((tq, d), row),
                  pl.BlockSpec((mem_len, 2 * d), lambda b, i: (b, 0)),
                  _resident((d, d)), _resident((d, d)), _resident((1, d)), _resident((1, d))],
        out_specs=(pl.BlockSpec((tq, d), row), pl.BlockSpec((tq, d // 2), row)),
        scratch_shapes=[pltpu.VMEM((tq, d), BF16)],
        compiler_params=_params(("parallel", "parallel"), est),
        name="cross_attention_ln2",
    )(x, kv, w_cq, w_co, ln_g, ln_b)


def _router_kernel(x_ref, wr_ref, rb_ref, idx_ref, rank_ref, gates_ref, cnt_ref, run_ref, *, n_experts):
    tm = x_ref.shape[0]
    n_groups = N_EXPERT_GROUPS
    per = n_experts // n_groups

    @pl.when(pl.program_id(0) == 0)
    def _():
        run_ref[...] = jnp.zeros_like(run_ref)

    xv = x_ref[...]
    x_hi = xv.astype(BF16)
    x_lo = (xv - x_hi.astype(F32)).astype(BF16)
    nt = (((1,), (1,)), ((), ()))
    logits = (lax.dot_general(wr_ref[0], x_hi, nt, preferred_element_type=F32)
              + lax.dot_general(wr_ref[0], x_lo, nt, preferred_element_type=F32)
              + lax.dot_general(wr_ref[1], x_hi, nt, preferred_element_type=F32))
    sc = _sigmoid(logits[0:n_experts]).reshape(per, n_groups, tm)
    choice = sc + rb_ref[0:n_experts].reshape(per, n_groups, 1)
    neg = -jnp.inf
    member = lax.broadcasted_iota(jnp.int32, (per, n_groups, tm), 0)
    group3 = lax.broadcasted_iota(jnp.int32, (per, n_groups, tm), 1)
    m1 = jnp.max(choice, axis=0, keepdims=True)
    i1 = jnp.min(jnp.where(choice == m1, member, per), axis=0, keepdims=True)
    m2 = jnp.max(jnp.where(member == i1, neg, choice), axis=0, keepdims=True)
    gscore = (m1 + m2)
    gsel = jnp.zeros((1, n_groups, tm), F32)
    gidx = lax.broadcasted_iota(jnp.int32, (1, n_groups, tm), 1)
    cur = gscore
    for _ in range(TOP_GROUPS):
        mx = jnp.max(cur, axis=1, keepdims=True)
        ix = jnp.min(jnp.where(cur == mx, gidx, n_groups), axis=1, keepdims=True)
        hit = gidx == ix
        gsel = jnp.where(hit, 1.0, gsel)
        cur = jnp.where(hit, neg, cur)
    cur = jnp.where(gsel > 0.0, choice, neg)
    eidx = group3 * per + member
    esel = jnp.zeros((per, n_groups, tm), F32)
    hits, idx_rows = [], []
    for _ in range(TOP_K):
        mx = jnp.max(jnp.max(cur, axis=0, keepdims=True), axis=1, keepdims=True)
        ix = jnp.min(jnp.min(jnp.where(cur == mx, eidx, n_experts), axis=0, keepdims=True),
                     axis=1, keepdims=True)
        hit = eidx == ix
        esel = jnp.where(hit, 1.0, esel)
        cur = jnp.where(hit, neg, cur)
        hits.append(hit)
        idx_rows.append(ix.reshape(1, tm))
    w = esel * sc
    tot = jnp.sum(jnp.sum(w, axis=0, keepdims=True), axis=1, keepdims=True)
    gw = w / tot * ROUTED_SCALE

    r_i = lax.broadcasted_iota(jnp.int32, (tm, tm), 0)
    c_i = lax.broadcasted_iota(jnp.int32, (tm, tm), 1)
    upper = jnp.where(r_i <= c_i, 1.0, 0.0).astype(BF16)
    pref = jnp.dot(esel.reshape(n_experts, tm).astype(BF16), upper, preferred_element_type=F32)
    run = run_ref[...]
    rank3 = (jnp.concatenate([run] * (tm // LANES), axis=1) + pref - 1.0).reshape(per, n_groups, tm)

    def pick(hit, vals):
        return jnp.sum(jnp.sum(jnp.where(hit, vals, 0.0), axis=0, keepdims=True), axis=1).reshape(1, tm)

    idx_ref[...] = jnp.concatenate(idx_rows, axis=0)
    rank_ref[...] = jnp.concatenate([pick(h, rank3) for h in hits], axis=0).astype(jnp.int32)
    wk = jnp.concatenate([pick(h, gw) for h in hits] + [jnp.zeros((LANES - TOP_K, tm), F32)], axis=0)
    gates_ref[...] = wk.T
    run = run + jnp.broadcast_to(pref[:, tm - 1:tm], run.shape)
    run_ref[...] = run
    cnt_ref[...] = run


def _router(x, w_router, router_bias, *, tm, row0, t):
    d = x.shape[1]
    n_experts = w_router.shape[1]
    per = n_experts // N_EXPERT_GROUPS
    tm = min(tm, t)
    blk0 = row0 // tm

    def member_major(v):
        return v.reshape(N_EXPERT_GROUPS, per, *v.shape[1:]).swapaxes(0, 1).reshape(v.shape)

    wt = jnp.pad(member_major(w_router.astype(F32).T), ((0, LANES - n_experts), (0, 0)))
    w_hi = wt.astype(BF16)
    wr = jnp.stack([w_hi, (wt - w_hi.astype(F32)).astype(BF16)])
    rb = jnp.pad(member_major(router_bias.astype(F32)), (0, LANES - n_experts)).reshape(LANES, 1)
    est = (2 * tm * d * 4 + 2 * LANES * d * 2 + 2 * tm * LANES * 4 + 60 * n_experts * tm * 4 + 3 * tm * tm * 4)
    pick_spec = pl.BlockSpec((TOP_K, tm), lambda i: (0, i))
    idx_t, rank_t, gates, counts = pl.pallas_call(
        functools.partial(_router_kernel, n_experts=n_experts),
        out_shape=(jax.ShapeDtypeStruct((TOP_K, t), jnp.int32),
                   jax.ShapeDtypeStruct((TOP_K, t), jnp.int32),
                   jax.ShapeDtypeStruct((t, LANES), F32),
                   jax.ShapeDtypeStruct((n_experts, LANES), F32)),
        grid=(t // tm,),
        in_specs=[pl.BlockSpec((tm, d), lambda i: (blk0 + i, 0)),
                  _resident((2, LANES, d)), _resident((LANES, 1))],
        out_specs=(pick_spec, pick_spec,
                   pl.BlockSpec((tm, LANES), lambda i: (i, 0)),
                   pl.BlockSpec((n_experts, LANES), lambda i: (0, 0))),
        scratch_shapes=[pltpu.VMEM((n_experts, LANES), F32)],
        compiler_params=_params(("arbitrary",), est),
        name="router",
    )(x, wr, rb)
    counts = counts[:, 0].astype(jnp.int32).reshape(per, N_EXPERT_GROUPS).T.reshape(n_experts)
    return idx_t, rank_t, gates, counts


def _positions_kernel(cnt_ref, idx_ref, rank_ref, pos_ref, texp_ref, off_ref, *, n_experts, rows, n_tiles):
    @pl.when(pl.program_id(0) == 0)
    def _():
        def per_expert(e, toff):
            off_ref[e] = toff * rows
            nt = lax.shift_right_logical(cnt_ref[e] + (rows - 1), int(math.log2(rows)))

            def fill(j, c):
                texp_ref[toff + j] = e
                return c

            lax.fori_loop(0, nt, fill, 0)
            return toff + nt

        n_used = lax.fori_loop(0, n_experts, per_expert, 0)

        def tail(j, c):
            texp_ref[j] = n_experts - 1
            return c

        lax.fori_loop(n_used, n_tiles, tail, 0)
        texp_ref[n_tiles] = n_used

    idx = idx_ref[...]
    pos = rank_ref[...]
    for e in range(n_experts):
        pos = pos + jnp.where(idx == e, off_ref[e], 0)
    pos_ref[...] = pos


def _positions(counts, idx_t, rank_t, *, rows, n_tiles, tm):
    n_experts = counts.shape[0]
    k, t = idx_t.shape
    tm = min(tm, t)
    spec = pl.BlockSpec((k, tm), lambda i: (0, i))
    kern = functools.partial(_positions_kernel, n_experts=n_experts, rows=rows, n_tiles=n_tiles)
    return pl.pallas_call(
        kern,
        out_shape=(jax.ShapeDtypeStruct((k, t), jnp.int32),
                   jax.ShapeDtypeStruct((n_tiles + 1,), jnp.int32)),
        grid=(t // tm,),
        in_specs=[pl.BlockSpec(memory_space=pltpu.SMEM), spec, spec],
        out_specs=(spec, pl.BlockSpec(memory_space=pltpu.SMEM)),
        scratch_shapes=[pltpu.SMEM((n_experts,), jnp.int32)],
        compiler_params=_params(("arbitrary",), 16 * k * tm * 4),
        name="moe_positions",
    )(counts, idx_t, rank_t)


def _sc_mesh():
    return plsc.VectorSubcoreMesh(core_axis_name="c", subcore_axis_name="s",
                                  num_cores=SC_CORES, num_subcores=SC_SUBCORES)


def _sc_dispatch(packed, pos_chunks, n_rows, row0):
    w = packed.shape[1]
    n_chunks, k, n = pos_chunks.shape
    per_worker = n_chunks // (SC_CORES * SC_SUBCORES)

    @functools.partial(
        pl.kernel, mesh=_sc_mesh(),
        out_type=jax.ShapeDtypeStruct((n_rows, w), packed.dtype),
        scratch_types=[pltpu.VMEM((k, n), jnp.int32), pltpu.VMEM((n, w), packed.dtype),
                       pltpu.SemaphoreType.DMA],
        name="moe_dispatch_sc",
    )
    def scatter_rows(x_hbm, pos_hbm, out_hbm, idx_v, rows_v, sem):
        wid = lax.axis_index("s") * SC_CORES + lax.axis_index("c")

        @pl.loop(0, per_worker)
        def _(step):
            c = wid * per_worker + step
            pltpu.sync_copy(pos_hbm.at[c], idx_v)
            pltpu.sync_copy(x_hbm.at[pl.ds(row0 + c * n, n)], rows_v)
            copies = [pltpu.async_copy(rows_v, out_hbm.at[idx_v.at[kk]], sem) for kk in range(k)]
            for cp in copies:
                cp.wait()

    return scatter_rows(packed, pos_chunks)


def _sc_combine(sorted_rows, pos_chunks):
    _, w = sorted_rows.shape
    n_chunks, k, n = pos_chunks.shape
    per_worker = n_chunks // (SC_CORES * SC_SUBCORES)

    @functools.partial(
        pl.kernel, mesh=_sc_mesh(),
        out_type=jax.ShapeDtypeStruct((k, n_chunks * n, w), sorted_rows.dtype),
        scratch_types=[pltpu.VMEM((k, n), jnp.int32), pltpu.VMEM((n, w), sorted_rows.dtype),
                       pltpu.SemaphoreType.DMA],
        name="moe_combine_sc",
    )
    def gather_rows(y_hbm, pos_hbm, out_hbm, idx_v, rows_v, sem):
        wid = lax.axis_index("s") * SC_CORES + lax.axis_index("c")

        @pl.loop(0, per_worker)
        def _(step):
            c = wid * per_worker + step
            pltpu.sync_copy(pos_hbm.at[c], idx_v)
            for kk in range(k):
                pltpu.async_copy(y_hbm.at[idx_v.at[kk]], rows_v, sem).wait()
                pltpu.sync_copy(rows_v, out_hbm.at[kk, pl.ds(c * n, n)])

    return gather_rows(sorted_rows, pos_chunks)


def _expert_kernel(texp_ref, xs_ref, wgu_ref, wdn_ref, ys_ref, *, n_tiles):
    ff = wdn_ref.shape[1]

    @pl.when(pl.program_id(0) < texp_ref[n_tiles])
    def _():
        xv = _unpack_bf16_pairs(xs_ref[...]).astype(BF16)
        gu = jnp.dot(xv, wgu_ref[0], preferred_element_type=F32)
        hid = _silu(gu[:, 0:ff]) * gu[:, ff:2 * ff]
        ys_ref[...] = _pack_bf16_pairs(jnp.dot(hid.astype(BF16), wdn_ref[0], preferred_element_type=F32))


def _experts(tile_expert, sorted_rows, w_gu, w_dn, *, rows):
    n_rows, w = sorted_rows.shape
    n_tiles = n_rows // rows
    _, d, ff2 = w_gu.shape
    ff = ff2 // 2
    tile = lambda j, te: (jnp.minimum(j, te[n_tiles] - 1), 0)
    est = 4 * rows * w * 4 + 2 * (d * ff2 + ff * d) * 2 + 4 * rows * d * 4 + 4 * rows * ff2 * 4
    return pl.pallas_call(
        functools.partial(_expert_kernel, n_tiles=n_tiles),
        out_shape=jax.ShapeDtypeStruct((n_rows, w), sorted_rows.dtype),
        grid_spec=pltpu.PrefetchScalarGridSpec(
            num_scalar_prefetch=1,
            grid=(n_tiles,),
            in_specs=[pl.BlockSpec((rows, w), tile),
                      pl.BlockSpec((1, d, ff2), lambda j, te: (te[j], 0, 0)),
                      pl.BlockSpec((1, ff, d), lambda j, te: (te[j], 0, 0))],
            out_specs=pl.BlockSpec((rows, w), tile)),
        compiler_params=_params(("arbitrary",), est),
        name="moe_experts",
    )(tile_expert, sorted_rows, w_gu, w_dn)


def _moe_out_kernel(x_ref, yk_ref, gates_ref, wsgu_ref, wsdn_ref, g_ref, b_ref, *rest):
    out_ref = rest[-1]
    ff = wsdn_ref.shape[0]
    xv = x_ref[...]
    gu = jnp.dot(xv.astype(BF16), wsgu_ref[...], preferred_element_type=F32)
    hid = _silu(gu[:, 0:ff]) * gu[:, ff:2 * ff]
    acc = jnp.dot(hid.astype(BF16), wsdn_ref[...], preferred_element_type=F32)
    for k in range(yk_ref.shape[0]):
        acc = acc + gates_ref[:, k:k + 1] * _unpack_bf16_pairs(yk_ref[k])
    out_ref[...] = _layer_norm(DN_ALPHA * xv + acc, g_ref[...], b_ref[...])


def _moe_out(x, yk, gates, w_sgu, w_sdn, ln_g, ln_b, *, tm, row0, prev):
    t_all, d = x.shape
    k, t, w = yk.shape
    sff2 = w_sgu.shape[1]
    tm = min(tm, t)
    blk0 = row0 // tm
    est = (2 * tm * d * 4 * 2 + 2 * k * tm * w * 4 + 2 * tm * LANES * 4
           + (d * sff2 + (sff2 // 2) * d) * 2 + 6 * tm * d * 4)
    in_specs = [pl.BlockSpec((tm, d), lambda i: (blk0 + i, 0)),
                pl.BlockSpec((k, tm, w), lambda i: (0, i, 0)),
                pl.BlockSpec((tm, LANES), lambda i: (i, 0)),
                _resident((d, sff2)), _resident((sff2 // 2, d)),
                _resident((1, d)), _resident((1, d))]
    args = [x, yk, gates, w_sgu, w_sdn, ln_g, ln_b]
    aliases = {}
    if prev is not None:
        in_specs.append(pl.BlockSpec(memory_space=pl.ANY))
        args.append(prev)
        aliases = {len(args) - 1: 0}
    return pl.pallas_call(
        _moe_out_kernel,
        out_shape=jax.ShapeDtypeStruct((t_all, d), F32),
        grid=(t // tm,),
        in_specs=in_specs,
        out_specs=pl.BlockSpec((tm, d), lambda i: (blk0 + i, 0)),
        input_output_aliases=aliases,
        compiler_params=_params(("parallel",), est),
        name="moe_out_ln3",
    )(*args)


def _moe(x, packed, w_router, router_bias, w_gu, w_dn, w_sgu, w_sdn, ln_g, ln_b):
    t_all, d = x.shape
    n_experts = w_router.shape[1]
    n = SC_CHUNK_TOKENS
    splits = MOE_SPLITS if t_all % (MOE_SPLITS * n * SC_CORES * SC_SUBCORES) == 0 else 1
    t = t_all // splits
    rows = min(MOE_ROWS_PER_TILE, t)
    n_tiles = (t * TOP_K) // rows + n_experts
    parts = []
    for s in range(splits):
        row0 = s * t
        idx_t, rank_t, gates, counts = _router(x, w_router, router_bias, tm=512, row0=row0, t=t)
        pos_t, tile_expert = _positions(counts, idx_t, rank_t, rows=rows, n_tiles=n_tiles, tm=2048)
        pos_chunks = pos_t.reshape(TOP_K, t // n, n).transpose(1, 0, 2)
        sorted_x = _sc_dispatch(packed, pos_chunks, n_tiles * rows, row0)
        parts.append((row0, gates, tile_expert, pos_chunks, sorted_x))
    out = None
    for row0, gates, tile_expert, pos_chunks, sorted_x in parts:
        sorted_y = _experts(tile_expert, sorted_x, w_gu, w_dn, rows=rows)
        yk = _sc_combine(sorted_y, pos_chunks)
        out = _moe_out(x, yk, gates, w_sgu, w_sdn, ln_g, ln_b, tm=256, row0=row0, prev=out)
    return out


def kernel(x, mem, w_in, conv_w, conv_b, dt_bias, a_log, d_skip, ssd_norm_g, lambda_q1, lambda_k1, lambda_q2, lambda_k2, subln_g, rel_bias, w_ssd_br, w_diff_br, w_mix_out, ln1_g, ln1_b, w_cq, w_ckv, w_co, ln2_g, ln2_b, w_router, router_bias, w_exp_gu, w_exp_down, w_sh_gu, w_sh_down, ln3_g, ln3_b):
    batch, seq, d = x.shape
    depth = w_in.shape[0]
    inner = w_ssd_br.shape[1]
    xbc_width = conv_w.shape[2]
    heads = dt_bias.shape[1]
    diff_width = w_diff_br.shape[1]
    o_z, o_xbc = inner, inner + xbc_width
    o_dt = o_xbc + heads
    o_v = o_dt + 3 * diff_width
    t = batch * seq
    tq_attn = min(512, seq)

    def vec(v):
        return v.astype(F32).reshape(1, -1)

    xt = x.reshape(t, d)
    memt = mem.reshape(-1, d)
    bias_tiles = _bias_tiles(rel_bias, tq_attn)

    for l in range(depth):
        wl = w_in[l]
        w_big = jnp.concatenate([wl[:, o_z:o_xbc], wl[:, o_dt:o_v], wl[:, :o_z], wl[:, o_v:]],
                                axis=1).astype(BF16)
        assert 2 * d == inner
        w_dt = jnp.pad(wl[:, o_xbc:o_dt], ((0, 0), (0, LANES - heads))).astype(BF16)
        xbc, qkv, zg, dt_raw = _in_proj(xt, w_big, w_dt, n_xbc_cols=xbc_width, n_qkv_cols=3 * diff_width,
                                        n_z_cols=inner, tm=1024, tn=1024)

        y_ssd = _ssd_branch(zg, xbc, dt_raw, conv_w[l], conv_b[l], dt_bias[l], a_log[l], d_skip[l],
                            ssd_norm_g[l], batch=batch, seq=seq)

        lam_vecs = jnp.pad(jnp.stack([lambda_q1[l], lambda_k1[l], lambda_q2[l], lambda_k2[l]]).astype(F32),
                           ((0, SUBLANES - 4), (0, LANES - lambda_q1.shape[1])))
        o_diff = _diff_attention(qkv, lam_vecs, bias_tiles, vec(subln_g[l]), batch=batch, seq=seq,
                                 tq=tq_attn, layer_idx=l)

        xt = _mix(xt, y_ssd, o_diff, zg, w_ssd_br[l].astype(BF16), w_diff_br[l].astype(BF16),
                  w_mix_out[l].astype(BF16), vec(ln1_g[l]), vec(ln1_b[l]), tm=512, gate_block=1)

        kv = _matmul(memt, w_ckv[l].astype(BF16), BF16, tm=1024, tn=1024, name="mem_kv_proj")
        xt, packed = _cross_attention(xt, kv, w_cq[l].astype(BF16), w_co[l].astype(BF16), vec(ln2_g[l]),
                                      vec(ln2_b[l]), batch=batch, seq=seq, tq=1024)

        xt = _moe(xt, packed, w_router[l], router_bias[l], w_exp_gu[l].astype(BF16),
                  w_exp_down[l].astype(BF16), w_sh_gu[l].astype(BF16), w_sh_down[l].astype(BF16),
                  vec(ln3_g[l]), vec(ln3_b[l]))
    return xt.reshape(batch, seq, d)
```

```python
import functools
import math

import numpy as np
import jax
import jax.numpy as jnp
from jax import lax
from jax.experimental import pallas as pl
from jax.experimental.pallas import tpu as pltpu
from jax.experimental.pallas import tpu_sc as plsc

F32 = jnp.float32
BF16 = jnp.bfloat16

SSD_HEAD_DIM = 64
SSD_GROUPS = 4
SSD_STATE = 128
SSD_CONV = 4
SSD_CHUNK = 128
DIFF_HEADS = 8
REL_BUCKETS = 32
REL_MAX_DIST = 128
MEM_HEADS = 4
TOP_K = 8
N_EXPERT_GROUPS = 8
TOP_GROUPS = 4
ROUTED_SCALE = 2.5
NORM_EPS = 1e-5
DEPTH = 1
DN_ALPHA = (2.0 * DEPTH) ** 0.25

LANES = 128
SUBLANES = 8
VMEM_CAP_BYTES = 64 * 1024 * 1024
MASK_VALUE = -1e30
LOG2E = math.log2(math.e)
SC_CORES = 2
SC_SUBCORES = 16
SC_CHUNK_TOKENS = 128
MOE_SPLITS = 2
MOE_ROWS_PER_TILE = 1024


def _vmem_limit(estimate_bytes):
    return int(min(estimate_bytes * 5 // 4 + (4 << 20), VMEM_CAP_BYTES - (6 << 20)))


def _params(semantics, vmem_estimate):
    return pltpu.CompilerParams(dimension_semantics=semantics,
                                vmem_limit_bytes=_vmem_limit(vmem_estimate))


def _resident(shape):
    nd = len(shape)
    return pl.BlockSpec(shape, lambda *_: (0,) * nd, pipeline_mode=pl.Buffered(1))


def _layer_norm(v, g, b):
    mu = jnp.mean(v, axis=-1, keepdims=True)
    d = v - mu
    var = jnp.mean(d * d, axis=-1, keepdims=True)
    return d * lax.rsqrt(var + NORM_EPS) * g + b


def _sigmoid(v):
    return 0.5 * jnp.tanh(0.5 * v) + 0.5


def _silu(v):
    h = 0.5 * v
    return h + h * jnp.tanh(h)


def _matmul_kernel(x_ref, w_ref, o_ref, xb_ref):
    @pl.when(pl.program_id(1) == 0)
    def _():
        xb_ref[...] = x_ref[...].astype(BF16)

    o_ref[...] = jnp.dot(xb_ref[...], w_ref[...], preferred_element_type=F32).astype(o_ref.dtype)


def _matmul(x, w, out_dtype, *, tm, tn, name):
    m, k = x.shape
    n = w.shape[1]
    tm, tn = min(tm, m), min(tn, n)
    est = (2 * tm * k * x.dtype.itemsize + tm * k * 2 + 2 * k * tn * 2
           + 2 * tm * tn * jnp.dtype(out_dtype).itemsize + tm * tn * 4)
    return pl.pallas_call(
        _matmul_kernel,
        out_shape=jax.ShapeDtypeStruct((m, n), out_dtype),
        grid=(m // tm, n // tn),
        in_specs=[pl.BlockSpec((tm, k), lambda i, j: (i, 0)),
                  pl.BlockSpec((k, tn), lambda i, j: (0, j))],
        out_specs=pl.BlockSpec((tm, tn), lambda i, j: (i, j)),
        scratch_shapes=[pltpu.VMEM((tm, k), BF16)],
        compiler_params=_params(("parallel", "arbitrary"), est),
        name=name,
    )(x, w)


def _in_proj_kernel(x_ref, w_ref, wdt_ref, xbc_ref, qkv_ref, zg_ref, dt_ref, xb_ref, *, n_xbc, n_qkv, n_z):
    j = pl.program_id(1)
    j_z = n_xbc + n_qkv

    @pl.when(j == 0)
    def _():
        xb = x_ref[...].astype(BF16)
        xb_ref[...] = xb
        dt_ref[...] = jnp.dot(xb, wdt_ref[...], preferred_element_type=F32)

    acc = jnp.dot(xb_ref[...], w_ref[...], preferred_element_type=F32)

    @pl.when(j < n_xbc)
    def _():
        xbc_ref[...] = acc.astype(BF16)

    @pl.when(jnp.logical_and(j >= n_xbc, j < j_z))
    def _():
        for hh in range(qkv_ref.shape[0]):
            qkv_ref[hh] = acc[:, hh * LANES:(hh + 1) * LANES].astype(BF16)

    @pl.when(jnp.logical_and(j >= j_z, j < j_z + n_z))
    def _():
        zg_ref[...] = acc.astype(BF16)

    @pl.when(j >= j_z + n_z)
    def _():
        zg_ref[...] = _sigmoid(acc).astype(BF16)


def _in_proj(x, w, w_dt, *, n_xbc_cols, n_qkv_cols, n_z_cols, tm, tn):
    m, k = x.shape
    n = w.shape[1]
    tm = min(tm, m)
    n_xbc, n_qkv, n_z = n_xbc_cols // tn, n_qkv_cols // tn, n_z_cols // tn
    n_zg = n // tn - n_xbc - n_qkv
    per = tn // LANES
    est = (2 * tm * k * 4 + tm * k * 2 + 2 * k * tn * 2 + 3 * 2 * tm * tn * 2 + 2 * tm * tn * 4
           + k * LANES * 2 + 2 * tm * LANES * 4)
    kern = functools.partial(_in_proj_kernel, n_xbc=n_xbc, n_qkv=n_qkv, n_z=n_z)
    return pl.pallas_call(
        kern,
        out_shape=(jax.ShapeDtypeStruct((m, n_xbc_cols), BF16),
                   jax.ShapeDtypeStruct((n_qkv_cols // LANES, m, LANES), BF16),
                   jax.ShapeDtypeStruct((m, n_zg * tn), BF16),
                   jax.ShapeDtypeStruct((m, LANES), F32)),
        grid=(m // tm, n // tn),
        in_specs=[pl.BlockSpec((tm, k), lambda i, j: (i, 0)),
                  pl.BlockSpec((k, tn), lambda i, j: (0, j)),
                  _resident((k, LANES))],
        out_specs=(pl.BlockSpec((tm, tn), lambda i, j: (i, jnp.minimum(j, n_xbc - 1))),
                   pl.BlockSpec((per, tm, LANES), lambda i, j: (jnp.clip(j - n_xbc, 0, n_qkv - 1), i, 0)),
                   pl.BlockSpec((tm, tn), lambda i, j: (i, jnp.clip(j - n_xbc - n_qkv, 0, n_zg - 1))),
                   pl.BlockSpec((tm, LANES), lambda i, j: (i, 0))),
        scratch_shapes=[pltpu.VMEM((tm, k), BF16)],
        compiler_params=_params(("parallel", "arbitrary"), est),
        name="in_proj",
    )(x, w, w_dt)


def _split3(v):
    hi = v.astype(BF16)
    r1 = v - hi.astype(F32)
    mid = r1.astype(BF16)
    lo = (r1 - mid.astype(F32)).astype(BF16)
    return hi, mid, lo


def _ssd_kernel(z_ref, xbc_ref, dt_ref, convw_ref, convb_ref, dtb_ref, alog_ref,
                dskip_ref, g_ref, y_ref, tail_ref, state_ref, xa_ref, yacc_ref,
                *, inner, n_groups, d_state):
    L = SSD_CHUNK
    width = xbc_ref.shape[1]
    n_pairs = inner // LANES
    pairs_per_group = n_pairs // n_groups
    group_width = inner // n_groups

    @pl.when(pl.program_id(1) == 0)
    def _():
        tail_ref[...] = jnp.zeros_like(tail_ref)
        state_ref[...] = jnp.zeros_like(state_ref)

    slab = 512
    n_t = L // SUBLANES
    sub = lax.broadcasted_iota(jnp.int32, (n_t, SUBLANES, slab), 1)
    for c0 in range(0, width, slab):
        cs = slice(c0, c0 + slab)
        u = xbc_ref[:, cs].astype(F32)
        tiles = jnp.concatenate([tail_ref[:, cs], u], axis=0).reshape(n_t + 1, SUBLANES, slab)
        acc = convb_ref[:, cs] + u * convw_ref[SSD_CONV - 1:SSD_CONV, cs]
        for j in range(1, SSD_CONV):
            rot = pltpu.roll(tiles, j, axis=1)
            shifted = jnp.where(sub < j, rot[0:n_t], rot[1:n_t + 1]).reshape(L, slab)
            acc = acc + shifted * convw_ref[SSD_CONV - 1 - j:SSD_CONV - j, cs]
        tail_ref[:, cs] = u[L - SUBLANES:L]
        xa_ref[:, cs] = _silu(acc)

    dtr = dt_ref[...] + dtb_ref[...]
    dt = jnp.maximum(dtr, 0.0) + jnp.log(1.0 + jnp.exp(-jnp.abs(dtr)))
    a = -jnp.exp(alog_ref[...]) * dt
    row_i = lax.broadcasted_iota(jnp.int32, (L, L), 0)
    col_i = lax.broadcasted_iota(jnp.int32, (L, L), 1)
    causal = row_i >= col_i
    tril = jnp.where(causal, 1.0, 0.0).astype(BF16)
    acs = sum(jnp.dot(tril, part, preferred_element_type=F32) for part in _split3(a)) * LOG2E
    acs_t = acs.T
    dt_t = dt.T
    lane_lo = lax.broadcasted_iota(jnp.int32, (L, LANES), 1) < SSD_HEAD_DIM

    b0 = inner
    c0 = inner + n_groups * d_state
    for g in range(n_groups):
        bg = xa_ref[:, b0 + g * d_state:b0 + (g + 1) * d_state]
        cg = xa_ref[:, c0 + g * d_state:c0 + (g + 1) * d_state]
        cb = lax.dot_general(cg.astype(BF16), bg.astype(BF16), (((1,), (1,)), ((), ())),
                             preferred_element_type=F32)
        bg_t = bg.T
        for pp in range(g * pairs_per_group, (g + 1) * pairs_per_group):
            xs_pair = xa_ref[:, pp * LANES:(pp + 1) * LANES].astype(BF16)
            st_old = state_ref[pp]
            rhs = jnp.concatenate([xs_pair, st_old.astype(BF16)], axis=0)
            lhs, b_ws, keep = [], [], []
            for side in range(2):
                h = 2 * pp + side
                col = jnp.broadcast_to(acs[:, h:h + 1], (L, L))
                row = acs_t[h:h + 1, :]
                dt_row = dt_t[h:h + 1, :]
                last = acs[L - 1:L, h:h + 1]
                dec = jnp.exp2(jnp.where(causal, col - row, MASK_VALUE))
                m_in = (cb * dec * dt_row).astype(BF16)
                c_w = (cg * jnp.exp2(col[:, :d_state])).astype(BF16)
                lhs.append(jnp.concatenate([m_in, c_w], axis=1))
                b_ws.append((bg_t * (jnp.exp2(last - row) * dt_row)).astype(BF16))
                keep.append(jnp.exp2(last))
            y2 = jnp.dot(jnp.concatenate(lhs, axis=0), rhs, preferred_element_type=F32)
            s2 = jnp.dot(jnp.concatenate(b_ws, axis=0), xs_pair, preferred_element_type=F32)
            yacc_ref[:, pp * LANES:(pp + 1) * LANES] = jnp.where(lane_lo, y2[0:L], y2[L:2 * L])
            state_ref[pp] = jnp.where(lane_lo, st_old * keep[0] + s2[0:d_state],
                                      st_old * keep[1] + s2[d_state:2 * d_state])

    for g in range(n_groups):
        cs = slice(g * group_width, (g + 1) * group_width)
        zz = z_ref[:, cs].astype(F32)
        yv = (yacc_ref[:, cs] + dskip_ref[:, cs] * xa_ref[:, cs]) * _silu(zz)
        ms = jnp.mean(yv * yv, axis=-1, keepdims=True)
        y_ref[:, cs] = (yv * lax.rsqrt(ms + NORM_EPS) * g_ref[:, cs]).astype(y_ref.dtype)


def _ssd_branch(zg, xbc, dt_raw, conv_w, conv_b, dt_bias, a_log, d_skip, norm_g, *, batch, seq):
    t = zg.shape[0]
    inner = norm_g.shape[0]
    width = conv_w.shape[1]
    heads = inner // SSD_HEAD_DIM
    n_chunks = seq // SSD_CHUNK
    L = SSD_CHUNK

    def pad_heads(v):
        return jnp.pad(v.astype(F32), (0, LANES - heads)).reshape(1, LANES)

    convw = jnp.pad(conv_w.astype(F32), ((0, SUBLANES - SSD_CONV), (0, 0)))
    dskip = jnp.repeat(d_skip.astype(F32), SSD_HEAD_DIM).reshape(1, inner)
    row = lambda b, c: (b * n_chunks + c, 0)
    const = lambda b, c: (0, 0)
    est = (2 * L * (inner + width) * 2 + 2 * L * LANES * 4 + 2 * L * inner * 2
           + (inner // LANES) * SSD_STATE * LANES * 4 + L * (2 * width + inner) * 4 + (8 << 20))
    kern = functools.partial(_ssd_kernel, inner=inner, n_groups=SSD_GROUPS, d_state=SSD_STATE)
    return pl.pallas_call(
        kern,
        out_shape=jax.ShapeDtypeStruct((t, inner), BF16),
        grid=(batch, n_chunks),
        in_specs=[pl.BlockSpec((L, inner), row),
                  pl.BlockSpec((L, width), row),
                  pl.BlockSpec((L, LANES), row),
                  pl.BlockSpec((SUBLANES, width), const),
                  pl.BlockSpec((1, width), const),
                  pl.BlockSpec((1, LANES), const),
                  pl.BlockSpec((1, LANES), const),
                  pl.BlockSpec((1, inner), const),
                  pl.BlockSpec((1, inner), const)],
        out_specs=pl.BlockSpec((L, inner), row),
        scratch_shapes=[pltpu.VMEM((SUBLANES, width), F32),
                        pltpu.VMEM((inner // LANES, SSD_STATE, LANES), F32),
                        pltpu.VMEM((L, width), F32),
                        pltpu.VMEM((L, inner), F32)],
        compiler_params=_params(("parallel", "arbitrary"), est),
        name="ssd_scan",
    )(zg, xbc, dt_raw, convw, conv_b.astype(F32).reshape(1, width), pad_heads(dt_bias),
      pad_heads(a_log), dskip, norm_g.astype(F32).reshape(1, inner))


def _bucket_tiles(tq):
    max_exact = REL_BUCKETS // 2
    qi = np.arange(tq)[:, None]
    ki = np.arange(tq)[None, :]

    def bucket(dist):
        d = np.maximum(dist, 1).astype(np.float32)
        large = max_exact + (np.log(d / np.float32(max_exact)) / np.float32(math.log(REL_MAX_DIST / max_exact))
                             * np.float32(REL_BUCKETS - max_exact)).astype(np.int32)
        large = np.minimum(large, REL_BUCKETS - 1)
        return np.where(dist < max_exact, dist, large).astype(np.int32)

    diag = np.where(qi >= ki, bucket(np.maximum(qi - ki, 0)), -1)
    prev = bucket(tq + qi - ki)
    far = bucket(np.arange(tq + 1, 1 << 16))
    assert (far == REL_BUCKETS - 1).all()
    return np.stack([diag, prev]).astype(np.int32)


def _bias_kernel(rb_ref, bucket_ref, o_ref):
    h = pl.program_id(0)
    bk = bucket_ref[...]
    acc = jnp.zeros(bk.shape, F32)
    for b in range(REL_BUCKETS):
        acc = jnp.where(bk == b, rb_ref[b, h], acc)
    o_ref[0] = jnp.where(bk < 0, MASK_VALUE, (acc - rb_ref[REL_BUCKETS - 1, h]) * LOG2E)


def _bias_tiles(rel_bias, tq):
    buckets = jnp.asarray(_bucket_tiles(tq))
    return pl.pallas_call(
        _bias_kernel,
        out_shape=jax.ShapeDtypeStruct((DIFF_HEADS, 2, tq, tq), F32),
        grid=(DIFF_HEADS,),
        in_specs=[pl.BlockSpec(memory_space=pltpu.SMEM),
                  pl.BlockSpec((2, tq, tq), lambda h: (0, 0, 0))],
        out_specs=pl.BlockSpec((1, 2, tq, tq), lambda h: (h, 0, 0, 0)),
        compiler_params=_params(("arbitrary",), 8 * tq * tq * 4),
        name="t5_bias_tiles",
    )(rel_bias.astype(F32), buckets)


def _attn_kernel(lam_ref, q_ref, k_ref, v_ref, bias_ref, g_ref, o_ref,
                 q2_ref, vaug_ref, m_ref, acc_ref, *, tq, rows, lam_init):
    i = pl.program_id(2)
    dh = LANES // 2

    @pl.when(i == 0)
    def _():
        vaug_ref[:, 0:LANES] = v_ref[0]
        vaug_ref[:, LANES:2 * LANES] = jnp.ones((v_ref.shape[1], LANES), BF16)

    lane = lax.broadcasted_iota(jnp.int32, (tq, LANES), 1)
    qs = (q_ref[0].astype(F32) * (dh ** -0.5 * LOG2E)).astype(BF16)
    zero = jnp.zeros_like(qs)
    q2_ref[0:tq] = jnp.where(lane < dh, qs, zero)
    q2_ref[tq:2 * tq] = jnp.where(lane >= dh, qs, zero)
    m_ref[...] = jnp.full(m_ref.shape, MASK_VALUE, F32)
    acc_ref[...] = jnp.zeros_like(acc_ref)

    def step(j, bias_idx):
        start = pl.multiple_of(j * tq, tq)
        s_all = lax.dot_general(q2_ref[...], k_ref[0, pl.ds(start, tq), :], (((1,), (1,)), ((), ())),
                                preferred_element_type=F32)
        for r0 in range(0, 2 * tq, rows):
            rs = slice(r0, r0 + rows)
            q0 = r0 % tq
            klen = q0 + rows if bias_idx == 0 else tq
            vb = vaug_ref[pl.ds(start, klen), :]
            s = s_all[rs, 0:klen]
            if bias_idx is not None:
                s = s + bias_ref[0, bias_idx, q0:q0 + rows, 0:klen]
            m_old = m_ref[rs]
            m_new = jnp.maximum(m_old, jnp.max(s, axis=1, keepdims=True))
            alpha = jnp.exp2(m_old - m_new)
            p = jnp.exp2(s - jnp.concatenate([m_new] * (klen // LANES), axis=1))
            pv = jnp.dot(p.astype(BF16), vb, preferred_element_type=F32)
            acc_ref[rs] = jnp.concatenate([alpha, alpha], axis=1) * acc_ref[rs] + pv
            m_ref[rs] = m_new

    def far_step(j, carry):
        step(j, None)
        return carry

    lax.fori_loop(0, jnp.maximum(i - 1, 0), far_step, 0)

    @pl.when(i >= 1)
    def _():
        step(i - 1, 1)

    step(i, 0)

    lv = lam_ref[...]
    s1 = jnp.sum(lv[0:1] * lv[1:2], axis=1, keepdims=True)
    s2 = jnp.sum(lv[2:3] * lv[3:4], axis=1, keepdims=True)
    lam = jnp.exp(s1) - jnp.exp(s2) + lam_init
    o1 = acc_ref[0:tq, 0:LANES] / acc_ref[0:tq, LANES:2 * LANES]
    o2 = acc_ref[tq:2 * tq, 0:LANES] / acc_ref[tq:2 * tq, LANES:2 * LANES]
    o = o1 - lam * o2
    ms = jnp.mean(o * o, axis=-1, keepdims=True)
    o_ref[...] = (o * lax.rsqrt(ms + NORM_EPS) * g_ref[...] * (1.0 - lam_init)).astype(o_ref.dtype)


def _diff_attention(qkv, lam_vecs, bias_tiles, subln_g, *, batch, seq, tq, layer_idx):
    t = qkv.shape[1]
    nq = seq // tq
    rows = min(256, tq)
    lam_init = 0.8 - 0.6 * math.exp(-0.3 * layer_idx)
    kern = functools.partial(_attn_kernel, tq=tq, rows=rows, lam_init=lam_init)
    est = (2 * tq * LANES * 2 + 4 * seq * LANES * 2 + 4 * tq * tq * 4 + 2 * tq * LANES * 2
           + 2 * tq * LANES * 2 + seq * 2 * LANES * 2 + 2 * tq * LANES * 4 + 2 * tq * 2 * LANES * 4
           + 8 * rows * tq * 4)
    return pl.pallas_call(
        kern,
        out_shape=jax.ShapeDtypeStruct((t, DIFF_HEADS * LANES), BF16),
        grid=(batch, DIFF_HEADS, nq),
        in_specs=[pl.BlockSpec((SUBLANES, LANES), lambda b, h, i: (0, 0)),
                  pl.BlockSpec((1, tq, LANES), lambda b, h, i: (h, b * nq + i, 0)),
                  pl.BlockSpec((1, seq, LANES), lambda b, h, i: (DIFF_HEADS + h, b, 0)),
                  pl.BlockSpec((1, seq, LANES), lambda b, h, i: (2 * DIFF_HEADS + h, b, 0)),
                  pl.BlockSpec((1, 2, tq, tq), lambda b, h, i: (h, 0, 0, 0)),
                  pl.BlockSpec((1, LANES), lambda b, h, i: (0, 0))],
        out_specs=pl.BlockSpec((tq, LANES), lambda b, h, i: (b * nq + i, h)),
        scratch_shapes=[pltpu.VMEM((2 * tq, LANES), BF16),
                        pltpu.VMEM((seq, 2 * LANES), BF16),
                        pltpu.VMEM((2 * tq, LANES), F32),
                        pltpu.VMEM((2 * tq, 2 * LANES), F32)],
        compiler_params=_params(("parallel", "parallel", "arbitrary"), est),
        name="diff_attention",
    )(lam_vecs, qkv, qkv, qkv, bias_tiles, subln_g)


def _mix_kernel(x_ref, y_ref, o_ref, gate_ref, wssd_ref, wdiff_ref, wmix_ref, g_ref, b_ref, out_ref):
    d = x_ref.shape[1]
    ssd = jnp.dot(y_ref[...], wssd_ref[...], preferred_element_type=F32)
    dif = jnp.dot(o_ref[...], wdiff_ref[...], preferred_element_type=F32)
    merged = gate_ref[:, 0:d].astype(F32) * ssd + gate_ref[:, d:2 * d].astype(F32) * dif
    mixed = jnp.dot(merged.astype(BF16), wmix_ref[...], preferred_element_type=F32)
    out_ref[...] = _layer_norm(DN_ALPHA * x_ref[...] + mixed, g_ref[...], b_ref[...])


def _mix(x, y_ssd, o_diff, proj, w_ssd, w_diff, w_mix, ln_g, ln_b, *, tm, gate_block):
    t, d = x.shape
    tm = min(tm, t)
    inner = y_ssd.shape[1]
    dw = o_diff.shape[1]
    row = lambda i: (i, 0)
    est = (2 * tm * (d * 4 + inner * 2 + dw * 2 + 2 * d * 2 + d * 4)
           + (inner * d + dw * d + d * d) * 2 + 6 * tm * d * 4)
    return pl.pallas_call(
        _mix_kernel,
        out_shape=jax.ShapeDtypeStruct((t, d), F32),
        grid=(t // tm,),
        in_specs=[pl.BlockSpec((tm, d), row),
                  pl.BlockSpec((tm, inner), row),
                  pl.BlockSpec((tm, dw), row),
                  pl.BlockSpec((tm, 2 * d), lambda i: (i, gate_block)),
                  _resident((inner, d)), _resident((dw, d)), _resident((d, d)),
                  _resident((1, d)), _resident((1, d))],
        out_specs=pl.BlockSpec((tm, d), row),
        compiler_params=_params(("parallel",), est),
        name="mix_ln1",
    )(x, y_ssd, o_diff, proj, w_ssd, w_diff, w_mix, ln_g, ln_b)


def _pack_bf16_pairs(v):
    w = v.shape[1] // 2
    lo = lax.bitcast_convert_type(v[:, :w].astype(BF16).astype(F32), jnp.int32)
    hi = lax.bitcast_convert_type(v[:, w:].astype(BF16).astype(F32), jnp.int32)
    return jnp.bitwise_or(hi, lax.shift_right_logical(lo, 16))


def _unpack_bf16_pairs(words):
    lo = lax.bitcast_convert_type(lax.shift_left(words, 16), F32)
    hi = lax.bitcast_convert_type(jnp.bitwise_and(words, -65536), F32)
    return jnp.concatenate([lo, hi], axis=1)


def _xattn_kernel(x_ref, kv_ref, wq_ref, wo_ref, g_ref, b_ref, out_ref, packed_ref, o_scr, *, heads):
    d = x_ref.shape[1]
    dh = d // heads
    xv = x_ref[...]
    q = jnp.dot(xv.astype(BF16), wq_ref[...], preferred_element_type=F32)
    q = (q * (dh ** -0.5)).astype(BF16)
    for h in range(heads):
        kh = kv_ref[:, h * dh:(h + 1) * dh]
        vh = kv_ref[:, d + h * dh:d + (h + 1) * dh]
        s = lax.dot_general(q[:, h * dh:(h + 1) * dh], kh, (((1,), (1,)), ((), ())),
                            preferred_element_type=F32)
        p = jnp.exp(s - jnp.max(s, axis=1, keepdims=True))
        p = p / jnp.sum(p, axis=1, keepdims=True)
        o_scr[:, h * dh:(h + 1) * dh] = jnp.dot(p.astype(BF16), vh, preferred_element_type=F32).astype(BF16)
    att = jnp.dot(o_scr[...], wo_ref[...], preferred_element_type=F32)
    y = _layer_norm(DN_ALPHA * xv + att, g_ref[...], b_ref[...])
    out_ref[...] = y
    packed_ref[...] = _pack_bf16_pairs(y)


def _cross_attention(x, kv, w_cq, w_co, ln_g, ln_b, *, batch, seq, tq):
    t, d = x.shape
    mem_len = kv.shape[0] // batch
    tq = min(tq, seq)
    nq = seq // tq
    est = (2 * tq * d * 4 * 2 + 2 * tq * d * 2 + 2 * mem_len * 2 * d * 2 + 2 * d * d * 2 + tq * d * 2
           + 8 * tq * d * 4)
    row = lambda b, i: (b * nq + i, 0)
    return pl.pallas_call(
        functools.partial(_xattn_kernel, heads=MEM_HEADS),
        out_shape=(jax.ShapeDtypeStruct((t, d), F32), jax.ShapeDtypeStruct((t, d // 2), jnp.int32)),
        grid=(batch, nq),
        in_specs=[pl.BlockSpec((tq, d), row),
                  pl.BlockSpec((mem_len, 2 * d), lambda b, i: (b, 0)),
                  _resident((d, d)), _resident((d, d)), _resident((1, d)), _resident((1, d))],
        out_specs=(pl.BlockSpec((tq, d), row), pl.BlockSpec((tq, d // 2), row)),
        scratch_shapes=[pltpu.VMEM((tq, d), BF16)],
        compiler_params=_params(("parallel", "parallel"), est),
        name="cross_attention_ln2",
    )(x, kv, w_cq, w_co, ln_g, ln_b)


def _router_kernel(x_ref, wr_ref, rb_ref, idx_ref, rank_ref, gates_ref, cnt_ref, run_ref, *, n_experts):
    tm = x_ref.shape[0]
    n_groups = N_EXPERT_GROUPS
    per = n_experts // n_groups

    @pl.when(pl.program_id(0) == 0)
    def _():
        run_ref[...] = jnp.zeros_like(run_ref)

    xv = x_ref[...]
    x_hi = xv.astype(BF16)
    x_lo = (xv - x_hi.astype(F32)).astype(BF16)
    nt = (((1,), (1,)), ((), ()))
    logits = (lax.dot_general(wr_ref[0], x_hi, nt, preferred_element_type=F32)
              + lax.dot_general(wr_ref[0], x_lo, nt, preferred_element_type=F32)
              + lax.dot_general(wr_ref[1], x_hi, nt, preferred_element_type=F32))
    sc = _sigmoid(logits[0:n_experts]).reshape(per, n_groups, tm)
    choice = sc + rb_ref[0:n_experts].reshape(per, n_groups, 1)
    neg = -jnp.inf
    member = lax.broadcasted_iota(jnp.int32, (per, n_groups, tm), 0)
    group3 = lax.broadcasted_iota(jnp.int32, (per, n_groups, tm), 1)
    m1 = jnp.max(choice, axis=0, keepdims=True)
    i1 = jnp.min(jnp.where(choice == m1, member, per), axis=0, keepdims=True)
    m2 = jnp.max(jnp.where(member == i1, neg, choice), axis=0, keepdims=True)
    gscore = (m1 + m2)
    gsel = jnp.zeros((1, n_groups, tm), F32)
    gidx = lax.broadcasted_iota(jnp.int32, (1, n_groups, tm), 1)
    cur = gscore
    for _ in range(TOP_GROUPS):
        mx = jnp.max(cur, axis=1, keepdims=True)
        ix = jnp.min(jnp.where(cur == mx, gidx, n_groups), axis=1, keepdims=True)
        hit = gidx == ix
        gsel = jnp.where(hit, 1.0, gsel)
        cur = jnp.where(hit, neg, cur)
    cur = jnp.where(gsel > 0.0, choice, neg)
    eidx = group3 * per + member
    esel = jnp.zeros((per, n_groups, tm), F32)
    hits, idx_rows = [], []
    for _ in range(TOP_K):
        mx = jnp.max(jnp.max(cur, axis=0, keepdims=True), axis=1, keepdims=True)
        ix = jnp.min(jnp.min(jnp.where(cur == mx, eidx, n_experts), axis=0, keepdims=True),
                     axis=1, keepdims=True)
        hit = eidx == ix
        esel = jnp.where(hit, 1.0, esel)
        cur = jnp.where(hit, neg, cur)
        hits.append(hit)
        idx_rows.append(ix.reshape(1, tm))
    w = esel * sc
    tot = jnp.sum(jnp.sum(w, axis=0, keepdims=True), axis=1, keepdims=True)
    gw = w / tot * ROUTED_SCALE

    r_i = lax.broadcasted_iota(jnp.int32, (tm, tm), 0)
    c_i = lax.broadcasted_iota(jnp.int32, (tm, tm), 1)
    upper = jnp.where(r_i <= c_i, 1.0, 0.0).astype(BF16)
    pref = jnp.dot(esel.reshape(n_experts, tm).astype(BF16), upper, preferred_element_type=F32)
    run = run_ref[...]
    rank3 = (jnp.concatenate([run] * (tm // LANES), axis=1) + pref - 1.0).reshape(per, n_groups, tm)

    def pick(hit, vals):
        return jnp.sum(jnp.sum(jnp.where(hit, vals, 0.0), axis=0, keepdims=True), axis=1).reshape(1, tm)

    idx_ref[...] = jnp.concatenate(idx_rows, axis=0)
    rank_ref[...] = jnp.concatenate([pick(h, rank3) for h in hits], axis=0).astype(jnp.int32)
    wk = jnp.concatenate([pick(h, gw) for h in hits] + [jnp.zeros((LANES - TOP_K, tm), F32)], axis=0)
    gates_ref[...] = wk.T
    run = run + jnp.broadcast_to(pref[:, tm - 1:tm], run.shape)
    run_ref[...] = run
    cnt_ref[...] = run


def _router(x, w_router, router_bias, *, tm, row0, t):
    d = x.shape[1]
    n_experts = w_router.shape[1]
    per = n_experts // N_EXPERT_GROUPS
    tm = min(tm, t)
    blk0 = row0 // tm

    def member_major(v):
        return v.reshape(N_EXPERT_GROUPS, per, *v.shape[1:]).swapaxes(0, 1).reshape(v.shape)

    wt = jnp.pad(member_major(w_router.astype(F32).T), ((0, LANES - n_experts), (0, 0)))
    w_hi = wt.astype(BF16)
    wr = jnp.stack([w_hi, (wt - w_hi.astype(F32)).astype(BF16)])
    rb = jnp.pad(member_major(router_bias.astype(F32)), (0, LANES - n_experts)).reshape(LANES, 1)
    est = (2 * tm * d * 4 + 2 * LANES * d * 2 + 2 * tm * LANES * 4 + 60 * n_experts * tm * 4 + 3 * tm * tm * 4)
    pick_spec = pl.BlockSpec((TOP_K, tm), lambda i: (0, i))
    idx_t, rank_t, gates, counts = pl.pallas_call(
        functools.partial(_router_kernel, n_experts=n_experts),
        out_shape=(jax.ShapeDtypeStruct((TOP_K, t), jnp.int32),
                   jax.ShapeDtypeStruct((TOP_K, t), jnp.int32),
                   jax.ShapeDtypeStruct((t, LANES), F32),
                   jax.ShapeDtypeStruct((n_experts, LANES), F32)),
        grid=(t // tm,),
        in_specs=[pl.BlockSpec((tm, d), lambda i: (blk0 + i, 0)),
                  _resident((2, LANES, d)), _resident((LANES, 1))],
        out_specs=(pick_spec, pick_spec,
                   pl.BlockSpec((tm, LANES), lambda i: (i, 0)),
                   pl.BlockSpec((n_experts, LANES), lambda i: (0, 0))),
        scratch_shapes=[pltpu.VMEM((n_experts, LANES), F32)],
        compiler_params=_params(("arbitrary",), est),
        name="router",
    )(x, wr, rb)
    counts = counts[:, 0].astype(jnp.int32).reshape(per, N_EXPERT_GROUPS).T.reshape(n_experts)
    return idx_t, rank_t, gates, counts


def _positions_kernel(cnt_ref, idx_ref, rank_ref, pos_ref, texp_ref, off_ref, *, n_experts, rows, n_tiles):
    @pl.when(pl.program_id(0) == 0)
    def _():
        def per_expert(e, toff):
            off_ref[e] = toff * rows
            nt = lax.shift_right_logical(cnt_ref[e] + (rows - 1), int(math.log2(rows)))

            def fill(j, c):
                texp_ref[toff + j] = e
                return c

            lax.fori_loop(0, nt, fill, 0)
            return toff + nt

        n_used = lax.fori_loop(0, n_experts, per_expert, 0)

        def tail(j, c):
            texp_ref[j] = n_experts - 1
            return c

        lax.fori_loop(n_used, n_tiles, tail, 0)
        texp_ref[n_tiles] = n_used

    idx = idx_ref[...]
    pos = rank_ref[...]
    for e in range(n_experts):
        pos = pos + jnp.where(idx == e, off_ref[e], 0)
    pos_ref[...] = pos


def _positions(counts, idx_t, rank_t, *, rows, n_tiles, tm):
    n_experts = counts.shape[0]
    k, t = idx_t.shape
    tm = min(tm, t)
    spec = pl.BlockSpec((k, tm), lambda i: (0, i))
    kern = functools.partial(_positions_kernel, n_experts=n_experts, rows=rows, n_tiles=n_tiles)
    return pl.pallas_call(
        kern,
        out_shape=(jax.ShapeDtypeStruct((k, t), jnp.int32),
                   jax.ShapeDtypeStruct((n_tiles + 1,), jnp.int32)),
        grid=(t // tm,),
        in_specs=[pl.BlockSpec(memory_space=pltpu.SMEM), spec, spec],
        out_specs=(spec, pl.BlockSpec(memory_space=pltpu.SMEM)),
        scratch_shapes=[pltpu.SMEM((n_experts,), jnp.int32)],
        compiler_params=_params(("arbitrary",), 16 * k * tm * 4),
        name="moe_positions",
    )(counts, idx_t, rank_t)


def _sc_mesh():
    return plsc.VectorSubcoreMesh(core_axis_name="c", subcore_axis_name="s",
                                  num_cores=SC_CORES, num_subcores=SC_SUBCORES)


def _sc_dispatch(packed, pos_chunks, n_rows, row0):
    w = packed.shape[1]
    n_chunks, k, n = pos_chunks.shape
    per_worker = n_chunks // (SC_CORES * SC_SUBCORES)

    @functools.partial(
        pl.kernel, mesh=_sc_mesh(),
        out_type=jax.ShapeDtypeStruct((n_rows, w), packed.dtype),
        scratch_types=[pltpu.VMEM((k, n), jnp.int32), pltpu.VMEM((n, w), packed.dtype),
                       pltpu.SemaphoreType.DMA],
        name="moe_dispatch_sc",
    )
    def scatter_rows(x_hbm, pos_hbm, out_hbm, idx_v, rows_v, sem):
        wid = lax.axis_index("s") * SC_CORES + lax.axis_index("c")

        @pl.loop(0, per_worker)
        def _(step):
            c = wid * per_worker + step
            pltpu.sync_copy(pos_hbm.at[c], idx_v)
            pltpu.sync_copy(x_hbm.at[pl.ds(row0 + c * n, n)], rows_v)
            copies = [pltpu.async_copy(rows_v, out_hbm.at[idx_v.at[kk]], sem) for kk in range(k)]
            for cp in copies:
                cp.wait()

    return scatter_rows(packed, pos_chunks)


def _sc_combine(sorted_rows, pos_chunks):
    _, w = sorted_rows.shape
    n_chunks, k, n = pos_chunks.shape
    per_worker = n_chunks // (SC_CORES * SC_SUBCORES)

    @functools.partial(
        pl.kernel, mesh=_sc_mesh(),
        out_type=jax.ShapeDtypeStruct((k, n_chunks * n, w), sorted_rows.dtype),
        scratch_types=[pltpu.VMEM((k, n), jnp.int32), pltpu.VMEM((n, w), sorted_rows.dtype),
                       pltpu.SemaphoreType.DMA],
        name="moe_combine_sc",
    )
    def gather_rows(y_hbm, pos_hbm, out_hbm, idx_v, rows_v, sem):
        wid = lax.axis_index("s") * SC_CORES + lax.axis_index("c")

        @pl.loop(0, per_worker)
        def _(step):
            c = wid * per_worker + step
            pltpu.sync_copy(pos_hbm.at[c], idx_v)
            for kk in range(k):
                pltpu.async_copy(y_hbm.at[idx_v.at[kk]], rows_v, sem).wait()
                pltpu.sync_copy(rows_v, out_hbm.at[kk, pl.ds(c * n, n)])

    return gather_rows(sorted_rows, pos_chunks)


def _expert_kernel(texp_ref, xs_ref, wgu_ref, wdn_ref, ys_ref, *, n_tiles):
    ff = wdn_ref.shape[1]

    @pl.when(pl.program_id(0) < texp_ref[n_tiles])
    def _():
        xv = _unpack_bf16_pairs(xs_ref[...]).astype(BF16)
        gu = jnp.dot(xv, wgu_ref[0], preferred_element_type=F32)
        hid = _silu(gu[:, 0:ff]) * gu[:, ff:2 * ff]
        ys_ref[...] = _pack_bf16_pairs(jnp.dot(hid.astype(BF16), wdn_ref[0], preferred_element_type=F32))


def _experts(tile_expert, sorted_rows, w_gu, w_dn, *, rows):
    n_rows, w = sorted_rows.shape
    n_tiles = n_rows // rows
    _, d, ff2 = w_gu.shape
    ff = ff2 // 2
    tile = lambda j, te: (jnp.minimum(j, te[n_tiles] - 1), 0)
    est = 4 * rows * w * 4 + 2 * (d * ff2 + ff * d) * 2 + 4 * rows * d * 4 + 4 * rows * ff2 * 4
    return pl.pallas_call(
        functools.partial(_expert_kernel, n_tiles=n_tiles),
        out_shape=jax.ShapeDtypeStruct((n_rows, w), sorted_rows.dtype),
        grid_spec=pltpu.PrefetchScalarGridSpec(
            num_scalar_prefetch=1,
            grid=(n_tiles,),
            in_specs=[pl.BlockSpec((rows, w), tile),
                      pl.BlockSpec((1, d, ff2), lambda j, te: (te[j], 0, 0)),
                      pl.BlockSpec((1, ff, d), lambda j, te: (te[j], 0, 0))],
            out_specs=pl.BlockSpec((rows, w), tile)),
        compiler_params=_params(("arbitrary",), est),
        name="moe_experts",
    )(tile_expert, sorted_rows, w_gu, w_dn)


def _moe_out_kernel(x_ref, yk_ref, gates_ref, wsgu_ref, wsdn_ref, g_ref, b_ref, *rest):
    out_ref = rest[-1]
    ff = wsdn_ref.shape[0]
    xv = x_ref[...]
    gu = jnp.dot(xv.astype(BF16), wsgu_ref[...], preferred_element_type=F32)
    hid = _silu(gu[:, 0:ff]) * gu[:, ff:2 * ff]
    acc = jnp.dot(hid.astype(BF16), wsdn_ref[...], preferred_element_type=F32)
    for k in range(yk_ref.shape[0]):
        acc = acc + gates_ref[:, k:k + 1] * _unpack_bf16_pairs(yk_ref[k])
    out_ref[...] = _layer_norm(DN_ALPHA * xv + acc, g_ref[...], b_ref[...])


def _moe_out(x, yk, gates, w_sgu, w_sdn, ln_g, ln_b, *, tm, row0, prev):
    t_all, d = x.shape
    k, t, w = yk.shape
    sff2 = w_sgu.shape[1]
    tm = min(tm, t)
    blk0 = row0 // tm
    est = (2 * tm * d * 4 * 2 + 2 * k * tm * w * 4 + 2 * tm * LANES * 4
           + (d * sff2 + (sff2 // 2) * d) * 2 + 6 * tm * d * 4)
    in_specs = [pl.BlockSpec((tm, d), lambda i: (blk0 + i, 0)),
                pl.BlockSpec((k, tm, w), lambda i: (0, i, 0)),
                pl.BlockSpec((tm, LANES), lambda i: (i, 0)),
                _resident((d, sff2)), _resident((sff2 // 2, d)),
                _resident((1, d)), _resident((1, d))]
    args = [x, yk, gates, w_sgu, w_sdn, ln_g, ln_b]
    aliases = {}
    if prev is not None:
        in_specs.append(pl.BlockSpec(memory_space=pl.ANY))
        args.append(prev)
        aliases = {len(args) - 1: 0}
    return pl.pallas_call(
        _moe_out_kernel,
        out_shape=jax.ShapeDtypeStruct((t_all, d), F32),
        grid=(t // tm,),
        in_specs=in_specs,
        out_specs=pl.BlockSpec((tm, d), lambda i: (blk0 + i, 0)),
        input_output_aliases=aliases,
        compiler_params=_params(("parallel",), est),
        name="moe_out_ln3",
    )(*args)


def _moe(x, packed, w_router, router_bias, w_gu, w_dn, w_sgu, w_sdn, ln_g, ln_b):
    t_all, d = x.shape
    n_experts = w_router.shape[1]
    n = SC_CHUNK_TOKENS
    splits = MOE_SPLITS if t_all % (MOE_SPLITS * n * SC_CORES * SC_SUBCORES) == 0 else 1
    t = t_all // splits
    rows = min(MOE_ROWS_PER_TILE, t)
    n_tiles = (t * TOP_K) // rows + n_experts
    parts = []
    for s in range(splits):
        row0 = s * t
        idx_t, rank_t, gates, counts = _router(x, w_router, router_bias, tm=512, row0=row0, t=t)
        pos_t, tile_expert = _positions(counts, idx_t, rank_t, rows=rows, n_tiles=n_tiles, tm=2048)
        pos_chunks = pos_t.reshape(TOP_K, t // n, n).transpose(1, 0, 2)
        sorted_x = _sc_dispatch(packed, pos_chunks, n_tiles * rows, row0)
        parts.append((row0, gates, tile_expert, pos_chunks, sorted_x))
    out = None
    for row0, gates, tile_expert, pos_chunks, sorted_x in parts:
        sorted_y = _experts(tile_expert, sorted_x, w_gu, w_dn, rows=rows)
        yk = _sc_combine(sorted_y, pos_chunks)
        out = _moe_out(x, yk, gates, w_sgu, w_sdn, ln_g, ln_b, tm=512, row0=row0, prev=out)
    return out


def kernel(x, mem, w_in, conv_w, conv_b, dt_bias, a_log, d_skip, ssd_norm_g, lambda_q1, lambda_k1, lambda_q2, lambda_k2, subln_g, rel_bias, w_ssd_br, w_diff_br, w_mix_out, ln1_g, ln1_b, w_cq, w_ckv, w_co, ln2_g, ln2_b, w_router, router_bias, w_exp_gu, w_exp_down, w_sh_gu, w_sh_down, ln3_g, ln3_b):
    batch, seq, d = x.shape
    depth = w_in.shape[0]
    inner = w_ssd_br.shape[1]
    xbc_width = conv_w.shape[2]
    heads = dt_bias.shape[1]
    diff_width = w_diff_br.shape[1]
    o_z, o_xbc = inner, inner + xbc_width
    o_dt = o_xbc + heads
    o_v = o_dt + 3 * diff_width
    t = batch * seq
    tq_attn = min(512, seq)

    def vec(v):
        return v.astype(F32).reshape(1, -1)

    xt = x.reshape(t, d)
    memt = mem.reshape(-1, d)
    bias_tiles = _bias_tiles(rel_bias, tq_attn)

    for l in range(depth):
        wl = w_in[l]
        w_big = jnp.concatenate([wl[:, o_z:o_xbc], wl[:, o_dt:o_v], wl[:, :o_z], wl[:, o_v:]],
                                axis=1).astype(BF16)
        assert 2 * d == inner
        w_dt = jnp.pad(wl[:, o_xbc:o_dt], ((0, 0), (0, LANES - heads))).astype(BF16)
        xbc, qkv, zg, dt_raw = _in_proj(xt, w_big, w_dt, n_xbc_cols=xbc_width, n_qkv_cols=3 * diff_width,
                                        n_z_cols=inner, tm=1024, tn=1024)

        y_ssd = _ssd_branch(zg, xbc, dt_raw, conv_w[l], conv_b[l], dt_bias[l], a_log[l], d_skip[l],
                            ssd_norm_g[l], batch=batch, seq=seq)

        lam_vecs = jnp.pad(jnp.stack([lambda_q1[l], lambda_k1[l], lambda_q2[l], lambda_k2[l]]).astype(F32),
                           ((0, SUBLANES - 4), (0, LANES - lambda_q1.shape[1])))
        o_diff = _diff_attention(qkv, lam_vecs, bias_tiles, vec(subln_g[l]), batch=batch, seq=seq,
                                 tq=tq_attn, layer_idx=l)

        xt = _mix(xt, y_ssd, o_diff, zg, w_ssd_br[l].astype(BF16), w_diff_br[l].astype(BF16),
                  w_mix_out[l].astype(BF16), vec(ln1_g[l]), vec(ln1_b[l]), tm=512, gate_block=1)

        kv = _matmul(memt, w_ckv[l].astype(BF16), BF16, tm=1024, tn=1024, name="mem_kv_proj")
        xt, packed = _cross_attention(xt, kv, w_cq[l].astype(BF16), w_co[l].astype(BF16), vec(ln2_g[l]),
                                      vec(ln2_b[l]), batch=batch, seq=seq, tq=1024)

        xt = _moe(xt, packed, w_router[l], router_bias[l], w_exp_gu[l].astype(BF16),
                  w_exp_down[l].astype(BF16), w_sh_gu[l].astype(BF16), w_sh_down[l].astype(BF16),
                  vec(ln3_g[l]), vec(ln3_b[l]))
    return xt.reshape(batch, seq, d)
```

```python
import functools
import math

import numpy as np
import jax
import jax.numpy as jnp
from jax import lax
from jax.experimental import pallas as pl
from jax.experimental.pallas import tpu as pltpu
from jax.experimental.pallas import tpu_sc as plsc

F32 = jnp.float32
BF16 = jnp.bfloat16

SSD_HEAD_DIM = 64
SSD_GROUPS = 4
SSD_STATE = 128
SSD_CONV = 4
SSD_CHUNK = 128
DIFF_HEADS = 8
REL_BUCKETS = 32
REL_MAX_DIST = 128
MEM_HEADS = 4
TOP_K = 8
N_EXPERT_GROUPS = 8
TOP_GROUPS = 4
ROUTED_SCALE = 2.5
NORM_EPS = 1e-5
DEPTH = 1
DN_ALPHA = (2.0 * DEPTH) ** 0.25

LANES = 128
SUBLANES = 8
VMEM_CAP_BYTES = 64 * 1024 * 1024
MASK_VALUE = -1e30
LOG2E = math.log2(math.e)
ATTN_HEADS_PER_STEP = 2
SC_CORES = 2
SC_SUBCORES = 16
SC_CHUNK_TOKENS = 128
MOE_SPLITS = 2
MOE_ROWS_PER_TILE = 1024


def _vmem_limit(estimate_bytes):
    return int(min(estimate_bytes * 5 // 4 + (4 << 20), VMEM_CAP_BYTES - (6 << 20)))


def _params(semantics, vmem_estimate):
    return pltpu.CompilerParams(dimension_semantics=semantics,
                                vmem_limit_bytes=_vmem_limit(vmem_estimate))


def _resident(shape):
    nd = len(shape)
    return pl.BlockSpec(shape, lambda *_: (0,) * nd, pipeline_mode=pl.Buffered(1))


def _layer_norm(v, g, b):
    mu = jnp.mean(v, axis=-1, keepdims=True)
    d = v - mu
    var = jnp.mean(d * d, axis=-1, keepdims=True)
    return d * lax.rsqrt(var + NORM_EPS) * g + b


def _sigmoid(v):
    return 0.5 * jnp.tanh(0.5 * v) + 0.5


def _silu(v):
    h = 0.5 * v
    return h + h * jnp.tanh(h)


def _matmul_kernel(x_ref, w_ref, o_ref, xb_ref):
    @pl.when(pl.program_id(1) == 0)
    def _():
        xb_ref[...] = x_ref[...].astype(BF16)

    o_ref[...] = jnp.dot(xb_ref[...], w_ref[...], preferred_element_type=F32).astype(o_ref.dtype)


def _matmul(x, w, out_dtype, *, tm, tn, name):
    m, k = x.shape
    n = w.shape[1]
    tm, tn = min(tm, m), min(tn, n)
    est = (2 * tm * k * x.dtype.itemsize + tm * k * 2 + 2 * k * tn * 2
           + 2 * tm * tn * jnp.dtype(out_dtype).itemsize + tm * tn * 4)
    return pl.pallas_call(
        _matmul_kernel,
        out_shape=jax.ShapeDtypeStruct((m, n), out_dtype),
        grid=(m // tm, n // tn),
        in_specs=[pl.BlockSpec((tm, k), lambda i, j: (i, 0)),
                  pl.BlockSpec((k, tn), lambda i, j: (0, j))],
        out_specs=pl.BlockSpec((tm, tn), lambda i, j: (i, j)),
        scratch_shapes=[pltpu.VMEM((tm, k), BF16)],
        compiler_params=_params(("parallel", "arbitrary"), est),
        name=name,
    )(x, w)


def _in_proj_kernel(x_ref, w_ref, wdt_ref, xbc_ref, qkv_ref, zg_ref, dt_ref, xb_ref, *, n_xbc, n_qkv, n_z):
    j = pl.program_id(1)
    j_z = n_xbc + n_qkv

    @pl.when(j == 0)
    def _():
        xb = x_ref[...].astype(BF16)
        xb_ref[...] = xb
        dt_ref[...] = jnp.dot(xb, wdt_ref[...], preferred_element_type=F32)

    acc = jnp.dot(xb_ref[...], w_ref[...], preferred_element_type=F32)

    @pl.when(j < n_xbc)
    def _():
        xbc_ref[...] = acc.astype(BF16)

    @pl.when(jnp.logical_and(j >= n_xbc, j < j_z))
    def _():
        for hh in range(qkv_ref.shape[0]):
            qkv_ref[hh] = acc[:, hh * LANES:(hh + 1) * LANES].astype(BF16)

    @pl.when(jnp.logical_and(j >= j_z, j < j_z + n_z))
    def _():
        zg_ref[...] = acc.astype(BF16)

    @pl.when(j >= j_z + n_z)
    def _():
        zg_ref[...] = _sigmoid(acc).astype(BF16)


def _in_proj(x, w, w_dt, *, n_xbc_cols, n_qkv_cols, n_z_cols, tm, tn):
    m, k = x.shape
    n = w.shape[1]
    tm = min(tm, m)
    n_xbc, n_qkv, n_z = n_xbc_cols // tn, n_qkv_cols // tn, n_z_cols // tn
    n_zg = n // tn - n_xbc - n_qkv
    per = tn // LANES
    est = (2 * tm * k * 4 + tm * k * 2 + 2 * k * tn * 2 + 3 * 2 * tm * tn * 2 + 2 * tm * tn * 4
           + k * LANES * 2 + 2 * tm * LANES * 4)
    kern = functools.partial(_in_proj_kernel, n_xbc=n_xbc, n_qkv=n_qkv, n_z=n_z)
    return pl.pallas_call(
        kern,
        out_shape=(jax.ShapeDtypeStruct((m, n_xbc_cols), BF16),
                   jax.ShapeDtypeStruct((n_qkv_cols // LANES, m, LANES), BF16),
                   jax.ShapeDtypeStruct((m, n_zg * tn), BF16),
                   jax.ShapeDtypeStruct((m, LANES), F32)),
        grid=(m // tm, n // tn),
        in_specs=[pl.BlockSpec((tm, k), lambda i, j: (i, 0)),
                  pl.BlockSpec((k, tn), lambda i, j: (0, j)),
                  _resident((k, LANES))],
        out_specs=(pl.BlockSpec((tm, tn), lambda i, j: (i, jnp.minimum(j, n_xbc - 1))),
                   pl.BlockSpec((per, tm, LANES), lambda i, j: (jnp.clip(j - n_xbc, 0, n_qkv - 1), i, 0)),
                   pl.BlockSpec((tm, tn), lambda i, j: (i, jnp.clip(j - n_xbc - n_qkv, 0, n_zg - 1))),
                   pl.BlockSpec((tm, LANES), lambda i, j: (i, 0))),
        scratch_shapes=[pltpu.VMEM((tm, k), BF16)],
        compiler_params=_params(("parallel", "arbitrary"), est),
        name="in_proj",
    )(x, w, w_dt)


def _split3(v):
    hi = v.astype(BF16)
    r1 = v - hi.astype(F32)
    mid = r1.astype(BF16)
    lo = (r1 - mid.astype(F32)).astype(BF16)
    return hi, mid, lo


def _ssd_kernel(z_ref, xbc_ref, dt_ref, convw_ref, convb_ref, dtb_ref, alog_ref,
                dskip_ref, g_ref, y_ref, tail_ref, state_ref, xa_ref, yacc_ref,
                *, inner, n_groups, d_state):
    L = SSD_CHUNK
    width = xbc_ref.shape[1]
    n_pairs = inner // LANES
    pairs_per_group = n_pairs // n_groups
    group_width = inner // n_groups

    @pl.when(pl.program_id(1) == 0)
    def _():
        tail_ref[...] = jnp.zeros_like(tail_ref)
        state_ref[...] = jnp.zeros_like(state_ref)

    slab = 512
    n_t = L // SUBLANES
    sub = lax.broadcasted_iota(jnp.int32, (n_t, SUBLANES, slab), 1)
    for c0 in range(0, width, slab):
        cs = slice(c0, c0 + slab)
        u = xbc_ref[:, cs].astype(F32)
        tiles = jnp.concatenate([tail_ref[:, cs], u], axis=0).reshape(n_t + 1, SUBLANES, slab)
        acc = convb_ref[:, cs] + u * convw_ref[SSD_CONV - 1:SSD_CONV, cs]
        for j in range(1, SSD_CONV):
            rot = pltpu.roll(tiles, j, axis=1)
            shifted = jnp.where(sub < j, rot[0:n_t], rot[1:n_t + 1]).reshape(L, slab)
            acc = acc + shifted * convw_ref[SSD_CONV - 1 - j:SSD_CONV - j, cs]
        tail_ref[:, cs] = u[L - SUBLANES:L]
        xa_ref[:, cs] = _silu(acc)

    dtr = dt_ref[...] + dtb_ref[...]
    dt = jnp.maximum(dtr, 0.0) + jnp.log(1.0 + jnp.exp(-jnp.abs(dtr)))
    a = -jnp.exp(alog_ref[...]) * dt
    row_i = lax.broadcasted_iota(jnp.int32, (L, L), 0)
    col_i = lax.broadcasted_iota(jnp.int32, (L, L), 1)
    causal = row_i >= col_i
    tril = jnp.where(causal, 1.0, 0.0).astype(BF16)
    acs = sum(jnp.dot(tril, part, preferred_element_type=F32) for part in _split3(a)) * LOG2E
    acs_t = acs.T
    dt_t = dt.T
    lane_lo = lax.broadcasted_iota(jnp.int32, (L, LANES), 1) < SSD_HEAD_DIM

    b0 = inner
    c0 = inner + n_groups * d_state
    for g in range(n_groups):
        bg = xa_ref[:, b0 + g * d_state:b0 + (g + 1) * d_state]
        cg = xa_ref[:, c0 + g * d_state:c0 + (g + 1) * d_state]
        cb = lax.dot_general(cg.astype(BF16), bg.astype(BF16), (((1,), (1,)), ((), ())),
                             preferred_element_type=F32)
        bg_t = bg.T
        for pp in range(g * pairs_per_group, (g + 1) * pairs_per_group):
            xs_pair = xa_ref[:, pp * LANES:(pp + 1) * LANES].astype(BF16)
            st_old = state_ref[pp]
            rhs = jnp.concatenate([xs_pair, st_old.astype(BF16)], axis=0)
            lhs, b_ws, keep = [], [], []
            for side in range(2):
                h = 2 * pp + side
                col = jnp.broadcast_to(acs[:, h:h + 1], (L, L))
                row = acs_t[h:h + 1, :]
                dt_row = dt_t[h:h + 1, :]
                last = acs[L - 1:L, h:h + 1]
                dec = jnp.exp2(jnp.where(causal, col - row, MASK_VALUE))
                m_in = (cb * dec * dt_row).astype(BF16)
                c_w = (cg * jnp.exp2(col[:, :d_state])).astype(BF16)
                lhs.append(jnp.concatenate([m_in, c_w], axis=1))
                b_ws.append((bg_t * (jnp.exp2(last - row) * dt_row)).astype(BF16))
                keep.append(jnp.exp2(last))
            y2 = jnp.dot(jnp.concatenate(lhs, axis=0), rhs, preferred_element_type=F32)
            s2 = jnp.dot(jnp.concatenate(b_ws, axis=0), xs_pair, preferred_element_type=F32)
            yacc_ref[:, pp * LANES:(pp + 1) * LANES] = jnp.where(lane_lo, y2[0:L], y2[L:2 * L])
            state_ref[pp] = jnp.where(lane_lo, st_old * keep[0] + s2[0:d_state],
                                      st_old * keep[1] + s2[d_state:2 * d_state])

    for g in range(n_groups):
        cs = slice(g * group_width, (g + 1) * group_width)
        zz = z_ref[:, cs].astype(F32)
        yv = (yacc_ref[:, cs] + dskip_ref[:, cs] * xa_ref[:, cs]) * _silu(zz)
        ms = jnp.mean(yv * yv, axis=-1, keepdims=True)
        y_ref[:, cs] = (yv * lax.rsqrt(ms + NORM_EPS) * g_ref[:, cs]).astype(y_ref.dtype)


def _ssd_branch(zg, xbc, dt_raw, conv_w, conv_b, dt_bias, a_log, d_skip, norm_g, *, batch, seq):
    t = zg.shape[0]
    inner = norm_g.shape[0]
    width = conv_w.shape[1]
    heads = inner // SSD_HEAD_DIM
    n_chunks = seq // SSD_CHUNK
    L = SSD_CHUNK

    def pad_heads(v):
        return jnp.pad(v.astype(F32), (0, LANES - heads)).reshape(1, LANES)

    convw = jnp.pad(conv_w.astype(F32), ((0, SUBLANES - SSD_CONV), (0, 0)))
    dskip = jnp.repeat(d_skip.astype(F32), SSD_HEAD_DIM).reshape(1, inner)
    row = lambda b, c: (b * n_chunks + c, 0)
    const = lambda b, c: (0, 0)
    est = (2 * L * (inner + width) * 2 + 2 * L * LANES * 4 + 2 * L * inner * 2
           + (inner // LANES) * SSD_STATE * LANES * 4 + L * (2 * width + inner) * 4 + (8 << 20))
    kern = functools.partial(_ssd_kernel, inner=inner, n_groups=SSD_GROUPS, d_state=SSD_STATE)
    return pl.pallas_call(
        kern,
        out_shape=jax.ShapeDtypeStruct((t, inner), BF16),
        grid=(batch, n_chunks),
        in_specs=[pl.BlockSpec((L, inner), row),
                  pl.BlockSpec((L, width), row),
                  pl.BlockSpec((L, LANES), row),
                  pl.BlockSpec((SUBLANES, width), const),
                  pl.BlockSpec((1, width), const),
                  pl.BlockSpec((1, LANES), const),
                  pl.BlockSpec((1, LANES), const),
                  pl.BlockSpec((1, inner), const),
                  pl.BlockSpec((1, inner), const)],
        out_specs=pl.BlockSpec((L, inner), row),
        scratch_shapes=[pltpu.VMEM((SUBLANES, width), F32),
                        pltpu.VMEM((inner // LANES, SSD_STATE, LANES), F32),
                        pltpu.VMEM((L, width), F32),
                        pltpu.VMEM((L, inner), F32)],
        compiler_params=_params(("parallel", "arbitrary"), est),
        name="ssd_scan",
    )(zg, xbc, dt_raw, convw, conv_b.astype(F32).reshape(1, width), pad_heads(dt_bias),
      pad_heads(a_log), dskip, norm_g.astype(F32).reshape(1, inner))


def _bucket_tiles(tq):
    max_exact = REL_BUCKETS // 2
    qi = np.arange(tq)[:, None]
    ki = np.arange(tq)[None, :]

    def bucket(dist):
        d = np.maximum(dist, 1).astype(np.float32)
        large = max_exact + (np.log(d / np.float32(max_exact)) / np.float32(math.log(REL_MAX_DIST / max_exact))
                             * np.float32(REL_BUCKETS - max_exact)).astype(np.int32)
        large = np.minimum(large, REL_BUCKETS - 1)
        return np.where(dist < max_exact, dist, large).astype(np.int32)

    diag = np.where(qi >= ki, bucket(np.maximum(qi - ki, 0)), -1)
    prev = bucket(tq + qi - ki)
    far = bucket(np.arange(tq + 1, 1 << 16))
    assert (far == REL_BUCKETS - 1).all()
    return np.stack([diag, prev]).astype(np.int32)


def _bias_kernel(rb_ref, bucket_ref, o_ref):
    h = pl.program_id(0)
    bk = bucket_ref[...]
    acc = jnp.zeros(bk.shape, F32)
    for b in range(REL_BUCKETS):
        acc = jnp.where(bk == b, rb_ref[b, h], acc)
    o_ref[0] = jnp.where(bk < 0, MASK_VALUE, (acc - rb_ref[REL_BUCKETS - 1, h]) * LOG2E)


def _bias_tiles(rel_bias, tq):
    buckets = jnp.asarray(_bucket_tiles(tq))
    return pl.pallas_call(
        _bias_kernel,
        out_shape=jax.ShapeDtypeStruct((DIFF_HEADS, 2, tq, tq), F32),
        grid=(DIFF_HEADS,),
        in_specs=[pl.BlockSpec(memory_space=pltpu.SMEM),
                  pl.BlockSpec((2, tq, tq), lambda h: (0, 0, 0))],
        out_specs=pl.BlockSpec((1, 2, tq, tq), lambda h: (h, 0, 0, 0)),
        compiler_params=_params(("arbitrary",), 8 * tq * tq * 4),
        name="t5_bias_tiles",
    )(rel_bias.astype(F32), buckets)


def _attn_kernel(lam_ref, q_ref, k_ref, v_ref, bias_ref, g_ref, o_ref,
                 q2_ref, vaug_ref, m_ref, acc_ref, *, tq, rows, lam_init):
    i = pl.program_id(2)
    dh = LANES // 2
    n_heads = q_ref.shape[0]

    @pl.when(i == 0)
    def _():
        for hd in range(n_heads):
            vaug_ref[hd, :, 0:LANES] = v_ref[hd]
            vaug_ref[hd, :, LANES:2 * LANES] = jnp.ones((v_ref.shape[1], LANES), BF16)

    lane = lax.broadcasted_iota(jnp.int32, (tq, LANES), 1)
    for hd in range(n_heads):
        qs = (q_ref[hd].astype(F32) * (dh ** -0.5 * LOG2E)).astype(BF16)
        zero = jnp.zeros_like(qs)
        q2_ref[hd, 0:tq] = jnp.where(lane < dh, qs, zero)
        q2_ref[hd, tq:2 * tq] = jnp.where(lane >= dh, qs, zero)
    m_ref[...] = jnp.full(m_ref.shape, MASK_VALUE, F32)
    acc_ref[...] = jnp.zeros_like(acc_ref)

    def step(j, bias_idx):
        start = pl.multiple_of(j * tq, tq)
        s_all = [lax.dot_general(q2_ref[hd], k_ref[hd, pl.ds(start, tq), :], (((1,), (1,)), ((), ())),
                                 preferred_element_type=F32) for hd in range(n_heads)]
        for r0 in range(0, 2 * tq, rows):
            rs = slice(r0, r0 + rows)
            q0 = r0 % tq
            klen = min(q0 + rows, tq) if bias_idx == 0 else tq
            for hd in range(n_heads):
                vb = vaug_ref[hd, pl.ds(start, klen), :]
                s = s_all[hd][rs, 0:klen]
                if bias_idx is not None:
                    s = s + bias_ref[hd, bias_idx, q0:q0 + rows, 0:klen]
                m_old = m_ref[hd, rs]
                m_new = jnp.maximum(m_old, jnp.max(s, axis=1, keepdims=True))
                alpha = jnp.exp2(m_old - m_new)
                p = jnp.exp2(s - jnp.concatenate([m_new] * (klen // LANES), axis=1))
                pv = jnp.dot(p.astype(BF16), vb, preferred_element_type=F32)
                acc_ref[hd, rs] = jnp.concatenate([alpha, alpha], axis=1) * acc_ref[hd, rs] + pv
                m_ref[hd, rs] = m_new

    def far_step(j, carry):
        step(j, None)
        return carry

    lax.fori_loop(0, jnp.maximum(i - 1, 0), far_step, 0)

    @pl.when(i >= 1)
    def _():
        step(i - 1, 1)

    step(i, 0)

    lv = lam_ref[...]
    s1 = jnp.sum(lv[0:1] * lv[1:2], axis=1, keepdims=True)
    s2 = jnp.sum(lv[2:3] * lv[3:4], axis=1, keepdims=True)
    lam = jnp.exp(s1) - jnp.exp(s2) + lam_init
    for hd in range(n_heads):
        o1 = acc_ref[hd, 0:tq, 0:LANES] / acc_ref[hd, 0:tq, LANES:2 * LANES]
        o2 = acc_ref[hd, tq:2 * tq, 0:LANES] / acc_ref[hd, tq:2 * tq, LANES:2 * LANES]
        o = o1 - lam * o2
        ms = jnp.mean(o * o, axis=-1, keepdims=True)
        o_ref[:, hd * LANES:(hd + 1) * LANES] = (
            o * lax.rsqrt(ms + NORM_EPS) * g_ref[...] * (1.0 - lam_init)).astype(o_ref.dtype)


def _diff_attention(qkv, lam_vecs, bias_tiles, subln_g, *, batch, seq, tq, layer_idx):
    t = qkv.shape[1]
    nq = seq // tq
    rows = min(512, tq)
    hps = ATTN_HEADS_PER_STEP
    n_hp = DIFF_HEADS // hps
    lam_init = 0.8 - 0.6 * math.exp(-0.3 * layer_idx)
    kern = functools.partial(_attn_kernel, tq=tq, rows=rows, lam_init=lam_init)
    est = hps * (2 * tq * LANES * 2 + 4 * seq * LANES * 2 + 4 * tq * tq * 4 + 2 * tq * LANES * 2
                 + 2 * tq * LANES * 2 + seq * 2 * LANES * 2 + 2 * tq * LANES * 4 + 2 * tq * 2 * LANES * 4
                 + 2 * tq * tq * 4 + 8 * rows * tq * 4)
    return pl.pallas_call(
        kern,
        out_shape=jax.ShapeDtypeStruct((t, DIFF_HEADS * LANES), BF16),
        grid=(batch, n_hp, nq),
        in_specs=[pl.BlockSpec((SUBLANES, LANES), lambda b, h, i: (0, 0)),
                  pl.BlockSpec((hps, tq, LANES), lambda b, h, i: (h, b * nq + i, 0)),
                  pl.BlockSpec((hps, seq, LANES), lambda b, h, i: (n_hp + h, b, 0)),
                  pl.BlockSpec((hps, seq, LANES), lambda b, h, i: (2 * n_hp + h, b, 0)),
                  pl.BlockSpec((hps, 2, tq, tq), lambda b, h, i: (h, 0, 0, 0)),
                  pl.BlockSpec((1, LANES), lambda b, h, i: (0, 0))],
        out_specs=pl.BlockSpec((tq, hps * LANES), lambda b, h, i: (b * nq + i, h)),
        scratch_shapes=[pltpu.VMEM((hps, 2 * tq, LANES), BF16),
                        pltpu.VMEM((hps, seq, 2 * LANES), BF16),
                        pltpu.VMEM((hps, 2 * tq, LANES), F32),
                        pltpu.VMEM((hps, 2 * tq, 2 * LANES), F32)],
        compiler_params=_params(("parallel", "parallel", "arbitrary"), est),
        name="diff_attention",
    )(lam_vecs, qkv, qkv, qkv, bias_tiles, subln_g)


def _mix_kernel(x_ref, y_ref, o_ref, gate_ref, wssd_ref, wdiff_ref, wmix_ref, g_ref, b_ref, out_ref):
    d = x_ref.shape[1]
    ssd = jnp.dot(y_ref[...], wssd_ref[...], preferred_element_type=F32)
    dif = jnp.dot(o_ref[...], wdiff_ref[...], preferred_element_type=F32)
    merged = gate_ref[:, 0:d].astype(F32) * ssd + gate_ref[:, d:2 * d].astype(F32) * dif
    mixed = jnp.dot(merged.astype(BF16), wmix_ref[...], preferred_element_type=F32)
    out_ref[...] = _layer_norm(DN_ALPHA * x_ref[...] + mixed, g_ref[...], b_ref[...])


def _mix(x, y_ssd, o_diff, proj, w_ssd, w_diff, w_mix, ln_g, ln_b, *, tm, gate_block):
    t, d = x.shape
    tm = min(tm, t)
    inner = y_ssd.shape[1]
    dw = o_diff.shape[1]
    row = lambda i: (i, 0)
    est = (2 * tm * (d * 4 + inner * 2 + dw * 2 + 2 * d * 2 + d * 4)
           + (inner * d + dw * d + d * d) * 2 + 6 * tm * d * 4)
    return pl.pallas_call(
        _mix_kernel,
        out_shape=jax.ShapeDtypeStruct((t, d), F32),
        grid=(t // tm,),
        in_specs=[pl.BlockSpec((tm, d), row),
                  pl.BlockSpec((tm, inner), row),
                  pl.BlockSpec((tm, dw), row),
                  pl.BlockSpec((tm, 2 * d), lambda i: (i, gate_block)),
                  _resident((inner, d)), _resident((dw, d)), _resident((d, d)),
                  _resident((1, d)), _resident((1, d))],
        out_specs=pl.BlockSpec((tm, d), row),
        compiler_params=_params(("parallel",), est),
        name="mix_ln1",
    )(x, y_ssd, o_diff, proj, w_ssd, w_diff, w_mix, ln_g, ln_b)


def _pack_bf16_pairs(v):
    w = v.shape[1] // 2
    lo = lax.bitcast_convert_type(v[:, :w].astype(BF16).astype(F32), jnp.int32)
    hi = lax.bitcast_convert_type(v[:, w:].astype(BF16).astype(F32), jnp.int32)
    return jnp.bitwise_or(hi, lax.shift_right_logical(lo, 16))


def _unpack_bf16_pairs(words):
    lo = lax.bitcast_convert_type(lax.shift_left(words, 16), F32)
    hi = lax.bitcast_convert_type(jnp.bitwise_and(words, -65536), F32)
    return jnp.concatenate([lo, hi], axis=1)


def _xattn_kernel(x_ref, kv_ref, wq_ref, wo_ref, g_ref, b_ref, out_ref, packed_ref, o_scr, *, heads):
    d = x_ref.shape[1]
    dh = d // heads
    xv = x_ref[...]
    q = jnp.dot(xv.astype(BF16), wq_ref[...], preferred_element_type=F32)
    q = (q * (dh ** -0.5)).astype(BF16)
    for h in range(heads):
        kh = kv_ref[:, h * dh:(h + 1) * dh]
        vh = kv_ref[:, d + h * dh:d + (h + 1) * dh]
        s = lax.dot_general(q[:, h * dh:(h + 1) * dh], kh, (((1,), (1,)), ((), ())),
                            preferred_element_type=F32)
        p = jnp.exp(s - jnp.max(s, axis=1, keepdims=True))
        p = p / jnp.sum(p, axis=1, keepdims=True)
        o_scr[:, h * dh:(h + 1) * dh] = jnp.dot(p.astype(BF16), vh, preferred_element_type=F32).astype(BF16)
    att = jnp.dot(o_scr[...], wo_ref[...], preferred_element_type=F32)
    y = _layer_norm(DN_ALPHA * xv + att, g_ref[...], b_ref[...])
    out_ref[...] = y
    packed_ref[...] = _pack_bf16_pairs(y)


def _cross_attention(x, kv, w_cq, w_co, ln_g, ln_b, *, batch, seq, tq):
    t, d = x.shape
    mem_len = kv.shape[0] // batch
    tq = min(tq, seq)
    nq = seq // tq
    est = (2 * tq * d * 4 * 2 + 2 * tq * d * 2 + 2 * mem_len * 2 * d * 2 + 2 * d * d * 2 + tq * d * 2
           + 8 * tq * d * 4)
    row = lambda b, i: (b * nq + i, 0)
    return pl.pallas_call(
        functools.partial(_xattn_kernel, heads=MEM_HEADS),
        out_shape=(jax.ShapeDtypeStruct((t, d), F32), jax.ShapeDtypeStruct((t, d // 2), jnp.int32)),
        grid=(batch, nq),
        in_specs=[pl.BlockSpec((tq, d), row),
                  pl.BlockSpec((mem_len, 2 * d), lambda b, i: (b, 0)),
                  _resident((d, d)), _resident((d, d)), _resident((1, d)), _resident((1, d))],
        out_specs=(pl.BlockSpec((tq, d), row), pl.BlockSpec((tq, d // 2), row)),
        scratch_shapes=[pltpu.VMEM((tq, d), BF16)],
        compiler_params=_params(("parallel", "parallel"), est),
        name="cross_attention_ln2",
    )(x, kv, w_cq, w_co, ln_g, ln_b)


def _router_kernel(x_ref, wr_ref, rb_ref, idx_ref, rank_ref, gates_ref, cnt_ref, run_ref, *, n_experts):
    tm = x_ref.shape[0]
    n_groups = N_EXPERT_GROUPS
    per = n_experts // n_groups

    @pl.when(pl.program_id(0) == 0)
    def _():
        run_ref[...] = jnp.zeros_like(run_ref)

    xv = x_ref[...]
    x_hi = xv.astype(BF16)
    x_lo = (xv - x_hi.astype(F32)).astype(BF16)
    nt = (((1,), (1,)), ((), ()))
    logits = (lax.dot_general(wr_ref[0], x_hi, nt, preferred_element_type=F32)
              + lax.dot_general(wr_ref[0], x_lo, nt, preferred_element_type=F32)
              + lax.dot_general(wr_ref[1], x_hi, nt, preferred_element_type=F32))
    sc = _sigmoid(logits[0:n_experts]).reshape(per, n_groups, tm)
    choice = sc + rb_ref[0:n_experts].reshape(per, n_groups, 1)
    neg = -jnp.inf
    member = lax.broadcasted_iota(jnp.int32, (per, n_groups, tm), 0)
    group3 = lax.broadcasted_iota(jnp.int32, (per, n_groups, tm), 1)
    m1 = jnp.max(choice, axis=0, keepdims=True)
    i1 = jnp.min(jnp.where(choice == m1, member, per), axis=0, keepdims=True)
    m2 = jnp.max(jnp.where(member == i1, neg, choice), axis=0, keepdims=True)
    gscore = (m1 + m2)
    gsel = jnp.zeros((1, n_groups, tm), F32)
    gidx = lax.broadcasted_iota(jnp.int32, (1, n_groups, tm), 1)
    cur = gscore
    for _ in range(TOP_GROUPS):
        mx = jnp.max(cur, axis=1, keepdims=True)
        ix = jnp.min(jnp.where(cur == mx, gidx, n_groups), axis=1, keepdims=True)
        hit = gidx == ix
        gsel = jnp.where(hit, 1.0, gsel)
        cur = jnp.where(hit, neg, cur)
    cur = jnp.where(gsel > 0.0, choice, neg)
    eidx = group3 * per + member
    esel = jnp.zeros((per, n_groups, tm), F32)
    hits, idx_rows = [], []
    for _ in range(TOP_K):
        mx = jnp.max(jnp.max(cur, axis=0, keepdims=True), axis=1, keepdims=True)
        ix = jnp.min(jnp.min(jnp.where(cur == mx, eidx, n_experts), axis=0, keepdims=True),
                     axis=1, keepdims=True)
        hit = eidx == ix
        esel = jnp.where(hit, 1.0, esel)
        cur = jnp.where(hit, neg, cur)
        hits.append(hit)
        idx_rows.append(ix.reshape(1, tm))
    w = esel * sc
    tot = jnp.sum(jnp.sum(w, axis=0, keepdims=True), axis=1, keepdims=True)
    gw = w / tot * ROUTED_SCALE

    r_i = lax.broadcasted_iota(jnp.int32, (tm, tm), 0)
    c_i = lax.broadcasted_iota(jnp.int32, (tm, tm), 1)
    upper = jnp.where(r_i <= c_i, 1.0, 0.0).astype(BF16)
    pref = jnp.dot(esel.reshape(n_experts, tm).astype(BF16), upper, preferred_element_type=F32)
    run = run_ref[...]
    rank3 = (jnp.concatenate([run] * (tm // LANES), axis=1) + pref - 1.0).reshape(per, n_groups, tm)

    def pick(hit, vals):
        return jnp.sum(jnp.sum(jnp.where(hit, vals, 0.0), axis=0, keepdims=True), axis=1).reshape(1, tm)

    idx_ref[...] = jnp.concatenate(idx_rows, axis=0)
    rank_ref[...] = jnp.concatenate([pick(h, rank3) for h in hits], axis=0).astype(jnp.int32)
    wk = jnp.concatenate([pick(h, gw) for h in hits] + [jnp.zeros((LANES - TOP_K, tm), F32)], axis=0)
    gates_ref[...] = wk.T
    run = run + jnp.broadcast_to(pref[:, tm - 1:tm], run.shape)
    run_ref[...] = run
    cnt_ref[...] = run


def _router(x, w_router, router_bias, *, tm, row0, t):
    d = x.shape[1]
    n_experts = w_router.shape[1]
    per = n_experts // N_EXPERT_GROUPS
    tm = min(tm, t)
    blk0 = row0 // tm

    def member_major(v):
        return v.reshape(N_EXPERT_GROUPS, per, *v.shape[1:]).swapaxes(0, 1).reshape(v.shape)

    wt = jnp.pad(member_major(w_router.astype(F32).T), ((0, LANES - n_experts), (0, 0)))
    w_hi = wt.astype(BF16)
    wr = jnp.stack([w_hi, (wt - w_hi.astype(F32)).astype(BF16)])
    rb = jnp.pad(member_major(router_bias.astype(F32)), (0, LANES - n_experts)).reshape(LANES, 1)
    est = (2 * tm * d * 4 + 2 * LANES * d * 2 + 2 * tm * LANES * 4 + 60 * n_experts * tm * 4 + 3 * tm * tm * 4)
    pick_spec = pl.BlockSpec((TOP_K, tm), lambda i: (0, i))
    idx_t, rank_t, gates, counts = pl.pallas_call(
        functools.partial(_router_kernel, n_experts=n_experts),
        out_shape=(jax.ShapeDtypeStruct((TOP_K, t), jnp.int32),
                   jax.ShapeDtypeStruct((TOP_K, t), jnp.int32),
                   jax.ShapeDtypeStruct((t, LANES), F32),
                   jax.ShapeDtypeStruct((n_experts, LANES), F32)),
        grid=(t // tm,),
        in_specs=[pl.BlockSpec((tm, d), lambda i: (blk0 + i, 0)),
                  _resident((2, LANES, d)), _resident((LANES, 1))],
        out_specs=(pick_spec, pick_spec,
                   pl.BlockSpec((tm, LANES), lambda i: (i, 0)),
                   pl.BlockSpec((n_experts, LANES), lambda i: (0, 0))),
        scratch_shapes=[pltpu.VMEM((n_experts, LANES), F32)],
        compiler_params=_params(("arbitrary",), est),
        name="router",
    )(x, wr, rb)
    counts = counts[:, 0].astype(jnp.int32).reshape(per, N_EXPERT_GROUPS).T.reshape(n_experts)
    return idx_t, rank_t, gates, counts


def _positions_kernel(cnt_ref, idx_ref, rank_ref, pos_ref, texp_ref, off_ref, *, n_experts, rows, n_tiles):
    @pl.when(pl.program_id(0) == 0)
    def _():
        def per_expert(e, toff):
            off_ref[e] = toff * rows
            nt = lax.shift_right_logical(cnt_ref[e] + (rows - 1), int(math.log2(rows)))

            def fill(j, c):
                texp_ref[toff + j] = e
                return c

            lax.fori_loop(0, nt, fill, 0)
            return toff + nt

        n_used = lax.fori_loop(0, n_experts, per_expert, 0)

        def tail(j, c):
            texp_ref[j] = n_experts - 1
            return c

        lax.fori_loop(n_used, n_tiles, tail, 0)
        texp_ref[n_tiles] = n_used

    idx = idx_ref[...]
    pos = rank_ref[...]
    for e in range(n_experts):
        pos = pos + jnp.where(idx == e, off_ref[e], 0)
    pos_ref[...] = pos


def _positions(counts, idx_t, rank_t, *, rows, n_tiles, tm):
    n_experts = counts.shape[0]
    k, t = idx_t.shape
    tm = min(tm, t)
    spec = pl.BlockSpec((k, tm), lambda i: (0, i))
    kern = functools.partial(_positions_kernel, n_experts=n_experts, rows=rows, n_tiles=n_tiles)
    return pl.pallas_call(
        kern,
        out_shape=(jax.ShapeDtypeStruct((k, t), jnp.int32),
                   jax.ShapeDtypeStruct((n_tiles + 1,), jnp.int32)),
        grid=(t // tm,),
        in_specs=[pl.BlockSpec(memory_space=pltpu.SMEM), spec, spec],
        out_specs=(spec, pl.BlockSpec(memory_space=pltpu.SMEM)),
        scratch_shapes=[pltpu.SMEM((n_experts,), jnp.int32)],
        compiler_params=_params(("arbitrary",), 16 * k * tm * 4),
        name="moe_positions",
    )(counts, idx_t, rank_t)


def _sc_mesh():
    return plsc.VectorSubcoreMesh(core_axis_name="c", subcore_axis_name="s",
                                  num_cores=SC_CORES, num_subcores=SC_SUBCORES)


def _sc_dispatch(packed, pos_chunks, n_rows, row0):
    w = packed.shape[1]
    n_chunks, k, n = pos_chunks.shape
    per_worker = n_chunks // (SC_CORES * SC_SUBCORES)

    @functools.partial(
        pl.kernel, mesh=_sc_mesh(),
        out_type=jax.ShapeDtypeStruct((n_rows, w), packed.dtype),
        scratch_types=[pltpu.VMEM((k, n), jnp.int32), pltpu.VMEM((n, w), packed.dtype),
                       pltpu.SemaphoreType.DMA],
        name="moe_dispatch_sc",
    )
    def scatter_rows(x_hbm, pos_hbm, out_hbm, idx_v, rows_v, sem):
        wid = lax.axis_index("s") * SC_CORES + lax.axis_index("c")

        @pl.loop(0, per_worker)
        def _(step):
            c = wid * per_worker + step
            pltpu.sync_copy(pos_hbm.at[c], idx_v)
            pltpu.sync_copy(x_hbm.at[pl.ds(row0 + c * n, n)], rows_v)
            copies = [pltpu.async_copy(rows_v, out_hbm.at[idx_v.at[kk]], sem) for kk in range(k)]
            for cp in copies:
                cp.wait()

    return scatter_rows(packed, pos_chunks)


def _sc_combine(sorted_rows, pos_chunks):
    _, w = sorted_rows.shape
    n_chunks, k, n = pos_chunks.shape
    per_worker = n_chunks // (SC_CORES * SC_SUBCORES)

    @functools.partial(
        pl.kernel, mesh=_sc_mesh(),
        out_type=jax.ShapeDtypeStruct((k, n_chunks * n, w), sorted_rows.dtype),
        scratch_types=[pltpu.VMEM((k, n), jnp.int32), pltpu.VMEM((n, w), sorted_rows.dtype),
                       pltpu.SemaphoreType.DMA],
        name="moe_combine_sc",
    )
    def gather_rows(y_hbm, pos_hbm, out_hbm, idx_v, rows_v, sem):
        wid = lax.axis_index("s") * SC_CORES + lax.axis_index("c")

        @pl.loop(0, per_worker)
        def _(step):
            c = wid * per_worker + step
            pltpu.sync_copy(pos_hbm.at[c], idx_v)
            for kk in range(k):
                pltpu.async_copy(y_hbm.at[idx_v.at[kk]], rows_v, sem).wait()
                pltpu.sync_copy(rows_v, out_hbm.at[kk, pl.ds(c * n, n)])

    return gather_rows(sorted_rows, pos_chunks)


def _expert_kernel(texp_ref, xs_ref, wgu_ref, wdn_ref, ys_ref, *, n_tiles):
    ff = wdn_ref.shape[1]

    @pl.when(pl.program_id(0) < texp_ref[n_tiles])
    def _():
        xv = _unpack_bf16_pairs(xs_ref[...]).astype(BF16)
        gu = jnp.dot(xv, wgu_ref[0], preferred_element_type=F32)
        hid = _silu(gu[:, 0:ff]) * gu[:, ff:2 * ff]
        ys_ref[...] = _pack_bf16_pairs(jnp.dot(hid.astype(BF16), wdn_ref[0], preferred_element_type=F32))


def _experts(tile_expert, sorted_rows, w_gu, w_dn, *, rows):
    n_rows, w = sorted_rows.shape
    n_tiles = n_rows // rows
    _, d, ff2 = w_gu.shape
    ff = ff2 // 2
    tile = lambda j, te: (jnp.minimum(j, te[n_tiles] - 1), 0)
    est = 4 * rows * w * 4 + 2 * (d * ff2 + ff * d) * 2 + 4 * rows * d * 4 + 4 * rows * ff2 * 4
    return pl.pallas_call(
        functools.partial(_expert_kernel, n_tiles=n_tiles),
        out_shape=jax.ShapeDtypeStruct((n_rows, w), sorted_rows.dtype),
        grid_spec=pltpu.PrefetchScalarGridSpec(
            num_scalar_prefetch=1,
            grid=(n_tiles,),
            in_specs=[pl.BlockSpec((rows, w), tile),
                      pl.BlockSpec((1, d, ff2), lambda j, te: (te[j], 0, 0)),
                      pl.BlockSpec((1, ff, d), lambda j, te: (te[j], 0, 0))],
            out_specs=pl.BlockSpec((rows, w), tile)),
        compiler_params=_params(("arbitrary",), est),
        name="moe_experts",
    )(tile_expert, sorted_rows, w_gu, w_dn)


def _moe_out_kernel(x_ref, yk_ref, gates_ref, wsgu_ref, wsdn_ref, g_ref, b_ref, *rest):
    out_ref = rest[-1]
    ff = wsdn_ref.shape[0]
    xv = x_ref[...]
    gu = jnp.dot(xv.astype(BF16), wsgu_ref[...], preferred_element_type=F32)
    hid = _silu(gu[:, 0:ff]) * gu[:, ff:2 * ff]
    acc = jnp.dot(hid.astype(BF16), wsdn_ref[...], preferred_element_type=F32)
    for k in range(yk_ref.shape[0]):
        acc = acc + gates_ref[:, k:k + 1] * _unpack_bf16_pairs(yk_ref[k])
    out_ref[...] = _layer_norm(DN_ALPHA * xv + acc, g_ref[...], b_ref[...])


def _moe_out(x, yk, gates, w_sgu, w_sdn, ln_g, ln_b, *, tm, row0, prev):
    t_all, d = x.shape
    k, t, w = yk.shape
    sff2 = w_sgu.shape[1]
    tm = min(tm, t)
    blk0 = row0 // tm
    est = (2 * tm * d * 4 * 2 + 2 * k * tm * w * 4 + 2 * tm * LANES * 4
           + (d * sff2 + (sff2 // 2) * d) * 2 + 6 * tm * d * 4)
    in_specs = [pl.BlockSpec((tm, d), lambda i: (blk0 + i, 0)),
                pl.BlockSpec((k, tm, w), lambda i: (0, i, 0)),
                pl.BlockSpec((tm, LANES), lambda i: (i, 0)),
                _resident((d, sff2)), _resident((sff2 // 2, d)),
                _resident((1, d)), _resident((1, d))]
    args = [x, yk, gates, w_sgu, w_sdn, ln_g, ln_b]
    aliases = {}
    if prev is not None:
        in_specs.append(pl.BlockSpec(memory_space=pl.ANY))
        args.append(prev)
        aliases = {len(args) - 1: 0}
    return pl.pallas_call(
        _moe_out_kernel,
        out_shape=jax.ShapeDtypeStruct((t_all, d), F32),
        grid=(t // tm,),
        in_specs=in_specs,
        out_specs=pl.BlockSpec((tm, d), lambda i: (blk0 + i, 0)),
        input_output_aliases=aliases,
        compiler_params=_params(("parallel",), est),
        name="moe_out_ln3",
    )(*args)


def _moe(x, packed, w_router, router_bias, w_gu, w_dn, w_sgu, w_sdn, ln_g, ln_b):
    t_all, d = x.shape
    n_experts = w_router.shape[1]
    n = SC_CHUNK_TOKENS
    splits = MOE_SPLITS if t_all % (MOE_SPLITS * n * SC_CORES * SC_SUBCORES) == 0 else 1
    t = t_all // splits
    rows = min(MOE_ROWS_PER_TILE, t)
    n_tiles = (t * TOP_K) // rows + n_experts
    parts = []
    for s in range(splits):
        row0 = s * t
        idx_t, rank_t, gates, counts = _router(x, w_router, router_bias, tm=512, row0=row0, t=t)
        pos_t, tile_expert = _positions(counts, idx_t, rank_t, rows=rows, n_tiles=n_tiles, tm=2048)
        pos_chunks = pos_t.reshape(TOP_K, t // n, n).transpose(1, 0, 2)
        sorted_x = _sc_dispatch(packed, pos_chunks, n_tiles * rows, row0)
        parts.append((row0, gates, tile_expert, pos_chunks, sorted_x))
    out = None
    for row0, gates, tile_expert, pos_chunks, sorted_x in parts:
        sorted_y = _experts(tile_expert, sorted_x, w_gu, w_dn, rows=rows)
        yk = _sc_combine(sorted_y, pos_chunks)
        out = _moe_out(x, yk, gates, w_sgu, w_sdn, ln_g, ln_b, tm=512, row0=row0, prev=out)
    return out


def kernel(x, mem, w_in, conv_w, conv_b, dt_bias, a_log, d_skip, ssd_norm_g, lambda_q1, lambda_k1, lambda_q2, lambda_k2, subln_g, rel_bias, w_ssd_br, w_diff_br, w_mix_out, ln1_g, ln1_b, w_cq, w_ckv, w_co, ln2_g, ln2_b, w_router, router_bias, w_exp_gu, w_exp_down, w_sh_gu, w_sh_down, ln3_g, ln3_b):
    batch, seq, d = x.shape
    depth = w_in.shape[0]
    inner = w_ssd_br.shape[1]
    xbc_width = conv_w.shape[2]
    heads = dt_bias.shape[1]
    diff_width = w_diff_br.shape[1]
    o_z, o_xbc = inner, inner + xbc_width
    o_dt = o_xbc + heads
    o_v = o_dt + 3 * diff_width
    t = batch * seq
    tq_attn = min(512, seq)

    def vec(v):
        return v.astype(F32).reshape(1, -1)

    xt = x.reshape(t, d)
    memt = mem.reshape(-1, d)
    bias_tiles = _bias_tiles(rel_bias, tq_attn)

    for l in range(depth):
        wl = w_in[l]
        w_big = jnp.concatenate([wl[:, o_z:o_xbc], wl[:, o_dt:o_v], wl[:, :o_z], wl[:, o_v:]],
                                axis=1).astype(BF16)
        assert 2 * d == inner
        w_dt = jnp.pad(wl[:, o_xbc:o_dt], ((0, 0), (0, LANES - heads))).astype(BF16)
        xbc, qkv, zg, dt_raw = _in_proj(xt, w_big, w_dt, n_xbc_cols=xbc_width, n_qkv_cols=3 * diff_width,
                                        n_z_cols=inner, tm=1024, tn=1024)

        y_ssd = _ssd_branch(zg, xbc, dt_raw, conv_w[l], conv_b[l], dt_bias[l], a_log[l], d_skip[l],
                            ssd_norm_g[l], batch=batch, seq=seq)

        lam_vecs = jnp.pad(jnp.stack([lambda_q1[l], lambda_k1[l], lambda_q2[l], lambda_k2[l]]).astype(F32),
                           ((0, SUBLANES - 4), (0, LANES - lambda_q1.shape[1])))
        o_diff = _diff_attention(qkv, lam_vecs, bias_tiles, vec(subln_g[l]), batch=batch, seq=seq,
                                 tq=tq_attn, layer_idx=l)

        xt = _mix(xt, y_ssd, o_diff, zg, w_ssd_br[l].astype(BF16), w_diff_br[l].astype(BF16),
                  w_mix_out[l].astype(BF16), vec(ln1_g[l]), vec(ln1_b[l]), tm=512, gate_block=1)

        kv = _matmul(memt, w_ckv[l].astype(BF16), BF16, tm=1024, tn=1024, name="mem_kv_proj")
        xt, packed = _cross_attention(xt, kv, w_cq[l].astype(BF16), w_co[l].astype(BF16), vec(ln2_g[l]),
                                      vec(ln2_b[l]), batch=batch, seq=seq, tq=1024)

        xt = _moe(xt, packed, w_router[l], router_bias[l], w_exp_gu[l].astype(BF16),
                  w_exp_down[l].astype(BF16), w_sh_gu[l].astype(BF16), w_sh_down[l].astype(BF16),
                  vec(ln3_g[l]), vec(ln3_b[l]))
    return xt.reshape(batch, seq, d)
```

```python
import functools
import math

import numpy as np
import jax
import jax.numpy as jnp
from jax import lax
from jax.experimental import pallas as pl
from jax.experimental.pallas import tpu as pltpu
from jax.experimental.pallas import tpu_sc as plsc

F32 = jnp.float32
BF16 = jnp.bfloat16

SSD_HEAD_DIM = 64
SSD_GROUPS = 4
SSD_STATE = 128
SSD_CONV = 4
SSD_CHUNK = 128
DIFF_HEADS = 8
REL_BUCKETS = 32
REL_MAX_DIST = 128
MEM_HEADS = 4
TOP_K = 8
N_EXPERT_GROUPS = 8
TOP_GROUPS = 4
ROUTED_SCALE = 2.5
NORM_EPS = 1e-5
DEPTH = 1
DN_ALPHA = (2.0 * DEPTH) ** 0.25

LANES = 128
SUBLANES = 8
VMEM_CAP_BYTES = 64 * 1024 * 1024
MASK_VALUE = -1e30
LOG2E = math.log2(math.e)
IN_PROJ_SLAB = 256
ATTN_HEADS_PER_STEP = 2
SC_CORES = 2
SC_SUBCORES = 16
SC_CHUNK_TOKENS = 128
MOE_SPLITS = 2
MOE_ROWS_PER_TILE = 1024


def _vmem_limit(estimate_bytes):
    return int(min(estimate_bytes * 5 // 4 + (4 << 20), VMEM_CAP_BYTES - (6 << 20)))


def _params(semantics, vmem_estimate):
    return pltpu.CompilerParams(dimension_semantics=semantics,
                                vmem_limit_bytes=_vmem_limit(vmem_estimate))


def _resident(shape):
    nd = len(shape)
    return pl.BlockSpec(shape, lambda *_: (0,) * nd, pipeline_mode=pl.Buffered(1))


def _layer_norm(v, g, b):
    mu = jnp.mean(v, axis=-1, keepdims=True)
    d = v - mu
    var = jnp.mean(d * d, axis=-1, keepdims=True)
    return d * lax.rsqrt(var + NORM_EPS) * g + b


def _sigmoid(v):
    return 0.5 * jnp.tanh(0.5 * v) + 0.5


def _silu(v):
    h = 0.5 * v
    return h + h * jnp.tanh(h)


def _matmul_kernel(x_ref, w_ref, o_ref, xb_ref):
    @pl.when(pl.program_id(1) == 0)
    def _():
        xb_ref[...] = x_ref[...].astype(BF16)

    o_ref[...] = jnp.dot(xb_ref[...], w_ref[...], preferred_element_type=F32).astype(o_ref.dtype)


def _matmul(x, w, out_dtype, *, tm, tn, name):
    m, k = x.shape
    n = w.shape[1]
    tm, tn = min(tm, m), min(tn, n)
    est = (2 * tm * k * x.dtype.itemsize + tm * k * 2 + 2 * k * tn * 2
           + 2 * tm * tn * jnp.dtype(out_dtype).itemsize + tm * tn * 4)
    return pl.pallas_call(
        _matmul_kernel,
        out_shape=jax.ShapeDtypeStruct((m, n), out_dtype),
        grid=(m // tm, n // tn),
        in_specs=[pl.BlockSpec((tm, k), lambda i, j: (i, 0)),
                  pl.BlockSpec((k, tn), lambda i, j: (0, j))],
        out_specs=pl.BlockSpec((tm, tn), lambda i, j: (i, j)),
        scratch_shapes=[pltpu.VMEM((tm, k), BF16)],
        compiler_params=_params(("parallel", "arbitrary"), est),
        name=name,
    )(x, w)


def _in_proj_kernel(x_ref, w_ref, wdt_ref, xbc_ref, qkv_ref, zg_ref, dt_ref, xb_ref, *, n_xbc, n_qkv, n_z):
    j = pl.program_id(1)
    j_z = n_xbc + n_qkv

    @pl.when(j == 0)
    def _():
        xb = x_ref[...].astype(BF16)
        xb_ref[...] = xb
        dt_ref[...] = jnp.dot(xb, wdt_ref[...], preferred_element_type=F32)

    tn = w_ref.shape[1]
    slabs = [slice(c0, c0 + IN_PROJ_SLAB) for c0 in range(0, tn, IN_PROJ_SLAB)]

    def product(cs):
        return jnp.dot(xb_ref[...], w_ref[:, cs], preferred_element_type=F32)

    @pl.when(j < n_xbc)
    def _():
        for cs in slabs:
            xbc_ref[:, cs] = product(cs).astype(BF16)

    @pl.when(jnp.logical_and(j >= n_xbc, j < j_z))
    def _():
        for cs in slabs:
            acc = product(cs)
            for hh in range(IN_PROJ_SLAB // LANES):
                qkv_ref[cs.start // LANES + hh] = acc[:, hh * LANES:(hh + 1) * LANES].astype(BF16)

    @pl.when(jnp.logical_and(j >= j_z, j < j_z + n_z))
    def _():
        for cs in slabs:
            zg_ref[:, cs] = product(cs).astype(BF16)

    @pl.when(j >= j_z + n_z)
    def _():
        for cs in slabs:
            zg_ref[:, cs] = _sigmoid(product(cs)).astype(BF16)


def _in_proj(x, w, w_dt, *, n_xbc_cols, n_qkv_cols, n_z_cols, tm, tn):
    m, k = x.shape
    n = w.shape[1]
    tm = min(tm, m)
    n_xbc, n_qkv, n_z = n_xbc_cols // tn, n_qkv_cols // tn, n_z_cols // tn
    n_zg = n // tn - n_xbc - n_qkv
    per = tn // LANES
    est = (2 * tm * k * 4 + tm * k * 2 + 2 * k * tn * 2 + 3 * 2 * tm * tn * 2 + 2 * tm * tn * 4
           + k * LANES * 2 + 2 * tm * LANES * 4)
    kern = functools.partial(_in_proj_kernel, n_xbc=n_xbc, n_qkv=n_qkv, n_z=n_z)
    return pl.pallas_call(
        kern,
        out_shape=(jax.ShapeDtypeStruct((m, n_xbc_cols), BF16),
                   jax.ShapeDtypeStruct((n_qkv_cols // LANES, m, LANES), BF16),
                   jax.ShapeDtypeStruct((m, n_zg * tn), BF16),
                   jax.ShapeDtypeStruct((m, LANES), F32)),
        grid=(m // tm, n // tn),
        in_specs=[pl.BlockSpec((tm, k), lambda i, j: (i, 0)),
                  pl.BlockSpec((k, tn), lambda i, j: (0, j)),
                  _resident((k, LANES))],
        out_specs=(pl.BlockSpec((tm, tn), lambda i, j: (i, jnp.minimum(j, n_xbc - 1))),
                   pl.BlockSpec((per, tm, LANES), lambda i, j: (jnp.clip(j - n_xbc, 0, n_qkv - 1), i, 0)),
                   pl.BlockSpec((tm, tn), lambda i, j: (i, jnp.clip(j - n_xbc - n_qkv, 0, n_zg - 1))),
                   pl.BlockSpec((tm, LANES), lambda i, j: (i, 0))),
        scratch_shapes=[pltpu.VMEM((tm, k), BF16)],
        compiler_params=_params(("parallel", "arbitrary"), est),
        name="in_proj",
    )(x, w, w_dt)


def _split3(v):
    hi = v.astype(BF16)
    r1 = v - hi.astype(F32)
    mid = r1.astype(BF16)
    lo = (r1 - mid.astype(F32)).astype(BF16)
    return hi, mid, lo


def _ssd_kernel(z_ref, xbc_ref, dt_ref, convw_ref, convb_ref, dtb_ref, alog_ref,
                dskip_ref, g_ref, y_ref, tail_ref, state_ref, xa_ref, yacc_ref,
                *, inner, n_groups, d_state):
    L = SSD_CHUNK
    width = xbc_ref.shape[1]
    n_pairs = inner // LANES
    pairs_per_group = n_pairs // n_groups
    group_width = inner // n_groups

    @pl.when(pl.program_id(1) == 0)
    def _():
        tail_ref[...] = jnp.zeros_like(tail_ref)
        state_ref[...] = jnp.zeros_like(state_ref)

    slab = 512
    n_t = L // SUBLANES
    sub = lax.broadcasted_iota(jnp.int32, (n_t, SUBLANES, slab), 1)
    for c0 in range(0, width, slab):
        cs = slice(c0, c0 + slab)
        u = xbc_ref[:, cs].astype(F32)
        tiles = jnp.concatenate([tail_ref[:, cs], u], axis=0).reshape(n_t + 1, SUBLANES, slab)
        acc = convb_ref[:, cs] + u * convw_ref[SSD_CONV - 1:SSD_CONV, cs]
        for j in range(1, SSD_CONV):
            rot = pltpu.roll(tiles, j, axis=1)
            shifted = jnp.where(sub < j, rot[0:n_t], rot[1:n_t + 1]).reshape(L, slab)
            acc = acc + shifted * convw_ref[SSD_CONV - 1 - j:SSD_CONV - j, cs]
        tail_ref[:, cs] = u[L - SUBLANES:L]
        xa_ref[:, cs] = _silu(acc)

    dtr = dt_ref[...] + dtb_ref[...]
    dt = jnp.maximum(dtr, 0.0) + jnp.log(1.0 + jnp.exp(-jnp.abs(dtr)))
    a = -jnp.exp(alog_ref[...]) * dt
    row_i = lax.broadcasted_iota(jnp.int32, (L, L), 0)
    col_i = lax.broadcasted_iota(jnp.int32, (L, L), 1)
    causal = row_i >= col_i
    tril = jnp.where(causal, 1.0, 0.0).astype(BF16)
    acs = sum(jnp.dot(tril, part, preferred_element_type=F32) for part in _split3(a)) * LOG2E
    acs_t = acs.T
    dt_t = dt.T
    lane_lo = lax.broadcasted_iota(jnp.int32, (L, LANES), 1) < SSD_HEAD_DIM

    b0 = inner
    c0 = inner + n_groups * d_state
    for g in range(n_groups):
        bg = xa_ref[:, b0 + g * d_state:b0 + (g + 1) * d_state]
        cg = xa_ref[:, c0 + g * d_state:c0 + (g + 1) * d_state]
        cb = lax.dot_general(cg.astype(BF16), bg.astype(BF16), (((1,), (1,)), ((), ())),
                             preferred_element_type=F32)
        bg_t = bg.T
        for pp in range(g * pairs_per_group, (g + 1) * pairs_per_group):
            xs_pair = xa_ref[:, pp * LANES:(pp + 1) * LANES].astype(BF16)
            st_old = state_ref[pp]
            rhs = jnp.concatenate([xs_pair, st_old.astype(BF16)], axis=0)
            lhs, b_ws, keep = [], [], []
            for side in range(2):
                h = 2 * pp + side
                col = jnp.broadcast_to(acs[:, h:h + 1], (L, L))
                row = acs_t[h:h + 1, :]
                dt_row = dt_t[h:h + 1, :]
                last = acs[L - 1:L, h:h + 1]
                dec = jnp.exp2(jnp.where(causal, col - row, MASK_VALUE))
                m_in = (cb * dec * dt_row).astype(BF16)
                c_w = (cg * jnp.exp2(col[:, :d_state])).astype(BF16)
                lhs.append(jnp.concatenate([m_in, c_w], axis=1))
                b_ws.append((bg_t * (jnp.exp2(last - row) * dt_row)).astype(BF16))
                keep.append(jnp.exp2(last))
            y2 = jnp.dot(jnp.concatenate(lhs, axis=0), rhs, preferred_element_type=F32)
            s2 = jnp.dot(jnp.concatenate(b_ws, axis=0), xs_pair, preferred_element_type=F32)
            yacc_ref[:, pp * LANES:(pp + 1) * LANES] = jnp.where(lane_lo, y2[0:L], y2[L:2 * L])
            state_ref[pp] = jnp.where(lane_lo, st_old * keep[0] + s2[0:d_state],
                                      st_old * keep[1] + s2[d_state:2 * d_state])

    for g in range(n_groups):
        cs = slice(g * group_width, (g + 1) * group_width)
        zz = z_ref[:, cs].astype(F32)
        yv = (yacc_ref[:, cs] + dskip_ref[:, cs] * xa_ref[:, cs]) * _silu(zz)
        ms = jnp.mean(yv * yv, axis=-1, keepdims=True)
        y_ref[:, cs] = (yv * lax.rsqrt(ms + NORM_EPS) * g_ref[:, cs]).astype(y_ref.dtype)


def _ssd_branch(zg, xbc, dt_raw, conv_w, conv_b, dt_bias, a_log, d_skip, norm_g, *, batch, seq):
    t = zg.shape[0]
    inner = norm_g.shape[0]
    width = conv_w.shape[1]
    heads = inner // SSD_HEAD_DIM
    n_chunks = seq // SSD_CHUNK
    L = SSD_CHUNK

    def pad_heads(v):
        return jnp.pad(v.astype(F32), (0, LANES - heads)).reshape(1, LANES)

    convw = jnp.pad(conv_w.astype(F32), ((0, SUBLANES - SSD_CONV), (0, 0)))
    dskip = jnp.repeat(d_skip.astype(F32), SSD_HEAD_DIM).reshape(1, inner)
    row = lambda b, c: (b * n_chunks + c, 0)
    const = lambda b, c: (0, 0)
    est = (2 * L * (inner + width) * 2 + 2 * L * LANES * 4 + 2 * L * inner * 2
           + (inner // LANES) * SSD_STATE * LANES * 4 + L * (2 * width + inner) * 4 + (8 << 20))
    kern = functools.partial(_ssd_kernel, inner=inner, n_groups=SSD_GROUPS, d_state=SSD_STATE)
    return pl.pallas_call(
        kern,
        out_shape=jax.ShapeDtypeStruct((t, inner), BF16),
        grid=(batch, n_chunks),
        in_specs=[pl.BlockSpec((L, inner), row),
                  pl.BlockSpec((L, width), row),
                  pl.BlockSpec((L, LANES), row),
                  pl.BlockSpec((SUBLANES, width), const),
                  pl.BlockSpec((1, width), const),
                  pl.BlockSpec((1, LANES), const),
                  pl.BlockSpec((1, LANES), const),
                  pl.BlockSpec((1, inner), const),
                  pl.BlockSpec((1, inner), const)],
        out_specs=pl.BlockSpec((L, inner), row),
        scratch_shapes=[pltpu.VMEM((SUBLANES, width), F32),
                        pltpu.VMEM((inner // LANES, SSD_STATE, LANES), F32),
                        pltpu.VMEM((L, width), F32),
                        pltpu.VMEM((L, inner), F32)],
        compiler_params=_params(("parallel", "arbitrary"), est),
        name="ssd_scan",
    )(zg, xbc, dt_raw, convw, conv_b.astype(F32).reshape(1, width), pad_heads(dt_bias),
      pad_heads(a_log), dskip, norm_g.astype(F32).reshape(1, inner))


def _bucket_tiles(tq):
    max_exact = REL_BUCKETS // 2
    qi = np.arange(tq)[:, None]
    ki = np.arange(tq)[None, :]

    def bucket(dist):
        d = np.maximum(dist, 1).astype(np.float32)
        large = max_exact + (np.log(d / np.float32(max_exact)) / np.float32(math.log(REL_MAX_DIST / max_exact))
                             * np.float32(REL_BUCKETS - max_exact)).astype(np.int32)
        large = np.minimum(large, REL_BUCKETS - 1)
        return np.where(dist < max_exact, dist, large).astype(np.int32)

    diag = np.where(qi >= ki, bucket(np.maximum(qi - ki, 0)), -1)
    prev = bucket(tq + qi - ki)
    far = bucket(np.arange(tq + 1, 1 << 16))
    assert (far == REL_BUCKETS - 1).all()
    return np.stack([diag, prev]).astype(np.int32)


def _bias_kernel(rb_ref, bucket_ref, o_ref):
    h = pl.program_id(0)
    bk = bucket_ref[...]
    acc = jnp.zeros(bk.shape, F32)
    for b in range(REL_BUCKETS):
        acc = jnp.where(bk == b, rb_ref[b, h], acc)
    o_ref[0] = jnp.where(bk < 0, MASK_VALUE, (acc - rb_ref[REL_BUCKETS - 1, h]) * LOG2E)


def _bias_tiles(rel_bias, tq):
    buckets = jnp.asarray(_bucket_tiles(tq))
    return pl.pallas_call(
        _bias_kernel,
        out_shape=jax.ShapeDtypeStruct((DIFF_HEADS, 2, tq, tq), F32),
        grid=(DIFF_HEADS,),
        in_specs=[pl.BlockSpec(memory_space=pltpu.SMEM),
                  pl.BlockSpec((2, tq, tq), lambda h: (0, 0, 0))],
        out_specs=pl.BlockSpec((1, 2, tq, tq), lambda h: (h, 0, 0, 0)),
        compiler_params=_params(("arbitrary",), 8 * tq * tq * 4),
        name="t5_bias_tiles",
    )(rel_bias.astype(F32), buckets)


def _attn_kernel(lam_ref, q_ref, k_ref, v_ref, bias_ref, g_ref, o_ref,
                 q2_ref, vaug_ref, m_ref, acc_ref, *, tq, rows, lam_init):
    i = pl.program_id(2)
    dh = LANES // 2
    n_heads = q_ref.shape[0]

    @pl.when(i == 0)
    def _():
        for hd in range(n_heads):
            vaug_ref[hd, :, 0:LANES] = v_ref[hd]
            vaug_ref[hd, :, LANES:2 * LANES] = jnp.ones((v_ref.shape[1], LANES), BF16)

    lane = lax.broadcasted_iota(jnp.int32, (tq, LANES), 1)
    for hd in range(n_heads):
        qs = (q_ref[hd].astype(F32) * (dh ** -0.5 * LOG2E)).astype(BF16)
        zero = jnp.zeros_like(qs)
        q2_ref[hd, 0:tq] = jnp.where(lane < dh, qs, zero)
        q2_ref[hd, tq:2 * tq] = jnp.where(lane >= dh, qs, zero)
    m_ref[...] = jnp.full(m_ref.shape, MASK_VALUE, F32)
    acc_ref[...] = jnp.zeros_like(acc_ref)

    def step(j, bias_idx):
        start = pl.multiple_of(j * tq, tq)
        s_all = [lax.dot_general(q2_ref[hd], k_ref[hd, pl.ds(start, tq), :], (((1,), (1,)), ((), ())),
                                 preferred_element_type=F32) for hd in range(n_heads)]
        for r0 in range(0, 2 * tq, rows):
            rs = slice(r0, r0 + rows)
            q0 = r0 % tq
            klen = min(q0 + rows, tq) if bias_idx == 0 else tq
            for hd in range(n_heads):
                vb = vaug_ref[hd, pl.ds(start, klen), :]
                s = s_all[hd][rs, 0:klen]
                if bias_idx is not None:
                    s = s + bias_ref[hd, bias_idx, q0:q0 + rows, 0:klen]
                m_old = m_ref[hd, rs]
                m_new = jnp.maximum(m_old, jnp.max(s, axis=1, keepdims=True))
                alpha = jnp.exp2(m_old - m_new)
                p = jnp.exp2(s - jnp.concatenate([m_new] * (klen // LANES), axis=1))
                pv = jnp.dot(p.astype(BF16), vb, preferred_element_type=F32)
                acc_ref[hd, rs] = jnp.concatenate([alpha, alpha], axis=1) * acc_ref[hd, rs] + pv
                m_ref[hd, rs] = m_new

    def far_step(j, carry):
        step(j, None)
        return carry

    lax.fori_loop(0, jnp.maximum(i - 1, 0), far_step, 0)

    @pl.when(i >= 1)
    def _():
        step(i - 1, 1)

    step(i, 0)

    lv = lam_ref[...]
    s1 = jnp.sum(lv[0:1] * lv[1:2], axis=1, keepdims=True)
    s2 = jnp.sum(lv[2:3] * lv[3:4], axis=1, keepdims=True)
    lam = jnp.exp(s1) - jnp.exp(s2) + lam_init
    for hd in range(n_heads):
        o1 = acc_ref[hd, 0:tq, 0:LANES] / acc_ref[hd, 0:tq, LANES:2 * LANES]
        o2 = acc_ref[hd, tq:2 * tq, 0:LANES] / acc_ref[hd, tq:2 * tq, LANES:2 * LANES]
        o = o1 - lam * o2
        ms = jnp.mean(o * o, axis=-1, keepdims=True)
        o_ref[:, hd * LANES:(hd + 1) * LANES] = (
            o * lax.rsqrt(ms + NORM_EPS) * g_ref[...] * (1.0 - lam_init)).astype(o_ref.dtype)


def _diff_attention(qkv, lam_vecs, bias_tiles, subln_g, *, batch, seq, tq, layer_idx):
    t = qkv.shape[1]
    nq = seq // tq
    rows = min(512, tq)
    hps = ATTN_HEADS_PER_STEP
    n_hp = DIFF_HEADS // hps
    lam_init = 0.8 - 0.6 * math.exp(-0.3 * layer_idx)
    kern = functools.partial(_attn_kernel, tq=tq, rows=rows, lam_init=lam_init)
    est = hps * (2 * tq * LANES * 2 + 4 * seq * LANES * 2 + 4 * tq * tq * 4 + 2 * tq * LANES * 2
                 + 2 * tq * LANES * 2 + seq * 2 * LANES * 2 + 2 * tq * LANES * 4 + 2 * tq * 2 * LANES * 4
                 + 2 * tq * tq * 4 + 8 * rows * tq * 4)
    return pl.pallas_call(
        kern,
        out_shape=jax.ShapeDtypeStruct((t, DIFF_HEADS * LANES), BF16),
        grid=(batch, n_hp, nq),
        in_specs=[pl.BlockSpec((SUBLANES, LANES), lambda b, h, i: (0, 0)),
                  pl.BlockSpec((hps, tq, LANES), lambda b, h, i: (h, b * nq + i, 0)),
                  pl.BlockSpec((hps, seq, LANES), lambda b, h, i: (n_hp + h, b, 0)),
                  pl.BlockSpec((hps, seq, LANES), lambda b, h, i: (2 * n_hp + h, b, 0)),
                  pl.BlockSpec((hps, 2, tq, tq), lambda b, h, i: (h, 0, 0, 0)),
                  pl.BlockSpec((1, LANES), lambda b, h, i: (0, 0))],
        out_specs=pl.BlockSpec((tq, hps * LANES), lambda b, h, i: (b * nq + i, h)),
        scratch_shapes=[pltpu.VMEM((hps, 2 * tq, LANES), BF16),
                        pltpu.VMEM((hps, seq, 2 * LANES), BF16),
                        pltpu.VMEM((hps, 2 * tq, LANES), F32),
                        pltpu.VMEM((hps, 2 * tq, 2 * LANES), F32)],
        compiler_params=_params(("parallel", "parallel", "arbitrary"), est),
        name="diff_attention",
    )(lam_vecs, qkv, qkv, qkv, bias_tiles, subln_g)


def _mix_kernel(x_ref, y_ref, o_ref, gate_ref, wssd_ref, wdiff_ref, wmix_ref, g_ref, b_ref, out_ref):
    d = x_ref.shape[1]
    ssd = jnp.dot(y_ref[...], wssd_ref[...], preferred_element_type=F32)
    dif = jnp.dot(o_ref[...], wdiff_ref[...], preferred_element_type=F32)
    merged = gate_ref[:, 0:d].astype(F32) * ssd + gate_ref[:, d:2 * d].astype(F32) * dif
    mixed = jnp.dot(merged.astype(BF16), wmix_ref[...], preferred_element_type=F32)
    out_ref[...] = _layer_norm(DN_ALPHA * x_ref[...] + mixed, g_ref[...], b_ref[...])


def _mix(x, y_ssd, o_diff, proj, w_ssd, w_diff, w_mix, ln_g, ln_b, *, tm, gate_block):
    t, d = x.shape
    tm = min(tm, t)
    inner = y_ssd.shape[1]
    dw = o_diff.shape[1]
    row = lambda i: (i, 0)
    est = (2 * tm * (d * 4 + inner * 2 + dw * 2 + 2 * d * 2 + d * 4)
           + (inner * d + dw * d + d * d) * 2 + 6 * tm * d * 4)
    return pl.pallas_call(
        _mix_kernel,
        out_shape=jax.ShapeDtypeStruct((t, d), F32),
        grid=(t // tm,),
        in_specs=[pl.BlockSpec((tm, d), row),
                  pl.BlockSpec((tm, inner), row),
                  pl.BlockSpec((tm, dw), row),
                  pl.BlockSpec((tm, 2 * d), lambda i: (i, gate_block)),
                  _resident((inner, d)), _resident((dw, d)), _resident((d, d)),
                  _resident((1, d)), _resident((1, d))],
        out_specs=pl.BlockSpec((tm, d), row),
        compiler_params=_params(("parallel",), est),
        name="mix_ln1",
    )(x, y_ssd, o_diff, proj, w_ssd, w_diff, w_mix, ln_g, ln_b)


def _pack_bf16_pairs(v):
    w = v.shape[1] // 2
    lo = lax.bitcast_convert_type(v[:, :w].astype(BF16).astype(F32), jnp.int32)
    hi = lax.bitcast_convert_type(v[:, w:].astype(BF16).astype(F32), jnp.int32)
    return jnp.bitwise_or(hi, lax.shift_right_logical(lo, 16))


def _unpack_bf16_pairs(words):
    lo = lax.bitcast_convert_type(lax.shift_left(words, 16), F32)
    hi = lax.bitcast_convert_type(jnp.bitwise_and(words, -65536), F32)
    return jnp.concatenate([lo, hi], axis=1)


def _xattn_kernel(x_ref, kv_ref, wq_ref, wo_ref, g_ref, b_ref, out_ref, packed_ref, o_scr, *, heads):
    d = x_ref.shape[1]
    dh = d // heads
    xv = x_ref[...]
    q = jnp.dot(xv.astype(BF16), wq_ref[...], preferred_element_type=F32)
    q = (q * (dh ** -0.5)).astype(BF16)
    for h in range(heads):
        kh = kv_ref[:, h * dh:(h + 1) * dh]
        vh = kv_ref[:, d + h * dh:d + (h + 1) * dh]
        s = lax.dot_general(q[:, h * dh:(h + 1) * dh], kh, (((1,), (1,)), ((), ())),
                            preferred_element_type=F32)
        p = jnp.exp(s - jnp.max(s, axis=1, keepdims=True))
        p = p / jnp.sum(p, axis=1, keepdims=True)
        o_scr[:, h * dh:(h + 1) * dh] = jnp.dot(p.astype(BF16), vh, preferred_element_type=F32).astype(BF16)
    att = jnp.dot(o_scr[...], wo_ref[...], preferred_element_type=F32)
    y = _layer_norm(DN_ALPHA * xv + att, g_ref[...], b_ref[...])
    out_ref[...] = y
    packed_ref[...] = _pack_bf16_pairs(y)


def _cross_attention(x, kv, w_cq, w_co, ln_g, ln_b, *, batch, seq, tq):
    t, d = x.shape
    mem_len = kv.shape[0] // batch
    tq = min(tq, seq)
    nq = seq // tq
    est = (2 * tq * d * 4 * 2 + 2 * tq * d * 2 + 2 * mem_len * 2 * d * 2 + 2 * d * d * 2 + tq * d * 2
           + 8 * tq * d * 4)
    row = lambda b, i: (b * nq + i, 0)
    return pl.pallas_call(
        functools.partial(_xattn_kernel, heads=MEM_HEADS),
        out_shape=(jax.ShapeDtypeStruct((t, d), F32), jax.ShapeDtypeStruct((t, d // 2), jnp.int32)),
        grid=(batch, nq),
        in_specs=[pl.BlockSpec((tq, d), row),
                  pl.BlockSpec((mem_len, 2 * d), lambda b, i: (b, 0)),
                  _resident((d, d)), _resident((d, d)), _resident((1, d)), _resident((1, d))],
        out_specs=(pl.BlockSpec((tq, d), row), pl.BlockSpec((tq, d // 2), row)),
        scratch_shapes=[pltpu.VMEM((tq, d), BF16)],
        compiler_params=_params(("parallel", "parallel"), est),
        name="cross_attention_ln2",
    )(x, kv, w_cq, w_co, ln_g, ln_b)


def _router_kernel(x_ref, wr_ref, rb_ref, idx_ref, rank_ref, gates_ref, cnt_ref, run_ref, *, n_experts):
    tm = x_ref.shape[0]
    n_groups = N_EXPERT_GROUPS
    per = n_experts // n_groups

    @pl.when(pl.program_id(0) == 0)
    def _():
        run_ref[...] = jnp.zeros_like(run_ref)

    xv = x_ref[...]
    x_hi = xv.astype(BF16)
    x_lo = (xv - x_hi.astype(F32)).astype(BF16)
    nt = (((1,), (1,)), ((), ()))
    logits = (lax.dot_general(wr_ref[0], x_hi, nt, preferred_element_type=F32)
              + lax.dot_general(wr_ref[0], x_lo, nt, preferred_element_type=F32)
              + lax.dot_general(wr_ref[1], x_hi, nt, preferred_element_type=F32))
    sc = _sigmoid(logits[0:n_experts]).reshape(per, n_groups, tm)
    choice = sc + rb_ref[0:n_experts].reshape(per, n_groups, 1)
    neg = -jnp.inf
    member = lax.broadcasted_iota(jnp.int32, (per, n_groups, tm), 0)
    group3 = lax.broadcasted_iota(jnp.int32, (per, n_groups, tm), 1)
    m1 = jnp.max(choice, axis=0, keepdims=True)
    i1 = jnp.min(jnp.where(choice == m1, member, per), axis=0, keepdims=True)
    m2 = jnp.max(jnp.where(member == i1, neg, choice), axis=0, keepdims=True)
    gscore = (m1 + m2)
    gsel = jnp.zeros((1, n_groups, tm), F32)
    gidx = lax.broadcasted_iota(jnp.int32, (1, n_groups, tm), 1)
    cur = gscore
    for _ in range(TOP_GROUPS):
        mx = jnp.max(cur, axis=1, keepdims=True)
        ix = jnp.min(jnp.where(cur == mx, gidx, n_groups), axis=1, keepdims=True)
        hit = gidx == ix
        gsel = jnp.where(hit, 1.0, gsel)
        cur = jnp.where(hit, neg, cur)
    cur = jnp.where(gsel > 0.0, choice, neg)
    eidx = group3 * per + member
    esel = jnp.zeros((per, n_groups, tm), F32)
    hits, idx_rows = [], []
    for _ in range(TOP_K):
        mx = jnp.max(jnp.max(cur, axis=0, keepdims=True), axis=1, keepdims=True)
        ix = jnp.min(jnp.min(jnp.where(cur == mx, eidx, n_experts), axis=0, keepdims=True),
                     axis=1, keepdims=True)
        hit = eidx == ix
        esel = jnp.where(hit, 1.0, esel)
        cur = jnp.where(hit, neg, cur)
        hits.append(hit)
        idx_rows.append(ix.reshape(1, tm))
    w = esel * sc
    tot = jnp.sum(jnp.sum(w, axis=0, keepdims=True), axis=1, keepdims=True)
    gw = w / tot * ROUTED_SCALE

    r_i = lax.broadcasted_iota(jnp.int32, (tm, tm), 0)
    c_i = lax.broadcasted_iota(jnp.int32, (tm, tm), 1)
    upper = jnp.where(r_i <= c_i, 1.0, 0.0).astype(BF16)
    pref = jnp.dot(esel.reshape(n_experts, tm).astype(BF16), upper, preferred_element_type=F32)
    run = run_ref[...]
    rank3 = (jnp.concatenate([run] * (tm // LANES), axis=1) + pref - 1.0).reshape(per, n_groups, tm)

    def pick(hit, vals):
        return jnp.sum(jnp.sum(jnp.where(hit, vals, 0.0), axis=0, keepdims=True), axis=1).reshape(1, tm)

    idx_ref[...] = jnp.concatenate(idx_rows, axis=0)
    rank_ref[...] = jnp.concatenate([pick(h, rank3) for h in hits], axis=0).astype(jnp.int32)
    wk = jnp.concatenate([pick(h, gw) for h in hits] + [jnp.zeros((LANES - TOP_K, tm), F32)], axis=0)
    gates_ref[...] = wk.T
    run = run + jnp.broadcast_to(pref[:, tm - 1:tm], run.shape)
    run_ref[...] = run
    cnt_ref[...] = run


def _router(x, w_router, router_bias, *, tm, row0, t):
    d = x.shape[1]
    n_experts = w_router.shape[1]
    per = n_experts // N_EXPERT_GROUPS
    tm = min(tm, t)
    blk0 = row0 // tm

    def member_major(v):
        return v.reshape(N_EXPERT_GROUPS, per, *v.shape[1:]).swapaxes(0, 1).reshape(v.shape)

    wt = jnp.pad(member_major(w_router.astype(F32).T), ((0, LANES - n_experts), (0, 0)))
    w_hi = wt.astype(BF16)
    wr = jnp.stack([w_hi, (wt - w_hi.astype(F32)).astype(BF16)])
    rb = jnp.pad(member_major(router_bias.astype(F32)), (0, LANES - n_experts)).reshape(LANES, 1)
    est = (2 * tm * d * 4 + 2 * LANES * d * 2 + 2 * tm * LANES * 4 + 60 * n_experts * tm * 4 + 3 * tm * tm * 4)
    pick_spec = pl.BlockSpec((TOP_K, tm), lambda i: (0, i))
    idx_t, rank_t, gates, counts = pl.pallas_call(
        functools.partial(_router_kernel, n_experts=n_experts),
        out_shape=(jax.ShapeDtypeStruct((TOP_K, t), jnp.int32),
                   jax.ShapeDtypeStruct((TOP_K, t), jnp.int32),
                   jax.ShapeDtypeStruct((t, LANES), F32),
                   jax.ShapeDtypeStruct((n_experts, LANES), F32)),
        grid=(t // tm,),
        in_specs=[pl.BlockSpec((tm, d), lambda i: (blk0 + i, 0)),
                  _resident((2, LANES, d)), _resident((LANES, 1))],
        out_specs=(pick_spec, pick_spec,
                   pl.BlockSpec((tm, LANES), lambda i: (i, 0)),
                   pl.BlockSpec((n_experts, LANES), lambda i: (0, 0))),
        scratch_shapes=[pltpu.VMEM((n_experts, LANES), F32)],
        compiler_params=_params(("arbitrary",), est),
        name="router",
    )(x, wr, rb)
    counts = counts[:, 0].astype(jnp.int32).reshape(per, N_EXPERT_GROUPS).T.reshape(n_experts)
    return idx_t, rank_t, gates, counts


def _positions_kernel(cnt_ref, idx_ref, rank_ref, pos_ref, texp_ref, off_ref, *, n_experts, rows, n_tiles):
    @pl.when(pl.program_id(0) == 0)
    def _():
        def per_expert(e, toff):
            off_ref[e] = toff * rows
            nt = lax.shift_right_logical(cnt_ref[e] + (rows - 1), int(math.log2(rows)))

            def fill(j, c):
                texp_ref[toff + j] = e
                return c

            lax.fori_loop(0, nt, fill, 0)
            return toff + nt

        n_used = lax.fori_loop(0, n_experts, per_expert, 0)

        def tail(j, c):
            texp_ref[j] = n_experts - 1
            return c

        lax.fori_loop(n_used, n_tiles, tail, 0)
        texp_ref[n_tiles] = n_used

    idx = idx_ref[...]
    pos = rank_ref[...]
    for e in range(n_experts):
        pos = pos + jnp.where(idx == e, off_ref[e], 0)
    pos_ref[...] = pos


def _positions(counts, idx_t, rank_t, *, rows, n_tiles, tm):
    n_experts = counts.shape[0]
    k, t = idx_t.shape
    tm = min(tm, t)
    spec = pl.BlockSpec((k, tm), lambda i: (0, i))
    kern = functools.partial(_positions_kernel, n_experts=n_experts, rows=rows, n_tiles=n_tiles)
    return pl.pallas_call(
        kern,
        out_shape=(jax.ShapeDtypeStruct((k, t), jnp.int32),
                   jax.ShapeDtypeStruct((n_tiles + 1,), jnp.int32)),
        grid=(t // tm,),
        in_specs=[pl.BlockSpec(memory_space=pltpu.SMEM), spec, spec],
        out_specs=(spec, pl.BlockSpec(memory_space=pltpu.SMEM)),
        scratch_shapes=[pltpu.SMEM((n_experts,), jnp.int32)],
        compiler_params=_params(("arbitrary",), 16 * k * tm * 4),
        name="moe_positions",
    )(counts, idx_t, rank_t)


def _sc_mesh():
    return plsc.VectorSubcoreMesh(core_axis_name="c", subcore_axis_name="s",
                                  num_cores=SC_CORES, num_subcores=SC_SUBCORES)


def _sc_dispatch(packed, pos_chunks, n_rows, row0):
    w = packed.shape[1]
    n_chunks, k, n = pos_chunks.shape
    per_worker = n_chunks // (SC_CORES * SC_SUBCORES)

    @functools.partial(
        pl.kernel, mesh=_sc_mesh(),
        out_type=jax.ShapeDtypeStruct((n_rows, w), packed.dtype),
        scratch_types=[pltpu.VMEM((k, n), jnp.int32), pltpu.VMEM((n, w), packed.dtype),
                       pltpu.SemaphoreType.DMA],
        name="moe_dispatch_sc",
    )
    def scatter_rows(x_hbm, pos_hbm, out_hbm, idx_v, rows_v, sem):
        wid = lax.axis_index("s") * SC_CORES + lax.axis_index("c")

        @pl.loop(0, per_worker)
        def _(step):
            c = wid * per_worker + step
            pltpu.sync_copy(pos_hbm.at[c], idx_v)
            pltpu.sync_copy(x_hbm.at[pl.ds(row0 + c * n, n)], rows_v)
            copies = [pltpu.async_copy(rows_v, out_hbm.at[idx_v.at[kk]], sem) for kk in range(k)]
            for cp in copies:
                cp.wait()

    return scatter_rows(packed, pos_chunks)


def _sc_combine(sorted_rows, pos_chunks):
    _, w = sorted_rows.shape
    n_chunks, k, n = pos_chunks.shape
    per_worker = n_chunks // (SC_CORES * SC_SUBCORES)

    @functools.partial(
        pl.kernel, mesh=_sc_mesh(),
        out_type=jax.ShapeDtypeStruct((k, n_chunks * n, w), sorted_rows.dtype),
        scratch_types=[pltpu.VMEM((k, n), jnp.int32), pltpu.VMEM((n, w), sorted_rows.dtype),
                       pltpu.SemaphoreType.DMA],
        name="moe_combine_sc",
    )
    def gather_rows(y_hbm, pos_hbm, out_hbm, idx_v, rows_v, sem):
        wid = lax.axis_index("s") * SC_CORES + lax.axis_index("c")

        @pl.loop(0, per_worker)
        def _(step):
            c = wid * per_worker + step
            pltpu.sync_copy(pos_hbm.at[c], idx_v)
            for kk in range(k):
                pltpu.async_copy(y_hbm.at[idx_v.at[kk]], rows_v, sem).wait()
                pltpu.sync_copy(rows_v, out_hbm.at[kk, pl.ds(c * n, n)])

    return gather_rows(sorted_rows, pos_chunks)


def _expert_kernel(texp_ref, xs_ref, wgu_ref, wdn_ref, ys_ref, *, n_tiles):
    ff = wdn_ref.shape[1]

    @pl.when(pl.program_id(0) < texp_ref[n_tiles])
    def _():
        xv = _unpack_bf16_pairs(xs_ref[...]).astype(BF16)
        gu = jnp.dot(xv, wgu_ref[0], preferred_element_type=F32)
        hid = _silu(gu[:, 0:ff]) * gu[:, ff:2 * ff]
        ys_ref[...] = _pack_bf16_pairs(jnp.dot(hid.astype(BF16), wdn_ref[0], preferred_element_type=F32))


def _experts(tile_expert, sorted_rows, w_gu, w_dn, *, rows):
    n_rows, w = sorted_rows.shape
    n_tiles = n_rows // rows
    _, d, ff2 = w_gu.shape
    ff = ff2 // 2
    tile = lambda j, te: (jnp.minimum(j, te[n_tiles] - 1), 0)
    est = 4 * rows * w * 4 + 2 * (d * ff2 + ff * d) * 2 + 4 * rows * d * 4 + 4 * rows * ff2 * 4
    return pl.pallas_call(
        functools.partial(_expert_kernel, n_tiles=n_tiles),
        out_shape=jax.ShapeDtypeStruct((n_rows, w), sorted_rows.dtype),
        grid_spec=pltpu.PrefetchScalarGridSpec(
            num_scalar_prefetch=1,
            grid=(n_tiles,),
            in_specs=[pl.BlockSpec((rows, w), tile),
                      pl.BlockSpec((1, d, ff2), lambda j, te: (te[j], 0, 0)),
                      pl.BlockSpec((1, ff, d), lambda j, te: (te[j], 0, 0))],
            out_specs=pl.BlockSpec((rows, w), tile)),
        compiler_params=_params(("arbitrary",), est),
        name="moe_experts",
    )(tile_expert, sorted_rows, w_gu, w_dn)


def _moe_out_kernel(x_ref, yk_ref, gates_ref, wsgu_ref, wsdn_ref, g_ref, b_ref, *rest):
    out_ref = rest[-1]
    ff = wsdn_ref.shape[0]
    xv = x_ref[...]
    gu = jnp.dot(xv.astype(BF16), wsgu_ref[...], preferred_element_type=F32)
    hid = _silu(gu[:, 0:ff]) * gu[:, ff:2 * ff]
    acc = jnp.dot(hid.astype(BF16), wsdn_ref[...], preferred_element_type=F32)
    for k in range(yk_ref.shape[0]):
        acc = acc + gates_ref[:, k:k + 1] * _unpack_bf16_pairs(yk_ref[k])
    out_ref[...] = _layer_norm(DN_ALPHA * xv + acc, g_ref[...], b_ref[...])


def _moe_out(x, yk, gates, w_sgu, w_sdn, ln_g, ln_b, *, tm, row0, prev):
    t_all, d = x.shape
    k, t, w = yk.shape
    sff2 = w_sgu.shape[1]
    tm = min(tm, t)
    blk0 = row0 // tm
    est = (2 * tm * d * 4 * 2 + 2 * k * tm * w * 4 + 2 * tm * LANES * 4
           + (d * sff2 + (sff2 // 2) * d) * 2 + 6 * tm * d * 4)
    in_specs = [pl.BlockSpec((tm, d), lambda i: (blk0 + i, 0)),
                pl.BlockSpec((k, tm, w), lambda i: (0, i, 0)),
                pl.BlockSpec((tm, LANES), lambda i: (i, 0)),
                _resident((d, sff2)), _resident((sff2 // 2, d)),
                _resident((1, d)), _resident((1, d))]
    args = [x, yk, gates, w_sgu, w_sdn, ln_g, ln_b]
    aliases = {}
    if prev is not None:
        in_specs.append(pl.BlockSpec(memory_space=pl.ANY))
        args.append(prev)
        aliases = {len(args) - 1: 0}
    return pl.pallas_call(
        _moe_out_kernel,
        out_shape=jax.ShapeDtypeStruct((t_all, d), F32),
        grid=(t // tm,),
        in_specs=in_specs,
        out_specs=pl.BlockSpec((tm, d), lambda i: (blk0 + i, 0)),
        input_output_aliases=aliases,
        compiler_params=_params(("parallel",), est),
        name="moe_out_ln3",
    )(*args)


def _moe(x, packed, w_router, router_bias, w_gu, w_dn, w_sgu, w_sdn, ln_g, ln_b):
    t_all, d = x.shape
    n_experts = w_router.shape[1]
    n = SC_CHUNK_TOKENS
    splits = MOE_SPLITS if t_all % (MOE_SPLITS * n * SC_CORES * SC_SUBCORES) == 0 else 1
    t = t_all // splits
    rows = min(MOE_ROWS_PER_TILE, t)
    n_tiles = (t * TOP_K) // rows + n_experts
    parts = []
    for s in range(splits):
        row0 = s * t
        idx_t, rank_t, gates, counts = _router(x, w_router, router_bias, tm=512, row0=row0, t=t)
        pos_t, tile_expert = _positions(counts, idx_t, rank_t, rows=rows, n_tiles=n_tiles, tm=2048)
        pos_chunks = pos_t.reshape(TOP_K, t // n, n).transpose(1, 0, 2)
        sorted_x = _sc_dispatch(packed, pos_chunks, n_tiles * rows, row0)
        parts.append((row0, gates, tile_expert, pos_chunks, sorted_x))
    out = None
    for row0, gates, tile_expert, pos_chunks, sorted_x in parts:
        sorted_y = _experts(tile_expert, sorted_x, w_gu, w_dn, rows=rows)
        yk = _sc_combine(sorted_y, pos_chunks)
        out = _moe_out(x, yk, gates, w_sgu, w_sdn, ln_g, ln_b, tm=512, row0=row0, prev=out)
    return out


def kernel(x, mem, w_in, conv_w, conv_b, dt_bias, a_log, d_skip, ssd_norm_g, lambda_q1, lambda_k1, lambda_q2, lambda_k2, subln_g, rel_bias, w_ssd_br, w_diff_br, w_mix_out, ln1_g, ln1_b, w_cq, w_ckv, w_co, ln2_g, ln2_b, w_router, router_bias, w_exp_gu, w_exp_down, w_sh_gu, w_sh_down, ln3_g, ln3_b):
    batch, seq, d = x.shape
    depth = w_in.shape[0]
    inner = w_ssd_br.shape[1]
    xbc_width = conv_w.shape[2]
    heads = dt_bias.shape[1]
    diff_width = w_diff_br.shape[1]
    o_z, o_xbc = inner, inner + xbc_width
    o_dt = o_xbc + heads
    o_v = o_dt + 3 * diff_width
    t = batch * seq
    tq_attn = min(512, seq)

    def vec(v):
        return v.astype(F32).reshape(1, -1)

    xt = x.reshape(t, d)
    memt = mem.reshape(-1, d)
    bias_tiles = _bias_tiles(rel_bias, tq_attn)

    for l in range(depth):
        wl = w_in[l]
        w_big = jnp.concatenate([wl[:, o_z:o_xbc], wl[:, o_dt:o_v], wl[:, :o_z], wl[:, o_v:]],
                                axis=1).astype(BF16)
        assert 2 * d == inner
        w_dt = jnp.pad(wl[:, o_xbc:o_dt], ((0, 0), (0, LANES - heads))).astype(BF16)
        xbc, qkv, zg, dt_raw = _in_proj(xt, w_big, w_dt, n_xbc_cols=xbc_width, n_qkv_cols=3 * diff_width,
                                        n_z_cols=inner, tm=1024, tn=1024)

        y_ssd = _ssd_branch(zg, xbc, dt_raw, conv_w[l], conv_b[l], dt_bias[l], a_log[l], d_skip[l],
                            ssd_norm_g[l], batch=batch, seq=seq)

        lam_vecs = jnp.pad(jnp.stack([lambda_q1[l], lambda_k1[l], lambda_q2[l], lambda_k2[l]]).astype(F32),
                           ((0, SUBLANES - 4), (0, LANES - lambda_q1.shape[1])))
        o_diff = _diff_attention(qkv, lam_vecs, bias_tiles, vec(subln_g[l]), batch=batch, seq=seq,
                                 tq=tq_attn, layer_idx=l)

        xt = _mix(xt, y_ssd, o_diff, zg, w_ssd_br[l].astype(BF16), w_diff_br[l].astype(BF16),
                  w_mix_out[l].astype(BF16), vec(ln1_g[l]), vec(ln1_b[l]), tm=512, gate_block=1)

        kv = _matmul(memt, w_ckv[l].astype(BF16), BF16, tm=1024, tn=1024, name="mem_kv_proj")
        xt, packed = _cross_attention(xt, kv, w_cq[l].astype(BF16), w_co[l].astype(BF16), vec(ln2_g[l]),
                                      vec(ln2_b[l]), batch=batch, seq=seq, tq=1024)

        xt = _moe(xt, packed, w_router[l], router_bias[l], w_exp_gu[l].astype(BF16),
                  w_exp_down[l].astype(BF16), w_sh_gu[l].astype(BF16), w_sh_down[l].astype(BF16),
                  vec(ln3_g[l]), vec(ln3_b[l]))
    return xt.reshape(batch, seq, d)
```

```python
import functools
import math

import numpy as np
import jax
import jax.numpy as jnp
from jax import lax
from jax.experimental import pallas as pl
from jax.experimental.pallas import tpu as pltpu
from jax.experimental.pallas import tpu_sc as plsc

F32 = jnp.float32
BF16 = jnp.bfloat16

SSD_HEAD_DIM = 64
SSD_GROUPS = 4
SSD_STATE = 128
SSD_CONV = 4
SSD_CHUNK = 128
DIFF_HEADS = 8
REL_BUCKETS = 32
REL_MAX_DIST = 128
MEM_HEADS = 4
TOP_K = 8
N_EXPERT_GROUPS = 8
TOP_GROUPS = 4
ROUTED_SCALE = 2.5
NORM_EPS = 1e-5
DEPTH = 1
DN_ALPHA = (2.0 * DEPTH) ** 0.25

LANES = 128
SUBLANES = 8
VMEM_CAP_BYTES = 64 * 1024 * 1024
MASK_VALUE = -1e30
LOG2E = math.log2(math.e)
PROJ_ROWS, PROJ_COLS = 1024, 1024
IN_PROJ_SLAB = 256
SSD_CHUNKS_PER_STEP = 2
ATTN_TILE = 512
ATTN_PV_ROWS = 512
ATTN_HEADS_PER_STEP = 2
MIX_ROWS = 512
XATTN_ROWS = 1024
ROUTER_ROWS = 512
POSITIONS_TOKENS = 2048
MOE_OUT_ROWS = 512
SC_CORES = 2
SC_SUBCORES = 16
SC_CHUNK_TOKENS = 128
MOE_SPLITS = 2
MOE_ROWS_PER_TILE = 1024


def _vmem_limit(estimate_bytes):
    return int(min(estimate_bytes * 5 // 4 + (4 << 20), VMEM_CAP_BYTES - (6 << 20)))


def _params(semantics, vmem_estimate):
    return pltpu.CompilerParams(dimension_semantics=semantics,
                                vmem_limit_bytes=_vmem_limit(vmem_estimate))


def _resident(shape):
    nd = len(shape)
    return pl.BlockSpec(shape, lambda *_: (0,) * nd, pipeline_mode=pl.Buffered(1))


def _layer_norm(v, g, b):
    mu = jnp.mean(v, axis=-1, keepdims=True)
    d = v - mu
    var = jnp.mean(d * d, axis=-1, keepdims=True)
    return d * lax.rsqrt(var + NORM_EPS) * g + b


def _sigmoid(v):
    return 0.5 * jnp.tanh(0.5 * v) + 0.5


def _silu(v):
    h = 0.5 * v
    return h + h * jnp.tanh(h)


def _matmul_kernel(x_ref, w_ref, o_ref, xb_ref):
    @pl.when(pl.program_id(1) == 0)
    def _():
        xb_ref[...] = x_ref[...].astype(BF16)

    o_ref[...] = jnp.dot(xb_ref[...], w_ref[...], preferred_element_type=F32).astype(o_ref.dtype)


def _matmul(x, w, out_dtype, *, tm, tn, name):
    m, k = x.shape
    n = w.shape[1]
    tm, tn = min(tm, m), min(tn, n)
    est = (2 * tm * k * x.dtype.itemsize + tm * k * 2 + 2 * k * tn * 2
           + 2 * tm * tn * jnp.dtype(out_dtype).itemsize + tm * tn * 4)
    return pl.pallas_call(
        _matmul_kernel,
        out_shape=jax.ShapeDtypeStruct((m, n), out_dtype),
        grid=(m // tm, n // tn),
        in_specs=[pl.BlockSpec((tm, k), lambda i, j: (i, 0)),
                  pl.BlockSpec((k, tn), lambda i, j: (0, j))],
        out_specs=pl.BlockSpec((tm, tn), lambda i, j: (i, j)),
        scratch_shapes=[pltpu.VMEM((tm, k), BF16)],
        compiler_params=_params(("parallel", "arbitrary"), est),
        name=name,
    )(x, w)


def _in_proj_kernel(x_ref, w_ref, wdt_ref, xbc_ref, qkv_ref, zg_ref, dt_ref, xb_ref, *, n_xbc, n_qkv, n_z):
    j = pl.program_id(1)
    j_z = n_xbc + n_qkv

    @pl.when(j == 0)
    def _():
        xb = x_ref[...].astype(BF16)
        xb_ref[...] = xb
        dt_ref[...] = jnp.dot(xb, wdt_ref[...], preferred_element_type=F32)

    tn = w_ref.shape[1]
    slabs = [slice(c0, c0 + IN_PROJ_SLAB) for c0 in range(0, tn, IN_PROJ_SLAB)]

    def product(cs):
        return jnp.dot(xb_ref[...], w_ref[:, cs], preferred_element_type=F32)

    @pl.when(j < n_xbc)
    def _():
        for cs in slabs:
            xbc_ref[:, cs] = product(cs).astype(BF16)

    @pl.when(jnp.logical_and(j >= n_xbc, j < j_z))
    def _():
        for cs in slabs:
            acc = product(cs)
            for hh in range(IN_PROJ_SLAB // LANES):
                qkv_ref[cs.start // LANES + hh] = acc[:, hh * LANES:(hh + 1) * LANES].astype(BF16)

    @pl.when(jnp.logical_and(j >= j_z, j < j_z + n_z))
    def _():
        for cs in slabs:
            zg_ref[:, cs] = product(cs).astype(BF16)

    @pl.when(j >= j_z + n_z)
    def _():
        for cs in slabs:
            zg_ref[:, cs] = _sigmoid(product(cs)).astype(BF16)


def _in_proj(x, w, w_dt, *, n_xbc_cols, n_qkv_cols, n_z_cols, tm, tn):
    m, k = x.shape
    n = w.shape[1]
    tm = min(tm, m)
    n_xbc, n_qkv, n_z = n_xbc_cols // tn, n_qkv_cols // tn, n_z_cols // tn
    n_zg = n // tn - n_xbc - n_qkv
    per = tn // LANES
    est = (2 * tm * k * 4 + tm * k * 2 + 2 * k * tn * 2 + 3 * 2 * tm * tn * 2 + 2 * tm * tn * 4
           + k * LANES * 2 + 2 * tm * LANES * 4)
    kern = functools.partial(_in_proj_kernel, n_xbc=n_xbc, n_qkv=n_qkv, n_z=n_z)
    return pl.pallas_call(
        kern,
        out_shape=(jax.ShapeDtypeStruct((m, n_xbc_cols), BF16),
                   jax.ShapeDtypeStruct((n_qkv_cols // LANES, m, LANES), BF16),
                   jax.ShapeDtypeStruct((m, n_zg * tn), BF16),
                   jax.ShapeDtypeStruct((m, LANES), F32)),
        grid=(m // tm, n // tn),
        in_specs=[pl.BlockSpec((tm, k), lambda i, j: (i, 0)),
                  pl.BlockSpec((k, tn), lambda i, j: (0, j)),
                  _resident((k, LANES))],
        out_specs=(pl.BlockSpec((tm, tn), lambda i, j: (i, jnp.minimum(j, n_xbc - 1))),
                   pl.BlockSpec((per, tm, LANES), lambda i, j: (jnp.clip(j - n_xbc, 0, n_qkv - 1), i, 0)),
                   pl.BlockSpec((tm, tn), lambda i, j: (i, jnp.clip(j - n_xbc - n_qkv, 0, n_zg - 1))),
                   pl.BlockSpec((tm, LANES), lambda i, j: (i, 0))),
        scratch_shapes=[pltpu.VMEM((tm, k), BF16)],
        compiler_params=_params(("parallel", "arbitrary"), est),
        name="in_proj",
    )(x, w, w_dt)


def _split3(v):
    hi = v.astype(BF16)
    r1 = v - hi.astype(F32)
    mid = r1.astype(BF16)
    lo = (r1 - mid.astype(F32)).astype(BF16)
    return hi, mid, lo


def _ssd_kernel(z_ref, xbc_ref, dt_ref, convw_ref, convb_ref, dtb_ref, alog_ref,
                dskip_ref, g_ref, y_ref, tail_ref, state_ref, xa_ref, yacc_ref, **static):
    @pl.when(pl.program_id(1) == 0)
    def _():
        tail_ref[...] = jnp.zeros_like(tail_ref)
        state_ref[...] = jnp.zeros_like(state_ref)

    for r0 in range(0, z_ref.shape[0], SSD_CHUNK):
        rows = pl.ds(r0, SSD_CHUNK)
        _ssd_chunk(z_ref.at[rows], xbc_ref.at[rows], dt_ref.at[rows], convw_ref, convb_ref, dtb_ref,
                   alog_ref, dskip_ref, g_ref, y_ref.at[rows], tail_ref, state_ref, xa_ref, yacc_ref,
                   **static)


def _ssd_chunk(z_ref, xbc_ref, dt_ref, convw_ref, convb_ref, dtb_ref, alog_ref,
               dskip_ref, g_ref, y_ref, tail_ref, state_ref, xa_ref, yacc_ref,
               *, inner, n_groups, d_state):
    L = SSD_CHUNK
    width = xbc_ref.shape[1]
    n_pairs = inner // LANES
    pairs_per_group = n_pairs // n_groups
    group_width = inner // n_groups

    slab = 512
    n_t = L // SUBLANES
    sub = lax.broadcasted_iota(jnp.int32, (n_t, SUBLANES, slab), 1)
    for c0 in range(0, width, slab):
        cs = slice(c0, c0 + slab)
        u = xbc_ref[:, cs].astype(F32)
        tiles = jnp.concatenate([tail_ref[:, cs], u], axis=0).reshape(n_t + 1, SUBLANES, slab)
        acc = convb_ref[:, cs] + u * convw_ref[SSD_CONV - 1:SSD_CONV, cs]
        for j in range(1, SSD_CONV):
            rot = pltpu.roll(tiles, j, axis=1)
            shifted = jnp.where(sub < j, rot[0:n_t], rot[1:n_t + 1]).reshape(L, slab)
            acc = acc + shifted * convw_ref[SSD_CONV - 1 - j:SSD_CONV - j, cs]
        tail_ref[:, cs] = u[L - SUBLANES:L]
        xa_ref[:, cs] = _silu(acc)

    dtr = dt_ref[...] + dtb_ref[...]
    dt = jnp.maximum(dtr, 0.0) + jnp.log(1.0 + jnp.exp(-jnp.abs(dtr)))
    a = -jnp.exp(alog_ref[...]) * dt
    row_i = lax.broadcasted_iota(jnp.int32, (L, L), 0)
    col_i = lax.broadcasted_iota(jnp.int32, (L, L), 1)
    causal = row_i >= col_i
    tril = jnp.where(causal, 1.0, 0.0).astype(BF16)
    acs = sum(jnp.dot(tril, part, preferred_element_type=F32) for part in _split3(a)) * LOG2E
    acs_t = acs.T
    dt_t = dt.T
    lane_lo = lax.broadcasted_iota(jnp.int32, (L, LANES), 1) < SSD_HEAD_DIM

    b0 = inner
    c0 = inner + n_groups * d_state
    for g in range(n_groups):
        bg = xa_ref[:, b0 + g * d_state:b0 + (g + 1) * d_state]
        cg = xa_ref[:, c0 + g * d_state:c0 + (g + 1) * d_state]
        cb = lax.dot_general(cg.astype(BF16), bg.astype(BF16), (((1,), (1,)), ((), ())),
                             preferred_element_type=F32)
        bg_t = bg.T
        for pp in range(g * pairs_per_group, (g + 1) * pairs_per_group):
            xs_pair = xa_ref[:, pp * LANES:(pp + 1) * LANES].astype(BF16)
            st_old = state_ref[pp]
            rhs = jnp.concatenate([xs_pair, st_old.astype(BF16)], axis=0)
            lhs, b_ws, keep = [], [], []
            for side in range(2):
                h = 2 * pp + side
                col = jnp.broadcast_to(acs[:, h:h + 1], (L, L))
                row = acs_t[h:h + 1, :]
                dt_row = dt_t[h:h + 1, :]
                last = acs[L - 1:L, h:h + 1]
                dec = jnp.exp2(jnp.where(causal, col - row, MASK_VALUE))
                m_in = (cb * dec * dt_row).astype(BF16)
                c_w = (cg * jnp.exp2(col[:, :d_state])).astype(BF16)
                lhs.append(jnp.concatenate([m_in, c_w], axis=1))
                b_ws.append((bg_t * (jnp.exp2(last - row) * dt_row)).astype(BF16))
                keep.append(jnp.exp2(last))
            y2 = jnp.dot(jnp.concatenate(lhs, axis=0), rhs, preferred_element_type=F32)
            s2 = jnp.dot(jnp.concatenate(b_ws, axis=0), xs_pair, preferred_element_type=F32)
            yacc_ref[:, pp * LANES:(pp + 1) * LANES] = jnp.where(lane_lo, y2[0:L], y2[L:2 * L])
            state_ref[pp] = jnp.where(lane_lo, st_old * keep[0] + s2[0:d_state],
                                      st_old * keep[1] + s2[d_state:2 * d_state])

    for g in range(n_groups):
        cs = slice(g * group_width, (g + 1) * group_width)
        zz = z_ref[:, cs].astype(F32)
        yv = (yacc_ref[:, cs] + dskip_ref[:, cs] * xa_ref[:, cs]) * _silu(zz)
        ms = jnp.mean(yv * yv, axis=-1, keepdims=True)
        y_ref[:, cs] = (yv * lax.rsqrt(ms + NORM_EPS) * g_ref[:, cs]).astype(y_ref.dtype)


def _ssd_branch(zg, xbc, dt_raw, conv_w, conv_b, dt_bias, a_log, d_skip, norm_g, *, batch, seq):
    t = zg.shape[0]
    inner = norm_g.shape[0]
    width = conv_w.shape[1]
    heads = inner // SSD_HEAD_DIM
    n_chunks = seq // SSD_CHUNK
    L = SSD_CHUNK

    def pad_heads(v):
        return jnp.pad(v.astype(F32), (0, LANES - heads)).reshape(1, LANES)

    convw = jnp.pad(conv_w.astype(F32), ((0, SUBLANES - SSD_CONV), (0, 0)))
    dskip = jnp.repeat(d_skip.astype(F32), SSD_HEAD_DIM).reshape(1, inner)
    cps = SSD_CHUNKS_PER_STEP if n_chunks % SSD_CHUNKS_PER_STEP == 0 else 1
    n_steps = n_chunks // cps
    rows = cps * L
    row = lambda b, c: (b * n_steps + c, 0)
    const = lambda b, c: (0, 0)
    est = (2 * rows * (inner + width) * 2 + 2 * rows * LANES * 4 + 2 * rows * inner * 2
           + (inner // LANES) * SSD_STATE * LANES * 4 + L * (2 * width + inner) * 4 + (8 << 20))
    kern = functools.partial(_ssd_kernel, inner=inner, n_groups=SSD_GROUPS, d_state=SSD_STATE)
    return pl.pallas_call(
        kern,
        out_shape=jax.ShapeDtypeStruct((t, inner), BF16),
        grid=(batch, n_steps),
        in_specs=[pl.BlockSpec((rows, inner), row),
                  pl.BlockSpec((rows, width), row),
                  pl.BlockSpec((rows, LANES), row),
                  pl.BlockSpec((SUBLANES, width), const),
                  pl.BlockSpec((1, width), const),
                  pl.BlockSpec((1, LANES), const),
                  pl.BlockSpec((1, LANES), const),
                  pl.BlockSpec((1, inner), const),
                  pl.BlockSpec((1, inner), const)],
        out_specs=pl.BlockSpec((rows, inner), row),
        scratch_shapes=[pltpu.VMEM((SUBLANES, width), F32),
                        pltpu.VMEM((inner // LANES, SSD_STATE, LANES), F32),
                        pltpu.VMEM((L, width), F32),
                        pltpu.VMEM((L, inner), F32)],
        compiler_params=_params(("parallel", "arbitrary"), est),
        name="ssd_scan",
    )(zg, xbc, dt_raw, convw, conv_b.astype(F32).reshape(1, width), pad_heads(dt_bias),
      pad_heads(a_log), dskip, norm_g.astype(F32).reshape(1, inner))


def _bucket_tiles(tq):
    max_exact = REL_BUCKETS // 2
    qi = np.arange(tq)[:, None]
    ki = np.arange(tq)[None, :]

    def bucket(dist):
        d = np.maximum(dist, 1).astype(np.float32)
        large = max_exact + (np.log(d / np.float32(max_exact)) / np.float32(math.log(REL_MAX_DIST / max_exact))
                             * np.float32(REL_BUCKETS - max_exact)).astype(np.int32)
        large = np.minimum(large, REL_BUCKETS - 1)
        return np.where(dist < max_exact, dist, large).astype(np.int32)

    diag = np.where(qi >= ki, bucket(np.maximum(qi - ki, 0)), -1)
    prev = bucket(tq + qi - ki)
    far = bucket(np.arange(tq + 1, 1 << 16))
    assert (far == REL_BUCKETS - 1).all()
    return np.stack([diag, prev]).astype(np.int32)


def _bias_kernel(rb_ref, bucket_ref, o_ref):
    h = pl.program_id(0)
    bk = bucket_ref[...]
    acc = jnp.zeros(bk.shape, F32)
    for b in range(REL_BUCKETS):
        acc = jnp.where(bk == b, rb_ref[b, h], acc)
    o_ref[0] = jnp.where(bk < 0, MASK_VALUE, (acc - rb_ref[REL_BUCKETS - 1, h]) * LOG2E)


def _bias_tiles(rel_bias, tq):
    buckets = jnp.asarray(_bucket_tiles(tq))
    return pl.pallas_call(
        _bias_kernel,
        out_shape=jax.ShapeDtypeStruct((DIFF_HEADS, 2, tq, tq), F32),
        grid=(DIFF_HEADS,),
        in_specs=[pl.BlockSpec(memory_space=pltpu.SMEM),
                  pl.BlockSpec((2, tq, tq), lambda h: (0, 0, 0))],
        out_specs=pl.BlockSpec((1, 2, tq, tq), lambda h: (h, 0, 0, 0)),
        compiler_params=_params(("arbitrary",), 8 * tq * tq * 4),
        name="t5_bias_tiles",
    )(rel_bias.astype(F32), buckets)


def _attn_kernel(lam_ref, q_ref, k_ref, v_ref, bias_ref, g_ref, o_ref,
                 q2_ref, vaug_ref, m_ref, acc_ref, *, tq, rows, lam_init):
    i = pl.program_id(2)
    dh = LANES // 2
    n_heads = q_ref.shape[0]

    @pl.when(i == 0)
    def _():
        for hd in range(n_heads):
            vaug_ref[hd, :, 0:LANES] = v_ref[hd]
            vaug_ref[hd, :, LANES:2 * LANES] = jnp.ones((v_ref.shape[1], LANES), BF16)

    lane = lax.broadcasted_iota(jnp.int32, (tq, LANES), 1)
    for hd in range(n_heads):
        qs = (q_ref[hd].astype(F32) * (dh ** -0.5 * LOG2E)).astype(BF16)
        zero = jnp.zeros_like(qs)
        q2_ref[hd, 0:tq] = jnp.where(lane < dh, qs, zero)
        q2_ref[hd, tq:2 * tq] = jnp.where(lane >= dh, qs, zero)
    m_ref[...] = jnp.full(m_ref.shape, MASK_VALUE, F32)
    acc_ref[...] = jnp.zeros_like(acc_ref)

    def step(j, bias_idx):
        start = pl.multiple_of(j * tq, tq)
        s_all = [lax.dot_general(q2_ref[hd], k_ref[hd, pl.ds(start, tq), :], (((1,), (1,)), ((), ())),
                                 preferred_element_type=F32) for hd in range(n_heads)]
        for r0 in range(0, 2 * tq, rows):
            rs = slice(r0, r0 + rows)
            q0 = r0 % tq
            klen = min(q0 + rows, tq) if bias_idx == 0 else tq
            for hd in range(n_heads):
                vb = vaug_ref[hd, pl.ds(start, klen), :]
                s = s_all[hd][rs, 0:klen]
                if bias_idx is not None:
                    s = s + bias_ref[hd, bias_idx, q0:q0 + rows, 0:klen]
                m_old = m_ref[hd, rs]
                m_new = jnp.maximum(m_old, jnp.max(s, axis=1, keepdims=True))
                alpha = jnp.exp2(m_old - m_new)
                p = jnp.exp2(s - jnp.concatenate([m_new] * (klen // LANES), axis=1))
                pv = jnp.dot(p.astype(BF16), vb, preferred_element_type=F32)
                acc_ref[hd, rs] = jnp.concatenate([alpha, alpha], axis=1) * acc_ref[hd, rs] + pv
                m_ref[hd, rs] = m_new

    def far_step(j, carry):
        step(j, None)
        return carry

    lax.fori_loop(0, jnp.maximum(i - 1, 0), far_step, 0)

    @pl.when(i >= 1)
    def _():
        step(i - 1, 1)

    step(i, 0)

    lv = lam_ref[...]
    s1 = jnp.sum(lv[0:1] * lv[1:2], axis=1, keepdims=True)
    s2 = jnp.sum(lv[2:3] * lv[3:4], axis=1, keepdims=True)
    lam = jnp.exp(s1) - jnp.exp(s2) + lam_init
    for hd in range(n_heads):
        o1 = acc_ref[hd, 0:tq, 0:LANES] / acc_ref[hd, 0:tq, LANES:2 * LANES]
        o2 = acc_ref[hd, tq:2 * tq, 0:LANES] / acc_ref[hd, tq:2 * tq, LANES:2 * LANES]
        o = o1 - lam * o2
        ms = jnp.mean(o * o, axis=-1, keepdims=True)
        o_ref[:, hd * LANES:(hd + 1) * LANES] = (
            o * lax.rsqrt(ms + NORM_EPS) * g_ref[...] * (1.0 - lam_init)).astype(o_ref.dtype)


def _diff_attention(qkv, lam_vecs, bias_tiles, subln_g, *, batch, seq, tq, layer_idx):
    t = qkv.shape[1]
    nq = seq // tq
    rows = min(ATTN_PV_ROWS, tq)
    hps = ATTN_HEADS_PER_STEP
    n_hp = DIFF_HEADS // hps
    lam_init = 0.8 - 0.6 * math.exp(-0.3 * layer_idx)
    kern = functools.partial(_attn_kernel, tq=tq, rows=rows, lam_init=lam_init)
    est = hps * (2 * tq * LANES * 2 + 4 * seq * LANES * 2 + 4 * tq * tq * 4 + 2 * tq * LANES * 2
                 + 2 * tq * LANES * 2 + seq * 2 * LANES * 2 + 2 * tq * LANES * 4 + 2 * tq * 2 * LANES * 4
                 + 2 * tq * tq * 4 + 8 * rows * tq * 4)
    return pl.pallas_call(
        kern,
        out_shape=jax.ShapeDtypeStruct((t, DIFF_HEADS * LANES), BF16),
        grid=(batch, n_hp, nq),
        in_specs=[pl.BlockSpec((SUBLANES, LANES), lambda b, h, i: (0, 0)),
                  pl.BlockSpec((hps, tq, LANES), lambda b, h, i: (h, b * nq + i, 0)),
                  pl.BlockSpec((hps, seq, LANES), lambda b, h, i: (n_hp + h, b, 0)),
                  pl.BlockSpec((hps, seq, LANES), lambda b, h, i: (2 * n_hp + h, b, 0)),
                  pl.BlockSpec((hps, 2, tq, tq), lambda b, h, i: (h, 0, 0, 0)),
                  pl.BlockSpec((1, LANES), lambda b, h, i: (0, 0))],
        out_specs=pl.BlockSpec((tq, hps * LANES), lambda b, h, i: (b * nq + i, h)),
        scratch_shapes=[pltpu.VMEM((hps, 2 * tq, LANES), BF16),
                        pltpu.VMEM((hps, seq, 2 * LANES), BF16),
                        pltpu.VMEM((hps, 2 * tq, LANES), F32),
                        pltpu.VMEM((hps, 2 * tq, 2 * LANES), F32)],
        compiler_params=_params(("parallel", "parallel", "arbitrary"), est),
        name="diff_attention",
    )(lam_vecs, qkv, qkv, qkv, bias_tiles, subln_g)


def _mix_kernel(x_ref, y_ref, o_ref, gate_ref, wssd_ref, wdiff_ref, wmix_ref, g_ref, b_ref, out_ref):
    d = x_ref.shape[1]
    ssd = jnp.dot(y_ref[...], wssd_ref[...], preferred_element_type=F32)
    dif = jnp.dot(o_ref[...], wdiff_ref[...], preferred_element_type=F32)
    merged = gate_ref[:, 0:d].astype(F32) * ssd + gate_ref[:, d:2 * d].astype(F32) * dif
    mixed = jnp.dot(merged.astype(BF16), wmix_ref[...], preferred_element_type=F32)
    out_ref[...] = _layer_norm(DN_ALPHA * x_ref[...] + mixed, g_ref[...], b_ref[...])


def _mix(x, y_ssd, o_diff, proj, w_ssd, w_diff, w_mix, ln_g, ln_b, *, tm, gate_block):
    t, d = x.shape
    tm = min(tm, t)
    inner = y_ssd.shape[1]
    dw = o_diff.shape[1]
    row = lambda i: (i, 0)
    est = (2 * tm * (d * 4 + inner * 2 + dw * 2 + 2 * d * 2 + d * 4)
           + (inner * d + dw * d + d * d) * 2 + 6 * tm * d * 4)
    return pl.pallas_call(
        _mix_kernel,
        out_shape=jax.ShapeDtypeStruct((t, d), F32),
        grid=(t // tm,),
        in_specs=[pl.BlockSpec((tm, d), row),
                  pl.BlockSpec((tm, inner), row),
                  pl.BlockSpec((tm, dw), row),
                  pl.BlockSpec((tm, 2 * d), lambda i: (i, gate_block)),
                  _resident((inner, d)), _resident((dw, d)), _resident((d, d)),
                  _resident((1, d)), _resident((1, d))],
        out_specs=pl.BlockSpec((tm, d), row),
        compiler_params=_params(("parallel",), est),
        name="mix_ln1",
    )(x, y_ssd, o_diff, proj, w_ssd, w_diff, w_mix, ln_g, ln_b)


def _pack_bf16_pairs(v):
    w = v.shape[1] // 2
    lo = lax.bitcast_convert_type(v[:, :w].astype(BF16).astype(F32), jnp.int32)
    hi = lax.bitcast_convert_type(v[:, w:].astype(BF16).astype(F32), jnp.int32)
    return jnp.bitwise_or(hi, lax.shift_right_logical(lo, 16))


def _unpack_bf16_pairs(words):
    lo = lax.bitcast_convert_type(lax.shift_left(words, 16), F32)
    hi = lax.bitcast_convert_type(jnp.bitwise_and(words, jnp.int32(-0x10000)), F32)
    return jnp.concatenate([lo, hi], axis=1)


def _xattn_kernel(x_ref, kv_ref, wq_ref, wo_ref, g_ref, b_ref, out_ref, packed_ref, o_scr, *, heads):
    d = x_ref.shape[1]
    dh = d // heads
    xv = x_ref[...]
    q = jnp.dot(xv.astype(BF16), wq_ref[...], preferred_element_type=F32)
    q = (q * (dh ** -0.5)).astype(BF16)
    for h in range(heads):
        kh = kv_ref[:, h * dh:(h + 1) * dh]
        vh = kv_ref[:, d + h * dh:d + (h + 1) * dh]
        s = lax.dot_general(q[:, h * dh:(h + 1) * dh], kh, (((1,), (1,)), ((), ())),
                            preferred_element_type=F32)
        p = jnp.exp(s - jnp.max(s, axis=1, keepdims=True))
        p = p / jnp.sum(p, axis=1, keepdims=True)
        o_scr[:, h * dh:(h + 1) * dh] = jnp.dot(p.astype(BF16), vh, preferred_element_type=F32).astype(BF16)
    att = jnp.dot(o_scr[...], wo_ref[...], preferred_element_type=F32)
    y = _layer_norm(DN_ALPHA * xv + att, g_ref[...], b_ref[...])
    out_ref[...] = y
    packed_ref[...] = _pack_bf16_pairs(y)


def _cross_attention(x, kv, w_cq, w_co, ln_g, ln_b, *, batch, seq, tq):
    t, d = x.shape
    mem_len = kv.shape[0] // batch
    tq = min(tq, seq)
    nq = seq // tq
    est = (2 * tq * d * 4 * 2 + 2 * tq * d * 2 + 2 * mem_len * 2 * d * 2 + 2 * d * d * 2 + tq * d * 2
           + 8 * tq * d * 4)
    row = lambda b, i: (b * nq + i, 0)
    return pl.pallas_call(
        functools.partial(_xattn_kernel, heads=MEM_HEADS),
        out_shape=(jax.ShapeDtypeStruct((t, d), F32), jax.ShapeDtypeStruct((t, d // 2), jnp.int32)),
        grid=(batch, nq),
        in_specs=[pl.BlockSpec((tq, d), row),
                  pl.BlockSpec((mem_len, 2 * d), lambda b, i: (b, 0)),
                  _resident((d, d)), _resident((d, d)), _resident((1, d)), _resident((1, d))],
        out_specs=(pl.BlockSpec((tq, d), row), pl.BlockSpec((tq, d // 2), row)),
        scratch_shapes=[pltpu.VMEM((tq, d), BF16)],
        compiler_params=_params(("parallel", "parallel"), est),
        name="cross_attention_ln2",
    )(x, kv, w_cq, w_co, ln_g, ln_b)


def _router_kernel(x_ref, wr_ref, rb_ref, idx_ref, rank_ref, gates_ref, cnt_ref, run_ref, *, n_experts):
    tm = x_ref.shape[0]
    n_groups = N_EXPERT_GROUPS
    per = n_experts // n_groups

    @pl.when(pl.program_id(0) == 0)
    def _():
        run_ref[...] = jnp.zeros_like(run_ref)

    xv = x_ref[...]
    x_hi = xv.astype(BF16)
    x_lo = (xv - x_hi.astype(F32)).astype(BF16)
    nt = (((1,), (1,)), ((), ()))
    logits = (lax.dot_general(wr_ref[0], x_hi, nt, preferred_element_type=F32)
              + lax.dot_general(wr_ref[0], x_lo, nt, preferred_element_type=F32)
              + lax.dot_general(wr_ref[1], x_hi, nt, preferred_element_type=F32))
    sc = _sigmoid(logits[0:n_experts]).reshape(per, n_groups, tm)
    choice = sc + rb_ref[0:n_experts].reshape(per, n_groups, 1)
    neg = -jnp.inf
    member = lax.broadcasted_iota(jnp.int32, (per, n_groups, tm), 0)
    group3 = lax.broadcasted_iota(jnp.int32, (per, n_groups, tm), 1)
    m1 = jnp.max(choice, axis=0, keepdims=True)
    i1 = jnp.min(jnp.where(choice == m1, member, per), axis=0, keepdims=True)
    m2 = jnp.max(jnp.where(member == i1, neg, choice), axis=0, keepdims=True)
    gscore = (m1 + m2)
    gsel = jnp.zeros((1, n_groups, tm), F32)
    gidx = lax.broadcasted_iota(jnp.int32, (1, n_groups, tm), 1)
    cur = gscore
    for _ in range(TOP_GROUPS):
        mx = jnp.max(cur, axis=1, keepdims=True)
        ix = jnp.min(jnp.where(cur == mx, gidx, n_groups), axis=1, keepdims=True)
        hit = gidx == ix
        gsel = jnp.where(hit, 1.0, gsel)
        cur = jnp.where(hit, neg, cur)
    cur = jnp.where(gsel > 0.0, choice, neg)
    eidx = group3 * per + member
    esel = jnp.zeros((per, n_groups, tm), F32)
    hits, idx_rows = [], []
    for _ in range(TOP_K):
        mx = jnp.max(jnp.max(cur, axis=0, keepdims=True), axis=1, keepdims=True)
        ix = jnp.min(jnp.min(jnp.where(cur == mx, eidx, n_experts), axis=0, keepdims=True),
                     axis=1, keepdims=True)
        hit = eidx == ix
        esel = jnp.where(hit, 1.0, esel)
        cur = jnp.where(hit, neg, cur)
        hits.append(hit)
        idx_rows.append(ix.reshape(1, tm))
    w = esel * sc
    tot = jnp.sum(jnp.sum(w, axis=0, keepdims=True), axis=1, keepdims=True)
    gw = w / tot * ROUTED_SCALE

    r_i = lax.broadcasted_iota(jnp.int32, (tm, tm), 0)
    c_i = lax.broadcasted_iota(jnp.int32, (tm, tm), 1)
    upper = jnp.where(r_i <= c_i, 1.0, 0.0).astype(BF16)
    pref = jnp.dot(esel.reshape(n_experts, tm).astype(BF16), upper, preferred_element_type=F32)
    run = run_ref[...]
    rank3 = (jnp.concatenate([run] * (tm // LANES), axis=1) + pref - 1.0).reshape(per, n_groups, tm)

    def pick(hit, vals):
        return jnp.sum(jnp.sum(jnp.where(hit, vals, 0.0), axis=0, keepdims=True), axis=1).reshape(1, tm)

    idx_ref[...] = jnp.concatenate(idx_rows, axis=0)
    rank_ref[...] = jnp.concatenate([pick(h, rank3) for h in hits], axis=0).astype(jnp.int32)
    wk = jnp.concatenate([pick(h, gw) for h in hits] + [jnp.zeros((LANES - TOP_K, tm), F32)], axis=0)
    gates_ref[...] = wk.T
    run = run + jnp.broadcast_to(pref[:, tm - 1:tm], run.shape)
    run_ref[...] = run
    cnt_ref[...] = run


def _router(x, w_router, router_bias, *, tm, row0, t):
    d = x.shape[1]
    n_experts = w_router.shape[1]
    per = n_experts // N_EXPERT_GROUPS
    tm = min(tm, t)
    blk0 = row0 // tm

    def member_major(v):
        return v.reshape(N_EXPERT_GROUPS, per, *v.shape[1:]).swapaxes(0, 1).reshape(v.shape)

    wt = jnp.pad(member_major(w_router.astype(F32).T), ((0, LANES - n_experts), (0, 0)))
    w_hi = wt.astype(BF16)
    wr = jnp.stack([w_hi, (wt - w_hi.astype(F32)).astype(BF16)])
    rb = jnp.pad(member_major(router_bias.astype(F32)), (0, LANES - n_experts)).reshape(LANES, 1)
    est = (2 * tm * d * 4 + 2 * LANES * d * 2 + 2 * tm * LANES * 4 + 60 * n_experts * tm * 4 + 3 * tm * tm * 4)
    pick_spec = pl.BlockSpec((TOP_K, tm), lambda i: (0, i))
    idx_t, rank_t, gates, counts = pl.pallas_call(
        functools.partial(_router_kernel, n_experts=n_experts),
        out_shape=(jax.ShapeDtypeStruct((TOP_K, t), jnp.int32),
                   jax.ShapeDtypeStruct((TOP_K, t), jnp.int32),
                   jax.ShapeDtypeStruct((t, LANES), F32),
                   jax.ShapeDtypeStruct((n_experts, LANES), F32)),
        grid=(t // tm,),
        in_specs=[pl.BlockSpec((tm, d), lambda i: (blk0 + i, 0)),
                  _resident((2, LANES, d)), _resident((LANES, 1))],
        out_specs=(pick_spec, pick_spec,
                   pl.BlockSpec((tm, LANES), lambda i: (i, 0)),
                   pl.BlockSpec((n_experts, LANES), lambda i: (0, 0))),
        scratch_shapes=[pltpu.VMEM((n_experts, LANES), F32)],
        compiler_params=_params(("arbitrary",), est),
        name="router",
    )(x, wr, rb)
    counts = counts[:, 0].astype(jnp.int32).reshape(per, N_EXPERT_GROUPS).T.reshape(n_experts)
    return idx_t, rank_t, gates, counts


def _positions_kernel(cnt_ref, idx_ref, rank_ref, pos_ref, texp_ref, off_ref, *, n_experts, rows, n_tiles):
    @pl.when(pl.program_id(0) == 0)
    def _():
        def per_expert(e, toff):
            off_ref[e] = toff * rows
            nt = lax.shift_right_logical(cnt_ref[e] + (rows - 1), int(math.log2(rows)))

            def fill(j, c):
                texp_ref[toff + j] = e
                return c

            lax.fori_loop(0, nt, fill, 0)
            return toff + nt

        n_used = lax.fori_loop(0, n_experts, per_expert, 0)

        def tail(j, c):
            texp_ref[j] = n_experts - 1
            return c

        lax.fori_loop(n_used, n_tiles, tail, 0)
        texp_ref[n_tiles] = n_used

    idx = idx_ref[...]
    pos = rank_ref[...]
    for e in range(n_experts):
        pos = pos + jnp.where(idx == e, off_ref[e], 0)
    pos_ref[...] = pos


def _positions(counts, idx_t, rank_t, *, rows, n_tiles, tm):
    n_experts = counts.shape[0]
    k, t = idx_t.shape
    tm = min(tm, t)
    spec = pl.BlockSpec((k, tm), lambda i: (0, i))
    kern = functools.partial(_positions_kernel, n_experts=n_experts, rows=rows, n_tiles=n_tiles)
    return pl.pallas_call(
        kern,
        out_shape=(jax.ShapeDtypeStruct((k, t), jnp.int32),
                   jax.ShapeDtypeStruct((n_tiles + 1,), jnp.int32)),
        grid=(t // tm,),
        in_specs=[pl.BlockSpec(memory_space=pltpu.SMEM), spec, spec],
        out_specs=(spec, pl.BlockSpec(memory_space=pltpu.SMEM)),
        scratch_shapes=[pltpu.SMEM((n_experts,), jnp.int32)],
        compiler_params=_params(("arbitrary",), 16 * k * tm * 4),
        name="moe_positions",
    )(counts, idx_t, rank_t)


def _sc_mesh():
    return plsc.VectorSubcoreMesh(core_axis_name="c", subcore_axis_name="s",
                                  num_cores=SC_CORES, num_subcores=SC_SUBCORES)


def _sc_dispatch(packed, pos_chunks, n_rows, row0):
    w = packed.shape[1]
    n_chunks, k, n = pos_chunks.shape
    per_worker = n_chunks // (SC_CORES * SC_SUBCORES)

    @functools.partial(
        pl.kernel, mesh=_sc_mesh(),
        out_type=jax.ShapeDtypeStruct((n_rows, w), packed.dtype),
        scratch_types=[pltpu.VMEM((k, n), jnp.int32), pltpu.VMEM((n, w), packed.dtype),
                       pltpu.SemaphoreType.DMA],
        name="moe_dispatch_sc",
    )
    def scatter_rows(x_hbm, pos_hbm, out_hbm, idx_v, rows_v, sem):
        wid = lax.axis_index("s") * SC_CORES + lax.axis_index("c")

        @pl.loop(0, per_worker)
        def _(step):
            c = wid * per_worker + step
            pltpu.sync_copy(pos_hbm.at[c], idx_v)
            pltpu.sync_copy(x_hbm.at[pl.ds(row0 + c * n, n)], rows_v)
            copies = [pltpu.async_copy(rows_v, out_hbm.at[idx_v.at[kk]], sem) for kk in range(k)]
            for cp in copies:
                cp.wait()

    return scatter_rows(packed, pos_chunks)


def _sc_combine(sorted_rows, pos_chunks):
    _, w = sorted_rows.shape
    n_chunks, k, n = pos_chunks.shape
    per_worker = n_chunks // (SC_CORES * SC_SUBCORES)

    @functools.partial(
        pl.kernel, mesh=_sc_mesh(),
        out_type=jax.ShapeDtypeStruct((k, n_chunks * n, w), sorted_rows.dtype),
        scratch_types=[pltpu.VMEM((k, n), jnp.int32), pltpu.VMEM((n, w), sorted_rows.dtype),
                       pltpu.SemaphoreType.DMA],
        name="moe_combine_sc",
    )
    def gather_rows(y_hbm, pos_hbm, out_hbm, idx_v, rows_v, sem):
        wid = lax.axis_index("s") * SC_CORES + lax.axis_index("c")

        @pl.loop(0, per_worker)
        def _(step):
            c = wid * per_worker + step
            pltpu.sync_copy(pos_hbm.at[c], idx_v)
            for kk in range(k):
                pltpu.async_copy(y_hbm.at[idx_v.at[kk]], rows_v, sem).wait()
                pltpu.sync_copy(rows_v, out_hbm.at[kk, pl.ds(c * n, n)])

    return gather_rows(sorted_rows, pos_chunks)


def _expert_kernel(texp_ref, xs_ref, wgu_ref, wdn_ref, ys_ref, *, n_tiles):
    ff = wdn_ref.shape[1]

    @pl.when(pl.program_id(0) < texp_ref[n_tiles])
    def _():
        xv = _unpack_bf16_pairs(xs_ref[...]).astype(BF16)
        gu = jnp.dot(xv, wgu_ref[0], preferred_element_type=F32)
        hid = _silu(gu[:, 0:ff]) * gu[:, ff:2 * ff]
        ys_ref[...] = _pack_bf16_pairs(jnp.dot(hid.astype(BF16), wdn_ref[0], preferred_element_type=F32))


def _experts(tile_expert, sorted_rows, w_gu, w_dn, *, rows):
    n_rows, w = sorted_rows.shape
    n_tiles = n_rows // rows
    _, d, ff2 = w_gu.shape
    ff = ff2 // 2
    tile = lambda j, te: (jnp.minimum(j, te[n_tiles] - 1), 0)
    est = 4 * rows * w * 4 + 2 * (d * ff2 + ff * d) * 2 + 4 * rows * d * 4 + 4 * rows * ff2 * 4
    return pl.pallas_call(
        functools.partial(_expert_kernel, n_tiles=n_tiles),
        out_shape=jax.ShapeDtypeStruct((n_rows, w), sorted_rows.dtype),
        grid_spec=pltpu.PrefetchScalarGridSpec(
            num_scalar_prefetch=1,
            grid=(n_tiles,),
            in_specs=[pl.BlockSpec((rows, w), tile),
                      pl.BlockSpec((1, d, ff2), lambda j, te: (te[j], 0, 0)),
                      pl.BlockSpec((1, ff, d), lambda j, te: (te[j], 0, 0))],
            out_specs=pl.BlockSpec((rows, w), tile)),
        compiler_params=_params(("arbitrary",), est),
        name="moe_experts",
    )(tile_expert, sorted_rows, w_gu, w_dn)


def _moe_out_kernel(x_ref, yk_ref, gates_ref, wsgu_ref, wsdn_ref, g_ref, b_ref, *rest):
    out_ref = rest[-1]
    ff = wsdn_ref.shape[0]
    xv = x_ref[...]
    gu = jnp.dot(xv.astype(BF16), wsgu_ref[...], preferred_element_type=F32)
    hid = _silu(gu[:, 0:ff]) * gu[:, ff:2 * ff]
    acc = jnp.dot(hid.astype(BF16), wsdn_ref[...], preferred_element_type=F32)
    for k in range(yk_ref.shape[0]):
        acc = acc + gates_ref[:, k:k + 1] * _unpack_bf16_pairs(yk_ref[k])
    out_ref[...] = _layer_norm(DN_ALPHA * xv + acc, g_ref[...], b_ref[...])


def _moe_out(x, yk, gates, w_sgu, w_sdn, ln_g, ln_b, *, tm, row0, prev):
    t_all, d = x.shape
    k, t, w = yk.shape
    sff2 = w_sgu.shape[1]
    tm = min(tm, t)
    blk0 = row0 // tm
    est = (2 * tm * d * 4 * 2 + 2 * k * tm * w * 4 + 2 * tm * LANES * 4
           + (d * sff2 + (sff2 // 2) * d) * 2 + 6 * tm * d * 4)
    in_specs = [pl.BlockSpec((tm, d), lambda i: (blk0 + i, 0)),
                pl.BlockSpec((k, tm, w), lambda i: (0, i, 0)),
                pl.BlockSpec((tm, LANES), lambda i: (i, 0)),
                _resident((d, sff2)), _resident((sff2 // 2, d)),
                _resident((1, d)), _resident((1, d))]
    args = [x, yk, gates, w_sgu, w_sdn, ln_g, ln_b]
    aliases = {}
    if prev is not None:
        in_specs.append(pl.BlockSpec(memory_space=pl.ANY))
        args.append(prev)
        aliases = {len(args) - 1: 0}
    return pl.pallas_call(
        _moe_out_kernel,
        out_shape=jax.ShapeDtypeStruct((t_all, d), F32),
        grid=(t // tm,),
        in_specs=in_specs,
        out_specs=pl.BlockSpec((tm, d), lambda i: (blk0 + i, 0)),
        input_output_aliases=aliases,
        compiler_params=_params(("parallel",), est),
        name="moe_out_ln3",
    )(*args)


def _moe(x, packed, w_router, router_bias, w_gu, w_dn, w_sgu, w_sdn, ln_g, ln_b):
    t_all, d = x.shape
    n_experts = w_router.shape[1]
    n = SC_CHUNK_TOKENS
    splits = MOE_SPLITS if t_all % (MOE_SPLITS * n * SC_CORES * SC_SUBCORES) == 0 else 1
    t = t_all // splits
    rows = min(MOE_ROWS_PER_TILE, t)
    n_tiles = (t * TOP_K) // rows + n_experts
    parts = []
    for s in range(splits):
        row0 = s * t
        idx_t, rank_t, gates, counts = _router(x, w_router, router_bias, tm=ROUTER_ROWS, row0=row0, t=t)
        pos_t, tile_expert = _positions(counts, idx_t, rank_t, rows=rows, n_tiles=n_tiles,
                                        tm=POSITIONS_TOKENS)
        pos_chunks = pos_t.reshape(TOP_K, t // n, n).transpose(1, 0, 2)
        sorted_x = _sc_dispatch(packed, pos_chunks, n_tiles * rows, row0)
        parts.append((row0, gates, tile_expert, pos_chunks, sorted_x))
    out = None
    for row0, gates, tile_expert, pos_chunks, sorted_x in parts:
        sorted_y = _experts(tile_expert, sorted_x, w_gu, w_dn, rows=rows)
        yk = _sc_combine(sorted_y, pos_chunks)
        out = _moe_out(x, yk, gates, w_sgu, w_sdn, ln_g, ln_b, tm=MOE_OUT_ROWS, row0=row0, prev=out)
    return out


def kernel(x, mem, w_in, conv_w, conv_b, dt_bias, a_log, d_skip, ssd_norm_g, lambda_q1, lambda_k1, lambda_q2, lambda_k2, subln_g, rel_bias, w_ssd_br, w_diff_br, w_mix_out, ln1_g, ln1_b, w_cq, w_ckv, w_co, ln2_g, ln2_b, w_router, router_bias, w_exp_gu, w_exp_down, w_sh_gu, w_sh_down, ln3_g, ln3_b):
    batch, seq, d = x.shape
    depth = w_in.shape[0]
    inner = w_ssd_br.shape[1]
    xbc_width = conv_w.shape[2]
    heads = dt_bias.shape[1]
    diff_width = w_diff_br.shape[1]
    o_z, o_xbc = inner, inner + xbc_width
    o_dt = o_xbc + heads
    o_v = o_dt + 3 * diff_width
    t = batch * seq
    tq_attn = min(ATTN_TILE, seq)

    def vec(v):
        return v.astype(F32).reshape(1, -1)

    xt = x.reshape(t, d)
    memt = mem.reshape(-1, d)
    bias_tiles = _bias_tiles(rel_bias, tq_attn)

    for l in range(depth):
        wl = w_in[l]
        w_big = jnp.concatenate([wl[:, o_z:o_xbc], wl[:, o_dt:o_v], wl[:, :o_z], wl[:, o_v:]],
                                axis=1).astype(BF16)
        assert 2 * d == inner
        w_dt = jnp.pad(wl[:, o_xbc:o_dt], ((0, 0), (0, LANES - heads))).astype(BF16)
        xbc, qkv, zg, dt_raw = _in_proj(xt, w_big, w_dt, n_xbc_cols=xbc_width, n_qkv_cols=3 * diff_width,
                                        n_z_cols=inner, tm=PROJ_ROWS, tn=PROJ_COLS)

        y_ssd = _ssd_branch(zg, xbc, dt_raw, conv_w[l], conv_b[l], dt_bias[l], a_log[l], d_skip[l],
                            ssd_norm_g[l], batch=batch, seq=seq)

        lam_vecs = jnp.pad(jnp.stack([lambda_q1[l], lambda_k1[l], lambda_q2[l], lambda_k2[l]]).astype(F32),
                           ((0, SUBLANES - 4), (0, LANES - lambda_q1.shape[1])))
        o_diff = _diff_attention(qkv, lam_vecs, bias_tiles, vec(subln_g[l]), batch=batch, seq=seq,
                                 tq=tq_attn, layer_idx=l)

        xt = _mix(xt, y_ssd, o_diff, zg, w_ssd_br[l].astype(BF16), w_diff_br[l].astype(BF16),
                  w_mix_out[l].astype(BF16), vec(ln1_g[l]), vec(ln1_b[l]), tm=MIX_ROWS, gate_block=1)

        kv = _matmul(memt, w_ckv[l].astype(BF16), BF16, tm=PROJ_ROWS, tn=PROJ_COLS, name="mem_kv_proj")
        xt, packed = _cross_attention(xt, kv, w_cq[l].astype(BF16), w_co[l].astype(BF16), vec(ln2_g[l]),
                                      vec(ln2_b[l]), batch=batch, seq=seq, tq=XATTN_ROWS)

        xt = _moe(xt, packed, w_router[l], router_bias[l], w_exp_gu[l].astype(BF16),
                  w_exp_down[l].astype(BF16), w_sh_gu[l].astype(BF16), w_sh_down[l].astype(BF16),
                  vec(ln3_g[l]), vec(ln3_b[l]))
    return xt.reshape(batch, seq, d)
```

```python
import functools
import math

import numpy as np
import jax
import jax.numpy as jnp
from jax import lax
from jax.experimental import pallas as pl
from jax.experimental.pallas import tpu as pltpu
from jax.experimental.pallas import tpu_sc as plsc

F32 = jnp.float32
BF16 = jnp.bfloat16

SSD_HEAD_DIM = 64
SSD_GROUPS = 4
SSD_STATE = 128
SSD_CONV = 4
SSD_CHUNK = 128
DIFF_HEADS = 8
REL_BUCKETS = 32
REL_MAX_DIST = 128
MEM_HEADS = 4
TOP_K = 8
N_EXPERT_GROUPS = 8
TOP_GROUPS = 4
ROUTED_SCALE = 2.5
NORM_EPS = 1e-5
DEPTH = 1
DN_ALPHA = (2.0 * DEPTH) ** 0.25

LANES = 128
SUBLANES = 8
VMEM_CAP_BYTES = 64 * 1024 * 1024
MASK_VALUE = -1e30
LOG2E = math.log2(math.e)
PROJ_ROWS, PROJ_COLS = 1024, 1024
IN_PROJ_SLAB = 256
SSD_CHUNKS_PER_STEP = 2
ATTN_TILE = 512
ATTN_PV_ROWS = 512
ATTN_HEADS_PER_STEP = 2
MIX_ROWS = 512
XATTN_ROWS = 1024
ROUTER_ROWS = 512
POSITIONS_TOKENS = 2048
MOE_OUT_ROWS = 512
SC_CORES = 2
SC_SUBCORES = 16
SC_CHUNK_TOKENS = 128
MOE_SPLITS = 2
MOE_ROWS_PER_TILE = 1024


def _vmem_limit(estimate_bytes):
    return int(min(estimate_bytes * 5 // 4 + (4 << 20), VMEM_CAP_BYTES - (6 << 20)))


def _params(semantics, vmem_estimate):
    return pltpu.CompilerParams(dimension_semantics=semantics,
                                vmem_limit_bytes=_vmem_limit(vmem_estimate))


def _resident(shape):
    nd = len(shape)
    return pl.BlockSpec(shape, lambda *_: (0,) * nd, pipeline_mode=pl.Buffered(1))


def _layer_norm(v, g, b):
    mu = jnp.mean(v, axis=-1, keepdims=True)
    d = v - mu
    var = jnp.mean(d * d, axis=-1, keepdims=True)
    return d * lax.rsqrt(var + NORM_EPS) * g + b


def _sigmoid(v):
    return 0.5 * jnp.tanh(0.5 * v) + 0.5


def _silu(v):
    h = 0.5 * v
    return h + h * jnp.tanh(h)


def _matmul_kernel(x_ref, w_ref, o_ref, xb_ref):
    @pl.when(pl.program_id(1) == 0)
    def _():
        xb_ref[...] = x_ref[...].astype(BF16)

    o_ref[...] = jnp.dot(xb_ref[...], w_ref[...], preferred_element_type=F32).astype(o_ref.dtype)


def _matmul(x, w, out_dtype, *, tm, tn, name):
    m, k = x.shape
    n = w.shape[1]
    tm, tn = min(tm, m), min(tn, n)
    est = (2 * tm * k * x.dtype.itemsize + tm * k * 2 + 2 * k * tn * 2
           + 2 * tm * tn * jnp.dtype(out_dtype).itemsize + tm * tn * 4)
    return pl.pallas_call(
        _matmul_kernel,
        out_shape=jax.ShapeDtypeStruct((m, n), out_dtype),
        grid=(m // tm, n // tn),
        in_specs=[pl.BlockSpec((tm, k), lambda i, j: (i, 0)),
                  pl.BlockSpec((k, tn), lambda i, j: (0, j))],
        out_specs=pl.BlockSpec((tm, tn), lambda i, j: (i, j)),
        scratch_shapes=[pltpu.VMEM((tm, k), BF16)],
        compiler_params=_params(("parallel", "arbitrary"), est),
        name=name,
    )(x, w)


def _in_proj_kernel(x_ref, w_ref, wdt_ref, xbc_ref, qkv_ref, zg_ref, dt_ref, xb_ref, *, n_xbc, n_qkv, n_z):
    j = pl.program_id(1)
    j_z = n_xbc + n_qkv

    @pl.when(j == 0)
    def _():
        xb = x_ref[...].astype(BF16)
        xb_ref[...] = xb
        dt_ref[...] = jnp.dot(xb, wdt_ref[...], preferred_element_type=F32)

    tn = w_ref.shape[1]
    slabs = [slice(c0, c0 + IN_PROJ_SLAB) for c0 in range(0, tn, IN_PROJ_SLAB)]

    def product(cs):
        return jnp.dot(xb_ref[...], w_ref[:, cs], preferred_element_type=F32)

    @pl.when(j < n_xbc)
    def _():
        for cs in slabs:
            xbc_ref[:, cs] = product(cs).astype(BF16)

    @pl.when(jnp.logical_and(j >= n_xbc, j < j_z))
    def _():
        for cs in slabs:
            acc = product(cs)
            for hh in range(IN_PROJ_SLAB // LANES):
                qkv_ref[cs.start // LANES + hh] = acc[:, hh * LANES:(hh + 1) * LANES].astype(BF16)

    @pl.when(jnp.logical_and(j >= j_z, j < j_z + n_z))
    def _():
        for cs in slabs:
            zg_ref[:, cs] = product(cs).astype(BF16)

    @pl.when(j >= j_z + n_z)
    def _():
        for cs in slabs:
            zg_ref[:, cs] = _sigmoid(product(cs)).astype(BF16)


def _in_proj(x, w, w_dt, *, n_xbc_cols, n_qkv_cols, n_z_cols, tm, tn):
    m, k = x.shape
    n = w.shape[1]
    tm = min(tm, m)
    n_xbc, n_qkv, n_z = n_xbc_cols // tn, n_qkv_cols // tn, n_z_cols // tn
    n_zg = n // tn - n_xbc - n_qkv
    per = tn // LANES
    est = (2 * tm * k * 4 + tm * k * 2 + 2 * k * tn * 2 + 3 * 2 * tm * tn * 2 + 2 * tm * tn * 4
           + k * LANES * 2 + 2 * tm * LANES * 4)
    kern = functools.partial(_in_proj_kernel, n_xbc=n_xbc, n_qkv=n_qkv, n_z=n_z)
    return pl.pallas_call(
        kern,
        out_shape=(jax.ShapeDtypeStruct((m, n_xbc_cols), BF16),
                   jax.ShapeDtypeStruct((n_qkv_cols // LANES, m, LANES), BF16),
                   jax.ShapeDtypeStruct((m, n_zg * tn), BF16),
                   jax.ShapeDtypeStruct((m, LANES), F32)),
        grid=(m // tm, n // tn),
        in_specs=[pl.BlockSpec((tm, k), lambda i, j: (i, 0)),
                  pl.BlockSpec((k, tn), lambda i, j: (0, j)),
                  _resident((k, LANES))],
        out_specs=(pl.BlockSpec((tm, tn), lambda i, j: (i, jnp.minimum(j, n_xbc - 1))),
                   pl.BlockSpec((per, tm, LANES), lambda i, j: (jnp.clip(j - n_xbc, 0, n_qkv - 1), i, 0)),
                   pl.BlockSpec((tm, tn), lambda i, j: (i, jnp.clip(j - n_xbc - n_qkv, 0, n_zg - 1))),
                   pl.BlockSpec((tm, LANES), lambda i, j: (i, 0))),
        scratch_shapes=[pltpu.VMEM((tm, k), BF16)],
        compiler_params=_params(("parallel", "arbitrary"), est),
        name="in_proj",
    )(x, w, w_dt)


def _split3(v):
    hi = v.astype(BF16)
    r1 = v - hi.astype(F32)
    mid = r1.astype(BF16)
    lo = (r1 - mid.astype(F32)).astype(BF16)
    return hi, mid, lo


def _ssd_kernel(z_ref, xbc_ref, dt_ref, convw_ref, convb_ref, dtb_ref, alog_ref,
                dskip_ref, g_ref, y_ref, tail_ref, state_ref, xa_ref, yacc_ref, **static):
    @pl.when(pl.program_id(1) == 0)
    def _():
        tail_ref[...] = jnp.zeros_like(tail_ref)
        state_ref[...] = jnp.zeros_like(state_ref)

    for r0 in range(0, z_ref.shape[0], SSD_CHUNK):
        rows = pl.ds(r0, SSD_CHUNK)
        _ssd_chunk(z_ref.at[rows], xbc_ref.at[rows], dt_ref.at[rows], convw_ref, convb_ref, dtb_ref,
                   alog_ref, dskip_ref, g_ref, y_ref.at[rows], tail_ref, state_ref, xa_ref, yacc_ref,
                   **static)


def _ssd_chunk(z_ref, xbc_ref, dt_ref, convw_ref, convb_ref, dtb_ref, alog_ref,
               dskip_ref, g_ref, y_ref, tail_ref, state_ref, xa_ref, yacc_ref,
               *, inner, n_groups, d_state):
    L = SSD_CHUNK
    width = xbc_ref.shape[1]
    n_pairs = inner // LANES
    pairs_per_group = n_pairs // n_groups
    group_width = inner // n_groups

    slab = 512
    n_t = L // SUBLANES
    sub = lax.broadcasted_iota(jnp.int32, (n_t, SUBLANES, slab), 1)
    for c0 in range(0, width, slab):
        cs = slice(c0, c0 + slab)
        u = xbc_ref[:, cs].astype(F32)
        tiles = jnp.concatenate([tail_ref[:, cs], u], axis=0).reshape(n_t + 1, SUBLANES, slab)
        acc = convb_ref[:, cs] + u * convw_ref[SSD_CONV - 1:SSD_CONV, cs]
        for j in range(1, SSD_CONV):
            rot = pltpu.roll(tiles, j, axis=1)
            shifted = jnp.where(sub < j, rot[0:n_t], rot[1:n_t + 1]).reshape(L, slab)
            acc = acc + shifted * convw_ref[SSD_CONV - 1 - j:SSD_CONV - j, cs]
        tail_ref[:, cs] = u[L - SUBLANES:L]
        xa_ref[:, cs] = _silu(acc)

    dtr = dt_ref[...] + dtb_ref[...]
    dt = jnp.maximum(dtr, 0.0) + jnp.log(1.0 + jnp.exp(-jnp.abs(dtr)))
    a = -jnp.exp(alog_ref[...]) * dt
    row_i = lax.broadcasted_iota(jnp.int32, (L, L), 0)
    col_i = lax.broadcasted_iota(jnp.int32, (L, L), 1)
    causal = row_i >= col_i
    tril = jnp.where(causal, 1.0, 0.0).astype(BF16)
    acs = sum(jnp.dot(tril, part, preferred_element_type=F32) for part in _split3(a)) * LOG2E
    acs_t = acs.T
    dt_t = dt.T
    acs_dt_t = acs_t - jnp.log2(dt_t)
    lane_lo = lax.broadcasted_iota(jnp.int32, (L, LANES), 1) < SSD_HEAD_DIM

    b0 = inner
    c0 = inner + n_groups * d_state
    for g in range(n_groups):
        bg = xa_ref[:, b0 + g * d_state:b0 + (g + 1) * d_state]
        cg = xa_ref[:, c0 + g * d_state:c0 + (g + 1) * d_state]
        cb = lax.dot_general(cg.astype(BF16), bg.astype(BF16), (((1,), (1,)), ((), ())),
                             preferred_element_type=F32)
        bg_t = bg.T
        for pp in range(g * pairs_per_group, (g + 1) * pairs_per_group):
            xs_pair = xa_ref[:, pp * LANES:(pp + 1) * LANES].astype(BF16)
            st_old = state_ref[pp]
            rhs = jnp.concatenate([xs_pair, st_old.astype(BF16)], axis=0)
            lhs, b_ws, keep = [], [], []
            for side in range(2):
                h = 2 * pp + side
                col = jnp.broadcast_to(acs[:, h:h + 1], (L, L))
                row = acs_t[h:h + 1, :]
                dt_row = dt_t[h:h + 1, :]
                last = acs[L - 1:L, h:h + 1]
                dec_dt = jnp.exp2(jnp.where(causal, col - acs_dt_t[h:h + 1, :], MASK_VALUE))
                m_in = (cb * dec_dt).astype(BF16)
                c_w = (cg * jnp.exp2(col[:, :d_state])).astype(BF16)
                lhs.append(jnp.concatenate([m_in, c_w], axis=1))
                b_ws.append((bg_t * (jnp.exp2(last - row) * dt_row)).astype(BF16))
                keep.append(jnp.exp2(last))
            y2 = jnp.dot(jnp.concatenate(lhs, axis=0), rhs, preferred_element_type=F32)
            s2 = jnp.dot(jnp.concatenate(b_ws, axis=0), xs_pair, preferred_element_type=F32)
            yacc_ref[:, pp * LANES:(pp + 1) * LANES] = jnp.where(lane_lo, y2[0:L], y2[L:2 * L])
            keep_pair = jnp.where(lane_lo[0:1, :], keep[0], keep[1])
            state_ref[pp] = st_old * keep_pair + jnp.where(lane_lo[0:d_state], s2[0:d_state],
                                                           s2[d_state:2 * d_state])

    for g in range(n_groups):
        cs = slice(g * group_width, (g + 1) * group_width)
        zz = z_ref[:, cs].astype(F32)
        yv = (yacc_ref[:, cs] + dskip_ref[:, cs] * xa_ref[:, cs]) * _silu(zz)
        ms = jnp.mean(yv * yv, axis=-1, keepdims=True)
        y_ref[:, cs] = (yv * lax.rsqrt(ms + NORM_EPS) * g_ref[:, cs]).astype(y_ref.dtype)


def _ssd_branch(zg, xbc, dt_raw, conv_w, conv_b, dt_bias, a_log, d_skip, norm_g, *, batch, seq):
    t = zg.shape[0]
    inner = norm_g.shape[0]
    width = conv_w.shape[1]
    heads = inner // SSD_HEAD_DIM
    n_chunks = seq // SSD_CHUNK
    L = SSD_CHUNK

    def pad_heads(v):
        return jnp.pad(v.astype(F32), (0, LANES - heads)).reshape(1, LANES)

    convw = jnp.pad(conv_w.astype(F32), ((0, SUBLANES - SSD_CONV), (0, 0)))
    dskip = jnp.repeat(d_skip.astype(F32), SSD_HEAD_DIM).reshape(1, inner)
    cps = SSD_CHUNKS_PER_STEP if n_chunks % SSD_CHUNKS_PER_STEP == 0 else 1
    n_steps = n_chunks // cps
    rows = cps * L
    row = lambda b, c: (b * n_steps + c, 0)
    const = lambda b, c: (0, 0)
    est = (2 * rows * (inner + width) * 2 + 2 * rows * LANES * 4 + 2 * rows * inner * 2
           + (inner // LANES) * SSD_STATE * LANES * 4 + L * (2 * width + inner) * 4 + (8 << 20))
    kern = functools.partial(_ssd_kernel, inner=inner, n_groups=SSD_GROUPS, d_state=SSD_STATE)
    return pl.pallas_call(
        kern,
        out_shape=jax.ShapeDtypeStruct((t, inner), BF16),
        grid=(batch, n_steps),
        in_specs=[pl.BlockSpec((rows, inner), row),
                  pl.BlockSpec((rows, width), row),
                  pl.BlockSpec((rows, LANES), row),
                  pl.BlockSpec((SUBLANES, width), const),
                  pl.BlockSpec((1, width), const),
                  pl.BlockSpec((1, LANES), const),
                  pl.BlockSpec((1, LANES), const),
                  pl.BlockSpec((1, inner), const),
                  pl.BlockSpec((1, inner), const)],
        out_specs=pl.BlockSpec((rows, inner), row),
        scratch_shapes=[pltpu.VMEM((SUBLANES, width), F32),
                        pltpu.VMEM((inner // LANES, SSD_STATE, LANES), F32),
                        pltpu.VMEM((L, width), F32),
                        pltpu.VMEM((L, inner), F32)],
        compiler_params=_params(("parallel", "arbitrary"), est),
        name="ssd_scan",
    )(zg, xbc, dt_raw, convw, conv_b.astype(F32).reshape(1, width), pad_heads(dt_bias),
      pad_heads(a_log), dskip, norm_g.astype(F32).reshape(1, inner))


def _bucket_tiles(tq):
    max_exact = REL_BUCKETS // 2
    qi = np.arange(tq)[:, None]
    ki = np.arange(tq)[None, :]

    def bucket(dist):
        d = np.maximum(dist, 1).astype(np.float32)
        large = max_exact + (np.log(d / np.float32(max_exact)) / np.float32(math.log(REL_MAX_DIST / max_exact))
                             * np.float32(REL_BUCKETS - max_exact)).astype(np.int32)
        large = np.minimum(large, REL_BUCKETS - 1)
        return np.where(dist < max_exact, dist, large).astype(np.int32)

    diag = np.where(qi >= ki, bucket(np.maximum(qi - ki, 0)), -1)
    prev = bucket(tq + qi - ki)
    far = bucket(np.arange(tq + 1, 1 << 16))
    assert (far == REL_BUCKETS - 1).all()
    return np.stack([diag, prev]).astype(np.int32)


def _bias_kernel(rb_ref, bucket_ref, o_ref):
    h = pl.program_id(0)
    bk = bucket_ref[...]
    acc = jnp.zeros(bk.shape, F32)
    for b in range(REL_BUCKETS):
        acc = jnp.where(bk == b, rb_ref[b, h], acc)
    o_ref[0] = jnp.where(bk < 0, MASK_VALUE, (acc - rb_ref[REL_BUCKETS - 1, h]) * LOG2E)


def _bias_tiles(rel_bias, tq):
    buckets = jnp.asarray(_bucket_tiles(tq))
    return pl.pallas_call(
        _bias_kernel,
        out_shape=jax.ShapeDtypeStruct((DIFF_HEADS, 2, tq, tq), F32),
        grid=(DIFF_HEADS,),
        in_specs=[pl.BlockSpec(memory_space=pltpu.SMEM),
                  pl.BlockSpec((2, tq, tq), lambda h: (0, 0, 0))],
        out_specs=pl.BlockSpec((1, 2, tq, tq), lambda h: (h, 0, 0, 0)),
        compiler_params=_params(("arbitrary",), 8 * tq * tq * 4),
        name="t5_bias_tiles",
    )(rel_bias.astype(F32), buckets)


def _attn_kernel(lam_ref, q_ref, k_ref, v_ref, bias_ref, g_ref, o_ref,
                 q2_ref, vaug_ref, m_ref, acc_ref, *, tq, rows, lam_init):
    i = pl.program_id(2)
    dh = LANES // 2
    n_heads = q_ref.shape[0]

    @pl.when(i == 0)
    def _():
        for hd in range(n_heads):
            vaug_ref[hd, :, 0:LANES] = v_ref[hd]
            vaug_ref[hd, :, LANES:2 * LANES] = jnp.ones((v_ref.shape[1], LANES), BF16)

    lane = lax.broadcasted_iota(jnp.int32, (tq, LANES), 1)
    for hd in range(n_heads):
        qs = (q_ref[hd].astype(F32) * (dh ** -0.5 * LOG2E)).astype(BF16)
        zero = jnp.zeros_like(qs)
        q2_ref[hd, 0:tq] = jnp.where(lane < dh, qs, zero)
        q2_ref[hd, tq:2 * tq] = jnp.where(lane >= dh, qs, zero)
    m_ref[...] = jnp.full(m_ref.shape, MASK_VALUE, F32)
    acc_ref[...] = jnp.zeros_like(acc_ref)

    def step(j, bias_idx):
        start = pl.multiple_of(j * tq, tq)
        s_all = [lax.dot_general(q2_ref[hd], k_ref[hd, pl.ds(start, tq), :], (((1,), (1,)), ((), ())),
                                 preferred_element_type=F32) for hd in range(n_heads)]
        for r0 in range(0, 2 * tq, rows):
            rs = slice(r0, r0 + rows)
            q0 = r0 % tq
            klen = min(q0 + rows, tq) if bias_idx == 0 else tq
            for hd in range(n_heads):
                vb = vaug_ref[hd, pl.ds(start, klen), :]
                s = s_all[hd][rs, 0:klen]
                if bias_idx is not None:
                    s = s + bias_ref[hd, bias_idx, q0:q0 + rows, 0:klen]
                m_old = m_ref[hd, rs]
                m_new = jnp.maximum(m_old, jnp.max(s, axis=1, keepdims=True))
                alpha = jnp.exp2(m_old - m_new)
                p = jnp.exp2(s - jnp.concatenate([m_new] * (klen // LANES), axis=1))
                pv = jnp.dot(p.astype(BF16), vb, preferred_element_type=F32)
                acc_ref[hd, rs] = jnp.concatenate([alpha, alpha], axis=1) * acc_ref[hd, rs] + pv
                m_ref[hd, rs] = m_new

    def far_step(j, carry):
        step(j, None)
        return carry

    lax.fori_loop(0, jnp.maximum(i - 1, 0), far_step, 0)

    @pl.when(i >= 1)
    def _():
        step(i - 1, 1)

    step(i, 0)

    lv = lam_ref[...]
    s1 = jnp.sum(lv[0:1] * lv[1:2], axis=1, keepdims=True)
    s2 = jnp.sum(lv[2:3] * lv[3:4], axis=1, keepdims=True)
    lam = jnp.exp(s1) - jnp.exp(s2) + lam_init
    for hd in range(n_heads):
        o1 = acc_ref[hd, 0:tq, 0:LANES] / acc_ref[hd, 0:tq, LANES:2 * LANES]
        o2 = acc_ref[hd, tq:2 * tq, 0:LANES] / acc_ref[hd, tq:2 * tq, LANES:2 * LANES]
        o = o1 - lam * o2
        ms = jnp.mean(o * o, axis=-1, keepdims=True)
        o_ref[:, hd * LANES:(hd + 1) * LANES] = (
            o * lax.rsqrt(ms + NORM_EPS) * g_ref[...] * (1.0 - lam_init)).astype(o_ref.dtype)


def _diff_attention(qkv, lam_vecs, bias_tiles, subln_g, *, batch, seq, tq, layer_idx):
    t = qkv.shape[1]
    nq = seq // tq
    rows = min(ATTN_PV_ROWS, tq)
    hps = ATTN_HEADS_PER_STEP
    n_hp = DIFF_HEADS // hps
    lam_init = 0.8 - 0.6 * math.exp(-0.3 * layer_idx)
    kern = functools.partial(_attn_kernel, tq=tq, rows=rows, lam_init=lam_init)
    est = hps * (2 * tq * LANES * 2 + 4 * seq * LANES * 2 + 4 * tq * tq * 4 + 2 * tq * LANES * 2
                 + 2 * tq * LANES * 2 + seq * 2 * LANES * 2 + 2 * tq * LANES * 4 + 2 * tq * 2 * LANES * 4
                 + 2 * tq * tq * 4 + 8 * rows * tq * 4)
    return pl.pallas_call(
        kern,
        out_shape=jax.ShapeDtypeStruct((t, DIFF_HEADS * LANES), BF16),
        grid=(batch, n_hp, nq),
        in_specs=[pl.BlockSpec((SUBLANES, LANES), lambda b, h, i: (0, 0)),
                  pl.BlockSpec((hps, tq, LANES), lambda b, h, i: (h, b * nq + i, 0)),
                  pl.BlockSpec((hps, seq, LANES), lambda b, h, i: (n_hp + h, b, 0)),
                  pl.BlockSpec((hps, seq, LANES), lambda b, h, i: (2 * n_hp + h, b, 0)),
                  pl.BlockSpec((hps, 2, tq, tq), lambda b, h, i: (h, 0, 0, 0)),
                  pl.BlockSpec((1, LANES), lambda b, h, i: (0, 0))],
        out_specs=pl.BlockSpec((tq, hps * LANES), lambda b, h, i: (b * nq + i, h)),
        scratch_shapes=[pltpu.VMEM((hps, 2 * tq, LANES), BF16),
                        pltpu.VMEM((hps, seq, 2 * LANES), BF16),
                        pltpu.VMEM((hps, 2 * tq, LANES), F32),
                        pltpu.VMEM((hps, 2 * tq, 2 * LANES), F32)],
        compiler_params=_params(("parallel", "parallel", "arbitrary"), est),
        name="diff_attention",
    )(lam_vecs, qkv, qkv, qkv, bias_tiles, subln_g)


def _mix_kernel(x_ref, y_ref, o_ref, gate_ref, wssd_ref, wdiff_ref, wmix_ref, g_ref, b_ref, out_ref):
    d = x_ref.shape[1]
    ssd = jnp.dot(y_ref[...], wssd_ref[...], preferred_element_type=F32)
    dif = jnp.dot(o_ref[...], wdiff_ref[...], preferred_element_type=F32)
    merged = gate_ref[:, 0:d].astype(F32) * ssd + gate_ref[:, d:2 * d].astype(F32) * dif
    mixed = jnp.dot(merged.astype(BF16), wmix_ref[...], preferred_element_type=F32)
    out_ref[...] = _layer_norm(DN_ALPHA * x_ref[...] + mixed, g_ref[...], b_ref[...])


def _mix(x, y_ssd, o_diff, proj, w_ssd, w_diff, w_mix, ln_g, ln_b, *, tm, gate_block):
    t, d = x.shape
    tm = min(tm, t)
    inner = y_ssd.shape[1]
    dw = o_diff.shape[1]
    row = lambda i: (i, 0)
    est = (2 * tm * (d * 4 + inner * 2 + dw * 2 + 2 * d * 2 + d * 4)
           + (inner * d + dw * d + d * d) * 2 + 6 * tm * d * 4)
    return pl.pallas_call(
        _mix_kernel,
        out_shape=jax.ShapeDtypeStruct((t, d), F32),
        grid=(t // tm,),
        in_specs=[pl.BlockSpec((tm, d), row),
                  pl.BlockSpec((tm, inner), row),
                  pl.BlockSpec((tm, dw), row),
                  pl.BlockSpec((tm, 2 * d), lambda i: (i, gate_block)),
                  _resident((inner, d)), _resident((dw, d)), _resident((d, d)),
                  _resident((1, d)), _resident((1, d))],
        out_specs=pl.BlockSpec((tm, d), row),
        compiler_params=_params(("parallel",), est),
        name="mix_ln1",
    )(x, y_ssd, o_diff, proj, w_ssd, w_diff, w_mix, ln_g, ln_b)


def _pack_bf16_pairs(v):
    w = v.shape[1] // 2
    lo = lax.bitcast_convert_type(v[:, :w].astype(BF16).astype(F32), jnp.int32)
    hi = lax.bitcast_convert_type(v[:, w:].astype(BF16).astype(F32), jnp.int32)
    return jnp.bitwise_or(hi, lax.shift_right_logical(lo, 16))


def _unpack_bf16_pairs(words):
    lo = lax.bitcast_convert_type(lax.shift_left(words, 16), F32)
    hi = lax.bitcast_convert_type(jnp.bitwise_and(words, jnp.int32(-0x10000)), F32)
    return jnp.concatenate([lo, hi], axis=1)


def _xattn_kernel(x_ref, kv_ref, wq_ref, wo_ref, g_ref, b_ref, out_ref, packed_ref, o_scr, *, heads):
    d = x_ref.shape[1]
    dh = d // heads
    xv = x_ref[...]
    q = jnp.dot(xv.astype(BF16), wq_ref[...], preferred_element_type=F32)
    q = (q * (dh ** -0.5)).astype(BF16)
    for h in range(heads):
        kh = kv_ref[:, h * dh:(h + 1) * dh]
        vh = kv_ref[:, d + h * dh:d + (h + 1) * dh]
        s = lax.dot_general(q[:, h * dh:(h + 1) * dh], kh, (((1,), (1,)), ((), ())),
                            preferred_element_type=F32)
        p = jnp.exp(s - jnp.max(s, axis=1, keepdims=True))
        p = p / jnp.sum(p, axis=1, keepdims=True)
        o_scr[:, h * dh:(h + 1) * dh] = jnp.dot(p.astype(BF16), vh, preferred_element_type=F32).astype(BF16)
    att = jnp.dot(o_scr[...], wo_ref[...], preferred_element_type=F32)
    y = _layer_norm(DN_ALPHA * xv + att, g_ref[...], b_ref[...])
    out_ref[...] = y
    packed_ref[...] = _pack_bf16_pairs(y)


def _cross_attention(x, kv, w_cq, w_co, ln_g, ln_b, *, batch, seq, tq):
    t, d = x.shape
    mem_len = kv.shape[0] // batch
    tq = min(tq, seq)
    nq = seq // tq
    est = (2 * tq * d * 4 * 2 + 2 * tq * d * 2 + 2 * mem_len * 2 * d * 2 + 2 * d * d * 2 + tq * d * 2
           + 8 * tq * d * 4)
    row = lambda b, i: (b * nq + i, 0)
    return pl.pallas_call(
        functools.partial(_xattn_kernel, heads=MEM_HEADS),
        out_shape=(jax.ShapeDtypeStruct((t, d), F32), jax.ShapeDtypeStruct((t, d // 2), jnp.int32)),
        grid=(batch, nq),
        in_specs=[pl.BlockSpec((tq, d), row),
                  pl.BlockSpec((mem_len, 2 * d), lambda b, i: (b, 0)),
                  _resident((d, d)), _resident((d, d)), _resident((1, d)), _resident((1, d))],
        out_specs=(pl.BlockSpec((tq, d), row), pl.BlockSpec((tq, d // 2), row)),
        scratch_shapes=[pltpu.VMEM((tq, d), BF16)],
        compiler_params=_params(("parallel", "parallel"), est),
        name="cross_attention_ln2",
    )(x, kv, w_cq, w_co, ln_g, ln_b)


def _router_kernel(x_ref, wr_ref, rb_ref, idx_ref, rank_ref, gates_ref, cnt_ref, run_ref, *, n_experts):
    tm = x_ref.shape[0]
    n_groups = N_EXPERT_GROUPS
    per = n_experts // n_groups

    @pl.when(pl.program_id(0) == 0)
    def _():
        run_ref[...] = jnp.zeros_like(run_ref)

    xv = x_ref[...]
    x_hi = xv.astype(BF16)
    x_lo = (xv - x_hi.astype(F32)).astype(BF16)
    nt = (((1,), (1,)), ((), ()))
    logits = (lax.dot_general(wr_ref[0], x_hi, nt, preferred_element_type=F32)
              + lax.dot_general(wr_ref[0], x_lo, nt, preferred_element_type=F32)
              + lax.dot_general(wr_ref[1], x_hi, nt, preferred_element_type=F32))
    sc = _sigmoid(logits[0:n_experts]).reshape(per, n_groups, tm)
    choice = sc + rb_ref[0:n_experts].reshape(per, n_groups, 1)
    neg = -jnp.inf
    member = lax.broadcasted_iota(jnp.int32, (per, n_groups, tm), 0)
    group3 = lax.broadcasted_iota(jnp.int32, (per, n_groups, tm), 1)
    m1 = jnp.max(choice, axis=0, keepdims=True)
    i1 = jnp.min(jnp.where(choice == m1, member, per), axis=0, keepdims=True)
    m2 = jnp.max(jnp.where(member == i1, neg, choice), axis=0, keepdims=True)
    gscore = (m1 + m2)
    gsel = jnp.zeros((1, n_groups, tm), F32)
    gidx = lax.broadcasted_iota(jnp.int32, (1, n_groups, tm), 1)
    cur = gscore
    for _ in range(TOP_GROUPS):
        mx = jnp.max(cur, axis=1, keepdims=True)
        ix = jnp.min(jnp.where(cur == mx, gidx, n_groups), axis=1, keepdims=True)
        hit = gidx == ix
        gsel = jnp.where(hit, 1.0, gsel)
        cur = jnp.where(hit, neg, cur)
    cur = jnp.where(gsel > 0.0, choice, neg)
    eidx = group3 * per + member
    esel = jnp.zeros((per, n_groups, tm), F32)
    hits, idx_rows = [], []
    for _ in range(TOP_K):
        mx = jnp.max(jnp.max(cur, axis=0, keepdims=True), axis=1, keepdims=True)
        ix = jnp.min(jnp.min(jnp.where(cur == mx, eidx, n_experts), axis=0, keepdims=True),
                     axis=1, keepdims=True)
        hit = eidx == ix
        esel = jnp.where(hit, 1.0, esel)
        cur = jnp.where(hit, neg, cur)
        hits.append(hit)
        idx_rows.append(ix.reshape(1, tm))
    w = esel * sc
    tot = jnp.sum(jnp.sum(w, axis=0, keepdims=True), axis=1, keepdims=True)
    gw = w / tot * ROUTED_SCALE

    r_i = lax.broadcasted_iota(jnp.int32, (tm, tm), 0)
    c_i = lax.broadcasted_iota(jnp.int32, (tm, tm), 1)
    upper = jnp.where(r_i <= c_i, 1.0, 0.0).astype(BF16)
    pref = jnp.dot(esel.reshape(n_experts, tm).astype(BF16), upper, preferred_element_type=F32)
    run = run_ref[...]
    rank3 = (jnp.concatenate([run] * (tm // LANES), axis=1) + pref - 1.0).reshape(per, n_groups, tm)

    def pick(hit, vals):
        return jnp.sum(jnp.sum(jnp.where(hit, vals, 0.0), axis=0, keepdims=True), axis=1).reshape(1, tm)

    idx_ref[...] = jnp.concatenate(idx_rows, axis=0)
    rank_ref[...] = jnp.concatenate([pick(h, rank3) for h in hits], axis=0).astype(jnp.int32)
    wk = jnp.concatenate([pick(h, gw) for h in hits] + [jnp.zeros((LANES - TOP_K, tm), F32)], axis=0)
    gates_ref[...] = wk.T
    run = run + jnp.broadcast_to(pref[:, tm - 1:tm], run.shape)
    run_ref[...] = run
    cnt_ref[...] = run


def _router(x, w_router, router_bias, *, tm, row0, t):
    d = x.shape[1]
    n_experts = w_router.shape[1]
    per = n_experts // N_EXPERT_GROUPS
    tm = min(tm, t)
    blk0 = row0 // tm

    def member_major(v):
        return v.reshape(N_EXPERT_GROUPS, per, *v.shape[1:]).swapaxes(0, 1).reshape(v.shape)

    wt = jnp.pad(member_major(w_router.astype(F32).T), ((0, LANES - n_experts), (0, 0)))
    w_hi = wt.astype(BF16)
    wr = jnp.stack([w_hi, (wt - w_hi.astype(F32)).astype(BF16)])
    rb = jnp.pad(member_major(router_bias.astype(F32)), (0, LANES - n_experts)).reshape(LANES, 1)
    est = (2 * tm * d * 4 + 2 * LANES * d * 2 + 2 * tm * LANES * 4 + 60 * n_experts * tm * 4 + 3 * tm * tm * 4)
    pick_spec = pl.BlockSpec((TOP_K, tm), lambda i: (0, i))
    idx_t, rank_t, gates, counts = pl.pallas_call(
        functools.partial(_router_kernel, n_experts=n_experts),
        out_shape=(jax.ShapeDtypeStruct((TOP_K, t), jnp.int32),
                   jax.ShapeDtypeStruct((TOP_K, t), jnp.int32),
                   jax.ShapeDtypeStruct((t, LANES), F32),
                   jax.ShapeDtypeStruct((n_experts, LANES), F32)),
        grid=(t // tm,),
        in_specs=[pl.BlockSpec((tm, d), lambda i: (blk0 + i, 0)),
                  _resident((2, LANES, d)), _resident((LANES, 1))],
        out_specs=(pick_spec, pick_spec,
                   pl.BlockSpec((tm, LANES), lambda i: (i, 0)),
                   pl.BlockSpec((n_experts, LANES), lambda i: (0, 0))),
        scratch_shapes=[pltpu.VMEM((n_experts, LANES), F32)],
        compiler_params=_params(("arbitrary",), est),
        name="router",
    )(x, wr, rb)
    counts = counts[:, 0].astype(jnp.int32).reshape(per, N_EXPERT_GROUPS).T.reshape(n_experts)
    return idx_t, rank_t, gates, counts


def _positions_kernel(cnt_ref, idx_ref, rank_ref, pos_ref, texp_ref, off_ref, *, n_experts, rows, n_tiles):
    @pl.when(pl.program_id(0) == 0)
    def _():
        def per_expert(e, toff):
            off_ref[e] = toff * rows
            nt = lax.shift_right_logical(cnt_ref[e] + (rows - 1), int(math.log2(rows)))

            def fill(j, c):
                texp_ref[toff + j] = e
                return c

            lax.fori_loop(0, nt, fill, 0)
            return toff + nt

        n_used = lax.fori_loop(0, n_experts, per_expert, 0)

        def tail(j, c):
            texp_ref[j] = n_experts - 1
            return c

        lax.fori_loop(n_used, n_tiles, tail, 0)
        texp_ref[n_tiles] = n_used

    idx = idx_ref[...]
    pos = rank_ref[...]
    for e in range(n_experts):
        pos = pos + jnp.where(idx == e, off_ref[e], 0)
    pos_ref[...] = pos


def _positions(counts, idx_t, rank_t, *, rows, n_tiles, tm):
    n_experts = counts.shape[0]
    k, t = idx_t.shape
    tm = min(tm, t)
    spec = pl.BlockSpec((k, tm), lambda i: (0, i))
    kern = functools.partial(_positions_kernel, n_experts=n_experts, rows=rows, n_tiles=n_tiles)
    return pl.pallas_call(
        kern,
        out_shape=(jax.ShapeDtypeStruct((k, t), jnp.int32),
                   jax.ShapeDtypeStruct((n_tiles + 1,), jnp.int32)),
        grid=(t // tm,),
        in_specs=[pl.BlockSpec(memory_space=pltpu.SMEM), spec, spec],
        out_specs=(spec, pl.BlockSpec(memory_space=pltpu.SMEM)),
        scratch_shapes=[pltpu.SMEM((n_experts,), jnp.int32)],
        compiler_params=_params(("arbitrary",), 16 * k * tm * 4),
        name="moe_positions",
    )(counts, idx_t, rank_t)


def _sc_mesh():
    return plsc.VectorSubcoreMesh(core_axis_name="c", subcore_axis_name="s",
                                  num_cores=SC_CORES, num_subcores=SC_SUBCORES)


def _sc_dispatch(packed, pos_chunks, n_rows, row0):
    w = packed.shape[1]
    n_chunks, k, n = pos_chunks.shape
    per_worker = n_chunks // (SC_CORES * SC_SUBCORES)

    @functools.partial(
        pl.kernel, mesh=_sc_mesh(),
        out_type=jax.ShapeDtypeStruct((n_rows, w), packed.dtype),
        scratch_types=[pltpu.VMEM((k, n), jnp.int32), pltpu.VMEM((n, w), packed.dtype),
                       pltpu.SemaphoreType.DMA],
        name="moe_dispatch_sc",
    )
    def scatter_rows(x_hbm, pos_hbm, out_hbm, idx_v, rows_v, sem):
        wid = lax.axis_index("s") * SC_CORES + lax.axis_index("c")

        @pl.loop(0, per_worker)
        def _(step):
            c = wid * per_worker + step
            pltpu.sync_copy(pos_hbm.at[c], idx_v)
            pltpu.sync_copy(x_hbm.at[pl.ds(row0 + c * n, n)], rows_v)
            copies = [pltpu.async_copy(rows_v, out_hbm.at[idx_v.at[kk]], sem) for kk in range(k)]
            for cp in copies:
                cp.wait()

    return scatter_rows(packed, pos_chunks)


def _sc_combine(sorted_rows, pos_chunks):
    _, w = sorted_rows.shape
    n_chunks, k, n = pos_chunks.shape
    per_worker = n_chunks // (SC_CORES * SC_SUBCORES)

    @functools.partial(
        pl.kernel, mesh=_sc_mesh(),
        out_type=jax.ShapeDtypeStruct((k, n_chunks * n, w), sorted_rows.dtype),
        scratch_types=[pltpu.VMEM((k, n), jnp.int32), pltpu.VMEM((n, w), sorted_rows.dtype),
                       pltpu.SemaphoreType.DMA],
        name="moe_combine_sc",
    )
    def gather_rows(y_hbm, pos_hbm, out_hbm, idx_v, rows_v, sem):
        wid = lax.axis_index("s") * SC_CORES + lax.axis_index("c")

        @pl.loop(0, per_worker)
        def _(step):
            c = wid * per_worker + step
            pltpu.sync_copy(pos_hbm.at[c], idx_v)
            for kk in range(k):
                pltpu.async_copy(y_hbm.at[idx_v.at[kk]], rows_v, sem).wait()
                pltpu.sync_copy(rows_v, out_hbm.at[kk, pl.ds(c * n, n)])

    return gather_rows(sorted_rows, pos_chunks)


def _expert_kernel(texp_ref, xs_ref, wgu_ref, wdn_ref, ys_ref, *, n_tiles):
    ff = wdn_ref.shape[1]

    @pl.when(pl.program_id(0) < texp_ref[n_tiles])
    def _():
        xv = _unpack_bf16_pairs(xs_ref[...]).astype(BF16)
        gu = jnp.dot(xv, wgu_ref[0], preferred_element_type=F32)
        hid = _silu(gu[:, 0:ff]) * gu[:, ff:2 * ff]
        ys_ref[...] = _pack_bf16_pairs(jnp.dot(hid.astype(BF16), wdn_ref[0], preferred_element_type=F32))


def _experts(tile_expert, sorted_rows, w_gu, w_dn, *, rows):
    n_rows, w = sorted_rows.shape
    n_tiles = n_rows // rows
    _, d, ff2 = w_gu.shape
    ff = ff2 // 2
    tile = lambda j, te: (jnp.minimum(j, te[n_tiles] - 1), 0)
    est = 4 * rows * w * 4 + 2 * (d * ff2 + ff * d) * 2 + 4 * rows * d * 4 + 4 * rows * ff2 * 4
    return pl.pallas_call(
        functools.partial(_expert_kernel, n_tiles=n_tiles),
        out_shape=jax.ShapeDtypeStruct((n_rows, w), sorted_rows.dtype),
        grid_spec=pltpu.PrefetchScalarGridSpec(
            num_scalar_prefetch=1,
            grid=(n_tiles,),
            in_specs=[pl.BlockSpec((rows, w), tile),
                      pl.BlockSpec((1, d, ff2), lambda j, te: (te[j], 0, 0)),
                      pl.BlockSpec((1, ff, d), lambda j, te: (te[j], 0, 0))],
            out_specs=pl.BlockSpec((rows, w), tile)),
        compiler_params=_params(("arbitrary",), est),
        name="moe_experts",
    )(tile_expert, sorted_rows, w_gu, w_dn)


def _moe_out_kernel(x_ref, yk_ref, gates_ref, wsgu_ref, wsdn_ref, g_ref, b_ref, *rest):
    out_ref = rest[-1]
    ff = wsdn_ref.shape[0]
    xv = x_ref[...]
    gu = jnp.dot(xv.astype(BF16), wsgu_ref[...], preferred_element_type=F32)
    hid = _silu(gu[:, 0:ff]) * gu[:, ff:2 * ff]
    acc = jnp.dot(hid.astype(BF16), wsdn_ref[...], preferred_element_type=F32)
    for k in range(yk_ref.shape[0]):
        acc = acc + gates_ref[:, k:k + 1] * _unpack_bf16_pairs(yk_ref[k])
    out_ref[...] = _layer_norm(DN_ALPHA * xv + acc, g_ref[...], b_ref[...])


def _moe_out(x, yk, gates, w_sgu, w_sdn, ln_g, ln_b, *, tm, row0, prev):
    t_all, d = x.shape
    k, t, w = yk.shape
    sff2 = w_sgu.shape[1]
    tm = min(tm, t)
    blk0 = row0 // tm
    est = (2 * tm * d * 4 * 2 + 2 * k * tm * w * 4 + 2 * tm * LANES * 4
           + (d * sff2 + (sff2 // 2) * d) * 2 + 6 * tm * d * 4)
    in_specs = [pl.BlockSpec((tm, d), lambda i: (blk0 + i, 0)),
                pl.BlockSpec((k, tm, w), lambda i: (0, i, 0)),
                pl.BlockSpec((tm, LANES), lambda i: (i, 0)),
                _resident((d, sff2)), _resident((sff2 // 2, d)),
                _resident((1, d)), _resident((1, d))]
    args = [x, yk, gates, w_sgu, w_sdn, ln_g, ln_b]
    aliases = {}
    if prev is not None:
        in_specs.append(pl.BlockSpec(memory_space=pl.ANY))
        args.append(prev)
        aliases = {len(args) - 1: 0}
    return pl.pallas_call(
        _moe_out_kernel,
        out_shape=jax.ShapeDtypeStruct((t_all, d), F32),
        grid=(t // tm,),
        in_specs=in_specs,
        out_specs=pl.BlockSpec((tm, d), lambda i: (blk0 + i, 0)),
        input_output_aliases=aliases,
        compiler_params=_params(("parallel",), est),
        name="moe_out_ln3",
    )(*args)


def _moe(x, packed, w_router, router_bias, w_gu, w_dn, w_sgu, w_sdn, ln_g, ln_b):
    t_all, d = x.shape
    n_experts = w_router.shape[1]
    n = SC_CHUNK_TOKENS
    splits = MOE_SPLITS if t_all % (MOE_SPLITS * n * SC_CORES * SC_SUBCORES) == 0 else 1
    t = t_all // splits
    rows = min(MOE_ROWS_PER_TILE, t)
    n_tiles = (t * TOP_K) // rows + n_experts
    parts = []
    for s in range(splits):
        row0 = s * t
        idx_t, rank_t, gates, counts = _router(x, w_router, router_bias, tm=ROUTER_ROWS, row0=row0, t=t)
        pos_t, tile_expert = _positions(counts, idx_t, rank_t, rows=rows, n_tiles=n_tiles,
                                        tm=POSITIONS_TOKENS)
        pos_chunks = pos_t.reshape(TOP_K, t // n, n).transpose(1, 0, 2)
        sorted_x = _sc_dispatch(packed, pos_chunks, n_tiles * rows, row0)
        parts.append((row0, gates, tile_expert, pos_chunks, sorted_x))
    out = None
    for row0, gates, tile_expert, pos_chunks, sorted_x in parts:
        sorted_y = _experts(tile_expert, sorted_x, w_gu, w_dn, rows=rows)
        yk = _sc_combine(sorted_y, pos_chunks)
        out = _moe_out(x, yk, gates, w_sgu, w_sdn, ln_g, ln_b, tm=MOE_OUT_ROWS, row0=row0, prev=out)
    return out


def kernel(x, mem, w_in, conv_w, conv_b, dt_bias, a_log, d_skip, ssd_norm_g, lambda_q1, lambda_k1, lambda_q2, lambda_k2, subln_g, rel_bias, w_ssd_br, w_diff_br, w_mix_out, ln1_g, ln1_b, w_cq, w_ckv, w_co, ln2_g, ln2_b, w_router, router_bias, w_exp_gu, w_exp_down, w_sh_gu, w_sh_down, ln3_g, ln3_b):
    batch, seq, d = x.shape
    depth = w_in.shape[0]
    inner = w_ssd_br.shape[1]
    xbc_width = conv_w.shape[2]
    heads = dt_bias.shape[1]
    diff_width = w_diff_br.shape[1]
    o_z, o_xbc = inner, inner + xbc_width
    o_dt = o_xbc + heads
    o_v = o_dt + 3 * diff_width
    t = batch * seq
    tq_attn = min(ATTN_TILE, seq)

    def vec(v):
        return v.astype(F32).reshape(1, -1)

    xt = x.reshape(t, d)
    memt = mem.reshape(-1, d)
    bias_tiles = _bias_tiles(rel_bias, tq_attn)

    for l in range(depth):
        wl = w_in[l]
        w_big = jnp.concatenate([wl[:, o_z:o_xbc], wl[:, o_dt:o_v], wl[:, :o_z], wl[:, o_v:]],
                                axis=1).astype(BF16)
        assert 2 * d == inner
        w_dt = jnp.pad(wl[:, o_xbc:o_dt], ((0, 0), (0, LANES - heads))).astype(BF16)
        xbc, qkv, zg, dt_raw = _in_proj(xt, w_big, w_dt, n_xbc_cols=xbc_width, n_qkv_cols=3 * diff_width,
                                        n_z_cols=inner, tm=PROJ_ROWS, tn=PROJ_COLS)

        y_ssd = _ssd_branch(zg, xbc, dt_raw, conv_w[l], conv_b[l], dt_bias[l], a_log[l], d_skip[l],
                            ssd_norm_g[l], batch=batch, seq=seq)

        lam_vecs = jnp.pad(jnp.stack([lambda_q1[l], lambda_k1[l], lambda_q2[l], lambda_k2[l]]).astype(F32),
                           ((0, SUBLANES - 4), (0, LANES - lambda_q1.shape[1])))
        o_diff = _diff_attention(qkv, lam_vecs, bias_tiles, vec(subln_g[l]), batch=batch, seq=seq,
                                 tq=tq_attn, layer_idx=l)

        xt = _mix(xt, y_ssd, o_diff, zg, w_ssd_br[l].astype(BF16), w_diff_br[l].astype(BF16),
                  w_mix_out[l].astype(BF16), vec(ln1_g[l]), vec(ln1_b[l]), tm=MIX_ROWS, gate_block=1)

        kv = _matmul(memt, w_ckv[l].astype(BF16), BF16, tm=PROJ_ROWS, tn=PROJ_COLS, name="mem_kv_proj")
        xt, packed = _cross_attention(xt, kv, w_cq[l].astype(BF16), w_co[l].astype(BF16), vec(ln2_g[l]),
                                      vec(ln2_b[l]), batch=batch, seq=seq, tq=XATTN_ROWS)

        xt = _moe(xt, packed, w_router[l], router_bias[l], w_exp_gu[l].astype(BF16),
                  w_exp_down[l].astype(BF16), w_sh_gu[l].astype(BF16), w_sh_down[l].astype(BF16),
                  vec(ln3_g[l]), vec(ln3_b[l]))
    return xt.reshape(batch, seq, d)
```

```python
import functools
import math

import numpy as np
import jax
import jax.numpy as jnp
from jax import lax
from jax.experimental import pallas as pl
from jax.experimental.pallas import tpu as pltpu
from jax.experimental.pallas import tpu_sc as plsc

F32 = jnp.float32
BF16 = jnp.bfloat16

SSD_HEAD_DIM = 64
SSD_GROUPS = 4
SSD_STATE = 128
SSD_CONV = 4
SSD_CHUNK = 128
DIFF_HEADS = 8
REL_BUCKETS = 32
REL_MAX_DIST = 128
MEM_HEADS = 4
TOP_K = 8
N_EXPERT_GROUPS = 8
TOP_GROUPS = 4
ROUTED_SCALE = 2.5
NORM_EPS = 1e-5
DEPTH = 1
DN_ALPHA = (2.0 * DEPTH) ** 0.25

LANES = 128
SUBLANES = 8
VMEM_CAP_BYTES = 64 * 1024 * 1024
MASK_VALUE = -1e30
LOG2E = math.log2(math.e)
PROJ_ROWS, PROJ_COLS = 1024, 1024
IN_PROJ_SLAB = 256
SSD_CHUNKS_PER_STEP = 2
ATTN_TILE = 512
ATTN_PV_ROWS = 512
ATTN_HEADS_PER_STEP = 2
MIX_ROWS = 512
XATTN_ROWS = 1024
ROUTER_ROWS = 512
POSITIONS_TOKENS = 2048
MOE_OUT_ROWS = 512
SC_CORES = 2
SC_SUBCORES = 16
SC_CHUNK_TOKENS = 128
MOE_SPLITS = 2
MOE_ROWS_PER_TILE = 1024


def _vmem_limit(estimate_bytes):
    return int(min(estimate_bytes * 5 // 4 + (4 << 20), VMEM_CAP_BYTES - (6 << 20)))


def _params(semantics, vmem_estimate):
    return pltpu.CompilerParams(dimension_semantics=semantics,
                                vmem_limit_bytes=_vmem_limit(vmem_estimate))


def _resident(shape):
    nd = len(shape)
    return pl.BlockSpec(shape, lambda *_: (0,) * nd, pipeline_mode=pl.Buffered(1))


def _layer_norm(v, g, b):
    mu = jnp.mean(v, axis=-1, keepdims=True)
    d = v - mu
    var = jnp.mean(d * d, axis=-1, keepdims=True)
    return d * lax.rsqrt(var + NORM_EPS) * g + b


def _sigmoid(v):
    return 0.5 * jnp.tanh(0.5 * v) + 0.5


def _silu(v):
    h = 0.5 * v
    return h + h * jnp.tanh(h)


def _matmul_kernel(x_ref, w_ref, o_ref, xb_ref):
    @pl.when(pl.program_id(1) == 0)
    def _():
        xb_ref[...] = x_ref[...].astype(BF16)

    o_ref[...] = jnp.dot(xb_ref[...], w_ref[...], preferred_element_type=F32).astype(o_ref.dtype)


def _matmul(x, w, out_dtype, *, tm, tn, name):
    m, k = x.shape
    n = w.shape[1]
    tm, tn = min(tm, m), min(tn, n)
    est = (2 * tm * k * x.dtype.itemsize + tm * k * 2 + 2 * k * tn * 2
           + 2 * tm * tn * jnp.dtype(out_dtype).itemsize + tm * tn * 4)
    return pl.pallas_call(
        _matmul_kernel,
        out_shape=jax.ShapeDtypeStruct((m, n), out_dtype),
        grid=(m // tm, n // tn),
        in_specs=[pl.BlockSpec((tm, k), lambda i, j: (i, 0)),
                  pl.BlockSpec((k, tn), lambda i, j: (0, j))],
        out_specs=pl.BlockSpec((tm, tn), lambda i, j: (i, j)),
        scratch_shapes=[pltpu.VMEM((tm, k), BF16)],
        compiler_params=_params(("parallel", "arbitrary"), est),
        name=name,
    )(x, w)


def _in_proj_kernel(x_ref, w_ref, wdt_ref, xbc_ref, qkv_ref, zg_ref, dt_ref, xb_ref, *, n_xbc, n_qkv, n_z):
    j = pl.program_id(1)
    j_z = n_xbc + n_qkv

    @pl.when(j == 0)
    def _():
        xb = x_ref[...].astype(BF16)
        xb_ref[...] = xb
        dt_ref[...] = jnp.dot(xb, wdt_ref[...], preferred_element_type=F32)

    tn = w_ref.shape[1]
    slabs = [slice(c0, c0 + IN_PROJ_SLAB) for c0 in range(0, tn, IN_PROJ_SLAB)]

    def product(cs):
        return jnp.dot(xb_ref[...], w_ref[:, cs], preferred_element_type=F32)

    @pl.when(j < n_xbc)
    def _():
        for cs in slabs:
            xbc_ref[:, cs] = product(cs).astype(BF16)

    @pl.when(jnp.logical_and(j >= n_xbc, j < j_z))
    def _():
        for cs in slabs:
            acc = product(cs)
            for hh in range(IN_PROJ_SLAB // LANES):
                qkv_ref[cs.start // LANES + hh] = acc[:, hh * LANES:(hh + 1) * LANES].astype(BF16)

    @pl.when(jnp.logical_and(j >= j_z, j < j_z + n_z))
    def _():
        for cs in slabs:
            zg_ref[:, cs] = product(cs).astype(BF16)

    @pl.when(j >= j_z + n_z)
    def _():
        for cs in slabs:
            zg_ref[:, cs] = _sigmoid(product(cs)).astype(BF16)


def _in_proj(x, w, w_dt, *, n_xbc_cols, n_qkv_cols, n_z_cols, tm, tn):
    m, k = x.shape
    n = w.shape[1]
    tm = min(tm, m)
    n_xbc, n_qkv, n_z = n_xbc_cols // tn, n_qkv_cols // tn, n_z_cols // tn
    n_zg = n // tn - n_xbc - n_qkv
    per = tn // LANES
    est = (2 * tm * k * 4 + tm * k * 2 + 2 * k * tn * 2 + 3 * 2 * tm * tn * 2 + 2 * tm * tn * 4
           + k * LANES * 2 + 2 * tm * LANES * 4)
    kern = functools.partial(_in_proj_kernel, n_xbc=n_xbc, n_qkv=n_qkv, n_z=n_z)
    return pl.pallas_call(
        kern,
        out_shape=(jax.ShapeDtypeStruct((m, n_xbc_cols), BF16),
                   jax.ShapeDtypeStruct((n_qkv_cols // LANES, m, LANES), BF16),
                   jax.ShapeDtypeStruct((m, n_zg * tn), BF16),
                   jax.ShapeDtypeStruct((m, LANES), F32)),
        grid=(m // tm, n // tn),
        in_specs=[pl.BlockSpec((tm, k), lambda i, j: (i, 0)),
                  pl.BlockSpec((k, tn), lambda i, j: (0, j)),
                  _resident((k, LANES))],
        out_specs=(pl.BlockSpec((tm, tn), lambda i, j: (i, jnp.minimum(j, n_xbc - 1))),
                   pl.BlockSpec((per, tm, LANES), lambda i, j: (jnp.clip(j - n_xbc, 0, n_qkv - 1), i, 0)),
                   pl.BlockSpec((tm, tn), lambda i, j: (i, jnp.clip(j - n_xbc - n_qkv, 0, n_zg - 1))),
                   pl.BlockSpec((tm, LANES), lambda i, j: (i, 0))),
        scratch_shapes=[pltpu.VMEM((tm, k), BF16)],
        compiler_params=_params(("parallel", "arbitrary"), est),
        name="in_proj",
    )(x, w, w_dt)


def _split3(v):
    hi = v.astype(BF16)
    r1 = v - hi.astype(F32)
    mid = r1.astype(BF16)
    lo = (r1 - mid.astype(F32)).astype(BF16)
    return hi, mid, lo


def _ssd_kernel(z_ref, xbc_ref, dt_ref, convw_ref, convb_ref, dtb_ref, alog_ref,
                dskip_ref, g_ref, y_ref, tail_ref, state_ref, xa_ref, yacc_ref, **static):
    @pl.when(pl.program_id(1) == 0)
    def _():
        tail_ref[...] = jnp.zeros_like(tail_ref)
        state_ref[...] = jnp.zeros_like(state_ref)

    for r0 in range(0, z_ref.shape[0], SSD_CHUNK):
        rows = pl.ds(r0, SSD_CHUNK)
        _ssd_chunk(z_ref.at[rows], xbc_ref.at[rows], dt_ref.at[rows], convw_ref, convb_ref, dtb_ref,
                   alog_ref, dskip_ref, g_ref, y_ref.at[rows], tail_ref, state_ref, xa_ref, yacc_ref,
                   **static)


def _ssd_chunk(z_ref, xbc_ref, dt_ref, convw_ref, convb_ref, dtb_ref, alog_ref,
               dskip_ref, g_ref, y_ref, tail_ref, state_ref, xa_ref, yacc_ref,
               *, inner, n_groups, d_state):
    L = SSD_CHUNK
    width = xbc_ref.shape[1]
    n_pairs = inner // LANES
    pairs_per_group = n_pairs // n_groups
    group_width = inner // n_groups

    slab = 512
    n_t = L // SUBLANES
    sub = lax.broadcasted_iota(jnp.int32, (n_t, SUBLANES, slab), 1)
    for c0 in range(0, width, slab):
        cs = slice(c0, c0 + slab)
        u = xbc_ref[:, cs].astype(F32)
        tiles = jnp.concatenate([tail_ref[:, cs], u], axis=0).reshape(n_t + 1, SUBLANES, slab)
        acc = convb_ref[:, cs] + u * convw_ref[SSD_CONV - 1:SSD_CONV, cs]
        for j in range(1, SSD_CONV):
            rot = pltpu.roll(tiles, j, axis=1)
            shifted = jnp.where(sub < j, rot[0:n_t], rot[1:n_t + 1]).reshape(L, slab)
            acc = acc + shifted * convw_ref[SSD_CONV - 1 - j:SSD_CONV - j, cs]
        tail_ref[:, cs] = u[L - SUBLANES:L]
        xa_ref[:, cs] = _silu(acc)

    dtr = dt_ref[...] + dtb_ref[...]
    dt = jnp.maximum(dtr, 0.0) + jnp.log(1.0 + jnp.exp(-jnp.abs(dtr)))
    a = -jnp.exp(alog_ref[...]) * dt
    row_i = lax.broadcasted_iota(jnp.int32, (L, L), 0)
    col_i = lax.broadcasted_iota(jnp.int32, (L, L), 1)
    causal = row_i >= col_i
    tril = jnp.where(causal, 1.0, 0.0).astype(BF16)
    acs = sum(jnp.dot(tril, part, preferred_element_type=F32) for part in _split3(a)) * LOG2E
    acs_t = acs.T
    dt_t = dt.T
    acs_dt_t = acs_t - jnp.log2(dt_t)
    lane_lo = lax.broadcasted_iota(jnp.int32, (L, LANES), 1) < SSD_HEAD_DIM

    b0 = inner
    c0 = inner + n_groups * d_state
    for g in range(n_groups):
        bg = xa_ref[:, b0 + g * d_state:b0 + (g + 1) * d_state]
        cg = xa_ref[:, c0 + g * d_state:c0 + (g + 1) * d_state]
        cg_b = cg.astype(BF16)
        cb_b = lax.dot_general(cg_b, bg.astype(BF16), (((1,), (1,)), ((), ())),
                               preferred_element_type=F32).astype(BF16)
        bgt_b = bg.T.astype(BF16)
        for pp in range(g * pairs_per_group, (g + 1) * pairs_per_group):
            xs_pair = xa_ref[:, pp * LANES:(pp + 1) * LANES].astype(BF16)
            st_old = state_ref[pp]
            rhs = jnp.concatenate([xs_pair, st_old.astype(BF16)], axis=0)
            lhs, b_ws, keep = [], [], []
            for side in range(2):
                h = 2 * pp + side
                col = jnp.broadcast_to(acs[:, h:h + 1], (L, L))
                row = acs_t[h:h + 1, :]
                dt_row = dt_t[h:h + 1, :]
                last = acs[L - 1:L, h:h + 1]
                dec_dt = jnp.exp2(jnp.where(causal, col - acs_dt_t[h:h + 1, :], MASK_VALUE))
                m_in = cb_b * dec_dt.astype(BF16)
                c_w = cg_b * jnp.exp2(col[:, :d_state]).astype(BF16)
                lhs.append(jnp.concatenate([m_in, c_w], axis=1))
                b_ws.append(bgt_b * (jnp.exp2(last - row) * dt_row).astype(BF16))
                keep.append(jnp.exp2(last))
            y2 = jnp.dot(jnp.concatenate(lhs, axis=0), rhs, preferred_element_type=F32)
            s2 = jnp.dot(jnp.concatenate(b_ws, axis=0), xs_pair, preferred_element_type=F32)
            yacc_ref[:, pp * LANES:(pp + 1) * LANES] = jnp.where(lane_lo, y2[0:L], y2[L:2 * L])
            keep_pair = jnp.where(lane_lo[0:1, :], keep[0], keep[1])
            state_ref[pp] = st_old * keep_pair + jnp.where(lane_lo[0:d_state], s2[0:d_state],
                                                           s2[d_state:2 * d_state])

    for g in range(n_groups):
        cs = slice(g * group_width, (g + 1) * group_width)
        zz = z_ref[:, cs].astype(F32)
        yv = (yacc_ref[:, cs] + dskip_ref[:, cs] * xa_ref[:, cs]) * _silu(zz)
        ms = jnp.mean(yv * yv, axis=-1, keepdims=True)
        y_ref[:, cs] = (yv * lax.rsqrt(ms + NORM_EPS) * g_ref[:, cs]).astype(y_ref.dtype)


def _ssd_branch(zg, xbc, dt_raw, conv_w, conv_b, dt_bias, a_log, d_skip, norm_g, *, batch, seq):
    t = zg.shape[0]
    inner = norm_g.shape[0]
    width = conv_w.shape[1]
    heads = inner // SSD_HEAD_DIM
    n_chunks = seq // SSD_CHUNK
    L = SSD_CHUNK

    def pad_heads(v):
        return jnp.pad(v.astype(F32), (0, LANES - heads)).reshape(1, LANES)

    convw = jnp.pad(conv_w.astype(F32), ((0, SUBLANES - SSD_CONV), (0, 0)))
    dskip = jnp.repeat(d_skip.astype(F32), SSD_HEAD_DIM).reshape(1, inner)
    cps = SSD_CHUNKS_PER_STEP if n_chunks % SSD_CHUNKS_PER_STEP == 0 else 1
    n_steps = n_chunks // cps
    rows = cps * L
    row = lambda b, c: (b * n_steps + c, 0)
    const = lambda b, c: (0, 0)
    est = (2 * rows * (inner + width) * 2 + 2 * rows * LANES * 4 + 2 * rows * inner * 2
           + (inner // LANES) * SSD_STATE * LANES * 4 + L * (2 * width + inner) * 4 + (8 << 20))
    kern = functools.partial(_ssd_kernel, inner=inner, n_groups=SSD_GROUPS, d_state=SSD_STATE)
    return pl.pallas_call(
        kern,
        out_shape=jax.ShapeDtypeStruct((t, inner), BF16),
        grid=(batch, n_steps),
        in_specs=[pl.BlockSpec((rows, inner), row),
                  pl.BlockSpec((rows, width), row),
                  pl.BlockSpec((rows, LANES), row),
                  pl.BlockSpec((SUBLANES, width), const),
                  pl.BlockSpec((1, width), const),
                  pl.BlockSpec((1, LANES), const),
                  pl.BlockSpec((1, LANES), const),
                  pl.BlockSpec((1, inner), const),
                  pl.BlockSpec((1, inner), const)],
        out_specs=pl.BlockSpec((rows, inner), row),
        scratch_shapes=[pltpu.VMEM((SUBLANES, width), F32),
                        pltpu.VMEM((inner // LANES, SSD_STATE, LANES), F32),
                        pltpu.VMEM((L, width), F32),
                        pltpu.VMEM((L, inner), F32)],
        compiler_params=_params(("parallel", "arbitrary"), est),
        name="ssd_scan",
    )(zg, xbc, dt_raw, convw, conv_b.astype(F32).reshape(1, width), pad_heads(dt_bias),
      pad_heads(a_log), dskip, norm_g.astype(F32).reshape(1, inner))


def _bucket_tiles(tq):
    max_exact = REL_BUCKETS // 2
    qi = np.arange(tq)[:, None]
    ki = np.arange(tq)[None, :]

    def bucket(dist):
        d = np.maximum(dist, 1).astype(np.float32)
        large = max_exact + (np.log(d / np.float32(max_exact)) / np.float32(math.log(REL_MAX_DIST / max_exact))
                             * np.float32(REL_BUCKETS - max_exact)).astype(np.int32)
        large = np.minimum(large, REL_BUCKETS - 1)
        return np.where(dist < max_exact, dist, large).astype(np.int32)

    diag = np.where(qi >= ki, bucket(np.maximum(qi - ki, 0)), -1)
    prev = bucket(tq + qi - ki)
    far = bucket(np.arange(tq + 1, 1 << 16))
    assert (far == REL_BUCKETS - 1).all()
    return np.stack([diag, prev]).astype(np.int32)


def _bias_kernel(rb_ref, bucket_ref, o_ref):
    h = pl.program_id(0)
    bk = bucket_ref[...]
    acc = jnp.zeros(bk.shape, F32)
    for b in range(REL_BUCKETS):
        acc = jnp.where(bk == b, rb_ref[b, h], acc)
    o_ref[0] = jnp.where(bk < 0, MASK_VALUE, (acc - rb_ref[REL_BUCKETS - 1, h]) * LOG2E)


def _bias_tiles(rel_bias, tq):
    buckets = jnp.asarray(_bucket_tiles(tq))
    return pl.pallas_call(
        _bias_kernel,
        out_shape=jax.ShapeDtypeStruct((DIFF_HEADS, 2, tq, tq), F32),
        grid=(DIFF_HEADS,),
        in_specs=[pl.BlockSpec(memory_space=pltpu.SMEM),
                  pl.BlockSpec((2, tq, tq), lambda h: (0, 0, 0))],
        out_specs=pl.BlockSpec((1, 2, tq, tq), lambda h: (h, 0, 0, 0)),
        compiler_params=_params(("arbitrary",), 8 * tq * tq * 4),
        name="t5_bias_tiles",
    )(rel_bias.astype(F32), buckets)


def _attn_kernel(lam_ref, q_ref, k_ref, v_ref, bias_ref, g_ref, o_ref,
                 q2_ref, vaug_ref, m_ref, acc_ref, *, tq, rows, lam_init):
    i = pl.program_id(2)
    dh = LANES // 2
    n_heads = q_ref.shape[0]

    @pl.when(i == 0)
    def _():
        for hd in range(n_heads):
            vaug_ref[hd, :, 0:LANES] = v_ref[hd]
            vaug_ref[hd, :, LANES:2 * LANES] = jnp.ones((v_ref.shape[1], LANES), BF16)

    lane = lax.broadcasted_iota(jnp.int32, (tq, LANES), 1)
    for hd in range(n_heads):
        qs = (q_ref[hd].astype(F32) * (dh ** -0.5 * LOG2E)).astype(BF16)
        zero = jnp.zeros_like(qs)
        q2_ref[hd, 0:tq] = jnp.where(lane < dh, qs, zero)
        q2_ref[hd, tq:2 * tq] = jnp.where(lane >= dh, qs, zero)
    m_ref[...] = jnp.full(m_ref.shape, MASK_VALUE, F32)
    acc_ref[...] = jnp.zeros_like(acc_ref)

    def step(j, bias_idx):
        start = pl.multiple_of(j * tq, tq)
        s_all = [lax.dot_general(q2_ref[hd], k_ref[hd, pl.ds(start, tq), :], (((1,), (1,)), ((), ())),
                                 preferred_element_type=F32) for hd in range(n_heads)]
        for r0 in range(0, 2 * tq, rows):
            rs = slice(r0, r0 + rows)
            q0 = r0 % tq
            klen = min(q0 + rows, tq) if bias_idx == 0 else tq
            for hd in range(n_heads):
                vb = vaug_ref[hd, pl.ds(start, klen), :]
                s = s_all[hd][rs, 0:klen]
                if bias_idx is not None:
                    s = s + bias_ref[hd, bias_idx, q0:q0 + rows, 0:klen]
                m_old = m_ref[hd, rs]
                m_new = jnp.maximum(m_old, jnp.max(s, axis=1, keepdims=True))
                alpha = jnp.exp2(m_old - m_new)
                p = jnp.exp2(s - jnp.concatenate([m_new] * (klen // LANES), axis=1))
                pv = jnp.dot(p.astype(BF16), vb, preferred_element_type=F32)
                acc_ref[hd, rs] = jnp.concatenate([alpha, alpha], axis=1) * acc_ref[hd, rs] + pv
                m_ref[hd, rs] = m_new

    def far_step(j, carry):
        step(j, None)
        return carry

    lax.fori_loop(0, jnp.maximum(i - 1, 0), far_step, 0)

    @pl.when(i >= 1)
    def _():
        step(i - 1, 1)

    step(i, 0)

    lv = lam_ref[...]
    s1 = jnp.sum(lv[0:1] * lv[1:2], axis=1, keepdims=True)
    s2 = jnp.sum(lv[2:3] * lv[3:4], axis=1, keepdims=True)
    lam = jnp.exp(s1) - jnp.exp(s2) + lam_init
    for hd in range(n_heads):
        o1 = acc_ref[hd, 0:tq, 0:LANES] / acc_ref[hd, 0:tq, LANES:2 * LANES]
        o2 = acc_ref[hd, tq:2 * tq, 0:LANES] / acc_ref[hd, tq:2 * tq, LANES:2 * LANES]
        o = o1 - lam * o2
        ms = jnp.mean(o * o, axis=-1, keepdims=True)
        o_ref[:, hd * LANES:(hd + 1) * LANES] = (
            o * lax.rsqrt(ms + NORM_EPS) * g_ref[...] * (1.0 - lam_init)).astype(o_ref.dtype)


def _diff_attention(qkv, lam_vecs, bias_tiles, subln_g, *, batch, seq, tq, layer_idx):
    t = qkv.shape[1]
    nq = seq // tq
    rows = min(ATTN_PV_ROWS, tq)
    hps = ATTN_HEADS_PER_STEP
    n_hp = DIFF_HEADS // hps
    lam_init = 0.8 - 0.6 * math.exp(-0.3 * layer_idx)
    kern = functools.partial(_attn_kernel, tq=tq, rows=rows, lam_init=lam_init)
    est = hps * (2 * tq * LANES * 2 + 4 * seq * LANES * 2 + 4 * tq * tq * 4 + 2 * tq * LANES * 2
                 + 2 * tq * LANES * 2 + seq * 2 * LANES * 2 + 2 * tq * LANES * 4 + 2 * tq * 2 * LANES * 4
                 + 2 * tq * tq * 4 + 8 * rows * tq * 4)
    return pl.pallas_call(
        kern,
        out_shape=jax.ShapeDtypeStruct((t, DIFF_HEADS * LANES), BF16),
        grid=(batch, n_hp, nq),
        in_specs=[pl.BlockSpec((SUBLANES, LANES), lambda b, h, i: (0, 0)),
                  pl.BlockSpec((hps, tq, LANES), lambda b, h, i: (h, b * nq + i, 0)),
                  pl.BlockSpec((hps, seq, LANES), lambda b, h, i: (n_hp + h, b, 0)),
                  pl.BlockSpec((hps, seq, LANES), lambda b, h, i: (2 * n_hp + h, b, 0)),
                  pl.BlockSpec((hps, 2, tq, tq), lambda b, h, i: (h, 0, 0, 0)),
                  pl.BlockSpec((1, LANES), lambda b, h, i: (0, 0))],
        out_specs=pl.BlockSpec((tq, hps * LANES), lambda b, h, i: (b * nq + i, h)),
        scratch_shapes=[pltpu.VMEM((hps, 2 * tq, LANES), BF16),
                        pltpu.VMEM((hps, seq, 2 * LANES), BF16),
                        pltpu.VMEM((hps, 2 * tq, LANES), F32),
                        pltpu.VMEM((hps, 2 * tq, 2 * LANES), F32)],
        compiler_params=_params(("parallel", "parallel", "arbitrary"), est),
        name="diff_attention",
    )(lam_vecs, qkv, qkv, qkv, bias_tiles, subln_g)


def _mix_kernel(x_ref, y_ref, o_ref, gate_ref, wssd_ref, wdiff_ref, wmix_ref, g_ref, b_ref, out_ref):
    d = x_ref.shape[1]
    ssd = jnp.dot(y_ref[...], wssd_ref[...], preferred_element_type=F32)
    dif = jnp.dot(o_ref[...], wdiff_ref[...], preferred_element_type=F32)
    merged = gate_ref[:, 0:d].astype(F32) * ssd + gate_ref[:, d:2 * d].astype(F32) * dif
    mixed = jnp.dot(merged.astype(BF16), wmix_ref[...], preferred_element_type=F32)
    out_ref[...] = _layer_norm(DN_ALPHA * x_ref[...] + mixed, g_ref[...], b_ref[...])


def _mix(x, y_ssd, o_diff, proj, w_ssd, w_diff, w_mix, ln_g, ln_b, *, tm, gate_block):
    t, d = x.shape
    tm = min(tm, t)
    inner = y_ssd.shape[1]
    dw = o_diff.shape[1]
    row = lambda i: (i, 0)
    est = (2 * tm * (d * 4 + inner * 2 + dw * 2 + 2 * d * 2 + d * 4)
           + (inner * d + dw * d + d * d) * 2 + 6 * tm * d * 4)
    return pl.pallas_call(
        _mix_kernel,
        out_shape=jax.ShapeDtypeStruct((t, d), F32),
        grid=(t // tm,),
        in_specs=[pl.BlockSpec((tm, d), row),
                  pl.BlockSpec((tm, inner), row),
                  pl.BlockSpec((tm, dw), row),
                  pl.BlockSpec((tm, 2 * d), lambda i: (i, gate_block)),
                  _resident((inner, d)), _resident((dw, d)), _resident((d, d)),
                  _resident((1, d)), _resident((1, d))],
        out_specs=pl.BlockSpec((tm, d), row),
        compiler_params=_params(("parallel",), est),
        name="mix_ln1",
    )(x, y_ssd, o_diff, proj, w_ssd, w_diff, w_mix, ln_g, ln_b)


def _pack_bf16_pairs(v):
    w = v.shape[1] // 2
    lo = lax.bitcast_convert_type(v[:, :w].astype(BF16).astype(F32), jnp.int32)
    hi = lax.bitcast_convert_type(v[:, w:].astype(BF16).astype(F32), jnp.int32)
    return jnp.bitwise_or(hi, lax.shift_right_logical(lo, 16))


def _unpack_bf16_pairs(words):
    lo = lax.bitcast_convert_type(lax.shift_left(words, 16), F32)
    hi = lax.bitcast_convert_type(jnp.bitwise_and(words, jnp.int32(-0x10000)), F32)
    return jnp.concatenate([lo, hi], axis=1)


def _xattn_kernel(x_ref, kv_ref, wq_ref, wo_ref, g_ref, b_ref, out_ref, packed_ref, o_scr, *, heads):
    d = x_ref.shape[1]
    dh = d // heads
    xv = x_ref[...]
    q = jnp.dot(xv.astype(BF16), wq_ref[...], preferred_element_type=F32)
    q = (q * (dh ** -0.5)).astype(BF16)
    for h in range(heads):
        kh = kv_ref[:, h * dh:(h + 1) * dh]
        vh = kv_ref[:, d + h * dh:d + (h + 1) * dh]
        s = lax.dot_general(q[:, h * dh:(h + 1) * dh], kh, (((1,), (1,)), ((), ())),
                            preferred_element_type=F32)
        p = jnp.exp(s - jnp.max(s, axis=1, keepdims=True))
        p = p / jnp.sum(p, axis=1, keepdims=True)
        o_scr[:, h * dh:(h + 1) * dh] = jnp.dot(p.astype(BF16), vh, preferred_element_type=F32).astype(BF16)
    att = jnp.dot(o_scr[...], wo_ref[...], preferred_element_type=F32)
    y = _layer_norm(DN_ALPHA * xv + att, g_ref[...], b_ref[...])
    out_ref[...] = y
    packed_ref[...] = _pack_bf16_pairs(y)


def _cross_attention(x, kv, w_cq, w_co, ln_g, ln_b, *, batch, seq, tq):
    t, d = x.shape
    mem_len = kv.shape[0] // batch
    tq = min(tq, seq)
    nq = seq // tq
    est = (2 * tq * d * 4 * 2 + 2 * tq * d * 2 + 2 * mem_len * 2 * d * 2 + 2 * d * d * 2 + tq * d * 2
           + 8 * tq * d * 4)
    row = lambda b, i: (b * nq + i, 0)
    return pl.pallas_call(
        functools.partial(_xattn_kernel, heads=MEM_HEADS),
        out_shape=(jax.ShapeDtypeStruct((t, d), F32), jax.ShapeDtypeStruct((t, d // 2), jnp.int32)),
        grid=(batch, nq),
        in_specs=[pl.BlockSpec((tq, d), row),
                  pl.BlockSpec((mem_len, 2 * d), lambda b, i: (b, 0)),
                  _resident((d, d)), _resident((d, d)), _resident((1, d)), _resident((1, d))],
        out_specs=(pl.BlockSpec((tq, d), row), pl.BlockSpec((tq, d // 2), row)),
        scratch_shapes=[pltpu.VMEM((tq, d), BF16)],
        compiler_params=_params(("parallel", "parallel"), est),
        name="cross_attention_ln2",
    )(x, kv, w_cq, w_co, ln_g, ln_b)


def _router_kernel(x_ref, wr_ref, rb_ref, idx_ref, rank_ref, gates_ref, cnt_ref, run_ref, *, n_experts):
    tm = x_ref.shape[0]
    n_groups = N_EXPERT_GROUPS
    per = n_experts // n_groups

    @pl.when(pl.program_id(0) == 0)
    def _():
        run_ref[...] = jnp.zeros_like(run_ref)

    xv = x_ref[...]
    x_hi = xv.astype(BF16)
    x_lo = (xv - x_hi.astype(F32)).astype(BF16)
    nt = (((1,), (1,)), ((), ()))
    logits = (lax.dot_general(wr_ref[0], x_hi, nt, preferred_element_type=F32)
              + lax.dot_general(wr_ref[0], x_lo, nt, preferred_element_type=F32)
              + lax.dot_general(wr_ref[1], x_hi, nt, preferred_element_type=F32))
    sc = _sigmoid(logits[0:n_experts]).reshape(per, n_groups, tm)
    choice = sc + rb_ref[0:n_experts].reshape(per, n_groups, 1)
    neg = -jnp.inf
    member = lax.broadcasted_iota(jnp.int32, (per, n_groups, tm), 0)
    group3 = lax.broadcasted_iota(jnp.int32, (per, n_groups, tm), 1)
    m1 = jnp.max(choice, axis=0, keepdims=True)
    i1 = jnp.min(jnp.where(choice == m1, member, per), axis=0, keepdims=True)
    m2 = jnp.max(jnp.where(member == i1, neg, choice), axis=0, keepdims=True)
    gscore = (m1 + m2)
    gsel = jnp.zeros((1, n_groups, tm), F32)
    gidx = lax.broadcasted_iota(jnp.int32, (1, n_groups, tm), 1)
    cur = gscore
    for _ in range(TOP_GROUPS):
        mx = jnp.max(cur, axis=1, keepdims=True)
        ix = jnp.min(jnp.where(cur == mx, gidx, n_groups), axis=1, keepdims=True)
        hit = gidx == ix
        gsel = jnp.where(hit, 1.0, gsel)
        cur = jnp.where(hit, neg, cur)
    cur = jnp.where(gsel > 0.0, choice, neg)
    eidx = group3 * per + member
    esel = jnp.zeros((per, n_groups, tm), F32)
    hits, idx_rows = [], []
    for _ in range(TOP_K):
        mx = jnp.max(jnp.max(cur, axis=0, keepdims=True), axis=1, keepdims=True)
        ix = jnp.min(jnp.min(jnp.where(cur == mx, eidx, n_experts), axis=0, keepdims=True),
                     axis=1, keepdims=True)
        hit = eidx == ix
        esel = jnp.where(hit, 1.0, esel)
        cur = jnp.where(hit, neg, cur)
        hits.append(hit)
        idx_rows.append(ix.reshape(1, tm))
    w = esel * sc
    tot = jnp.sum(jnp.sum(w, axis=0, keepdims=True), axis=1, keepdims=True)
    gw = w / tot * ROUTED_SCALE

    r_i = lax.broadcasted_iota(jnp.int32, (tm, tm), 0)
    c_i = lax.broadcasted_iota(jnp.int32, (tm, tm), 1)
    upper = jnp.where(r_i <= c_i, 1.0, 0.0).astype(BF16)
    pref = jnp.dot(esel.reshape(n_experts, tm).astype(BF16), upper, preferred_element_type=F32)
    run = run_ref[...]
    rank3 = (jnp.concatenate([run] * (tm // LANES), axis=1) + pref - 1.0).reshape(per, n_groups, tm)

    def pick(hit, vals):
        return jnp.sum(jnp.sum(jnp.where(hit, vals, 0.0), axis=0, keepdims=True), axis=1).reshape(1, tm)

    idx_ref[...] = jnp.concatenate(idx_rows, axis=0)
    rank_ref[...] = jnp.concatenate([pick(h, rank3) for h in hits], axis=0).astype(jnp.int32)
    wk = jnp.concatenate([pick(h, gw) for h in hits] + [jnp.zeros((LANES - TOP_K, tm), F32)], axis=0)
    gates_ref[...] = wk.T
    run = run + jnp.broadcast_to(pref[:, tm - 1:tm], run.shape)
    run_ref[...] = run
    cnt_ref[...] = run


def _router(x, w_router, router_bias, *, tm, row0, t):
    d = x.shape[1]
    n_experts = w_router.shape[1]
    per = n_experts // N_EXPERT_GROUPS
    tm = min(tm, t)
    blk0 = row0 // tm

    def member_major(v):
        return v.reshape(N_EXPERT_GROUPS, per, *v.shape[1:]).swapaxes(0, 1).reshape(v.shape)

    wt = jnp.pad(member_major(w_router.astype(F32).T), ((0, LANES - n_experts), (0, 0)))
    w_hi = wt.astype(BF16)
    wr = jnp.stack([w_hi, (wt - w_hi.astype(F32)).astype(BF16)])
    rb = jnp.pad(member_major(router_bias.astype(F32)), (0, LANES - n_experts)).reshape(LANES, 1)
    est = (2 * tm * d * 4 + 2 * LANES * d * 2 + 2 * tm * LANES * 4 + 60 * n_experts * tm * 4 + 3 * tm * tm * 4)
    pick_spec = pl.BlockSpec((TOP_K, tm), lambda i: (0, i))
    idx_t, rank_t, gates, counts = pl.pallas_call(
        functools.partial(_router_kernel, n_experts=n_experts),
        out_shape=(jax.ShapeDtypeStruct((TOP_K, t), jnp.int32),
                   jax.ShapeDtypeStruct((TOP_K, t), jnp.int32),
                   jax.ShapeDtypeStruct((t, LANES), F32),
                   jax.ShapeDtypeStruct((n_experts, LANES), F32)),
        grid=(t // tm,),
        in_specs=[pl.BlockSpec((tm, d), lambda i: (blk0 + i, 0)),
                  _resident((2, LANES, d)), _resident((LANES, 1))],
        out_specs=(pick_spec, pick_spec,
                   pl.BlockSpec((tm, LANES), lambda i: (i, 0)),
                   pl.BlockSpec((n_experts, LANES), lambda i: (0, 0))),
        scratch_shapes=[pltpu.VMEM((n_experts, LANES), F32)],
        compiler_params=_params(("arbitrary",), est),
        name="router",
    )(x, wr, rb)
    counts = counts[:, 0].astype(jnp.int32).reshape(per, N_EXPERT_GROUPS).T.reshape(n_experts)
    return idx_t, rank_t, gates, counts


def _positions_kernel(cnt_ref, idx_ref, rank_ref, pos_ref, texp_ref, off_ref, *, n_experts, rows, n_tiles):
    @pl.when(pl.program_id(0) == 0)
    def _():
        def per_expert(e, toff):
            off_ref[e] = toff * rows
            nt = lax.shift_right_logical(cnt_ref[e] + (rows - 1), int(math.log2(rows)))

            def fill(j, c):
                texp_ref[toff + j] = e
                return c

            lax.fori_loop(0, nt, fill, 0)
            return toff + nt

        n_used = lax.fori_loop(0, n_experts, per_expert, 0)

        def tail(j, c):
            texp_ref[j] = n_experts - 1
            return c

        lax.fori_loop(n_used, n_tiles, tail, 0)
        texp_ref[n_tiles] = n_used

    idx = idx_ref[...]
    pos = rank_ref[...]
    for e in range(n_experts):
        pos = pos + jnp.where(idx == e, off_ref[e], 0)
    pos_ref[...] = pos


def _positions(counts, idx_t, rank_t, *, rows, n_tiles, tm):
    n_experts = counts.shape[0]
    k, t = idx_t.shape
    tm = min(tm, t)
    spec = pl.BlockSpec((k, tm), lambda i: (0, i))
    kern = functools.partial(_positions_kernel, n_experts=n_experts, rows=rows, n_tiles=n_tiles)
    return pl.pallas_call(
        kern,
        out_shape=(jax.ShapeDtypeStruct((k, t), jnp.int32),
                   jax.ShapeDtypeStruct((n_tiles + 1,), jnp.int32)),
        grid=(t // tm,),
        in_specs=[pl.BlockSpec(memory_space=pltpu.SMEM), spec, spec],
        out_specs=(spec, pl.BlockSpec(memory_space=pltpu.SMEM)),
        scratch_shapes=[pltpu.SMEM((n_experts,), jnp.int32)],
        compiler_params=_params(("arbitrary",), 16 * k * tm * 4),
        name="moe_positions",
    )(counts, idx_t, rank_t)


def _sc_mesh():
    return plsc.VectorSubcoreMesh(core_axis_name="c", subcore_axis_name="s",
                                  num_cores=SC_CORES, num_subcores=SC_SUBCORES)


def _sc_dispatch(packed, pos_chunks, n_rows, row0):
    w = packed.shape[1]
    n_chunks, k, n = pos_chunks.shape
    per_worker = n_chunks // (SC_CORES * SC_SUBCORES)

    @functools.partial(
        pl.kernel, mesh=_sc_mesh(),
        out_type=jax.ShapeDtypeStruct((n_rows, w), packed.dtype),
        scratch_types=[pltpu.VMEM((k, n), jnp.int32), pltpu.VMEM((n, w), packed.dtype),
                       pltpu.SemaphoreType.DMA],
        name="moe_dispatch_sc",
    )
    def scatter_rows(x_hbm, pos_hbm, out_hbm, idx_v, rows_v, sem):
        wid = lax.axis_index("s") * SC_CORES + lax.axis_index("c")

        @pl.loop(0, per_worker)
        def _(step):
            c = wid * per_worker + step
            pltpu.sync_copy(pos_hbm.at[c], idx_v)
            pltpu.sync_copy(x_hbm.at[pl.ds(row0 + c * n, n)], rows_v)
            copies = [pltpu.async_copy(rows_v, out_hbm.at[idx_v.at[kk]], sem) for kk in range(k)]
            for cp in copies:
                cp.wait()

    return scatter_rows(packed, pos_chunks)


def _sc_combine(sorted_rows, pos_chunks):
    _, w = sorted_rows.shape
    n_chunks, k, n = pos_chunks.shape
    per_worker = n_chunks // (SC_CORES * SC_SUBCORES)

    @functools.partial(
        pl.kernel, mesh=_sc_mesh(),
        out_type=jax.ShapeDtypeStruct((k, n_chunks * n, w), sorted_rows.dtype),
        scratch_types=[pltpu.VMEM((k, n), jnp.int32), pltpu.VMEM((n, w), sorted_rows.dtype),
                       pltpu.SemaphoreType.DMA],
        name="moe_combine_sc",
    )
    def gather_rows(y_hbm, pos_hbm, out_hbm, idx_v, rows_v, sem):
        wid = lax.axis_index("s") * SC_CORES + lax.axis_index("c")

        @pl.loop(0, per_worker)
        def _(step):
            c = wid * per_worker + step
            pltpu.sync_copy(pos_hbm.at[c], idx_v)
            for kk in range(k):
                pltpu.async_copy(y_hbm.at[idx_v.at[kk]], rows_v, sem).wait()
                pltpu.sync_copy(rows_v, out_hbm.at[kk, pl.ds(c * n, n)])

    return gather_rows(sorted_rows, pos_chunks)


def _expert_kernel(texp_ref, xs_ref, wgu_ref, wdn_ref, ys_ref, *, n_tiles):
    ff = wdn_ref.shape[1]

    @pl.when(pl.program_id(0) < texp_ref[n_tiles])
    def _():
        xv = _unpack_bf16_pairs(xs_ref[...]).astype(BF16)
        gu = jnp.dot(xv, wgu_ref[0], preferred_element_type=F32)
        hid = _silu(gu[:, 0:ff]) * gu[:, ff:2 * ff]
        ys_ref[...] = _pack_bf16_pairs(jnp.dot(hid.astype(BF16), wdn_ref[0], preferred_element_type=F32))


def _experts(tile_expert, sorted_rows, w_gu, w_dn, *, rows):
    n_rows, w = sorted_rows.shape
    n_tiles = n_rows // rows
    _, d, ff2 = w_gu.shape
    ff = ff2 // 2
    tile = lambda j, te: (jnp.minimum(j, te[n_tiles] - 1), 0)
    est = 4 * rows * w * 4 + 2 * (d * ff2 + ff * d) * 2 + 4 * rows * d * 4 + 4 * rows * ff2 * 4
    return pl.pallas_call(
        functools.partial(_expert_kernel, n_tiles=n_tiles),
        out_shape=jax.ShapeDtypeStruct((n_rows, w), sorted_rows.dtype),
        grid_spec=pltpu.PrefetchScalarGridSpec(
            num_scalar_prefetch=1,
            grid=(n_tiles,),
            in_specs=[pl.BlockSpec((rows, w), tile),
                      pl.BlockSpec((1, d, ff2), lambda j, te: (te[j], 0, 0)),
                      pl.BlockSpec((1, ff, d), lambda j, te: (te[j], 0, 0))],
            out_specs=pl.BlockSpec((rows, w), tile)),
        compiler_params=_params(("arbitrary",), est),
        name="moe_experts",
    )(tile_expert, sorted_rows, w_gu, w_dn)


def _moe_out_kernel(x_ref, yk_ref, gates_ref, wsgu_ref, wsdn_ref, g_ref, b_ref, *rest):
    out_ref = rest[-1]
    ff = wsdn_ref.shape[0]
    xv = x_ref[...]
    gu = jnp.dot(xv.astype(BF16), wsgu_ref[...], preferred_element_type=F32)
    hid = _silu(gu[:, 0:ff]) * gu[:, ff:2 * ff]
    acc = jnp.dot(hid.astype(BF16), wsdn_ref[...], preferred_element_type=F32)
    for k in range(yk_ref.shape[0]):
        acc = acc + gates_ref[:, k:k + 1] * _unpack_bf16_pairs(yk_ref[k])
    out_ref[...] = _layer_norm(DN_ALPHA * xv + acc, g_ref[...], b_ref[...])


def _moe_out(x, yk, gates, w_sgu, w_sdn, ln_g, ln_b, *, tm, row0, prev):
    t_all, d = x.shape
    k, t, w = yk.shape
    sff2 = w_sgu.shape[1]
    tm = min(tm, t)
    blk0 = row0 // tm
    est = (2 * tm * d * 4 * 2 + 2 * k * tm * w * 4 + 2 * tm * LANES * 4
           + (d * sff2 + (sff2 // 2) * d) * 2 + 6 * tm * d * 4)
    in_specs = [pl.BlockSpec((tm, d), lambda i: (blk0 + i, 0)),
                pl.BlockSpec((k, tm, w), lambda i: (0, i, 0)),
                pl.BlockSpec((tm, LANES), lambda i: (i, 0)),
                _resident((d, sff2)), _resident((sff2 // 2, d)),
                _resident((1, d)), _resident((1, d))]
    args = [x, yk, gates, w_sgu, w_sdn, ln_g, ln_b]
    aliases = {}
    if prev is not None:
        in_specs.append(pl.BlockSpec(memory_space=pl.ANY))
        args.append(prev)
        aliases = {len(args) - 1: 0}
    return pl.pallas_call(
        _moe_out_kernel,
        out_shape=jax.ShapeDtypeStruct((t_all, d), F32),
        grid=(t // tm,),
        in_specs=in_specs,
        out_specs=pl.BlockSpec((tm, d), lambda i: (blk0 + i, 0)),
        input_output_aliases=aliases,
        compiler_params=_params(("parallel",), est),
        name="moe_out_ln3",
    )(*args)


def _moe(x, packed, w_router, router_bias, w_gu, w_dn, w_sgu, w_sdn, ln_g, ln_b):
    t_all, d = x.shape
    n_experts = w_router.shape[1]
    n = SC_CHUNK_TOKENS
    splits = MOE_SPLITS if t_all % (MOE_SPLITS * n * SC_CORES * SC_SUBCORES) == 0 else 1
    t = t_all // splits
    rows = min(MOE_ROWS_PER_TILE, t)
    n_tiles = (t * TOP_K) // rows + n_experts
    parts = []
    for s in range(splits):
        row0 = s * t
        idx_t, rank_t, gates, counts = _router(x, w_router, router_bias, tm=ROUTER_ROWS, row0=row0, t=t)
        pos_t, tile_expert = _positions(counts, idx_t, rank_t, rows=rows, n_tiles=n_tiles,
                                        tm=POSITIONS_TOKENS)
        pos_chunks = pos_t.reshape(TOP_K, t // n, n).transpose(1, 0, 2)
        sorted_x = _sc_dispatch(packed, pos_chunks, n_tiles * rows, row0)
        parts.append((row0, gates, tile_expert, pos_chunks, sorted_x))
    out = None
    for row0, gates, tile_expert, pos_chunks, sorted_x in parts:
        sorted_y = _experts(tile_expert, sorted_x, w_gu, w_dn, rows=rows)
        yk = _sc_combine(sorted_y, pos_chunks)
        out = _moe_out(x, yk, gates, w_sgu, w_sdn, ln_g, ln_b, tm=MOE_OUT_ROWS, row0=row0, prev=out)
    return out


def kernel(x, mem, w_in, conv_w, conv_b, dt_bias, a_log, d_skip, ssd_norm_g, lambda_q1, lambda_k1, lambda_q2, lambda_k2, subln_g, rel_bias, w_ssd_br, w_diff_br, w_mix_out, ln1_g, ln1_b, w_cq, w_ckv, w_co, ln2_g, ln2_b, w_router, router_bias, w_exp_gu, w_exp_down, w_sh_gu, w_sh_down, ln3_g, ln3_b):
    batch, seq, d = x.shape
    depth = w_in.shape[0]
    inner = w_ssd_br.shape[1]
    xbc_width = conv_w.shape[2]
    heads = dt_bias.shape[1]
    diff_width = w_diff_br.shape[1]
    o_z, o_xbc = inner, inner + xbc_width
    o_dt = o_xbc + heads
    o_v = o_dt + 3 * diff_width
    t = batch * seq
    tq_attn = min(ATTN_TILE, seq)

    def vec(v):
        return v.astype(F32).reshape(1, -1)

    xt = x.reshape(t, d)
    memt = mem.reshape(-1, d)
    bias_tiles = _bias_tiles(rel_bias, tq_attn)

    for l in range(depth):
        wl = w_in[l]
        w_big = jnp.concatenate([wl[:, o_z:o_xbc], wl[:, o_dt:o_v], wl[:, :o_z], wl[:, o_v:]],
                                axis=1).astype(BF16)
        assert 2 * d == inner
        w_dt = jnp.pad(wl[:, o_xbc:o_dt], ((0, 0), (0, LANES - heads))).astype(BF16)
        xbc, qkv, zg, dt_raw = _in_proj(xt, w_big, w_dt, n_xbc_cols=xbc_width, n_qkv_cols=3 * diff_width,
                                        n_z_cols=inner, tm=PROJ_ROWS, tn=PROJ_COLS)

        y_ssd = _ssd_branch(zg, xbc, dt_raw, conv_w[l], conv_b[l], dt_bias[l], a_log[l], d_skip[l],
                            ssd_norm_g[l], batch=batch, seq=seq)

        lam_vecs = jnp.pad(jnp.stack([lambda_q1[l], lambda_k1[l], lambda_q2[l], lambda_k2[l]]).astype(F32),
                           ((0, SUBLANES - 4), (0, LANES - lambda_q1.shape[1])))
        o_diff = _diff_attention(qkv, lam_vecs, bias_tiles, vec(subln_g[l]), batch=batch, seq=seq,
                                 tq=tq_attn, layer_idx=l)

        xt = _mix(xt, y_ssd, o_diff, zg, w_ssd_br[l].astype(BF16), w_diff_br[l].astype(BF16),
                  w_mix_out[l].astype(BF16), vec(ln1_g[l]), vec(ln1_b[l]), tm=MIX_ROWS, gate_block=1)

        kv = _matmul(memt, w_ckv[l].astype(BF16), BF16, tm=PROJ_ROWS, tn=PROJ_COLS, name="mem_kv_proj")
        xt, packed = _cross_attention(xt, kv, w_cq[l].astype(BF16), w_co[l].astype(BF16), vec(ln2_g[l]),
                                      vec(ln2_b[l]), batch=batch, seq=seq, tq=XATTN_ROWS)

        xt = _moe(xt, packed, w_router[l], router_bias[l], w_exp_gu[l].astype(BF16),
                  w_exp_down[l].astype(BF16), w_sh_gu[l].astype(BF16), w_sh_down[l].astype(BF16),
                  vec(ln3_g[l]), vec(ln3_b[l]))
    return xt.reshape(batch, seq, d)
```

```python
import functools
import math

import numpy as np
import jax
import jax.numpy as jnp
from jax import lax
from jax.experimental import pallas as pl
from jax.experimental.pallas import tpu as pltpu
from jax.experimental.pallas import tpu_sc as plsc

F32 = jnp.float32
BF16 = jnp.bfloat16

SSD_HEAD_DIM = 64
SSD_GROUPS = 4
SSD_STATE = 128
SSD_CONV = 4
SSD_CHUNK = 128
DIFF_HEADS = 8
REL_BUCKETS = 32
REL_MAX_DIST = 128
MEM_HEADS = 4
TOP_K = 8
N_EXPERT_GROUPS = 8
TOP_GROUPS = 4
ROUTED_SCALE = 2.5
NORM_EPS = 1e-5
DEPTH = 1
DN_ALPHA = (2.0 * DEPTH) ** 0.25

LANES = 128
SUBLANES = 8
VMEM_CAP_BYTES = 64 * 1024 * 1024
MASK_VALUE = -1e30
LOG2E = math.log2(math.e)
PROJ_ROWS, PROJ_COLS = 1024, 1024
IN_PROJ_SLAB = 256
SSD_CHUNKS_PER_STEP = 2
ATTN_TILE = 512
ATTN_PV_ROWS = 512
ATTN_HEADS_PER_STEP = 2
MIX_ROWS = 512
XATTN_ROWS = 1024
ROUTER_ROWS = 512
POSITIONS_TOKENS = 2048
MOE_OUT_ROWS = 512
SC_CORES = 2
SC_SUBCORES = 16
SC_CHUNK_TOKENS = 128
MOE_SPLITS = 2
MOE_ROWS_PER_TILE = 1024


def _vmem_limit(estimate_bytes):
    return int(min(estimate_bytes * 5 // 4 + (4 << 20), VMEM_CAP_BYTES - (6 << 20)))


def _params(semantics, vmem_estimate):
    return pltpu.CompilerParams(dimension_semantics=semantics,
                                vmem_limit_bytes=_vmem_limit(vmem_estimate))


def _resident(shape):
    nd = len(shape)
    return pl.BlockSpec(shape, lambda *_: (0,) * nd, pipeline_mode=pl.Buffered(1))


def _layer_norm(v, g, b):
    mu = jnp.mean(v, axis=-1, keepdims=True)
    d = v - mu
    var = jnp.mean(d * d, axis=-1, keepdims=True)
    return d * lax.rsqrt(var + NORM_EPS) * g + b


def _sigmoid(v):
    return 0.5 * jnp.tanh(0.5 * v) + 0.5


def _silu(v):
    h = 0.5 * v
    return h + h * jnp.tanh(h)


def _matmul_kernel(x_ref, w_ref, o_ref, xb_ref):
    @pl.when(pl.program_id(1) == 0)
    def _():
        xb_ref[...] = x_ref[...].astype(BF16)

    o_ref[...] = jnp.dot(xb_ref[...], w_ref[...], preferred_element_type=F32).astype(o_ref.dtype)


def _matmul(x, w, out_dtype, *, tm, tn, name):
    m, k = x.shape
    n = w.shape[1]
    tm, tn = min(tm, m), min(tn, n)
    est = (2 * tm * k * x.dtype.itemsize + tm * k * 2 + 2 * k * tn * 2
           + 2 * tm * tn * jnp.dtype(out_dtype).itemsize + tm * tn * 4)
    return pl.pallas_call(
        _matmul_kernel,
        out_shape=jax.ShapeDtypeStruct((m, n), out_dtype),
        grid=(m // tm, n // tn),
        in_specs=[pl.BlockSpec((tm, k), lambda i, j: (i, 0)),
                  pl.BlockSpec((k, tn), lambda i, j: (0, j))],
        out_specs=pl.BlockSpec((tm, tn), lambda i, j: (i, j)),
        scratch_shapes=[pltpu.VMEM((tm, k), BF16)],
        compiler_params=_params(("parallel", "arbitrary"), est),
        name=name,
    )(x, w)


def _in_proj_kernel(x_ref, w_ref, wdt_ref, xbc_ref, qkv_ref, zg_ref, dt_ref, xb_ref, *, n_xbc, n_qkv, n_z):
    j = pl.program_id(1)
    j_z = n_xbc + n_qkv

    @pl.when(j == 0)
    def _():
        xb = x_ref[...].astype(BF16)
        xb_ref[...] = xb
        dt_ref[...] = jnp.dot(xb, wdt_ref[...], preferred_element_type=F32)

    tn = w_ref.shape[1]
    slabs = [slice(c0, c0 + IN_PROJ_SLAB) for c0 in range(0, tn, IN_PROJ_SLAB)]

    def product(cs):
        return jnp.dot(xb_ref[...], w_ref[:, cs], preferred_element_type=F32)

    @pl.when(j < n_xbc)
    def _():
        for cs in slabs:
            xbc_ref[:, cs] = product(cs).astype(BF16)

    @pl.when(jnp.logical_and(j >= n_xbc, j < j_z))
    def _():
        for cs in slabs:
            acc = product(cs)
            for hh in range(IN_PROJ_SLAB // LANES):
                qkv_ref[cs.start // LANES + hh] = acc[:, hh * LANES:(hh + 1) * LANES].astype(BF16)

    @pl.when(jnp.logical_and(j >= j_z, j < j_z + n_z))
    def _():
        for cs in slabs:
            zg_ref[:, cs] = product(cs).astype(BF16)

    @pl.when(j >= j_z + n_z)
    def _():
        for cs in slabs:
            zg_ref[:, cs] = _sigmoid(product(cs)).astype(BF16)


def _in_proj(x, w, w_dt, *, n_xbc_cols, n_qkv_cols, n_z_cols, tm, tn):
    m, k = x.shape
    n = w.shape[1]
    tm = min(tm, m)
    n_xbc, n_qkv, n_z = n_xbc_cols // tn, n_qkv_cols // tn, n_z_cols // tn
    n_zg = n // tn - n_xbc - n_qkv
    per = tn // LANES
    est = (2 * tm * k * 4 + tm * k * 2 + 2 * k * tn * 2 + 3 * 2 * tm * tn * 2 + 2 * tm * tn * 4
           + k * LANES * 2 + 2 * tm * LANES * 4)
    kern = functools.partial(_in_proj_kernel, n_xbc=n_xbc, n_qkv=n_qkv, n_z=n_z)
    return pl.pallas_call(
        kern,
        out_shape=(jax.ShapeDtypeStruct((m, n_xbc_cols), BF16),
                   jax.ShapeDtypeStruct((n_qkv_cols // LANES, m, LANES), BF16),
                   jax.ShapeDtypeStruct((m, n_zg * tn), BF16),
                   jax.ShapeDtypeStruct((m, LANES), F32)),
        grid=(m // tm, n // tn),
        in_specs=[pl.BlockSpec((tm, k), lambda i, j: (i, 0)),
                  pl.BlockSpec((k, tn), lambda i, j: (0, j)),
                  _resident((k, LANES))],
        out_specs=(pl.BlockSpec((tm, tn), lambda i, j: (i, jnp.minimum(j, n_xbc - 1))),
                   pl.BlockSpec((per, tm, LANES), lambda i, j: (jnp.clip(j - n_xbc, 0, n_qkv - 1), i, 0)),
                   pl.BlockSpec((tm, tn), lambda i, j: (i, jnp.clip(j - n_xbc - n_qkv, 0, n_zg - 1))),
                   pl.BlockSpec((tm, LANES), lambda i, j: (i, 0))),
        scratch_shapes=[pltpu.VMEM((tm, k), BF16)],
        compiler_params=_params(("parallel", "arbitrary"), est),
        name="in_proj",
    )(x, w, w_dt)


def _split3(v):
    hi = v.astype(BF16)
    r1 = v - hi.astype(F32)
    mid = r1.astype(BF16)
    lo = (r1 - mid.astype(F32)).astype(BF16)
    return hi, mid, lo


def _ssd_kernel(z_ref, xbc_ref, dt_ref, convw_ref, convb_ref, dtb_ref, alog_ref,
                dskip_ref, g_ref, y_ref, tail_ref, state_ref, xa_ref, yacc_ref, **static):
    @pl.when(pl.program_id(1) == 0)
    def _():
        tail_ref[...] = jnp.zeros_like(tail_ref)
        state_ref[...] = jnp.zeros_like(state_ref)

    for r0 in range(0, z_ref.shape[0], SSD_CHUNK):
        rows = pl.ds(r0, SSD_CHUNK)
        _ssd_chunk(z_ref.at[rows], xbc_ref.at[rows], dt_ref.at[rows], convw_ref, convb_ref, dtb_ref,
                   alog_ref, dskip_ref, g_ref, y_ref.at[rows], tail_ref, state_ref, xa_ref, yacc_ref,
                   **static)


def _ssd_chunk(z_ref, xbc_ref, dt_ref, convw_ref, convb_ref, dtb_ref, alog_ref,
               dskip_ref, g_ref, y_ref, tail_ref, state_ref, xa_ref, yacc_ref,
               *, inner, n_groups, d_state):
    L = SSD_CHUNK
    width = xbc_ref.shape[1]
    n_pairs = inner // LANES
    pairs_per_group = n_pairs // n_groups
    group_width = inner // n_groups

    slab = 512
    n_t = L // SUBLANES
    sub = lax.broadcasted_iota(jnp.int32, (n_t, SUBLANES, slab), 1)
    for c0 in range(0, width, slab):
        cs = slice(c0, c0 + slab)
        u = xbc_ref[:, cs].astype(F32)
        tiles = jnp.concatenate([tail_ref[:, cs], u], axis=0).reshape(n_t + 1, SUBLANES, slab)
        acc = convb_ref[:, cs] + u * convw_ref[SSD_CONV - 1:SSD_CONV, cs]
        for j in range(1, SSD_CONV):
            rot = pltpu.roll(tiles, j, axis=1)
            shifted = jnp.where(sub < j, rot[0:n_t], rot[1:n_t + 1]).reshape(L, slab)
            acc = acc + shifted * convw_ref[SSD_CONV - 1 - j:SSD_CONV - j, cs]
        tail_ref[:, cs] = u[L - SUBLANES:L]
        xa_ref[:, cs] = _silu(acc)

    dtr = dt_ref[...] + dtb_ref[...]
    dt = jnp.maximum(dtr, 0.0) + jnp.log(1.0 + jnp.exp(-jnp.abs(dtr)))
    a = -jnp.exp(alog_ref[...]) * dt
    row_i = lax.broadcasted_iota(jnp.int32, (L, L), 0)
    col_i = lax.broadcasted_iota(jnp.int32, (L, L), 1)
    causal = row_i >= col_i
    tril = jnp.where(causal, 1.0, 0.0).astype(BF16)
    acs = sum(jnp.dot(tril, part, preferred_element_type=F32) for part in _split3(a)) * LOG2E
    acs_t = acs.T
    dt_t = dt.T
    acs_dt_t = acs_t - jnp.log2(dt_t)
    lane_lo = lax.broadcasted_iota(jnp.int32, (L, LANES), 1) < SSD_HEAD_DIM

    b0 = inner
    c0 = inner + n_groups * d_state
    for g in range(n_groups):
        bg = xa_ref[:, b0 + g * d_state:b0 + (g + 1) * d_state]
        cg = xa_ref[:, c0 + g * d_state:c0 + (g + 1) * d_state]
        cg_b = cg.astype(BF16)
        cb_b = lax.dot_general(cg_b, bg.astype(BF16), (((1,), (1,)), ((), ())),
                               preferred_element_type=F32).astype(BF16)
        bgt_b = bg.T.astype(BF16)
        for pp in range(g * pairs_per_group, (g + 1) * pairs_per_group):
            xs_pair = xa_ref[:, pp * LANES:(pp + 1) * LANES].astype(BF16)
            st_old = state_ref[pp]
            rhs = jnp.concatenate([xs_pair, st_old.astype(BF16)], axis=0)
            lhs, b_ws, keep = [], [], []
            for side in range(2):
                h = 2 * pp + side
                col = jnp.broadcast_to(acs[:, h:h + 1], (L, L))
                row = acs_t[h:h + 1, :]
                dt_row = dt_t[h:h + 1, :]
                last = acs[L - 1:L, h:h + 1]
                dec_dt = jnp.exp2(jnp.where(causal, col - acs_dt_t[h:h + 1, :], MASK_VALUE))
                m_in = cb_b * dec_dt.astype(BF16)
                c_w = cg_b * jnp.exp2(col[:, :d_state]).astype(BF16)
                lhs.append(jnp.concatenate([m_in, c_w], axis=1))
                b_ws.append(bgt_b * (jnp.exp2(last - row) * dt_row).astype(BF16))
                keep.append(jnp.exp2(last))
            y2 = jnp.dot(jnp.concatenate(lhs, axis=0), rhs, preferred_element_type=F32)
            s2 = jnp.dot(jnp.concatenate(b_ws, axis=0), xs_pair, preferred_element_type=F32)
            yacc_ref[:, pp * LANES:(pp + 1) * LANES] = jnp.where(lane_lo, y2[0:L], y2[L:2 * L])
            keep_pair = jnp.where(lane_lo[0:1, :], keep[0], keep[1])
            state_ref[pp] = st_old * keep_pair + jnp.where(lane_lo[0:d_state], s2[0:d_state],
                                                           s2[d_state:2 * d_state])

    for g in range(n_groups):
        cs = slice(g * group_width, (g + 1) * group_width)
        gate = _silu(z_ref[:, cs]).astype(F32)
        yv = (yacc_ref[:, cs] + dskip_ref[:, cs] * xa_ref[:, cs]) * gate
        ms = jnp.mean(yv * yv, axis=-1, keepdims=True)
        y_ref[:, cs] = (yv * lax.rsqrt(ms + NORM_EPS) * g_ref[:, cs]).astype(y_ref.dtype)


def _ssd_branch(zg, xbc, dt_raw, conv_w, conv_b, dt_bias, a_log, d_skip, norm_g, *, batch, seq):
    t = zg.shape[0]
    inner = norm_g.shape[0]
    width = conv_w.shape[1]
    heads = inner // SSD_HEAD_DIM
    n_chunks = seq // SSD_CHUNK
    L = SSD_CHUNK

    def pad_heads(v):
        return jnp.pad(v.astype(F32), (0, LANES - heads)).reshape(1, LANES)

    convw = jnp.pad(conv_w.astype(F32), ((0, SUBLANES - SSD_CONV), (0, 0)))
    dskip = jnp.repeat(d_skip.astype(F32), SSD_HEAD_DIM).reshape(1, inner)
    cps = SSD_CHUNKS_PER_STEP if n_chunks % SSD_CHUNKS_PER_STEP == 0 else 1
    n_steps = n_chunks // cps
    rows = cps * L
    row = lambda b, c: (b * n_steps + c, 0)
    const = lambda b, c: (0, 0)
    est = (2 * rows * (inner + width) * 2 + 2 * rows * LANES * 4 + 2 * rows * inner * 2
           + (inner // LANES) * SSD_STATE * LANES * 4 + L * (2 * width + inner) * 4 + (8 << 20))
    kern = functools.partial(_ssd_kernel, inner=inner, n_groups=SSD_GROUPS, d_state=SSD_STATE)
    return pl.pallas_call(
        kern,
        out_shape=jax.ShapeDtypeStruct((t, inner), BF16),
        grid=(batch, n_steps),
        in_specs=[pl.BlockSpec((rows, inner), row),
                  pl.BlockSpec((rows, width), row),
                  pl.BlockSpec((rows, LANES), row),
                  pl.BlockSpec((SUBLANES, width), const),
                  pl.BlockSpec((1, width), const),
                  pl.BlockSpec((1, LANES), const),
                  pl.BlockSpec((1, LANES), const),
                  pl.BlockSpec((1, inner), const),
                  pl.BlockSpec((1, inner), const)],
        out_specs=pl.BlockSpec((rows, inner), row),
        scratch_shapes=[pltpu.VMEM((SUBLANES, width), F32),
                        pltpu.VMEM((inner // LANES, SSD_STATE, LANES), F32),
                        pltpu.VMEM((L, width), F32),
                        pltpu.VMEM((L, inner), F32)],
        compiler_params=_params(("parallel", "arbitrary"), est),
        name="ssd_scan",
    )(zg, xbc, dt_raw, convw, conv_b.astype(F32).reshape(1, width), pad_heads(dt_bias),
      pad_heads(a_log), dskip, norm_g.astype(F32).reshape(1, inner))


def _bucket_tiles(tq):
    max_exact = REL_BUCKETS // 2
    qi = np.arange(tq)[:, None]
    ki = np.arange(tq)[None, :]

    def bucket(dist):
        d = np.maximum(dist, 1).astype(np.float32)
        large = max_exact + (np.log(d / np.float32(max_exact)) / np.float32(math.log(REL_MAX_DIST / max_exact))
                             * np.float32(REL_BUCKETS - max_exact)).astype(np.int32)
        large = np.minimum(large, REL_BUCKETS - 1)
        return np.where(dist < max_exact, dist, large).astype(np.int32)

    diag = np.where(qi >= ki, bucket(np.maximum(qi - ki, 0)), -1)
    prev = bucket(tq + qi - ki)
    far = bucket(np.arange(tq + 1, 1 << 16))
    assert (far == REL_BUCKETS - 1).all()
    return np.stack([diag, prev]).astype(np.int32)


def _bias_kernel(rb_ref, bucket_ref, o_ref):
    h = pl.program_id(0)
    bk = bucket_ref[...]
    acc = jnp.zeros(bk.shape, F32)
    for b in range(REL_BUCKETS):
        acc = jnp.where(bk == b, rb_ref[b, h], acc)
    o_ref[0] = jnp.where(bk < 0, MASK_VALUE, (acc - rb_ref[REL_BUCKETS - 1, h]) * LOG2E)


def _bias_tiles(rel_bias, tq):
    buckets = jnp.asarray(_bucket_tiles(tq))
    return pl.pallas_call(
        _bias_kernel,
        out_shape=jax.ShapeDtypeStruct((DIFF_HEADS, 2, tq, tq), F32),
        grid=(DIFF_HEADS,),
        in_specs=[pl.BlockSpec(memory_space=pltpu.SMEM),
                  pl.BlockSpec((2, tq, tq), lambda h: (0, 0, 0))],
        out_specs=pl.BlockSpec((1, 2, tq, tq), lambda h: (h, 0, 0, 0)),
        compiler_params=_params(("arbitrary",), 8 * tq * tq * 4),
        name="t5_bias_tiles",
    )(rel_bias.astype(F32), buckets)


def _attn_kernel(lam_ref, q_ref, k_ref, v_ref, bias_ref, g_ref, o_ref,
                 q2_ref, vaug_ref, m_ref, acc_ref, *, tq, rows, lam_init):
    i = pl.program_id(2)
    dh = LANES // 2
    n_heads = q_ref.shape[0]

    @pl.when(i == 0)
    def _():
        for hd in range(n_heads):
            vaug_ref[hd, :, 0:LANES] = v_ref[hd]
            vaug_ref[hd, :, LANES:2 * LANES] = jnp.ones((v_ref.shape[1], LANES), BF16)

    lane = lax.broadcasted_iota(jnp.int32, (tq, LANES), 1)
    for hd in range(n_heads):
        qs = (q_ref[hd].astype(F32) * (dh ** -0.5 * LOG2E)).astype(BF16)
        zero = jnp.zeros_like(qs)
        q2_ref[hd, 0:tq] = jnp.where(lane < dh, qs, zero)
        q2_ref[hd, tq:2 * tq] = jnp.where(lane >= dh, qs, zero)
    m_ref[...] = jnp.full(m_ref.shape, MASK_VALUE, F32)
    acc_ref[...] = jnp.zeros_like(acc_ref)

    def step(j, bias_idx):
        start = pl.multiple_of(j * tq, tq)
        s_all = [lax.dot_general(q2_ref[hd], k_ref[hd, pl.ds(start, tq), :], (((1,), (1,)), ((), ())),
                                 preferred_element_type=F32) for hd in range(n_heads)]
        step_rows = rows // 2 if bias_idx == 0 else rows
        for r0 in range(0, 2 * tq, step_rows):
            rs = slice(r0, r0 + step_rows)
            q0 = r0 % tq
            klen = min(q0 + step_rows, tq) if bias_idx == 0 else tq
            for hd in range(n_heads):
                vb = vaug_ref[hd, pl.ds(start, klen), :]
                s = s_all[hd][rs, 0:klen]
                if bias_idx is not None:
                    s = s + bias_ref[hd, bias_idx, q0:q0 + step_rows, 0:klen]
                m_old = m_ref[hd, rs]
                m_new = jnp.maximum(m_old, jnp.max(s, axis=1, keepdims=True))
                alpha = jnp.exp2(m_old - m_new)
                p = jnp.exp2(s - jnp.concatenate([m_new] * (klen // LANES), axis=1))
                pv = jnp.dot(p.astype(BF16), vb, preferred_element_type=F32)
                acc_ref[hd, rs] = jnp.concatenate([alpha, alpha], axis=1) * acc_ref[hd, rs] + pv
                m_ref[hd, rs] = m_new

    def far_step(j, carry):
        step(j, None)
        return carry

    lax.fori_loop(0, jnp.maximum(i - 1, 0), far_step, 0)

    @pl.when(i >= 1)
    def _():
        step(i - 1, 1)

    step(i, 0)

    lv = lam_ref[...]
    s1 = jnp.sum(lv[0:1] * lv[1:2], axis=1, keepdims=True)
    s2 = jnp.sum(lv[2:3] * lv[3:4], axis=1, keepdims=True)
    lam = jnp.exp(s1) - jnp.exp(s2) + lam_init
    for hd in range(n_heads):
        o1 = acc_ref[hd, 0:tq, 0:LANES] / acc_ref[hd, 0:tq, LANES:2 * LANES]
        o2 = acc_ref[hd, tq:2 * tq, 0:LANES] / acc_ref[hd, tq:2 * tq, LANES:2 * LANES]
        o = o1 - lam * o2
        ms = jnp.mean(o * o, axis=-1, keepdims=True)
        o_ref[:, hd * LANES:(hd + 1) * LANES] = (
            o * lax.rsqrt(ms + NORM_EPS) * g_ref[...] * (1.0 - lam_init)).astype(o_ref.dtype)


def _diff_attention(qkv, lam_vecs, bias_tiles, subln_g, *, batch, seq, tq, layer_idx):
    t = qkv.shape[1]
    nq = seq // tq
    rows = min(ATTN_PV_ROWS, tq)
    hps = ATTN_HEADS_PER_STEP
    n_hp = DIFF_HEADS // hps
    lam_init = 0.8 - 0.6 * math.exp(-0.3 * layer_idx)
    kern = functools.partial(_attn_kernel, tq=tq, rows=rows, lam_init=lam_init)
    est = hps * (2 * tq * LANES * 2 + 4 * seq * LANES * 2 + 4 * tq * tq * 4 + 2 * tq * LANES * 2
                 + 2 * tq * LANES * 2 + seq * 2 * LANES * 2 + 2 * tq * LANES * 4 + 2 * tq * 2 * LANES * 4
                 + 2 * tq * tq * 4 + 8 * rows * tq * 4)
    return pl.pallas_call(
        kern,
        out_shape=jax.ShapeDtypeStruct((t, DIFF_HEADS * LANES), BF16),
        grid=(batch, n_hp, nq),
        in_specs=[pl.BlockSpec((SUBLANES, LANES), lambda b, h, i: (0, 0)),
                  pl.BlockSpec((hps, tq, LANES), lambda b, h, i: (h, b * nq + i, 0)),
                  pl.BlockSpec((hps, seq, LANES), lambda b, h, i: (n_hp + h, b, 0)),
                  pl.BlockSpec((hps, seq, LANES), lambda b, h, i: (2 * n_hp + h, b, 0)),
                  pl.BlockSpec((hps, 2, tq, tq), lambda b, h, i: (h, 0, 0, 0)),
                  pl.BlockSpec((1, LANES), lambda b, h, i: (0, 0))],
        out_specs=pl.BlockSpec((tq, hps * LANES), lambda b, h, i: (b * nq + i, h)),
        scratch_shapes=[pltpu.VMEM((hps, 2 * tq, LANES), BF16),
                        pltpu.VMEM((hps, seq, 2 * LANES), BF16),
                        pltpu.VMEM((hps, 2 * tq, LANES), F32),
                        pltpu.VMEM((hps, 2 * tq, 2 * LANES), F32)],
        compiler_params=_params(("parallel", "parallel", "arbitrary"), est),
        name="diff_attention",
    )(lam_vecs, qkv, qkv, qkv, bias_tiles, subln_g)


def _mix_kernel(x_ref, y_ref, o_ref, gate_ref, wssd_ref, wdiff_ref, wmix_ref, g_ref, b_ref, out_ref):
    d = x_ref.shape[1]
    ssd = jnp.dot(y_ref[...], wssd_ref[...], preferred_element_type=F32)
    dif = jnp.dot(o_ref[...], wdiff_ref[...], preferred_element_type=F32)
    merged = gate_ref[:, 0:d].astype(F32) * ssd + gate_ref[:, d:2 * d].astype(F32) * dif
    mixed = jnp.dot(merged.astype(BF16), wmix_ref[...], preferred_element_type=F32)
    out_ref[...] = _layer_norm(DN_ALPHA * x_ref[...] + mixed, g_ref[...], b_ref[...])


def _mix(x, y_ssd, o_diff, proj, w_ssd, w_diff, w_mix, ln_g, ln_b, *, tm, gate_block):
    t, d = x.shape
    tm = min(tm, t)
    inner = y_ssd.shape[1]
    dw = o_diff.shape[1]
    row = lambda i: (i, 0)
    est = (2 * tm * (d * 4 + inner * 2 + dw * 2 + 2 * d * 2 + d * 4)
           + (inner * d + dw * d + d * d) * 2 + 6 * tm * d * 4)
    return pl.pallas_call(
        _mix_kernel,
        out_shape=jax.ShapeDtypeStruct((t, d), F32),
        grid=(t // tm,),
        in_specs=[pl.BlockSpec((tm, d), row),
                  pl.BlockSpec((tm, inner), row),
                  pl.BlockSpec((tm, dw), row),
                  pl.BlockSpec((tm, 2 * d), lambda i: (i, gate_block)),
                  _resident((inner, d)), _resident((dw, d)), _resident((d, d)),
                  _resident((1, d)), _resident((1, d))],
        out_specs=pl.BlockSpec((tm, d), row),
        compiler_params=_params(("parallel",), est),
        name="mix_ln1",
    )(x, y_ssd, o_diff, proj, w_ssd, w_diff, w_mix, ln_g, ln_b)


def _pack_bf16_pairs(v):
    w = v.shape[1] // 2
    lo = lax.bitcast_convert_type(v[:, :w].astype(BF16).astype(F32), jnp.int32)
    hi = lax.bitcast_convert_type(v[:, w:].astype(BF16).astype(F32), jnp.int32)
    return jnp.bitwise_or(hi, lax.shift_right_logical(lo, 16))


def _unpack_bf16_pairs(words):
    lo = lax.bitcast_convert_type(lax.shift_left(words, 16), F32)
    hi = lax.bitcast_convert_type(jnp.bitwise_and(words, jnp.int32(-0x10000)), F32)
    return jnp.concatenate([lo, hi], axis=1)


def _xattn_kernel(x_ref, kv_ref, wq_ref, wo_ref, g_ref, b_ref, out_ref, packed_ref, o_scr, *, heads):
    d = x_ref.shape[1]
    dh = d // heads
    xv = x_ref[...]
    q = jnp.dot(xv.astype(BF16), wq_ref[...], preferred_element_type=F32)
    q = (q * (dh ** -0.5)).astype(BF16)
    for h in range(heads):
        kh = kv_ref[:, h * dh:(h + 1) * dh]
        vh = kv_ref[:, d + h * dh:d + (h + 1) * dh]
        s = lax.dot_general(q[:, h * dh:(h + 1) * dh], kh, (((1,), (1,)), ((), ())),
                            preferred_element_type=F32)
        p = jnp.exp(s - jnp.max(s, axis=1, keepdims=True))
        p = p / jnp.sum(p, axis=1, keepdims=True)
        o_scr[:, h * dh:(h + 1) * dh] = jnp.dot(p.astype(BF16), vh, preferred_element_type=F32).astype(BF16)
    att = jnp.dot(o_scr[...], wo_ref[...], preferred_element_type=F32)
    y = _layer_norm(DN_ALPHA * xv + att, g_ref[...], b_ref[...])
    out_ref[...] = y
    packed_ref[...] = _pack_bf16_pairs(y)


def _cross_attention(x, kv, w_cq, w_co, ln_g, ln_b, *, batch, seq, tq):
    t, d = x.shape
    mem_len = kv.shape[0] // batch
    tq = min(tq, seq)
    nq = seq // tq
    est = (2 * tq * d * 4 * 2 + 2 * tq * d * 2 + 2 * mem_len * 2 * d * 2 + 2 * d * d * 2 + tq * d * 2
           + 8 * tq * d * 4)
    row = lambda b, i: (b * nq + i, 0)
    return pl.pallas_call(
        functools.partial(_xattn_kernel, heads=MEM_HEADS),
        out_shape=(jax.ShapeDtypeStruct((t, d), F32), jax.ShapeDtypeStruct((t, d // 2), jnp.int32)),
        grid=(batch, nq),
        in_specs=[pl.BlockSpec((tq, d), row),
                  pl.BlockSpec((mem_len, 2 * d), lambda b, i: (b, 0)),
                  _resident((d, d)), _resident((d, d)), _resident((1, d)), _resident((1, d))],
        out_specs=(pl.BlockSpec((tq, d), row), pl.BlockSpec((tq, d // 2), row)),
        scratch_shapes=[pltpu.VMEM((tq, d), BF16)],
        compiler_params=_params(("parallel", "parallel"), est),
        name="cross_attention_ln2",
    )(x, kv, w_cq, w_co, ln_g, ln_b)


def _router_kernel(x_ref, wr_ref, rb_ref, idx_ref, rank_ref, gates_ref, cnt_ref, run_ref, *, n_experts):
    tm = x_ref.shape[0]
    n_groups = N_EXPERT_GROUPS
    per = n_experts // n_groups

    @pl.when(pl.program_id(0) == 0)
    def _():
        run_ref[...] = jnp.zeros_like(run_ref)

    xv = x_ref[...]
    x_hi = xv.astype(BF16)
    x_lo = (xv - x_hi.astype(F32)).astype(BF16)
    nt = (((1,), (1,)), ((), ()))
    logits = (lax.dot_general(wr_ref[0], x_hi, nt, preferred_element_type=F32)
              + lax.dot_general(wr_ref[0], x_lo, nt, preferred_element_type=F32)
              + lax.dot_general(wr_ref[1], x_hi, nt, preferred_element_type=F32))
    sc = _sigmoid(logits[0:n_experts]).reshape(per, n_groups, tm)
    choice = sc + rb_ref[0:n_experts].reshape(per, n_groups, 1)
    neg = -jnp.inf
    member = lax.broadcasted_iota(jnp.int32, (per, n_groups, tm), 0)
    group3 = lax.broadcasted_iota(jnp.int32, (per, n_groups, tm), 1)
    m1 = jnp.max(choice, axis=0, keepdims=True)
    i1 = jnp.min(jnp.where(choice == m1, member, per), axis=0, keepdims=True)
    m2 = jnp.max(jnp.where(member == i1, neg, choice), axis=0, keepdims=True)
    gscore = (m1 + m2)
    gsel = jnp.zeros((1, n_groups, tm), F32)
    gidx = lax.broadcasted_iota(jnp.int32, (1, n_groups, tm), 1)
    cur = gscore
    for _ in range(TOP_GROUPS):
        mx = jnp.max(cur, axis=1, keepdims=True)
        ix = jnp.min(jnp.where(cur == mx, gidx, n_groups), axis=1, keepdims=True)
        hit = gidx == ix
        gsel = jnp.where(hit, 1.0, gsel)
        cur = jnp.where(hit, neg, cur)
    cur = jnp.where(gsel > 0.0, choice, neg)
    eidx = group3 * per + member
    esel = jnp.zeros((per, n_groups, tm), F32)
    hits, idx_rows = [], []
    for _ in range(TOP_K):
        mx = jnp.max(jnp.max(cur, axis=0, keepdims=True), axis=1, keepdims=True)
        ix = jnp.min(jnp.min(jnp.where(cur == mx, eidx, n_experts), axis=0, keepdims=True),
                     axis=1, keepdims=True)
        hit = eidx == ix
        esel = jnp.where(hit, 1.0, esel)
        cur = jnp.where(hit, neg, cur)
        hits.append(hit)
        idx_rows.append(ix.reshape(1, tm))
    w = esel * sc
    tot = jnp.sum(jnp.sum(w, axis=0, keepdims=True), axis=1, keepdims=True)
    gw = w / tot * ROUTED_SCALE

    r_i = lax.broadcasted_iota(jnp.int32, (tm, tm), 0)
    c_i = lax.broadcasted_iota(jnp.int32, (tm, tm), 1)
    upper = jnp.where(r_i <= c_i, 1.0, 0.0).astype(BF16)
    pref = jnp.dot(esel.reshape(n_experts, tm).astype(BF16), upper, preferred_element_type=F32)
    run = run_ref[...]
    rank3 = (jnp.concatenate([run] * (tm // LANES), axis=1) + pref - 1.0).reshape(per, n_groups, tm)

    def pick(hit, vals):
        return jnp.sum(jnp.sum(jnp.where(hit, vals, 0.0), axis=0, keepdims=True), axis=1).reshape(1, tm)

    idx_ref[...] = jnp.concatenate(idx_rows, axis=0)
    rank_ref[...] = jnp.concatenate([pick(h, rank3) for h in hits], axis=0).astype(jnp.int32)
    wk = jnp.concatenate([pick(h, gw) for h in hits] + [jnp.zeros((LANES - TOP_K, tm), F32)], axis=0)
    gates_ref[...] = wk.T
    run = run + jnp.broadcast_to(pref[:, tm - 1:tm], run.shape)
    run_ref[...] = run
    cnt_ref[...] = run


def _router(x, w_router, router_bias, *, tm, row0, t):
    d = x.shape[1]
    n_experts = w_router.shape[1]
    per = n_experts // N_EXPERT_GROUPS
    tm = min(tm, t)
    blk0 = row0 // tm

    def member_major(v):
        return v.reshape(N_EXPERT_GROUPS, per, *v.shape[1:]).swapaxes(0, 1).reshape(v.shape)

    wt = jnp.pad(member_major(w_router.astype(F32).T), ((0, LANES - n_experts), (0, 0)))
    w_hi = wt.astype(BF16)
    wr = jnp.stack([w_hi, (wt - w_hi.astype(F32)).astype(BF16)])
    rb = jnp.pad(member_major(router_bias.astype(F32)), (0, LANES - n_experts)).reshape(LANES, 1)
    est = (2 * tm * d * 4 + 2 * LANES * d * 2 + 2 * tm * LANES * 4 + 60 * n_experts * tm * 4 + 3 * tm * tm * 4)
    pick_spec = pl.BlockSpec((TOP_K, tm), lambda i: (0, i))
    idx_t, rank_t, gates, counts = pl.pallas_call(
        functools.partial(_router_kernel, n_experts=n_experts),
        out_shape=(jax.ShapeDtypeStruct((TOP_K, t), jnp.int32),
                   jax.ShapeDtypeStruct((TOP_K, t), jnp.int32),
                   jax.ShapeDtypeStruct((t, LANES), F32),
                   jax.ShapeDtypeStruct((n_experts, LANES), F32)),
        grid=(t // tm,),
        in_specs=[pl.BlockSpec((tm, d), lambda i: (blk0 + i, 0)),
                  _resident((2, LANES, d)), _resident((LANES, 1))],
        out_specs=(pick_spec, pick_spec,
                   pl.BlockSpec((tm, LANES), lambda i: (i, 0)),
                   pl.BlockSpec((n_experts, LANES), lambda i: (0, 0))),
        scratch_shapes=[pltpu.VMEM((n_experts, LANES), F32)],
        compiler_params=_params(("arbitrary",), est),
        name="router",
    )(x, wr, rb)
    counts = counts[:, 0].astype(jnp.int32).reshape(per, N_EXPERT_GROUPS).T.reshape(n_experts)
    return idx_t, rank_t, gates, counts


def _positions_kernel(cnt_ref, idx_ref, rank_ref, pos_ref, texp_ref, off_ref, *, n_experts, rows, n_tiles):
    @pl.when(pl.program_id(0) == 0)
    def _():
        def per_expert(e, toff):
            off_ref[e] = toff * rows
            nt = lax.shift_right_logical(cnt_ref[e] + (rows - 1), int(math.log2(rows)))

            def fill(j, c):
                texp_ref[toff + j] = e
                return c

            lax.fori_loop(0, nt, fill, 0)
            return toff + nt

        n_used = lax.fori_loop(0, n_experts, per_expert, 0)

        def tail(j, c):
            texp_ref[j] = n_experts - 1
            return c

        lax.fori_loop(n_used, n_tiles, tail, 0)
        texp_ref[n_tiles] = n_used

    idx = idx_ref[...]
    pos = rank_ref[...]
    for e in range(n_experts):
        pos = pos + jnp.where(idx == e, off_ref[e], 0)
    pos_ref[...] = pos


def _positions(counts, idx_t, rank_t, *, rows, n_tiles, tm):
    n_experts = counts.shape[0]
    k, t = idx_t.shape
    tm = min(tm, t)
    spec = pl.BlockSpec((k, tm), lambda i: (0, i))
    kern = functools.partial(_positions_kernel, n_experts=n_experts, rows=rows, n_tiles=n_tiles)
    return pl.pallas_call(
        kern,
        out_shape=(jax.ShapeDtypeStruct((k, t), jnp.int32),
                   jax.ShapeDtypeStruct((n_tiles + 1,), jnp.int32)),
        grid=(t // tm,),
        in_specs=[pl.BlockSpec(memory_space=pltpu.SMEM), spec, spec],
        out_specs=(spec, pl.BlockSpec(memory_space=pltpu.SMEM)),
        scratch_shapes=[pltpu.SMEM((n_experts,), jnp.int32)],
        compiler_params=_params(("arbitrary",), 16 * k * tm * 4),
        name="moe_positions",
    )(counts, idx_t, rank_t)


def _sc_mesh():
    return plsc.VectorSubcoreMesh(core_axis_name="c", subcore_axis_name="s",
                                  num_cores=SC_CORES, num_subcores=SC_SUBCORES)


def _sc_dispatch(packed, pos_chunks, n_rows, row0):
    w = packed.shape[1]
    n_chunks, k, n = pos_chunks.shape
    per_worker = n_chunks // (SC_CORES * SC_SUBCORES)

    @functools.partial(
        pl.kernel, mesh=_sc_mesh(),
        out_type=jax.ShapeDtypeStruct((n_rows, w), packed.dtype),
        scratch_types=[pltpu.VMEM((k, n), jnp.int32), pltpu.VMEM((n, w), packed.dtype),
                       pltpu.SemaphoreType.DMA],
        name="moe_dispatch_sc",
    )
    def scatter_rows(x_hbm, pos_hbm, out_hbm, idx_v, rows_v, sem):
        wid = lax.axis_index("s") * SC_CORES + lax.axis_index("c")

        @pl.loop(0, per_worker)
        def _(step):
            c = wid * per_worker + step
            pltpu.sync_copy(pos_hbm.at[c], idx_v)
            pltpu.sync_copy(x_hbm.at[pl.ds(row0 + c * n, n)], rows_v)
            copies = [pltpu.async_copy(rows_v, out_hbm.at[idx_v.at[kk]], sem) for kk in range(k)]
            for cp in copies:
                cp.wait()

    return scatter_rows(packed, pos_chunks)


def _sc_combine(sorted_rows, pos_chunks):
    _, w = sorted_rows.shape
    n_chunks, k, n = pos_chunks.shape
    per_worker = n_chunks // (SC_CORES * SC_SUBCORES)

    @functools.partial(
        pl.kernel, mesh=_sc_mesh(),
        out_type=jax.ShapeDtypeStruct((k, n_chunks * n, w), sorted_rows.dtype),
        scratch_types=[pltpu.VMEM((k, n), jnp.int32), pltpu.VMEM((n, w), sorted_rows.dtype),
                       pltpu.SemaphoreType.DMA],
        name="moe_combine_sc",
    )
    def gather_rows(y_hbm, pos_hbm, out_hbm, idx_v, rows_v, sem):
        wid = lax.axis_index("s") * SC_CORES + lax.axis_index("c")

        @pl.loop(0, per_worker)
        def _(step):
            c = wid * per_worker + step
            pltpu.sync_copy(pos_hbm.at[c], idx_v)
            for kk in range(k):
                pltpu.async_copy(y_hbm.at[idx_v.at[kk]], rows_v, sem).wait()
                pltpu.sync_copy(rows_v, out_hbm.at[kk, pl.ds(c * n, n)])

    return gather_rows(sorted_rows, pos_chunks)


def _expert_kernel(texp_ref, xs_ref, wgu_ref, wdn_ref, ys_ref, *, n_tiles):
    ff = wdn_ref.shape[1]

    @pl.when(pl.program_id(0) < texp_ref[n_tiles])
    def _():
        xv = _unpack_bf16_pairs(xs_ref[...]).astype(BF16)
        gu = jnp.dot(xv, wgu_ref[0], preferred_element_type=F32)
        hid = _silu(gu[:, 0:ff]) * gu[:, ff:2 * ff]
        ys_ref[...] = _pack_bf16_pairs(jnp.dot(hid.astype(BF16), wdn_ref[0], preferred_element_type=F32))


def _experts(tile_expert, sorted_rows, w_gu, w_dn, *, rows):
    n_rows, w = sorted_rows.shape
    n_tiles = n_rows // rows
    _, d, ff2 = w_gu.shape
    ff = ff2 // 2
    tile = lambda j, te: (jnp.minimum(j, te[n_tiles] - 1), 0)
    est = 4 * rows * w * 4 + 2 * (d * ff2 + ff * d) * 2 + 4 * rows * d * 4 + 4 * rows * ff2 * 4
    return pl.pallas_call(
        functools.partial(_expert_kernel, n_tiles=n_tiles),
        out_shape=jax.ShapeDtypeStruct((n_rows, w), sorted_rows.dtype),
        grid_spec=pltpu.PrefetchScalarGridSpec(
            num_scalar_prefetch=1,
            grid=(n_tiles,),
            in_specs=[pl.BlockSpec((rows, w), tile),
                      pl.BlockSpec((1, d, ff2), lambda j, te: (te[j], 0, 0)),
                      pl.BlockSpec((1, ff, d), lambda j, te: (te[j], 0, 0))],
            out_specs=pl.BlockSpec((rows, w), tile)),
        compiler_params=_params(("arbitrary",), est),
        name="moe_experts",
    )(tile_expert, sorted_rows, w_gu, w_dn)


def _moe_out_kernel(x_ref, yk_ref, gates_ref, wsgu_ref, wsdn_ref, g_ref, b_ref, *rest):
    out_ref = rest[-1]
    ff = wsdn_ref.shape[0]
    xv = x_ref[...]
    gu = jnp.dot(xv.astype(BF16), wsgu_ref[...], preferred_element_type=F32)
    hid = _silu(gu[:, 0:ff]) * gu[:, ff:2 * ff]
    acc = jnp.dot(hid.astype(BF16), wsdn_ref[...], preferred_element_type=F32)
    for k in range(yk_ref.shape[0]):
        acc = acc + gates_ref[:, k:k + 1] * _unpack_bf16_pairs(yk_ref[k])
    out_ref[...] = _layer_norm(DN_ALPHA * xv + acc, g_ref[...], b_ref[...])


def _moe_out(x, yk, gates, w_sgu, w_sdn, ln_g, ln_b, *, tm, row0, prev):
    t_all, d = x.shape
    k, t, w = yk.shape
    sff2 = w_sgu.shape[1]
    tm = min(tm, t)
    blk0 = row0 // tm
    est = (2 * tm * d * 4 * 2 + 2 * k * tm * w * 4 + 2 * tm * LANES * 4
           + (d * sff2 + (sff2 // 2) * d) * 2 + 6 * tm * d * 4)
    in_specs = [pl.BlockSpec((tm, d), lambda i: (blk0 + i, 0)),
                pl.BlockSpec((k, tm, w), lambda i: (0, i, 0)),
                pl.BlockSpec((tm, LANES), lambda i: (i, 0)),
                _resident((d, sff2)), _resident((sff2 // 2, d)),
                _resident((1, d)), _resident((1, d))]
    args = [x, yk, gates, w_sgu, w_sdn, ln_g, ln_b]
    aliases = {}
    if prev is not None:
        in_specs.append(pl.BlockSpec(memory_space=pl.ANY))
        args.append(prev)
        aliases = {len(args) - 1: 0}
    return pl.pallas_call(
        _moe_out_kernel,
        out_shape=jax.ShapeDtypeStruct((t_all, d), F32),
        grid=(t // tm,),
        in_specs=in_specs,
        out_specs=pl.BlockSpec((tm, d), lambda i: (blk0 + i, 0)),
        input_output_aliases=aliases,
        compiler_params=_params(("parallel",), est),
        name="moe_out_ln3",
    )(*args)


def _moe(x, packed, w_router, router_bias, w_gu, w_dn, w_sgu, w_sdn, ln_g, ln_b):
    t_all, d = x.shape
    n_experts = w_router.shape[1]
    n = SC_CHUNK_TOKENS
    splits = MOE_SPLITS if t_all % (MOE_SPLITS * n * SC_CORES * SC_SUBCORES) == 0 else 1
    t = t_all // splits
    rows = min(MOE_ROWS_PER_TILE, t)
    n_tiles = (t * TOP_K) // rows + n_experts
    parts = []
    for s in range(splits):
        row0 = s * t
        idx_t, rank_t, gates, counts = _router(x, w_router, router_bias, tm=ROUTER_ROWS, row0=row0, t=t)
        pos_t, tile_expert = _positions(counts, idx_t, rank_t, rows=rows, n_tiles=n_tiles,
                                        tm=POSITIONS_TOKENS)
        pos_chunks = pos_t.reshape(TOP_K, t // n, n).transpose(1, 0, 2)
        sorted_x = _sc_dispatch(packed, pos_chunks, n_tiles * rows, row0)
        parts.append((row0, gates, tile_expert, pos_chunks, sorted_x))
    out = None
    for row0, gates, tile_expert, pos_chunks, sorted_x in parts:
        sorted_y = _experts(tile_expert, sorted_x, w_gu, w_dn, rows=rows)
        yk = _sc_combine(sorted_y, pos_chunks)
        out = _moe_out(x, yk, gates, w_sgu, w_sdn, ln_g, ln_b, tm=MOE_OUT_ROWS, row0=row0, prev=out)
    return out


def kernel(x, mem, w_in, conv_w, conv_b, dt_bias, a_log, d_skip, ssd_norm_g, lambda_q1, lambda_k1, lambda_q2, lambda_k2, subln_g, rel_bias, w_ssd_br, w_diff_br, w_mix_out, ln1_g, ln1_b, w_cq, w_ckv, w_co, ln2_g, ln2_b, w_router, router_bias, w_exp_gu, w_exp_down, w_sh_gu, w_sh_down, ln3_g, ln3_b):
    batch, seq, d = x.shape
    depth = w_in.shape[0]
    inner = w_ssd_br.shape[1]
    xbc_width = conv_w.shape[2]
    heads = dt_bias.shape[1]
    diff_width = w_diff_br.shape[1]
    o_z, o_xbc = inner, inner + xbc_width
    o_dt = o_xbc + heads
    o_v = o_dt + 3 * diff_width
    t = batch * seq
    tq_attn = min(ATTN_TILE, seq)

    def vec(v):
        return v.astype(F32).reshape(1, -1)

    xt = x.reshape(t, d)
    memt = mem.reshape(-1, d)
    bias_tiles = _bias_tiles(rel_bias, tq_attn)

    for l in range(depth):
        wl = w_in[l]
        w_big = jnp.concatenate([wl[:, o_z:o_xbc], wl[:, o_dt:o_v], wl[:, :o_z], wl[:, o_v:]],
                                axis=1).astype(BF16)
        assert 2 * d == inner
        w_dt = jnp.pad(wl[:, o_xbc:o_dt], ((0, 0), (0, LANES - heads))).astype(BF16)
        xbc, qkv, zg, dt_raw = _in_proj(xt, w_big, w_dt, n_xbc_cols=xbc_width, n_qkv_cols=3 * diff_width,
                                        n_z_cols=inner, tm=PROJ_ROWS, tn=PROJ_COLS)

        y_ssd = _ssd_branch(zg, xbc, dt_raw, conv_w[l], conv_b[l], dt_bias[l], a_log[l], d_skip[l],
                            ssd_norm_g[l], batch=batch, seq=seq)

        lam_vecs = jnp.pad(jnp.stack([lambda_q1[l], lambda_k1[l], lambda_q2[l], lambda_k2[l]]).astype(F32),
                           ((0, SUBLANES - 4), (0, LANES - lambda_q1.shape[1])))
        o_diff = _diff_attention(qkv, lam_vecs, bias_tiles, vec(subln_g[l]), batch=batch, seq=seq,
                                 tq=tq_attn, layer_idx=l)

        xt = _mix(xt, y_ssd, o_diff, zg, w_ssd_br[l].astype(BF16), w_diff_br[l].astype(BF16),
                  w_mix_out[l].astype(BF16), vec(ln1_g[l]), vec(ln1_b[l]), tm=MIX_ROWS, gate_block=1)

        kv = _matmul(memt, w_ckv[l].astype(BF16), BF16, tm=PROJ_ROWS, tn=PROJ_COLS, name="mem_kv_proj")
        xt, packed = _cross_attention(xt, kv, w_cq[l].astype(BF16), w_co[l].astype(BF16), vec(ln2_g[l]),
                                      vec(ln2_b[l]), batch=batch, seq=seq, tq=XATTN_ROWS)

        xt = _moe(xt, packed, w_router[l], router_bias[l], w_exp_gu[l].astype(BF16),
                  w_exp_down[l].astype(BF16), w_sh_gu[l].astype(BF16), w_sh_down[l].astype(BF16),
                  vec(ln3_g[l]), vec(ln3_b[l]))
    return xt.reshape(batch, seq, d)
```
